```python
import math
import jax, jax.numpy as jnp
from jax import lax
import numpy as np

D_MODEL = 1024
BATCH = 2
SEQ = 8192
DEPTH = 2

PLE_DIM = 256
NSA_HEADS = 8
NSA_KV_GROUPS = 2
NSA_HEAD_DIM = 64
NSA_HPG = NSA_HEADS // NSA_KV_GROUPS
NSA_Q_W = NSA_HEADS * NSA_HEAD_DIM
NSA_KV_W = NSA_KV_GROUPS * NSA_HEAD_DIM
CMP_BLOCK = 32
CMP_STRIDE = 16
CMP_HIDDEN = 256
SEL_BLOCK = 64
SEL_TOPN = 16
WINDOW = 512
Q_BLOCK = 128
FORCE_SCORE = 1e4
MASK_VALUE = -1e30
SSD_HEADS = 8
SSD_HEAD_DIM = 64
SSD_INNER = SSD_HEADS * SSD_HEAD_DIM
SSD_GROUPS = 2
SSD_STATE = 64
SSD_CHUNK = 128
SSD_XBC_W = SSD_INNER + 2 * SSD_GROUPS * SSD_STATE
CONV_WIDTH = 4
LRU_WIDTH = 512
LRU_BLOCKS = 8
LRU_BLOCK_DIM = LRU_WIDTH // LRU_BLOCKS
LRU_C = 8.0
N_EXPERTS = 16
N_EXPERT_GROUPS = 4
EXPERTS_PER_GROUP = N_EXPERTS // N_EXPERT_GROUPS
TOP_K = 2
D_EXPERT = 256
ALPHA = (2 * DEPTH) ** 0.25
BETA = (8 * DEPTH) ** -0.25
LN_EPS = 1e-5
RMS_EPS = 1e-5
IN_SIZES = (NSA_Q_W, NSA_KV_W, NSA_KV_W, NSA_KV_W, NSA_KV_W, NSA_KV_W, NSA_KV_W, NSA_HEADS * 3,
            SSD_INNER, SSD_XBC_W, SSD_HEADS, LRU_WIDTH, LRU_WIDTH, 3 * D_MODEL)
IN_WIDTH = sum(IN_SIZES)

kernel_name = 'hybrid_nsa_ssd_rglru_moe_block'


def layer_norm(x, g, b):
    xf = x.astype(jnp.float32)
    mu = jnp.mean(xf, -1, keepdims=True)
    xc = xf - mu
    var = jnp.mean(xc * xc, -1, keepdims=True)
    return (xc * lax.rsqrt(var + LN_EPS) * g.astype(jnp.float32) + b.astype(jnp.float32)).astype(x.dtype)


def rms_norm(x, w):
    xf = x.astype(jnp.float32)
    ms = jnp.mean(xf * xf, -1, keepdims=True)
    return (xf * lax.rsqrt(ms + RMS_EPS) * w.astype(jnp.float32)).astype(x.dtype)


def causal_depthwise_conv(x, w, b):
    width = w.shape[0]
    S = x.shape[1]
    xp = jnp.pad(x, ((0, 0), (width - 1, 0), (0, 0)))
    y = b
    for k in range(width):
        y = y + xp[:, k:k + S] * w[k]
    return y


def masked_softmax(s, mask):
    s = jnp.where(mask, s.astype(jnp.float32), MASK_VALUE)
    m = jnp.max(s, -1, keepdims=True)
    e = jnp.where(mask, jnp.exp(s - m), 0.0)
    return e / jnp.maximum(jnp.sum(e, -1, keepdims=True), 1e-30)


def nsa_compress(kv, pe, w1, w2):
    Bsz, S, G, dk = kv.shape
    n_cmp = (S - CMP_BLOCK) // CMP_STRIDE + 1
    idx = np.arange(n_cmp)[:, None] * CMP_STRIDE + np.arange(CMP_BLOCK)[None, :]
    blk = kv[:, idx] + pe[None, None, :, None, :]
    blk = jnp.transpose(blk, (0, 1, 3, 2, 4)).reshape(Bsz, n_cmp, G, CMP_BLOCK * dk)
    return jax.nn.gelu(blk @ w1) @ w2


def nsa_attention(q, k_cmp, v_cmp, k_sel, v_sel, k_win, v_win, gate_logits,
                  pe_k, w1_k, w2_k, pe_v, w1_v, w2_v):
    Bsz, S = q.shape[:2]
    G, Hg, dk = NSA_KV_GROUPS, NSA_HPG, NSA_HEAD_DIM
    scale = dk ** -0.5
    q = q.reshape(Bsz, S, G, Hg, dk)
    k_cmp, v_cmp, k_sel, v_sel, k_win, v_win = [a.reshape(Bsz, S, G, dk) for a in (k_cmp, v_cmp, k_sel, v_sel, k_win, v_win)]
    t = jnp.arange(S)
    kc = nsa_compress(k_cmp, pe_k, w1_k, w2_k)
    vc = nsa_compress(v_cmp, pe_v, w1_v, w2_v)
    n_cmp = kc.shape[1]
    s_cmp = jnp.einsum('bsgjd,bngd->bgjsn', q, kc) * scale
    cmp_mask = (jnp.arange(n_cmp) * CMP_STRIDE + CMP_BLOCK - 1)[None, :] <= t[:, None]
    p_cmp = masked_softmax(s_cmp, cmp_mask)
    o_cmp = jnp.einsum('bgjsn,bngd->bsgjd', p_cmp, vc)
    n_sel = S // SEL_BLOCK
    ratio = SEL_BLOCK // CMP_STRIDE
    strides = np.arange(n_cmp)[:, None] + np.arange(CMP_BLOCK // CMP_STRIDE)[None, :]
    overlap = jnp.sum(jax.nn.one_hot(strides // ratio, n_sel, dtype=jnp.float32), axis=1)
    imp = jnp.einsum('bgjsn,nm->bgsm', p_cmp, overlap)
    blk = jnp.arange(n_sel)
    cur = t // SEL_BLOCK
    forced = (blk[None, :] == 0) | (blk[None, :] == cur[:, None]) | (blk[None, :] == cur[:, None] - 1)
    causal_blk = blk[None, :] * SEL_BLOCK <= t[:, None]
    imp = jnp.where(forced, FORCE_SCORE, jnp.where(causal_blk, imp, -1.0))
    n_top = min(SEL_TOPN, n_sel)
    _, sel_idx = lax.top_k(imp, n_top)
    ks_blocks = k_sel.reshape(Bsz, n_sel, SEL_BLOCK, G, dk).transpose(0, 3, 1, 2, 4)
    vs_blocks = v_sel.reshape(Bsz, n_sel, SEL_BLOCK, G, dk).transpose(0, 3, 1, 2, 4)
    kw_pad = jnp.pad(k_win, ((0, 0), (WINDOW, 0), (0, 0), (0, 0)))
    vw_pad = jnp.pad(v_win, ((0, 0), (WINDOW, 0), (0, 0), (0, 0)))
    bi = jnp.arange(Bsz)[:, None, None, None]
    gi = jnp.arange(G)[None, :, None, None]
    n_qb = S // Q_BLOCK
    q_blocks = q.reshape(Bsz, n_qb, Q_BLOCK, G, Hg, dk).transpose(1, 0, 2, 3, 4, 5)
    idx_blocks = sel_idx.reshape(Bsz, G, n_qb, Q_BLOCK, n_top).transpose(2, 0, 1, 3, 4)
    win_len = WINDOW + Q_BLOCK

    def block_fn(args):
        qb, idxb, t0 = args
        tq = t0 + jnp.arange(Q_BLOCK)
        ksg = ks_blocks[bi, gi, idxb]
        vsg = vs_blocks[bi, gi, idxb]
        s = jnp.einsum('bqgjd,bgqkld->bgjqkl', qb, ksg) * scale
        s = s.reshape(Bsz, G, Hg, Q_BLOCK, n_top * SEL_BLOCK)
        pos = idxb[..., None] * SEL_BLOCK + jnp.arange(SEL_BLOCK)
        smask = (pos <= tq[None, None, :, None, None]).reshape(Bsz, G, 1, Q_BLOCK, n_top * SEL_BLOCK)
        ps = masked_softmax(s, smask)
        o_s = jnp.einsum('bgjqm,bgqmd->bqgjd', ps, vsg.reshape(Bsz, G, Q_BLOCK, n_top * SEL_BLOCK, dk))
        kw = lax.dynamic_slice_in_dim(kw_pad, t0, win_len, axis=1)
        vw = lax.dynamic_slice_in_dim(vw_pad, t0, win_len, axis=1)
        sw = jnp.einsum('bqgjd,bmgd->bgjqm', qb, kw) * scale
        kpos = t0 - WINDOW + jnp.arange(win_len)
        diff = tq[:, None] - kpos[None, :]
        wmask = (diff >= 0) & (diff < WINDOW) & (kpos[None, :] >= 0)
        pw = masked_softmax(sw, wmask)
        o_w = jnp.einsum('bgjqm,bmgd->bqgjd', pw, vw)
        return o_s, o_w

    o_slc, o_win = lax.map(block_fn, (q_blocks, idx_blocks, jnp.arange(n_qb) * Q_BLOCK))
    o_slc = o_slc.transpose(1, 0, 2, 3, 4, 5).reshape(Bsz, S, G, Hg, dk)
    o_win = o_win.transpose(1, 0, 2, 3, 4, 5).reshape(Bsz, S, G, Hg, dk)
    g = jax.nn.sigmoid(gate_logits.reshape(Bsz, S, G, Hg, 3))
    o = g[..., 0:1] * o_cmp + g[..., 1:2] * o_slc + g[..., 2:3] * o_win
    return o.reshape(Bsz, S, NSA_Q_W)


def segsum(a):
    T = a.shape[-1]
    ae = jnp.broadcast_to(a[..., None], a.shape + (T,))
    ae = jnp.where(jnp.tril(jnp.ones((T, T), bool), -1), ae, 0.0)
    cs = jnp.cumsum(ae, axis=-2)
    return jnp.where(jnp.tril(jnp.ones((T, T), bool)), cs, -jnp.inf)


def ssd_chunked(x, dt, A, Bm, Cm):
    Bsz, S, H, P = x.shape
    G, N = Bm.shape[2], Bm.shape[3]
    Hg = H // G
    L = SSD_CHUNK
    nc = S // L
    a = (dt * A).reshape(Bsz, nc, L, H).transpose(0, 3, 1, 2)
    a_cum = jnp.cumsum(a, axis=-1)
    X = (x * dt[..., None]).reshape(Bsz, nc, L, G, Hg, P)
    Bc = Bm.reshape(Bsz, nc, L, G, N)
    Cc = Cm.reshape(Bsz, nc, L, G, N)
    Lmat = jnp.exp(segsum(a)).reshape(Bsz, G, Hg, nc, L, L)
    CB = jnp.einsum('bclgn,bcsgn->bcgls', Cc, Bc)
    y_diag = jnp.einsum('bcgls,bgjcls,bcsgjp->bclgjp', CB, Lmat, X)
    decay_states = jnp.exp(a_cum[..., -1:] - a_cum).reshape(Bsz, G, Hg, nc, L)
    states = jnp.einsum('bclgn,bgjcl,bclgjp->bcgjpn', Bc, decay_states, X)
    a_chunk = jnp.pad(a_cum[..., -1], ((0, 0), (0, 0), (1, 0)))
    decay_chunk = jnp.exp(segsum(a_chunk)).reshape(Bsz, G, Hg, nc + 1, nc + 1)
    states0 = jnp.concatenate([jnp.zeros_like(states[:, :1]), states], axis=1)
    new_states = jnp.einsum('bgjzc,bcgjpn->bzgjpn', decay_chunk, states0)
    prev_states = new_states[:, :-1]
    decay_out = jnp.exp(a_cum).reshape(Bsz, G, Hg, nc, L)
    y_off = jnp.einsum('bclgn,bcgjpn,bgjcl->bclgjp', Cc, prev_states, decay_out)
    return (y_diag + y_off).reshape(Bsz, S, H, P)


def ssd_mixer(z, xbc, dt_raw, conv_w, conv_b, dt_bias, a_log, d_skip, norm_w):
    Bsz, S = z.shape[:2]
    xbc = jax.nn.silu(causal_depthwise_conv(xbc, conv_w, conv_b))
    xs, Bm, Cm = jnp.split(xbc, [SSD_INNER, SSD_INNER + SSD_GROUPS * SSD_STATE], axis=-1)
    xs = xs.reshape(Bsz, S, SSD_HEADS, SSD_HEAD_DIM)
    Bm = Bm.reshape(Bsz, S, SSD_GROUPS, SSD_STATE)
    Cm = Cm.reshape(Bsz, S, SSD_GROUPS, SSD_STATE)
    dt = jax.nn.softplus(dt_raw + dt_bias)
    A = -jnp.exp(a_log)
    y = ssd_chunked(xs, dt, A, Bm, Cm) + d_skip[:, None] * xs
    y = y.reshape(Bsz, S, SSD_INNER) * jax.nn.silu(z)
    return rms_norm(y, norm_w)


def lru_combine(c1, c2):
    a1, b1 = c1
    a2, b2 = c2
    return a1 * a2, a2 * b1 + b2


def rglru_mixer(xr, yr, conv_w, conv_b, wa, ba, wx, bx, lam):
    Bsz, S = xr.shape[:2]
    xr = causal_depthwise_conv(xr, conv_w, conv_b)
    xb = xr.reshape(Bsz, S, LRU_BLOCKS, LRU_BLOCK_DIM)
    r = jax.nn.sigmoid(jnp.einsum('bsnc,ncd->bsnd', xb, wa).reshape(Bsz, S, LRU_WIDTH) + ba)
    i = jax.nn.sigmoid(jnp.einsum('bsnc,ncd->bsnd', xb, wx).reshape(Bsz, S, LRU_WIDTH) + bx)
    log_a = -LRU_C * r * jax.nn.softplus(-lam)
    a = jnp.exp(log_a)
    b = jnp.sqrt(-jnp.expm1(2.0 * log_a)) * (i * xr)
    _, h = lax.associative_scan(lru_combine, (a, b), axis=1)
    return h * jax.nn.gelu(yr)


def moe(x, router_w, router_b, w_gate, w_up, w_down):
    Bsz, S, D = x.shape
    xt = x.reshape(-1, D)
    aff = jax.nn.sigmoid((xt @ router_w).astype(jnp.float32))
    sel = aff + router_b.astype(jnp.float32)
    grp_score = jnp.sum(lax.top_k(sel.reshape(-1, N_EXPERT_GROUPS, EXPERTS_PER_GROUP), TOP_K)[0], axis=-1)
    best = jnp.argmax(grp_score, axis=-1)
    in_grp = (jnp.arange(N_EXPERTS) // EXPERTS_PER_GROUP)[None, :] == best[:, None]
    _, top_idx = lax.top_k(jnp.where(in_grp, sel, -jnp.inf), TOP_K)
    w = jnp.take_along_axis(aff, top_idx, axis=-1)
    w = w / jnp.sum(w, -1, keepdims=True)
    gates = jnp.sum(jax.nn.one_hot(top_idx, N_EXPERTS, dtype=jnp.float32) * w[..., None], axis=1)
    h = jax.nn.silu(jnp.einsum('td,edf->etf', xt, w_gate)) * jnp.einsum('td,edf->etf', xt, w_up)
    h = h * gates.T[:, :, None].astype(h.dtype)
    return jnp.einsum('etf,efd->td', h, w_down).reshape(Bsz, S, D)


def setup_inputs(seed: int = 0) -> dict:
    key = jax.random.key(seed)
    ks = iter(jax.random.split(key, 48))

    def nrm(shape, scale):
        return jax.random.normal(next(ks), shape, jnp.float32) * scale

    def unif(shape, lo, hi):
        return jax.random.uniform(next(ks), shape, jnp.float32, lo, hi)

    dk = NSA_HEAD_DIM
    dt0 = jnp.exp(unif((DEPTH, SSD_HEADS), math.log(1e-3), math.log(1e-1)))
    a0 = unif((DEPTH, LRU_WIDTH), 0.9, 0.999) ** (1.0 / LRU_C)
    return {
        'x': nrm((BATCH, SEQ, D_MODEL), 1.0),
        'p': nrm((DEPTH, BATCH, SEQ, PLE_DIM), 1.0),
        'w_in': nrm((DEPTH, D_MODEL, IN_WIDTH), D_MODEL ** -0.5),
        'nsa_pe_k': nrm((DEPTH, CMP_BLOCK, dk), 0.1),
        'nsa_w1_k': nrm((DEPTH, CMP_BLOCK * dk, CMP_HIDDEN), (CMP_BLOCK * dk) ** -0.5),
        'nsa_w2_k': nrm((DEPTH, CMP_HIDDEN, dk), CMP_HIDDEN ** -0.5),
        'nsa_pe_v': nrm((DEPTH, CMP_BLOCK, dk), 0.1),
        'nsa_w1_v': nrm((DEPTH, CMP_BLOCK * dk, CMP_HIDDEN), (CMP_BLOCK * dk) ** -0.5),
        'nsa_w2_v': nrm((DEPTH, CMP_HIDDEN, dk), CMP_HIDDEN ** -0.5),
        'ssd_conv_w': nrm((DEPTH, CONV_WIDTH, SSD_XBC_W), CONV_WIDTH ** -0.5),
        'ssd_conv_b': nrm((DEPTH, SSD_XBC_W), 0.02),
        'ssd_dt_bias': dt0 + jnp.log(-jnp.expm1(-dt0)),
        'ssd_a_log': jnp.log(unif((DEPTH, SSD_HEADS), 1.0, 16.0)),
        'ssd_d': 1.0 + nrm((DEPTH, SSD_HEADS), 0.1),
        'ssd_norm_w': 1.0 + nrm((DEPTH, SSD_INNER), 0.1),
        'lru_conv_w': nrm((DEPTH, CONV_WIDTH, LRU_WIDTH), CONV_WIDTH ** -0.5),
        'lru_conv_b': nrm((DEPTH, LRU_WIDTH), 0.02),
        'lru_wa': nrm((DEPTH, LRU_BLOCKS, LRU_BLOCK_DIM, LRU_BLOCK_DIM), LRU_BLOCK_DIM ** -0.5),
        'lru_ba': nrm((DEPTH, LRU_WIDTH), 0.02),
        'lru_wx': nrm((DEPTH, LRU_BLOCKS, LRU_BLOCK_DIM, LRU_BLOCK_DIM), LRU_BLOCK_DIM ** -0.5),
        'lru_bx': nrm((DEPTH, LRU_WIDTH), 0.02),
        'lru_lambda': jnp.log(a0) - jnp.log1p(-a0),
        'proj_nsa': nrm((DEPTH, NSA_Q_W, D_MODEL), NSA_Q_W ** -0.5),
        'proj_ssd': nrm((DEPTH, SSD_INNER, D_MODEL), SSD_INNER ** -0.5),
        'proj_lru': nrm((DEPTH, LRU_WIDTH, D_MODEL), LRU_WIDTH ** -0.5),
        'w_out': nrm((DEPTH, D_MODEL, D_MODEL), D_MODEL ** -0.5 * BETA),
        'ln1_g': 1.0 + nrm((DEPTH, D_MODEL), 0.05),
        'ln1_b': nrm((DEPTH, D_MODEL), 0.02),
        'router_w': nrm((D_MODEL, N_EXPERTS), D_MODEL ** -0.5),
        'router_b': nrm((N_EXPERTS,), 0.01),
        'exp_w_gate': nrm((DEPTH, N_EXPERTS, D_MODEL, D_EXPERT), D_MODEL ** -0.5),
        'exp_w_up': nrm((DEPTH, N_EXPERTS, D_MODEL, D_EXPERT), D_MODEL ** -0.5),
        'exp_w_down': nrm((DEPTH, N_EXPERTS, D_EXPERT, D_MODEL), D_EXPERT ** -0.5 * BETA),
        'ple_w_gate': nrm((DEPTH, D_MODEL, D_MODEL), D_MODEL ** -0.5),
        'ple_w_proj': nrm((DEPTH, PLE_DIM, D_MODEL), PLE_DIM ** -0.5 * BETA),
        'ln2_g': 1.0 + nrm((DEPTH, D_MODEL), 0.05),
        'ln2_b': nrm((DEPTH, D_MODEL), 0.02),
    }


def reference(x, p, w_in, nsa_pe_k, nsa_w1_k, nsa_w2_k, nsa_pe_v, nsa_w1_v, nsa_w2_v,
              ssd_conv_w, ssd_conv_b, ssd_dt_bias, ssd_a_log, ssd_d, ssd_norm_w,
              lru_conv_w, lru_conv_b, lru_wa, lru_ba, lru_wx, lru_bx, lru_lambda,
              proj_nsa, proj_ssd, proj_lru, w_out, ln1_g, ln1_b,
              router_w, router_b, exp_w_gate, exp_w_up, exp_w_down,
              ple_w_gate, ple_w_proj, ln2_g, ln2_b):
    offsets = np.cumsum(IN_SIZES)[:-1].tolist()
    for i in range(DEPTH):
        proj = x @ w_in[i]
        (q, k_cmp, v_cmp, k_sel, v_sel, k_win, v_win, nsa_gate,
         ssd_z, ssd_xbc, ssd_dt, lru_x, lru_y, merge) = jnp.split(proj, offsets, axis=-1)
        o_nsa = nsa_attention(q, k_cmp, v_cmp, k_sel, v_sel, k_win, v_win, nsa_gate,
                              nsa_pe_k[i], nsa_w1_k[i], nsa_w2_k[i], nsa_pe_v[i], nsa_w1_v[i], nsa_w2_v[i])
        o_ssd = ssd_mixer(ssd_z, ssd_xbc, ssd_dt, ssd_conv_w[i], ssd_conv_b[i], ssd_dt_bias[i],
                          ssd_a_log[i], ssd_d[i], ssd_norm_w[i])
        o_lru = rglru_mixer(lru_x, lru_y, lru_conv_w[i], lru_conv_b[i], lru_wa[i], lru_ba[i],
                            lru_wx[i], lru_bx[i], lru_lambda[i])
        g_nsa, g_ssd, g_lru = jnp.split(jax.nn.sigmoid(merge), 3, axis=-1)
        mixed = g_nsa * (o_nsa @ proj_nsa[i]) + g_ssd * (o_ssd @ proj_ssd[i]) + g_lru * (o_lru @ proj_lru[i])
        x = layer_norm(ALPHA * x + mixed @ w_out[i], ln1_g[i], ln1_b[i])
        ple = jax.nn.sigmoid(x @ ple_w_gate[i]) * (p[i] @ ple_w_proj[i])
        y = moe(x, router_w, router_b, exp_w_gate[i], exp_w_up[i], exp_w_down[i])
        x = layer_norm(ALPHA * x + y + ple, ln2_g[i], ln2_b[i])
    return x
```

```python
import functools
import math

import jax
import jax.numpy as jnp
from jax import lax
from jax.experimental import pallas as pl
from jax.experimental.pallas import tpu as pltpu

F32 = jnp.float32
BF16 = jnp.bfloat16

D_MODEL = 1024
PLE_DIM = 256
NSA_HEADS = 8
NSA_KV_GROUPS = 2
NSA_HEAD_DIM = 64
NSA_HPG = NSA_HEADS // NSA_KV_GROUPS
NSA_Q_W = NSA_HEADS * NSA_HEAD_DIM
NSA_KV_W = NSA_KV_GROUPS * NSA_HEAD_DIM
CMP_BLOCK = 32
CMP_STRIDE = 16
CMP_HIDDEN = 256
SEL_BLOCK = 64
SEL_TOPN = 16
WINDOW = 512
FORCE_SCORE = 1e4
MASK_VALUE = -1e30
SSD_HEADS = 8
SSD_HEAD_DIM = 64
SSD_INNER = SSD_HEADS * SSD_HEAD_DIM
SSD_GROUPS = 2
SSD_STATE = 64
SSD_CHUNK = 128
SSD_XBC_W = SSD_INNER + 2 * SSD_GROUPS * SSD_STATE
CONV_WIDTH = 4
LRU_WIDTH = 512
LRU_BLOCKS = 8
LRU_BLOCK_DIM = LRU_WIDTH // LRU_BLOCKS
LRU_C = 8.0
N_EXPERTS = 16
N_EXPERT_GROUPS = 4
EXPERTS_PER_GROUP = N_EXPERTS // N_EXPERT_GROUPS
TOP_K = 2
D_EXPERT = 256
DEPTH = 2
ALPHA = (2 * DEPTH) ** 0.25
LN_EPS = 1e-5
RMS_EPS = 1e-5
IN_SIZES = (NSA_Q_W, NSA_KV_W, NSA_KV_W, NSA_KV_W, NSA_KV_W, NSA_KV_W, NSA_KV_W, NSA_HEADS * 3,
            SSD_INNER, SSD_XBC_W, SSD_HEADS, LRU_WIDTH, LRU_WIDTH, 3 * D_MODEL)

LANES = 128
SEL_LANES = 128
HEAD_SLOT = 128

C_MERGE = 0
C_QEXT = 3072
C_SSDZ = 4096
C_XBC = 4608
C_KV = 5376
C_LRUX = 6144
C_LRUY = 6656
C_SMALL = 7168
PROJ_W = 7680
GATE_W = NSA_HEADS * 3

VMEM_LIMIT = 56 * 1024 * 1024


def _cparams(sem):
    return pltpu.CompilerParams(dimension_semantics=sem, vmem_limit_bytes=VMEM_LIMIT)


def _sigmoid(x):
    return 1.0 / (1.0 + jnp.exp(-x))


def _softplus(x):
    return jnp.maximum(x, 0.0) + jnp.log(1.0 + jnp.exp(-jnp.abs(x)))


def _gelu_tanh(x):
    c = math.sqrt(2.0 / math.pi)
    return 0.5 * x * (1.0 + jnp.tanh(c * (x + 0.044715 * (x * x * x))))


def _dot(a, b):
    return jnp.dot(a, b, preferred_element_type=F32)


def _dot_nt(a, b):
    return lax.dot_general(a, b, (((1,), (1,)), ((), ())), preferred_element_type=F32)


def _dot_tn(a, b):
    return lax.dot_general(a, b, (((0,), (0,)), ((), ())), preferred_element_type=F32)


def _dot_f32(a, b):
    return jnp.dot(a, b, preferred_element_type=F32, precision=lax.Precision.HIGHEST)


def _layer_norm(v, g, b):
    mu = jnp.mean(v, axis=-1, keepdims=True)
    vc = v - mu
    var = jnp.mean(vc * vc, axis=-1, keepdims=True)
    return vc * lax.rsqrt(var + LN_EPS) * g + b


def _matmul_kernel(x_ref, w_ref, o_ref):
    o_ref[...] = _dot(x_ref[...], w_ref[...]).astype(o_ref.dtype)


def _in_proj(xb, w):
    T, K = xb.shape
    N = w.shape[1]
    tm = min(1024, T)
    tn = 512
    return pl.pallas_call(
        _matmul_kernel,
        grid=(T // tm, N // tn),
        in_specs=[pl.BlockSpec((tm, K), lambda i, j: (i, 0)),
                  pl.BlockSpec((K, tn), lambda i, j: (0, j))],
        out_specs=pl.BlockSpec((tm, tn), lambda i, j: (i, j)),
        out_shape=jax.ShapeDtypeStruct((T, N), BF16),
        compiler_params=_cparams(("parallel", "arbitrary")),
        name="in_proj",
    )(xb, w)


def _prep_w_in(w):
    offs = [0]
    for s in IN_SIZES:
        offs.append(offs[-1] + s)
    piece = lambda k: w[:, offs[k]:offs[k + 1]]
    d = w.shape[0]
    q = piece(0).reshape(d, NSA_KV_GROUPS, NSA_HPG, NSA_HEAD_DIM) * (NSA_HEAD_DIM ** -0.5)
    q_ext = jnp.zeros((d, NSA_KV_GROUPS, NSA_HPG, HEAD_SLOT), F32)
    for g in range(NSA_KV_GROUPS):
        q_ext = q_ext.at[:, g, :, g * NSA_HEAD_DIM:(g + 1) * NSA_HEAD_DIM].set(q[:, g])
    q_ext = q_ext.reshape(d, NSA_HEADS * HEAD_SLOT)
    small = jnp.concatenate([piece(7), piece(10)], axis=1)
    small = jnp.pad(small, ((0, 0), (0, PROJ_W - C_SMALL - small.shape[1])))
    out = jnp.concatenate([piece(13), q_ext, piece(8), piece(9),
                           piece(1), piece(2), piece(3), piece(4), piece(5), piece(6),
                           piece(11), piece(12), small], axis=1)
    assert out.shape[1] == PROJ_W
    return out.astype(BF16)


def _compress_kernel(rk_ref, rv_ref, pek_ref, pev_ref, w1k_ref, w1v_ref, w2k_ref, w2v_ref,
                     kc_ref, vc_ref):
    half = CMP_STRIDE * NSA_HEAD_DIM
    for r_ref, pe_ref, w1_ref, w2_ref, o_ref in ((rk_ref, pek_ref, w1k_ref, w2k_ref, kc_ref),
                                                 (rv_ref, pev_ref, w1v_ref, w2v_ref, vc_ref)):
        r = r_ref[0, 0]
        nr = r.shape[0]
        u = _dot(r, w1_ref[0:half, :])
        v = _dot(r, w1_ref[half:2 * half, :])
        c = _dot(pe_ref[...], w1_ref[...])[0:1]
        hid = u + pltpu.roll(v, nr - 1, axis=0) + c
        act = _gelu_tanh(hid)
        o_ref[0, 0] = _dot(act.astype(BF16), w2_ref[0]).astype(o_ref.dtype)


def _nsa_compress(rk, rv, pek, pev, w1k, w1v, w2k, w2v):
    B, G, NR, W = rk.shape
    blk_r = pl.BlockSpec((1, 1, NR, W), lambda b, g: (b, g, 0, 0))
    full2 = lambda a: pl.BlockSpec(a.shape, lambda b, g: (0, 0))
    blk_w2 = pl.BlockSpec((1, CMP_HIDDEN, HEAD_SLOT), lambda b, g: (g, 0, 0))
    blk_o = pl.BlockSpec((1, 1, NR, HEAD_SLOT), lambda b, g: (b, g, 0, 0))
    shp = jax.ShapeDtypeStruct((B, G, NR, HEAD_SLOT), BF16)
    return pl.pallas_call(
        _compress_kernel,
        grid=(B, G),
        in_specs=[blk_r, blk_r, full2(pek), full2(pev), full2(w1k), full2(w1v), blk_w2, blk_w2],
        out_specs=[blk_o, blk_o],
        out_shape=[shp, shp],
        compiler_params=_cparams(("parallel", "parallel")),
        name="nsa_compress",
    )(rk, rv, pek, pev, w1k, w1v, w2k, w2v)


def _stack_heads(q):
    return jnp.concatenate([q[:, j * HEAD_SLOT:(j + 1) * HEAD_SLOT] for j in range(NSA_HPG)], axis=0)


def _cmp_attn_kernel(q_ref, kc_ref, vc_ref, ov_ref, ocmp_ref, sel_ref, *, tq):
    i = pl.program_id(2)
    q2 = _stack_heads(q_ref[...])
    kc = kc_ref[0, 0]
    vc = vc_ref[0, 0]
    nc = kc.shape[0]
    s = _dot_nt(q2, kc)
    row = lax.broadcasted_iota(jnp.int32, s.shape, 0)
    n = lax.broadcasted_iota(jnp.int32, s.shape, 1)
    t = i * tq + (row & (tq - 1))
    mask = n * CMP_STRIDE + (CMP_BLOCK - 1) <= t
    s = jnp.where(mask, s, MASK_VALUE)
    m = jnp.max(s, axis=-1, keepdims=True)
    e = jnp.where(mask, jnp.exp(s - m), 0.0)
    den = jnp.maximum(jnp.sum(e, axis=-1, keepdims=True), 1e-30)
    p = e * (1.0 / den)
    o = _dot(p.astype(BF16), vc)
    for j in range(NSA_HPG):
        ocmp_ref[:, j * HEAD_SLOT:(j + 1) * HEAD_SLOT] = o[j * tq:(j + 1) * tq].astype(ocmp_ref.dtype)
    ps = p[0:tq] + p[tq:2 * tq] + p[2 * tq:3 * tq] + p[3 * tq:4 * tq]
    ov = ov_ref[...]
    hi = ps.astype(BF16)
    r1 = ps - hi.astype(F32)
    mid = r1.astype(BF16)
    lo = (r1 - mid.astype(F32)).astype(BF16)
    imp = _dot(hi, ov) + _dot(mid, ov) + _dot(lo, ov)
    blk = lax.broadcasted_iota(jnp.int32, imp.shape, 1)
    tt = i * tq + lax.broadcasted_iota(jnp.int32, imp.shape, 0)
    cur = tt // SEL_BLOCK
    forced = (blk == 0) | (blk == cur) | (blk == cur - 1)
    causal = blk * SEL_BLOCK <= tt
    v0 = jnp.where(forced, FORCE_SCORE, jnp.where(causal, imp, -1.0))

    def pick(_, carry):
        v, sel = carry
        idx = jnp.argmax(v, axis=-1, keepdims=True)
        hit = blk == idx
        return jnp.where(hit, -jnp.inf, v), jnp.where(hit, 1.0, sel)

    _, sel = lax.fori_loop(0, SEL_TOPN, pick, (v0, jnp.zeros_like(v0)))
    sel_ref[0, 0] = sel.astype(sel_ref.dtype)


def _nsa_cmp_attn(proj, kc, vc, ov, B, S):
    T = B * S
    tq = min(256, S)
    nq = S // tq
    G = NSA_KV_GROUPS
    NC = kc.shape[2]
    qw = NSA_HPG * HEAD_SLOT
    qblk = C_QEXT // qw
    kern = functools.partial(_cmp_attn_kernel, tq=tq)
    return pl.pallas_call(
        kern,
        grid=(B, G, nq),
        in_specs=[pl.BlockSpec((tq, qw), lambda b, g, i: (b * nq + i, qblk + g)),
                  pl.BlockSpec((1, 1, NC, HEAD_SLOT), lambda b, g, i: (b, g, 0, 0)),
                  pl.BlockSpec((1, 1, NC, HEAD_SLOT), lambda b, g, i: (b, g, 0, 0)),
                  pl.BlockSpec(ov.shape, lambda b, g, i: (0, 0))],
        out_specs=[pl.BlockSpec((tq, qw), lambda b, g, i: (b * nq + i, g)),
                   pl.BlockSpec((1, 1, tq, SEL_LANES), lambda b, g, i: (b, g, i, 0))],
        out_shape=[jax.ShapeDtypeStruct((T, NSA_HEADS * HEAD_SLOT), BF16),
                   jax.ShapeDtypeStruct((B, G, S, SEL_LANES), BF16)],
        compiler_params=_cparams(("parallel", "parallel", "parallel")),
        name="nsa_cmp_attn",
    )(proj, kc, vc, ov)


SEL_TK = 512
NSA_TQ = 128
WIN_KEYS = WINDOW + NSA_TQ


def _sel_win_kernel(q_ref, ksel_ref, vsel_ref, kwin_ref, vwin_ref, sel_ref, emat_ref, ocmp_ref,
                    gate_ref, o_ref):
    g = pl.program_id(1)
    i = pl.program_id(2)
    tq = NSA_TQ
    rows = NSA_HPG * tq
    t0 = i * tq
    q2 = _stack_heads(q_ref[...])
    selm1 = sel_ref[0, 0] - 1.0

    def sel_step(kv, carry, diagonal):
        m, l, acc = carry
        off = pl.multiple_of(kv * SEL_TK, SEL_TK)
        k = ksel_ref[pl.ds(off, SEL_TK), :]
        v = vsel_ref[pl.ds(off, SEL_TK), :]
        s = _dot_nt(q2, k)
        bias = _dot(selm1, emat_ref[:, pl.ds(off, SEL_TK)])
        if diagonal:
            kpos = off + lax.broadcasted_iota(jnp.int32, bias.shape, 1)
            tq_pos = t0 + lax.broadcasted_iota(jnp.int32, bias.shape, 0)
            bias = jnp.where(kpos <= tq_pos, bias, MASK_VALUE)
        s = s + jnp.concatenate([bias] * NSA_HPG, axis=0)
        m_new = jnp.maximum(m, jnp.max(s, axis=-1, keepdims=True))
        alpha = jnp.exp(m - m_new)
        p = jnp.exp(s - m_new)
        l = alpha * l + jnp.sum(p, axis=-1, keepdims=True)
        acc = alpha * acc + _dot(p.astype(BF16), v)
        return m_new, l, acc

    carry = (jnp.full((rows, 1), MASK_VALUE, F32), jnp.zeros((rows, 1), F32),
             jnp.zeros((rows, HEAD_SLOT), F32))
    kd = t0 // SEL_TK
    carry = lax.fori_loop(0, kd, lambda kv, c: sel_step(kv, c, False), carry)
    _, l, acc = sel_step(kd, carry, True)
    o_slc = acc * (1.0 / jnp.maximum(l, 1e-30))

    start = pl.multiple_of(jnp.maximum(t0 - WINDOW, 0), tq)
    kw = kwin_ref[pl.ds(start, WIN_KEYS), :]
    vw = vwin_ref[pl.ds(start, WIN_KEYS), :]
    sw = _dot_nt(q2, kw)
    kpos = start + lax.broadcasted_iota(jnp.int32, sw.shape, 1)
    tq_pos = t0 + (lax.broadcasted_iota(jnp.int32, sw.shape, 0) & (tq - 1))
    diff = tq_pos - kpos
    wmask = (diff >= 0) & (diff < WINDOW)
    sw = jnp.where(wmask, sw, MASK_VALUE)
    mw = jnp.max(sw, axis=-1, keepdims=True)
    ew = jnp.where(wmask, jnp.exp(sw - mw), 0.0)
    denw = jnp.maximum(jnp.sum(ew, axis=-1, keepdims=True), 1e-30)
    o_win = _dot((ew * (1.0 / denw)).astype(BF16), vw)

    gates = _sigmoid(gate_ref[...].astype(F32))
    lane = lax.broadcasted_iota(jnp.int32, gates.shape, 1)
    ocmp = ocmp_ref[...].astype(F32)
    for j in range(NSA_HPG):
        base = g * (NSA_HPG * 3) + j * 3
        gc = [jnp.sum(jnp.where(lane == base + c, gates, 0.0), axis=-1, keepdims=True) for c in range(3)]
        oj = (gc[0] * ocmp[:, j * HEAD_SLOT:(j + 1) * HEAD_SLOT]
              + gc[1] * o_slc[j * tq:(j + 1) * tq]
              + gc[2] * o_win[j * tq:(j + 1) * tq])
        o_ref[:, j * HEAD_SLOT:(j + 1) * HEAD_SLOT] = oj.astype(o_ref.dtype)


def _nsa_sel_win(proj, sel, emat, ocmp, B, S):
    T = B * S
    tq = NSA_TQ
    nq = S // tq
    G = NSA_KV_GROUPS
    qw = NSA_HPG * HEAD_SLOT
    qblk = C_QEXT // qw
    kvblk = C_KV // LANES
    kv_spec = lambda k: pl.BlockSpec((S, LANES), lambda b, g, i: (b, kvblk + k))
    return pl.pallas_call(
        _sel_win_kernel,
        grid=(B, G, nq),
        in_specs=[pl.BlockSpec((tq, qw), lambda b, g, i: (b * nq + i, qblk + g)),
                  kv_spec(2), kv_spec(3), kv_spec(4), kv_spec(5),
                  pl.BlockSpec((1, 1, tq, SEL_LANES), lambda b, g, i: (b, g, i, 0)),
                  pl.BlockSpec(emat.shape, lambda b, g, i: (0, 0)),
                  pl.BlockSpec((tq, qw), lambda b, g, i: (b * nq + i, g)),
                  pl.BlockSpec((tq, LANES), lambda b, g, i: (b * nq + i, C_SMALL // LANES))],
        out_specs=pl.BlockSpec((tq, qw), lambda b, g, i: (b * nq + i, g)),
        out_shape=jax.ShapeDtypeStruct((T, NSA_HEADS * HEAD_SLOT), BF16),
        compiler_params=_cparams(("parallel", "parallel", "parallel")),
        name="nsa_sel_win",
    )(proj, proj, proj, proj, proj, sel, emat, ocmp, proj)


TAIL_ROWS = 8


def _causal_conv(x, tail_ref, w, b):
    L = x.shape[0]
    xx = jnp.concatenate([tail_ref[...], x], axis=0)
    y = b + w[CONV_WIDTH - 1:CONV_WIDTH] * x
    for k in range(1, CONV_WIDTH):
        y = y + w[CONV_WIDTH - 1 - k:CONV_WIDTH - k] * xx[TAIL_ROWS - k:TAIL_ROWS - k + L]
    tail_ref[...] = x[L - TAIL_ROWS:L]
    return y


def _ssd_kernel(z_ref, xbc_ref, dtc_ref, dtr_ref, cw_ref, cb_ref, dtbc_ref, dtbr_ref, alc_ref, alr_ref,
                dsk_ref, nw_ref, o_ref, state_ref, tail_ref, y_ref):
    c = pl.program_id(1)
    L = SSD_CHUNK
    P = SSD_HEAD_DIM
    N = SSD_STATE

    @pl.when(c == 0)
    def _():
        state_ref[...] = jnp.zeros_like(state_ref)
        tail_ref[...] = jnp.zeros_like(tail_ref)

    conv = _causal_conv(xbc_ref[...].astype(F32), tail_ref, cw_ref[...], cb_ref[...])
    xbc = conv * _sigmoid(conv)
    xs = xbc[:, 0:SSD_INNER]
    bm = xbc[:, SSD_INNER:SSD_INNER + SSD_GROUPS * N]
    cm = xbc[:, SSD_INNER + SSD_GROUPS * N:SSD_INNER + 2 * SSD_GROUPS * N]

    dt_c = _softplus(dtc_ref[...] + dtbc_ref[...])
    dt_r = _softplus(dtr_ref[...] + dtbr_ref[...])
    a_c = dt_c * (-jnp.exp(alc_ref[...]))
    a_r = dt_r * (-jnp.exp(alr_ref[...]))
    ii = lax.broadcasted_iota(jnp.int32, (L, L), 0)
    jj = lax.broadcasted_iota(jnp.int32, (L, L), 1)
    tri = ii >= jj
    acum_c = _dot_f32(tri.astype(F32), a_c)
    acum_r = _dot_f32(a_r, (ii <= jj).astype(F32))

    for g in range(SSD_GROUPS):
        bg = bm[:, g * N:(g + 1) * N]
        cg = cm[:, g * N:(g + 1) * N]
        cgb = cg.astype(BF16)
        cb = _dot_nt(cgb, bg.astype(BF16))
        for j in range(SSD_HEADS // SSD_GROUPS):
            h = g * (SSD_HEADS // SSD_GROUPS) + j
            ac = acum_c[:, h:h + 1]
            ar = acum_r[h:h + 1, :]
            a_last = acum_c[L - 1:L, h:h + 1]
            xh_raw = xs[:, h * P:(h + 1) * P]
            xh = (xh_raw * dt_c[:, h:h + 1]).astype(BF16)
            lmat = jnp.where(tri, jnp.exp(ac - ar), 0.0)
            y = _dot((cb * lmat).astype(BF16), xh)
            prev = state_ref[h]
            y = y + _dot(cgb, prev.astype(BF16)) * jnp.exp(ac)
            bd = (bg * jnp.exp(a_last - ac)).astype(BF16)
            state_ref[h] = jnp.exp(a_last) * prev + _dot_tn(bd, xh)
            y_ref[:, h * P:(h + 1) * P] = y + dsk_ref[:, h * P:(h + 1) * P] * xh_raw

    zf = z_ref[...].astype(F32)
    yg = y_ref[...] * (zf * _sigmoid(zf))
    ms = jnp.mean(yg * yg, axis=-1, keepdims=True)
    o_ref[...] = (yg * lax.rsqrt(ms + RMS_EPS) * nw_ref[...]).astype(o_ref.dtype)


def _ssd_mixer(proj, dt_col, dt_row, conv_w, conv_b, dt_bias, a_log, d_skip, norm_w, B, S):
    T = B * S
    L = SSD_CHUNK
    nc = S // L
    H = SSD_HEADS
    full = lambda a: pl.BlockSpec(a.shape, lambda b, c: (0, 0))
    cb2 = conv_b.reshape(1, -1)
    dtb_c = dt_bias.reshape(1, H)
    dtb_r = dt_bias.reshape(H, 1)
    al_c = a_log.reshape(1, H)
    al_r = a_log.reshape(H, 1)
    dsk = jnp.repeat(d_skip, SSD_HEAD_DIM).reshape(1, SSD_INNER)
    nw = norm_w.reshape(1, SSD_INNER)
    return pl.pallas_call(
        _ssd_kernel,
        grid=(B, nc),
        in_specs=[pl.BlockSpec((L, SSD_INNER), lambda b, c: (b * nc + c, C_SSDZ // SSD_INNER)),
                  pl.BlockSpec((L, SSD_XBC_W), lambda b, c: (b * nc + c, C_XBC // SSD_XBC_W)),
                  pl.BlockSpec((L, H), lambda b, c: (b * nc + c, 0)),
                  pl.BlockSpec((H, L), lambda b, c: (0, b * nc + c)),
                  full(conv_w), full(cb2), full(dtb_c), full(dtb_r), full(al_c), full(al_r),
                  full(dsk), full(nw)],
        out_specs=pl.BlockSpec((L, SSD_INNER), lambda b, c: (b * nc + c, 0)),
        out_shape=jax.ShapeDtypeStruct((T, SSD_INNER), BF16),
        scratch_shapes=[pltpu.VMEM((H, SSD_STATE, SSD_HEAD_DIM), F32),
                        pltpu.VMEM((TAIL_ROWS, SSD_XBC_W), F32),
                        pltpu.VMEM((L, SSD_INNER), F32)],
        compiler_params=_cparams(("parallel", "arbitrary")),
        name="ssd_mixer",
    )(proj, proj, dt_col, dt_row, conv_w, cb2, dtb_c, dtb_r, al_c, al_r, dsk, nw)


def _lru_kernel(x_ref, y_ref, cw_ref, cb_ref, wa_ref, ba_ref, wx_ref, bx_ref, lam_ref, o_ref,
                h_ref, tail_ref, *, tc):
    c = pl.program_id(1)

    @pl.when(c == 0)
    def _():
        h_ref[...] = jnp.zeros_like(h_ref)
        tail_ref[...] = jnp.zeros_like(tail_ref)

    xr = _causal_conv(x_ref[...].astype(F32), tail_ref, cw_ref[...], cb_ref[...])
    xrb = xr.astype(BF16)
    r = _sigmoid(_dot(xrb, wa_ref[...]) + ba_ref[...])
    ig = _sigmoid(_dot(xrb, wx_ref[...]) + bx_ref[...])
    log_a = -LRU_C * r * _softplus(-lam_ref[...])
    a = jnp.exp(log_a)
    b = jnp.sqrt(1.0 - jnp.exp(2.0 * log_a)) * (ig * xr)
    row = lax.broadcasted_iota(jnp.int32, a.shape, 0)
    k = 1
    while k < tc:
        keep = row >= k
        a_s = jnp.where(keep, pltpu.roll(a, k, axis=0), 1.0)
        b_s = jnp.where(keep, pltpu.roll(b, k, axis=0), 0.0)
        b = a * b_s + b
        a = a * a_s
        k *= 2
    h = a * h_ref[0:1, :] + b
    h_ref[...] = jnp.broadcast_to(h[tc - 1:tc, :], h_ref.shape)
    o_ref[...] = (h * _gelu_tanh(y_ref[...].astype(F32))).astype(o_ref.dtype)


def _block_diag(w):
    nb, c, d = w.shape
    eye = jnp.eye(nb, dtype=w.dtype)
    return (eye[:, None, :, None] * w[:, :, None, :]).reshape(nb * c, nb * d)


def _lru_mixer(proj, conv_w, conv_b, wa, ba, wx, bx, lam, B, S):
    T = B * S
    tc = min(256, S)
    nt = S // tc
    W = LRU_WIDTH
    wa_bd = _block_diag(wa).astype(BF16)
    wx_bd = _block_diag(wx).astype(BF16)
    row = lambda v: v.reshape(1, W)
    full = lambda a: pl.BlockSpec(a.shape, lambda b, c: (0, 0))
    args = (conv_w, row(conv_b), wa_bd, row(ba), wx_bd, row(bx), row(lam))
    return pl.pallas_call(
        functools.partial(_lru_kernel, tc=tc),
        grid=(B, nt),
        in_specs=[pl.BlockSpec((tc, W), lambda b, c: (b * nt + c, C_LRUX // W)),
                  pl.BlockSpec((tc, W), lambda b, c: (b * nt + c, C_LRUY // W))]
                 + [full(a) for a in args],
        out_specs=pl.BlockSpec((tc, W), lambda b, c: (b * nt + c, 0)),
        out_shape=jax.ShapeDtypeStruct((T, W), BF16),
        scratch_shapes=[pltpu.VMEM((TAIL_ROWS, W), F32), pltpu.VMEM((TAIL_ROWS, W), F32)],
        compiler_params=_cparams(("parallel", "arbitrary")),
        name="lru_mixer",
    )(proj, proj, *args)


def _merge_kernel(x_ref, mg_ref, on_ref, os_ref, ol_ref, pn_ref, ps_ref, pl_ref, wo_ref, g_ref, b_ref,
                  of_ref, ob_ref):
    d = D_MODEL
    gate = _sigmoid(mg_ref[...].astype(F32))
    mixed = (gate[:, 0:d] * _dot(on_ref[...], pn_ref[...])
             + gate[:, d:2 * d] * _dot(os_ref[...], ps_ref[...])
             + gate[:, 2 * d:3 * d] * _dot(ol_ref[...], pl_ref[...]))
    v = ALPHA * x_ref[...] + _dot(mixed.astype(BF16), wo_ref[...])
    out = _layer_norm(v, g_ref[...], b_ref[...])
    of_ref[...] = out
    ob_ref[...] = out.astype(BF16)


def _merge(x, proj, o_nsa, o_ssd, o_lru, pn, ps, plru, wo, g, b):
    T = x.shape[0]
    tm = min(512, T)
    d = D_MODEL
    rowblk = lambda w: pl.BlockSpec((tm, w), lambda i: (i, 0))
    full = lambda a: pl.BlockSpec(a.shape, lambda i: (0, 0))
    g2, b2 = g.reshape(1, d), b.reshape(1, d)
    return pl.pallas_call(
        _merge_kernel,
        grid=(T // tm,),
        in_specs=[rowblk(d), pl.BlockSpec((tm, 3 * d), lambda i: (i, C_MERGE // (3 * d))),
                  rowblk(o_nsa.shape[1]), rowblk(o_ssd.shape[1]), rowblk(o_lru.shape[1]),
                  full(pn), full(ps), full(plru), full(wo), full(g2), full(b2)],
        out_specs=[rowblk(d), rowblk(d)],
        out_shape=[jax.ShapeDtypeStruct((T, d), F32), jax.ShapeDtypeStruct((T, d), BF16)],
        compiler_params=_cparams(("parallel",)),
        name="merge_ln",
    )(x, proj, o_nsa, o_ssd, o_lru, pn, ps, plru, wo, g2, b2)


def _route(sel, aff):
    epg = EXPERTS_PER_GROUP
    scores = []
    for gi in range(N_EXPERT_GROUPS):
        v = sel[gi * epg:(gi + 1) * epg]
        pair = None
        for a in range(epg):
            for b in range(a + 1, epg):
                sab = v[a] + v[b]
                pair = sab if pair is None else jnp.maximum(pair, sab)
        scores.append(pair)
    best = jnp.zeros_like(scores[0], dtype=jnp.int32)
    best_s = scores[0]
    for gi in range(1, N_EXPERT_GROUPS):
        better = scores[gi] > best_s
        best = jnp.where(better, gi, best)
        best_s = jnp.where(better, scores[gi], best_s)
    chosen = []
    for k in range(N_EXPERTS):
        gi = k // epg
        rank = jnp.zeros_like(best)
        for o in range(gi * epg, (gi + 1) * epg):
            if o == k:
                continue
            ahead = (sel[o] > sel[k]) | ((sel[o] == sel[k]) & (o < k))
            rank = rank + ahead.astype(jnp.int32)
        chosen.append((best == gi) & (rank < TOP_K))
    wsum = None
    for k in range(N_EXPERTS):
        wk = jnp.where(chosen[k], aff[k], 0.0)
        wsum = wk if wsum is None else wsum + wk
    inv = 1.0 / wsum
    return [jnp.where(chosen[k], aff[k], 0.0) * inv for k in range(N_EXPERTS)]


def _moe_kernel(xb_ref, xf_ref, p_ref, rw_ref, rb_ref, pg_ref, pp_ref, wg_ref, wu_ref, wd_ref,
                g_ref, b_ref, of_ref, ob_ref, acc_ref, gates_ref):
    e = pl.program_id(1)
    xb = xb_ref[...]
    tm = xb.shape[0]

    @pl.when(e == 0)
    def _():
        logits = _dot_nt(rw_ref[...], xb)
        aff = _sigmoid(logits)
        sel = aff + rb_ref[...]
        gate_rows = _route([sel[k:k + 1, :] for k in range(N_EXPERTS)],
                           [aff[k:k + 1, :] for k in range(N_EXPERTS)])
        gt = jnp.concatenate(gate_rows + [jnp.zeros((LANES - N_EXPERTS, tm), F32)], axis=0)
        gates_ref[...] = gt.T
        ple = _sigmoid(_dot(xb, pg_ref[...])) * _dot(p_ref[...].astype(BF16), pp_ref[...])
        acc_ref[...] = ple

    gates = gates_ref[...]
    lane = lax.broadcasted_iota(jnp.int32, gates.shape, 1)
    ge = jnp.sum(jnp.where(lane == e, gates, 0.0), axis=-1, keepdims=True)
    hg = _dot(xb, wg_ref[0])
    hu = _dot(xb, wu_ref[0])
    h = (hg * _sigmoid(hg)) * hu * ge
    acc_ref[...] += _dot(h.astype(BF16), wd_ref[0])

    @pl.when(e == N_EXPERTS - 1)
    def _():
        out = _layer_norm(ALPHA * xf_ref[...] + acc_ref[...], g_ref[...], b_ref[...])
        of_ref[...] = out
        ob_ref[...] = out.astype(BF16)


def _moe_ple(xb, xf, p, rw_t, rb, pg, pp, wg, wu, wd, g, b):
    T = xb.shape[0]
    tm = min(512, T)
    d = D_MODEL
    rowblk = lambda w: pl.BlockSpec((tm, w), lambda i, e: (i, 0))
    full = lambda a: pl.BlockSpec(a.shape, lambda i, e: (0, 0))
    g2, b2 = g.reshape(1, d), b.reshape(1, d)
    return pl.pallas_call(
        _moe_kernel,
        grid=(T // tm, N_EXPERTS),
        in_specs=[rowblk(d), rowblk(d), rowblk(PLE_DIM), full(rw_t), full(rb), full(pg), full(pp),
                  pl.BlockSpec((1, d, D_EXPERT), lambda i, e: (e, 0, 0)),
                  pl.BlockSpec((1, d, D_EXPERT), lambda i, e: (e, 0, 0)),
                  pl.BlockSpec((1, D_EXPERT, d), lambda i, e: (e, 0, 0)),
                  full(g2), full(b2)],
        out_specs=[rowblk(d), rowblk(d)],
        out_shape=[jax.ShapeDtypeStruct((T, d), F32), jax.ShapeDtypeStruct((T, d), BF16)],
        scratch_shapes=[pltpu.VMEM((tm, d), F32), pltpu.VMEM((tm, LANES), F32)],
        compiler_params=_cparams(("parallel", "arbitrary")),
        name="moe_ple_ln",
    )(xb, xf, p, rw_t, rb, pg, pp, wg, wu, wd, g2, b2)


def _overlap_matrix(nc):
    n = jnp.arange(nc)[:, None]
    m = jnp.arange(SEL_LANES)[None, :]
    ratio = SEL_BLOCK // CMP_STRIDE
    ov = jnp.zeros((nc, SEL_LANES), F32)
    for k in range(CMP_BLOCK // CMP_STRIDE):
        ov = ov + ((n + k) // ratio == m).astype(F32)
    return ov.astype(BF16)


def _expand_matrix(S):
    m = jnp.arange(SEL_LANES)[:, None]
    c = jnp.arange(S)[None, :]
    return jnp.where(c // SEL_BLOCK == m, -MASK_VALUE, 0.0).astype(BF16)


def _pad_w2(w2):
    out = jnp.zeros((NSA_KV_GROUPS, CMP_HIDDEN, HEAD_SLOT), F32)
    for g in range(NSA_KV_GROUPS):
        out = out.at[g, :, g * NSA_HEAD_DIM:(g + 1) * NSA_HEAD_DIM].set(w2)
    return out.astype(BF16)


def _pad_proj_nsa(w):
    d = w.shape[1]
    w4 = w.reshape(NSA_KV_GROUPS, NSA_HPG, NSA_HEAD_DIM, d)
    out = jnp.zeros((NSA_KV_GROUPS, NSA_HPG, HEAD_SLOT, d), F32)
    for g in range(NSA_KV_GROUPS):
        out = out.at[g, :, g * NSA_HEAD_DIM:(g + 1) * NSA_HEAD_DIM].set(w4[g])
    return out.reshape(NSA_HEADS * HEAD_SLOT, d).astype(BF16)


def _cmp_rows(kv, B, S):
    nr = S // CMP_STRIDE
    kv = kv.reshape(B, nr, CMP_STRIDE, NSA_KV_GROUPS, NSA_HEAD_DIM)
    return kv.transpose(0, 3, 1, 2, 4).reshape(B, NSA_KV_GROUPS, nr, CMP_STRIDE * NSA_HEAD_DIM)


def kernel(x, p, w_in, nsa_pe_k, nsa_w1_k, nsa_w2_k, nsa_pe_v, nsa_w1_v, nsa_w2_v, ssd_conv_w, ssd_conv_b, ssd_dt_bias, ssd_a_log, ssd_d, ssd_norm_w, lru_conv_w, lru_conv_b, lru_wa, lru_ba, lru_wx, lru_bx, lru_lambda, proj_nsa, proj_ssd, proj_lru, w_out, ln1_g, ln1_b, router_w, router_b, exp_w_gate, exp_w_up, exp_w_down, ple_w_gate, ple_w_proj, ln2_g, ln2_b):
    B, S, d = x.shape
    T = B * S
    depth = w_in.shape[0]
    assert d == D_MODEL and S % SEL_TK == 0 and S >= WIN_KEYS and S // SEL_BLOCK <= SEL_LANES
    nr = S // CMP_STRIDE
    ov = _overlap_matrix(nr)
    emat = _expand_matrix(S)
    rw_t = router_w.T.astype(BF16)
    rb = router_b.reshape(N_EXPERTS, 1).astype(F32)

    xf = x.reshape(T, d)
    xb = xf.astype(BF16)
    for i in range(depth):
        proj = _in_proj(xb, _prep_w_in(w_in[i]))

        kv0 = C_KV
        k_cmp = proj[:, kv0:kv0 + NSA_KV_W]
        v_cmp = proj[:, kv0 + NSA_KV_W:kv0 + 2 * NSA_KV_W]
        pe_rows = lambda pe: jnp.broadcast_to(pe.reshape(1, -1), (8, CMP_BLOCK * NSA_HEAD_DIM)).astype(BF16)
        kc, vc = _nsa_compress(_cmp_rows(k_cmp, B, S), _cmp_rows(v_cmp, B, S),
                               pe_rows(nsa_pe_k[i]), pe_rows(nsa_pe_v[i]),
                               nsa_w1_k[i].astype(BF16), nsa_w1_v[i].astype(BF16),
                               _pad_w2(nsa_w2_k[i]), _pad_w2(nsa_w2_v[i]))
        ocmp, sel = _nsa_cmp_attn(proj, kc, vc, ov, B, S)
        o_nsa = _nsa_sel_win(proj, sel, emat, ocmp, B, S)

        dt_col = proj[:, C_SMALL + GATE_W:C_SMALL + GATE_W + SSD_HEADS].astype(F32)
        o_ssd = _ssd_mixer(proj, dt_col, dt_col.T, ssd_conv_w[i], ssd_conv_b[i], ssd_dt_bias[i],
                           ssd_a_log[i], ssd_d[i], ssd_norm_w[i], B, S)
        o_lru = _lru_mixer(proj, lru_conv_w[i], lru_conv_b[i], lru_wa[i], lru_ba[i], lru_wx[i],
                           lru_bx[i], lru_lambda[i], B, S)

        xf, xb = _merge(xf, proj, o_nsa, o_ssd, o_lru, _pad_proj_nsa(proj_nsa[i]),
                        proj_ssd[i].astype(BF16), proj_lru[i].astype(BF16), w_out[i].astype(BF16),
                        ln1_g[i], ln1_b[i])
        xf, xb = _moe_ple(xb, xf, p[i].reshape(T, PLE_DIM), rw_t, rb,
                          ple_w_gate[i].astype(BF16), ple_w_proj[i].astype(BF16),
                          exp_w_gate[i].astype(BF16), exp_w_up[i].astype(BF16),
                          exp_w_down[i].astype(BF16), ln2_g[i], ln2_b[i])
    return xf.reshape(B, S, d)
```

```python
import functools
import math

import jax
import jax.numpy as jnp
from jax import lax
from jax.experimental import pallas as pl
from jax.experimental.pallas import tpu as pltpu

F32 = jnp.float32
BF16 = jnp.bfloat16

D_MODEL = 1024
PLE_DIM = 256
NSA_HEADS = 8
NSA_KV_GROUPS = 2
NSA_HEAD_DIM = 64
NSA_HPG = NSA_HEADS // NSA_KV_GROUPS
NSA_Q_W = NSA_HEADS * NSA_HEAD_DIM
NSA_KV_W = NSA_KV_GROUPS * NSA_HEAD_DIM
CMP_BLOCK = 32
CMP_STRIDE = 16
CMP_HIDDEN = 256
SEL_BLOCK = 64
SEL_TOPN = 16
WINDOW = 512
FORCE_SCORE = 1e4
MASK_VALUE = -1e30
LOG2E = 1.4426950408889634
SSD_HEADS = 8
SSD_HEAD_DIM = 64
SSD_INNER = SSD_HEADS * SSD_HEAD_DIM
SSD_GROUPS = 2
SSD_STATE = 64
SSD_CHUNK = 128
SSD_XBC_W = SSD_INNER + 2 * SSD_GROUPS * SSD_STATE
CONV_WIDTH = 4
LRU_WIDTH = 512
LRU_BLOCKS = 8
LRU_BLOCK_DIM = LRU_WIDTH // LRU_BLOCKS
LRU_C = 8.0
N_EXPERTS = 16
N_EXPERT_GROUPS = 4
EXPERTS_PER_GROUP = N_EXPERTS // N_EXPERT_GROUPS
TOP_K = 2
D_EXPERT = 256
DEPTH = 2
ALPHA = (2 * DEPTH) ** 0.25
LN_EPS = 1e-5
RMS_EPS = 1e-5
IN_SIZES = (NSA_Q_W, NSA_KV_W, NSA_KV_W, NSA_KV_W, NSA_KV_W, NSA_KV_W, NSA_KV_W, NSA_HEADS * 3,
            SSD_INNER, SSD_XBC_W, SSD_HEADS, LRU_WIDTH, LRU_WIDTH, 3 * D_MODEL)

LANES = 128
SEL_LANES = 128
HEAD_SLOT = 128

C_MERGE = 0
C_QEXT = 3072
C_SSDZ = 4096
C_XBC = 4608
C_KV = 5376
C_LRUX = 6144
C_LRUY = 6656
C_SMALL = 7168
PROJ_W = 7680
GATE_W = NSA_HEADS * 3

VMEM_LIMIT = 56 * 1024 * 1024


def _cparams(sem):
    return pltpu.CompilerParams(dimension_semantics=sem, vmem_limit_bytes=VMEM_LIMIT)


def _sigmoid(x):
    return 1.0 / (1.0 + jnp.exp(-x))


def _softplus(x):
    return jnp.maximum(x, 0.0) + jnp.log(1.0 + jnp.exp(-jnp.abs(x)))


def _gelu_tanh(x):
    c = math.sqrt(2.0 / math.pi)
    return 0.5 * x * (1.0 + jnp.tanh(c * (x + 0.044715 * (x * x * x))))


def _dot(a, b):
    return jnp.dot(a, b, preferred_element_type=F32)


def _dot_nt(a, b):
    return lax.dot_general(a, b, (((1,), (1,)), ((), ())), preferred_element_type=F32)


def _dot_tn(a, b):
    return lax.dot_general(a, b, (((0,), (0,)), ((), ())), preferred_element_type=F32)


def _dot_f32(a, b):
    return jnp.dot(a, b, preferred_element_type=F32, precision=lax.Precision.HIGHEST)


def _layer_norm(v, g, b):
    mu = jnp.mean(v, axis=-1, keepdims=True)
    vc = v - mu
    var = jnp.mean(vc * vc, axis=-1, keepdims=True)
    return vc * lax.rsqrt(var + LN_EPS) * g + b


def _matmul_kernel(x_ref, w_ref, o_ref):
    o_ref[...] = _dot(x_ref[...], w_ref[...]).astype(o_ref.dtype)


def _in_proj(xb, w):
    T, K = xb.shape
    N = w.shape[1]
    tm = min(1024, T)
    tn = 512
    return pl.pallas_call(
        _matmul_kernel,
        grid=(T // tm, N // tn),
        in_specs=[pl.BlockSpec((tm, K), lambda i, j: (i, 0)),
                  pl.BlockSpec((K, tn), lambda i, j: (0, j))],
        out_specs=pl.BlockSpec((tm, tn), lambda i, j: (i, j)),
        out_shape=jax.ShapeDtypeStruct((T, N), BF16),
        compiler_params=_cparams(("parallel", "arbitrary")),
        name="in_proj",
    )(xb, w)


def _prep_w_in(w):
    offs = [0]
    for s in IN_SIZES:
        offs.append(offs[-1] + s)
    piece = lambda k: w[:, offs[k]:offs[k + 1]]
    d = w.shape[0]
    q = piece(0).reshape(d, NSA_KV_GROUPS, NSA_HPG, NSA_HEAD_DIM) * (NSA_HEAD_DIM ** -0.5 * LOG2E)
    q_ext = jnp.zeros((d, NSA_KV_GROUPS, NSA_HPG, HEAD_SLOT), F32)
    for g in range(NSA_KV_GROUPS):
        q_ext = q_ext.at[:, g, :, g * NSA_HEAD_DIM:(g + 1) * NSA_HEAD_DIM].set(q[:, g])
    q_ext = q_ext.reshape(d, NSA_HEADS * HEAD_SLOT)
    small = jnp.concatenate([piece(7), piece(10)], axis=1)
    small = jnp.pad(small, ((0, 0), (0, PROJ_W - C_SMALL - small.shape[1])))
    out = jnp.concatenate([piece(13), q_ext, piece(8), piece(9),
                           piece(1), piece(2), piece(3), piece(4), piece(5), piece(6),
                           piece(11), piece(12), small], axis=1)
    assert out.shape[1] == PROJ_W
    return out.astype(BF16)


def _compress_kernel(rk_ref, rv_ref, pek_ref, pev_ref, w1k_ref, w1v_ref, w2k_ref, w2v_ref,
                     kc_ref, vc_ref):
    half = CMP_STRIDE * NSA_HEAD_DIM
    for r_ref, pe_ref, w1_ref, w2_ref, o_ref in ((rk_ref, pek_ref, w1k_ref, w2k_ref, kc_ref),
                                                 (rv_ref, pev_ref, w1v_ref, w2v_ref, vc_ref)):
        r = r_ref[0, 0]
        nr = r.shape[0]
        u = _dot(r, w1_ref[0:half, :])
        v = _dot(r, w1_ref[half:2 * half, :])
        c = _dot(pe_ref[...], w1_ref[...])[0:1]
        hid = u + pltpu.roll(v, nr - 1, axis=0) + c
        act = _gelu_tanh(hid)
        o_ref[0, 0] = _dot(act.astype(BF16), w2_ref[0]).astype(o_ref.dtype)


def _nsa_compress(rk, rv, pek, pev, w1k, w1v, w2k, w2v):
    B, G, NR, W = rk.shape
    blk_r = pl.BlockSpec((1, 1, NR, W), lambda b, g: (b, g, 0, 0))
    full2 = lambda a: pl.BlockSpec(a.shape, lambda b, g: (0, 0))
    blk_w2 = pl.BlockSpec((1, CMP_HIDDEN, HEAD_SLOT), lambda b, g: (g, 0, 0))
    blk_o = pl.BlockSpec((1, 1, NR, HEAD_SLOT), lambda b, g: (b, g, 0, 0))
    shp = jax.ShapeDtypeStruct((B, G, NR, HEAD_SLOT), BF16)
    return pl.pallas_call(
        _compress_kernel,
        grid=(B, G),
        in_specs=[blk_r, blk_r, full2(pek), full2(pev), full2(w1k), full2(w1v), blk_w2, blk_w2],
        out_specs=[blk_o, blk_o],
        out_shape=[shp, shp],
        compiler_params=_cparams(("parallel", "parallel")),
        name="nsa_compress",
    )(rk, rv, pek, pev, w1k, w1v, w2k, w2v)


def _stack_heads(q):
    return jnp.concatenate([q[:, j * HEAD_SLOT:(j + 1) * HEAD_SLOT] for j in range(NSA_HPG)], axis=0)


def _cmp_attn_kernel(q_ref, kc_ref, vc_ref, ov_ref, gate_ref, ocmp_ref, sel_ref, *, tq):
    g = pl.program_id(1)
    i = pl.program_id(2)
    q2 = _stack_heads(q_ref[...])
    kc = kc_ref[0, 0]
    vc = vc_ref[0, 0]
    nc = kc.shape[0]
    s = _dot_nt(q2, kc)
    row = lax.broadcasted_iota(jnp.int32, s.shape, 0)
    n = lax.broadcasted_iota(jnp.int32, s.shape, 1)
    t = i * tq + (row & (tq - 1))
    mask = n * CMP_STRIDE + (CMP_BLOCK - 1) <= t
    s = jnp.where(mask, s, MASK_VALUE)
    m = jnp.max(s, axis=-1, keepdims=True)
    e = jnp.where(mask, jnp.exp2(s - m), 0.0)
    den = jnp.maximum(jnp.sum(e, axis=-1, keepdims=True), 1e-30)
    p = e * (1.0 / den)
    o = _dot(p.astype(BF16), vc)
    gates = _sigmoid(gate_ref[...].astype(F32))
    glane = lax.broadcasted_iota(jnp.int32, gates.shape, 1)
    for j in range(NSA_HPG):
        gj = jnp.sum(jnp.where(glane == g * (NSA_HPG * 3) + j * 3, gates, 0.0), axis=-1, keepdims=True)
        ocmp_ref[:, j * HEAD_SLOT:(j + 1) * HEAD_SLOT] = (gj * o[j * tq:(j + 1) * tq]).astype(ocmp_ref.dtype)
    ps = p[0:tq] + p[tq:2 * tq] + p[2 * tq:3 * tq] + p[3 * tq:4 * tq]
    ov = ov_ref[...]
    hi = ps.astype(BF16)
    r1 = ps - hi.astype(F32)
    mid = r1.astype(BF16)
    lo = (r1 - mid.astype(F32)).astype(BF16)
    imp = _dot(hi, ov) + _dot(mid, ov) + _dot(lo, ov)
    blk = lax.broadcasted_iota(jnp.int32, imp.shape, 1)
    tt = i * tq + lax.broadcasted_iota(jnp.int32, imp.shape, 0)
    cur = tt // SEL_BLOCK
    forced = (blk == 0) | (blk == cur) | (blk == cur - 1)
    causal = blk * SEL_BLOCK <= tt
    v0 = jnp.where(forced, FORCE_SCORE, jnp.where(causal, imp, -1.0))

    def pick(_, carry):
        v, sel = carry
        idx = jnp.argmax(v, axis=-1, keepdims=True)
        hit = blk == idx
        return jnp.where(hit, -jnp.inf, v), jnp.where(hit, 1.0, sel)

    _, sel = lax.fori_loop(0, SEL_TOPN, pick, (v0, jnp.zeros_like(v0)))
    sel_ref[0, 0] = sel.astype(sel_ref.dtype)


def _nsa_cmp_attn(proj, kc, vc, ov, B, S):
    T = B * S
    tq = min(256, S)
    nq = S // tq
    G = NSA_KV_GROUPS
    NC = kc.shape[2]
    qw = NSA_HPG * HEAD_SLOT
    qblk = C_QEXT // qw
    kern = functools.partial(_cmp_attn_kernel, tq=tq)
    return pl.pallas_call(
        kern,
        grid=(B, G, nq),
        in_specs=[pl.BlockSpec((tq, qw), lambda b, g, i: (b * nq + i, qblk + g)),
                  pl.BlockSpec((1, 1, NC, HEAD_SLOT), lambda b, g, i: (b, g, 0, 0)),
                  pl.BlockSpec((1, 1, NC, HEAD_SLOT), lambda b, g, i: (b, g, 0, 0)),
                  pl.BlockSpec(ov.shape, lambda b, g, i: (0, 0)),
                  pl.BlockSpec((tq, LANES), lambda b, g, i: (b * nq + i, C_SMALL // LANES))],
        out_specs=[pl.BlockSpec((tq, qw), lambda b, g, i: (b * nq + i, g)),
                   pl.BlockSpec((1, 1, tq, SEL_LANES), lambda b, g, i: (b, g, i, 0))],
        out_shape=[jax.ShapeDtypeStruct((T, NSA_HEADS * HEAD_SLOT), BF16),
                   jax.ShapeDtypeStruct((B, G, S, SEL_LANES), BF16)],
        compiler_params=_cparams(("parallel", "parallel", "parallel")),
        name="nsa_cmp_attn",
    )(proj, kc, vc, ov, proj)


SEL_TK = 512
NSA_TQ = 256
WIN_KEYS = WINDOW + NSA_TQ


VT_ROWS = LANES + 16
HALF = 256
NCH = NSA_HPG * NSA_TQ // HALF


def _sel_win_kernel(q_ref, ksel_ref, vselt_ref, kwin_ref, vwint_ref, sel_ref, et_ref, ocmp_ref,
                    gate_ref, dbias_ref, wbias_ref, o_ref, s_ref, m_ref, acc_ref, gt_ref):
    g = pl.program_id(1)
    i = pl.program_id(2)
    tq = NSA_TQ
    t0 = i * tq
    q2 = _stack_heads(q_ref[...])
    selm1 = sel_ref[0, 0] - 1.0
    qx = jnp.concatenate([q2, jnp.concatenate([selm1] * NSA_HPG, axis=0)], axis=1)
    qxh = [qx[c * HALF:(c + 1) * HALF] for c in range(NCH)]
    q2h = [q2[c * HALF:(c + 1) * HALF] for c in range(NCH)]
    kpb = SEL_TK // LANES

    wblk = jnp.maximum((t0 - WINDOW) // LANES, 0)
    start = pl.multiple_of(wblk * LANES, LANES)
    kw = kwin_ref[pl.ds(start, WIN_KEYS), :]
    vwt = jnp.concatenate([vwint_ref[0, wblk + u] for u in range(WIN_KEYS // LANES)], axis=1)
    wbias = wbias_ref[jnp.minimum(i, WINDOW // tq)].astype(F32)
    sws = [_dot_nt(kw, q2h[c]) for c in range(NCH)]

    def scores_into(kv):
        off = pl.multiple_of(kv * SEL_TK, SEL_TK)
        kx = jnp.concatenate([ksel_ref[pl.ds(off, SEL_TK), :], et_ref[pl.ds(off, SEL_TK), :]], axis=1)
        for c in range(NCH):
            s_ref[:, c * HALF:(c + 1) * HALF] = _dot_nt(kx, qxh[c])

    def update(vt, s_chunks):
        for c in range(NCH):
            cols = slice(c * HALF, (c + 1) * HALF)
            s = s_chunks[c]
            m_old = m_ref[:, cols]
            m_new = jnp.maximum(m_old, jnp.max(s, axis=0, keepdims=True))
            p = jnp.exp2(s - m_new).astype(BF16)
            acc_ref[:, cols] = jnp.exp2(m_old - m_new) * acc_ref[:, cols] + _dot(vt, p)
            m_ref[:, cols] = m_new

    def sel_values(kv):
        return jnp.concatenate([vselt_ref[0, kv * kpb + u] for u in range(kpb)], axis=1)

    def load_scores():
        return [s_ref[:, c * HALF:(c + 1) * HALF] for c in range(NCH)]

    m_ref[...] = jnp.full(m_ref.shape, MASK_VALUE, F32)
    acc_ref[...] = jnp.zeros(acc_ref.shape, F32)
    kd = t0 // SEL_TK
    scores_into(0)
    ot_win = []
    for c in range(NCH):
        sw = sws[c] + wbias
        mw = jnp.max(sw, axis=0, keepdims=True)
        accw = _dot(vwt, jnp.exp2(sw - mw).astype(BF16))
        ot_win.append(accw[0:LANES] * (1.0 / jnp.maximum(accw[LANES:LANES + 1], 1e-30)))

    def body(kv, carry):
        s_chunks = load_scores()
        scores_into(kv + 1)
        update(sel_values(kv), s_chunks)
        return carry

    lax.fori_loop(0, kd, body, 0)
    dbias = dbias_ref[i % (SEL_TK // tq)].astype(F32)
    update(sel_values(kd), [sc + dbias for sc in load_scores()])
    acc = acc_ref[...]
    ot_slc = acc[0:LANES] * (1.0 / jnp.maximum(acc[LANES:LANES + 1], 1e-30))

    gt_ref[...] = _sigmoid(gate_ref[...].astype(F32)).T
    ocmp = ocmp_ref[...].astype(F32)
    for j in range(NSA_HPG):
        base = g * (NSA_HPG * 3) + j * 3
        ot = (gt_ref[pl.ds(base + 1, 1), :] * ot_slc[:, j * tq:(j + 1) * tq]
              + gt_ref[pl.ds(base + 2, 1), :] * ot_win[j])
        oj = ot.T + ocmp[:, j * HEAD_SLOT:(j + 1) * HEAD_SLOT]
        o_ref[:, j * HEAD_SLOT:(j + 1) * HEAD_SLOT] = oj.astype(o_ref.dtype)


def _values_t(v, B, S):
    vt = v.reshape(B, S // LANES, LANES, LANES).transpose(0, 1, 3, 2)
    ones = jnp.ones((B, S // LANES, VT_ROWS - LANES, LANES), v.dtype)
    return jnp.concatenate([vt, ones], axis=2)


def _diag_bias():
    r = jnp.arange(SEL_TK)[None, :, None]
    c = jnp.arange(NSA_TQ)[None, None, :]
    off = (jnp.arange(SEL_TK // NSA_TQ) * NSA_TQ)[:, None, None]
    return jnp.where(r <= off + c, 0.0, MASK_VALUE).astype(BF16)


def _window_bias():
    r = jnp.arange(WIN_KEYS)[None, :, None]
    c = jnp.arange(NSA_TQ)[None, None, :]
    off = jnp.minimum(jnp.arange(WINDOW // NSA_TQ + 1) * NSA_TQ, WINDOW)[:, None, None]
    diff = off + c - r
    return jnp.where((diff >= 0) & (diff < WINDOW), 0.0, MASK_VALUE).astype(BF16)


def _nsa_sel_win(proj, sel, et, ocmp, B, S):
    T = B * S
    tq = NSA_TQ
    assert HALF == tq
    dbias = _diag_bias()
    wbias = _window_bias()
    nq = S // tq
    G = NSA_KV_GROUPS
    qw = NSA_HPG * HEAD_SLOT
    qblk = C_QEXT // qw
    kvblk = C_KV // LANES
    vsel_t = _values_t(proj[:, C_KV + 3 * LANES:C_KV + 4 * LANES], B, S)
    vwin_t = _values_t(proj[:, C_KV + 5 * LANES:C_KV + 6 * LANES], B, S)
    kv_spec = lambda k: pl.BlockSpec((S, LANES), lambda b, g, i: (b, kvblk + k))
    vt_spec = pl.BlockSpec((1, S // LANES, VT_ROWS, LANES), lambda b, g, i: (b, 0, 0, 0))
    return pl.pallas_call(
        _sel_win_kernel,
        grid=(B, G, nq),
        in_specs=[pl.BlockSpec((tq, qw), lambda b, g, i: (b * nq + i, qblk + g)),
                  kv_spec(2), vt_spec, kv_spec(4), vt_spec,
                  pl.BlockSpec((1, 1, tq, SEL_LANES), lambda b, g, i: (b, g, i, 0)),
                  pl.BlockSpec(et.shape, lambda b, g, i: (0, 0)),
                  pl.BlockSpec((tq, qw), lambda b, g, i: (b * nq + i, g)),
                  pl.BlockSpec((tq, LANES), lambda b, g, i: (b * nq + i, C_SMALL // LANES)),
                  pl.BlockSpec(dbias.shape, lambda b, g, i: (0, 0, 0)),
                  pl.BlockSpec(wbias.shape, lambda b, g, i: (0, 0, 0))],
        out_specs=pl.BlockSpec((tq, qw), lambda b, g, i: (b * nq + i, g)),
        out_shape=jax.ShapeDtypeStruct((T, NSA_HEADS * HEAD_SLOT), BF16),
        scratch_shapes=[pltpu.VMEM((SEL_TK, NSA_HPG * tq), F32),
                        pltpu.VMEM((1, NSA_HPG * tq), F32),
                        pltpu.VMEM((VT_ROWS, NSA_HPG * tq), F32),
                        pltpu.VMEM((LANES, tq), F32)],
        compiler_params=_cparams(("parallel", "parallel", "parallel")),
        name="nsa_sel_win",
    )(proj, proj, vsel_t, proj, vwin_t, sel, et, ocmp, proj, dbias, wbias)


TAIL_ROWS = 8


def _causal_conv(x, tail_ref, w, b):
    L = x.shape[0]
    xx = jnp.concatenate([tail_ref[...], x], axis=0)
    y = b + w[CONV_WIDTH - 1:CONV_WIDTH] * x
    for k in range(1, CONV_WIDTH):
        y = y + w[CONV_WIDTH - 1 - k:CONV_WIDTH - k] * xx[TAIL_ROWS - k:TAIL_ROWS - k + L]
    tail_ref[...] = x[L - TAIL_ROWS:L]
    return y


def _ssd_kernel(z_ref, xbc_ref, dtc_ref, dtr_ref, cw_ref, cb_ref, dtbc_ref, dtbr_ref, alc_ref, alr_ref,
                dsk_ref, nw_ref, o_ref, state_ref, tail_ref, y_ref):
    c = pl.program_id(1)
    L = SSD_CHUNK
    P = SSD_HEAD_DIM
    N = SSD_STATE

    @pl.when(c == 0)
    def _():
        state_ref[...] = jnp.zeros_like(state_ref)
        tail_ref[...] = jnp.zeros_like(tail_ref)

    conv = _causal_conv(xbc_ref[...].astype(F32), tail_ref, cw_ref[...], cb_ref[...])
    xbc = conv * _sigmoid(conv)
    xs = xbc[:, 0:SSD_INNER]
    bm = xbc[:, SSD_INNER:SSD_INNER + SSD_GROUPS * N]
    cm = xbc[:, SSD_INNER + SSD_GROUPS * N:SSD_INNER + 2 * SSD_GROUPS * N]

    dt_c = _softplus(dtc_ref[...] + dtbc_ref[...])
    dt_r = _softplus(dtr_ref[...] + dtbr_ref[...])
    a_c = dt_c * (-jnp.exp(alc_ref[...]))
    a_r = dt_r * (-jnp.exp(alr_ref[...]))
    ii = lax.broadcasted_iota(jnp.int32, (L, L), 0)
    jj = lax.broadcasted_iota(jnp.int32, (L, L), 1)
    tri = ii >= jj
    acum_c = _dot_f32(tri.astype(F32), a_c)
    acum_r = _dot_f32(a_r, (ii <= jj).astype(F32))

    for g in range(SSD_GROUPS):
        bg = bm[:, g * N:(g + 1) * N]
        cg = cm[:, g * N:(g + 1) * N]
        cgb = cg.astype(BF16)
        cb = _dot_nt(cgb, bg.astype(BF16))
        for j in range(SSD_HEADS // SSD_GROUPS):
            h = g * (SSD_HEADS // SSD_GROUPS) + j
            ac = acum_c[:, h:h + 1]
            ar = acum_r[h:h + 1, :]
            a_last = acum_c[L - 1:L, h:h + 1]
            xh_raw = xs[:, h * P:(h + 1) * P]
            xh = (xh_raw * dt_c[:, h:h + 1]).astype(BF16)
            lmat = jnp.where(tri, jnp.exp(ac - ar), 0.0)
            y = _dot((cb * lmat).astype(BF16), xh)
            prev = state_ref[h]
            y = y + _dot(cgb, prev.astype(BF16)) * jnp.exp(ac)
            bd = (bg * jnp.exp(a_last - ac)).astype(BF16)
            state_ref[h] = jnp.exp(a_last) * prev + _dot_tn(bd, xh)
            y_ref[:, h * P:(h + 1) * P] = y + dsk_ref[:, h * P:(h + 1) * P] * xh_raw

    zf = z_ref[...].astype(F32)
    yg = y_ref[...] * (zf * _sigmoid(zf))
    ms = jnp.mean(yg * yg, axis=-1, keepdims=True)
    o_ref[...] = (yg * lax.rsqrt(ms + RMS_EPS) * nw_ref[...]).astype(o_ref.dtype)


def _ssd_mixer(proj, dt_col, dt_row, conv_w, conv_b, dt_bias, a_log, d_skip, norm_w, B, S):
    T = B * S
    L = SSD_CHUNK
    nc = S // L
    H = SSD_HEADS
    full = lambda a: pl.BlockSpec(a.shape, lambda b, c: (0, 0))
    cb2 = conv_b.reshape(1, -1)
    dtb_c = dt_bias.reshape(1, H)
    dtb_r = dt_bias.reshape(H, 1)
    al_c = a_log.reshape(1, H)
    al_r = a_log.reshape(H, 1)
    dsk = jnp.repeat(d_skip, SSD_HEAD_DIM).reshape(1, SSD_INNER)
    nw = norm_w.reshape(1, SSD_INNER)
    return pl.pallas_call(
        _ssd_kernel,
        grid=(B, nc),
        in_specs=[pl.BlockSpec((L, SSD_INNER), lambda b, c: (b * nc + c, C_SSDZ // SSD_INNER)),
                  pl.BlockSpec((L, SSD_XBC_W), lambda b, c: (b * nc + c, C_XBC // SSD_XBC_W)),
                  pl.BlockSpec((L, H), lambda b, c: (b * nc + c, 0)),
                  pl.BlockSpec((H, L), lambda b, c: (0, b * nc + c)),
                  full(conv_w), full(cb2), full(dtb_c), full(dtb_r), full(al_c), full(al_r),
                  full(dsk), full(nw)],
        out_specs=pl.BlockSpec((L, SSD_INNER), lambda b, c: (b * nc + c, 0)),
        out_shape=jax.ShapeDtypeStruct((T, SSD_INNER), BF16),
        scratch_shapes=[pltpu.VMEM((H, SSD_STATE, SSD_HEAD_DIM), F32),
                        pltpu.VMEM((TAIL_ROWS, SSD_XBC_W), F32),
                        pltpu.VMEM((L, SSD_INNER), F32)],
        compiler_params=_cparams(("parallel", "arbitrary")),
        name="ssd_mixer",
    )(proj, proj, dt_col, dt_row, conv_w, cb2, dtb_c, dtb_r, al_c, al_r, dsk, nw)


def _lru_kernel(x_ref, y_ref, cw_ref, cb_ref, wa_ref, ba_ref, wx_ref, bx_ref, lam_ref, o_ref,
                h_ref, tail_ref, *, tc):
    c = pl.program_id(1)

    @pl.when(c == 0)
    def _():
        h_ref[...] = jnp.zeros_like(h_ref)
        tail_ref[...] = jnp.zeros_like(tail_ref)

    xr = _causal_conv(x_ref[...].astype(F32), tail_ref, cw_ref[...], cb_ref[...])
    xrb = xr.astype(BF16)
    r = _sigmoid(_dot(xrb, wa_ref[...]) + ba_ref[...])
    ig = _sigmoid(_dot(xrb, wx_ref[...]) + bx_ref[...])
    log_a = -LRU_C * r * _softplus(-lam_ref[...])
    a = jnp.exp(log_a)
    b = jnp.sqrt(1.0 - jnp.exp(2.0 * log_a)) * (ig * xr)
    row = lax.broadcasted_iota(jnp.int32, a.shape, 0)
    k = 1
    while k < tc:
        keep = row >= k
        a_s = jnp.where(keep, pltpu.roll(a, k, axis=0), 1.0)
        b_s = jnp.where(keep, pltpu.roll(b, k, axis=0), 0.0)
        b = a * b_s + b
        a = a * a_s
        k *= 2
    h = a * h_ref[0:1, :] + b
    h_ref[...] = jnp.broadcast_to(h[tc - 1:tc, :], h_ref.shape)
    o_ref[...] = (h * _gelu_tanh(y_ref[...].astype(F32))).astype(o_ref.dtype)


def _block_diag(w):
    nb, c, d = w.shape
    eye = jnp.eye(nb, dtype=w.dtype)
    return (eye[:, None, :, None] * w[:, :, None, :]).reshape(nb * c, nb * d)


def _lru_mixer(proj, conv_w, conv_b, wa, ba, wx, bx, lam, B, S):
    T = B * S
    tc = min(256, S)
    nt = S // tc
    W = LRU_WIDTH
    wa_bd = _block_diag(wa).astype(BF16)
    wx_bd = _block_diag(wx).astype(BF16)
    row = lambda v: v.reshape(1, W)
    full = lambda a: pl.BlockSpec(a.shape, lambda b, c: (0, 0))
    args = (conv_w, row(conv_b), wa_bd, row(ba), wx_bd, row(bx), row(lam))
    return pl.pallas_call(
        functools.partial(_lru_kernel, tc=tc),
        grid=(B, nt),
        in_specs=[pl.BlockSpec((tc, W), lambda b, c: (b * nt + c, C_LRUX // W)),
                  pl.BlockSpec((tc, W), lambda b, c: (b * nt + c, C_LRUY // W))]
                 + [full(a) for a in args],
        out_specs=pl.BlockSpec((tc, W), lambda b, c: (b * nt + c, 0)),
        out_shape=jax.ShapeDtypeStruct((T, W), BF16),
        scratch_shapes=[pltpu.VMEM((TAIL_ROWS, W), F32), pltpu.VMEM((TAIL_ROWS, W), F32)],
        compiler_params=_cparams(("parallel", "arbitrary")),
        name="lru_mixer",
    )(proj, proj, *args)


def _merge_kernel(x_ref, mg_ref, on_ref, os_ref, ol_ref, pn_ref, ps_ref, pl_ref, wo_ref, g_ref, b_ref,
                  of_ref, ob_ref):
    d = D_MODEL
    gate = _sigmoid(mg_ref[...].astype(F32))
    mixed = (gate[:, 0:d] * _dot(on_ref[...], pn_ref[...])
             + gate[:, d:2 * d] * _dot(os_ref[...], ps_ref[...])
             + gate[:, 2 * d:3 * d] * _dot(ol_ref[...], pl_ref[...]))
    v = ALPHA * x_ref[...] + _dot(mixed.astype(BF16), wo_ref[...])
    out = _layer_norm(v, g_ref[...], b_ref[...])
    of_ref[...] = out
    ob_ref[...] = out.astype(BF16)


def _merge(x, proj, o_nsa, o_ssd, o_lru, pn, ps, plru, wo, g, b):
    T = x.shape[0]
    tm = min(512, T)
    d = D_MODEL
    rowblk = lambda w: pl.BlockSpec((tm, w), lambda i: (i, 0))
    full = lambda a: pl.BlockSpec(a.shape, lambda i: (0, 0))
    g2, b2 = g.reshape(1, d), b.reshape(1, d)
    return pl.pallas_call(
        _merge_kernel,
        grid=(T // tm,),
        in_specs=[rowblk(d), pl.BlockSpec((tm, 3 * d), lambda i: (i, C_MERGE // (3 * d))),
                  rowblk(o_nsa.shape[1]), rowblk(o_ssd.shape[1]), rowblk(o_lru.shape[1]),
                  full(pn), full(ps), full(plru), full(wo), full(g2), full(b2)],
        out_specs=[rowblk(d), rowblk(d)],
        out_shape=[jax.ShapeDtypeStruct((T, d), F32), jax.ShapeDtypeStruct((T, d), BF16)],
        compiler_params=_cparams(("parallel",)),
        name="merge_ln",
    )(x, proj, o_nsa, o_ssd, o_lru, pn, ps, plru, wo, g2, b2)


def _route(sel, aff):
    epg = EXPERTS_PER_GROUP
    scores = []
    for gi in range(N_EXPERT_GROUPS):
        v = sel[gi * epg:(gi + 1) * epg]
        pair = None
        for a in range(epg):
            for b in range(a + 1, epg):
                sab = v[a] + v[b]
                pair = sab if pair is None else jnp.maximum(pair, sab)
        scores.append(pair)
    best = jnp.zeros_like(scores[0], dtype=jnp.int32)
    best_s = scores[0]
    for gi in range(1, N_EXPERT_GROUPS):
        better = scores[gi] > best_s
        best = jnp.where(better, gi, best)
        best_s = jnp.where(better, scores[gi], best_s)
    chosen = []
    for k in range(N_EXPERTS):
        gi = k // epg
        rank = jnp.zeros_like(best)
        for o in range(gi * epg, (gi + 1) * epg):
            if o == k:
                continue
            ahead = (sel[o] > sel[k]) | ((sel[o] == sel[k]) & (o < k))
            rank = rank + ahead.astype(jnp.int32)
        chosen.append((best == gi) & (rank < TOP_K))
    wsum = None
    for k in range(N_EXPERTS):
        wk = jnp.where(chosen[k], aff[k], 0.0)
        wsum = wk if wsum is None else wsum + wk
    inv = 1.0 / wsum
    return [jnp.where(chosen[k], aff[k], 0.0) * inv for k in range(N_EXPERTS)]


def _moe_kernel(xb_ref, xf_ref, p_ref, rw_ref, rb_ref, pg_ref, pp_ref, wg_ref, wu_ref, wd_ref,
                g_ref, b_ref, of_ref, ob_ref, acc_ref, gates_ref):
    e = pl.program_id(1)
    xb = xb_ref[...]
    tm = xb.shape[0]

    @pl.when(e == 0)
    def _():
        logits = _dot_nt(rw_ref[...], xb)
        aff = _sigmoid(logits)
        sel = aff + rb_ref[...]
        gate_rows = _route([sel[k:k + 1, :] for k in range(N_EXPERTS)],
                           [aff[k:k + 1, :] for k in range(N_EXPERTS)])
        gt = jnp.concatenate(gate_rows + [jnp.zeros((LANES - N_EXPERTS, tm), F32)], axis=0)
        gates_ref[...] = gt.T
        ple = _sigmoid(_dot(xb, pg_ref[...])) * _dot(p_ref[...].astype(BF16), pp_ref[...])
        acc_ref[...] = ple

    gates = gates_ref[...]
    lane = lax.broadcasted_iota(jnp.int32, gates.shape, 1)
    ge = jnp.sum(jnp.where(lane == e, gates, 0.0), axis=-1, keepdims=True)
    hg = _dot(xb, wg_ref[0])
    hu = _dot(xb, wu_ref[0])
    h = (hg * _sigmoid(hg)) * hu * ge
    acc_ref[...] += _dot(h.astype(BF16), wd_ref[0])

    @pl.when(e == N_EXPERTS - 1)
    def _():
        out = _layer_norm(ALPHA * xf_ref[...] + acc_ref[...], g_ref[...], b_ref[...])
        of_ref[...] = out
        ob_ref[...] = out.astype(BF16)


def _moe_ple(xb, xf, p, rw_t, rb, pg, pp, wg, wu, wd, g, b):
    T = xb.shape[0]
    tm = min(512, T)
    d = D_MODEL
    rowblk = lambda w: pl.BlockSpec((tm, w), lambda i, e: (i, 0))
    full = lambda a: pl.BlockSpec(a.shape, lambda i, e: (0, 0))
    g2, b2 = g.reshape(1, d), b.reshape(1, d)
    return pl.pallas_call(
        _moe_kernel,
        grid=(T // tm, N_EXPERTS),
        in_specs=[rowblk(d), rowblk(d), rowblk(PLE_DIM), full(rw_t), full(rb), full(pg), full(pp),
                  pl.BlockSpec((1, d, D_EXPERT), lambda i, e: (e, 0, 0)),
                  pl.BlockSpec((1, d, D_EXPERT), lambda i, e: (e, 0, 0)),
                  pl.BlockSpec((1, D_EXPERT, d), lambda i, e: (e, 0, 0)),
                  full(g2), full(b2)],
        out_specs=[rowblk(d), rowblk(d)],
        out_shape=[jax.ShapeDtypeStruct((T, d), F32), jax.ShapeDtypeStruct((T, d), BF16)],
        scratch_shapes=[pltpu.VMEM((tm, d), F32), pltpu.VMEM((tm, LANES), F32)],
        compiler_params=_cparams(("parallel", "arbitrary")),
        name="moe_ple_ln",
    )(xb, xf, p, rw_t, rb, pg, pp, wg, wu, wd, g2, b2)


def _overlap_matrix(nc):
    n = jnp.arange(nc)[:, None]
    m = jnp.arange(SEL_LANES)[None, :]
    ratio = SEL_BLOCK // CMP_STRIDE
    ov = jnp.zeros((nc, SEL_LANES), F32)
    for k in range(CMP_BLOCK // CMP_STRIDE):
        ov = ov + ((n + k) // ratio == m).astype(F32)
    return ov.astype(BF16)


def _expand_matrix(S):
    c = jnp.arange(S)[:, None]
    m = jnp.arange(SEL_LANES)[None, :]
    return jnp.where(c // SEL_BLOCK == m, -MASK_VALUE, 0.0).astype(BF16)


def _pad_w2(w2):
    out = jnp.zeros((NSA_KV_GROUPS, CMP_HIDDEN, HEAD_SLOT), F32)
    for g in range(NSA_KV_GROUPS):
        out = out.at[g, :, g * NSA_HEAD_DIM:(g + 1) * NSA_HEAD_DIM].set(w2)
    return out.astype(BF16)


def _pad_proj_nsa(w):
    d = w.shape[1]
    w4 = w.reshape(NSA_KV_GROUPS, NSA_HPG, NSA_HEAD_DIM, d)
    out = jnp.zeros((NSA_KV_GROUPS, NSA_HPG, HEAD_SLOT, d), F32)
    for g in range(NSA_KV_GROUPS):
        out = out.at[g, :, g * NSA_HEAD_DIM:(g + 1) * NSA_HEAD_DIM].set(w4[g])
    return out.reshape(NSA_HEADS * HEAD_SLOT, d).astype(BF16)


def _cmp_rows(kv, B, S):
    nr = S // CMP_STRIDE
    kv = kv.reshape(B, nr, CMP_STRIDE, NSA_KV_GROUPS, NSA_HEAD_DIM)
    return kv.transpose(0, 3, 1, 2, 4).reshape(B, NSA_KV_GROUPS, nr, CMP_STRIDE * NSA_HEAD_DIM)


def kernel(x, p, w_in, nsa_pe_k, nsa_w1_k, nsa_w2_k, nsa_pe_v, nsa_w1_v, nsa_w2_v, ssd_conv_w, ssd_conv_b, ssd_dt_bias, ssd_a_log, ssd_d, ssd_norm_w, lru_conv_w, lru_conv_b, lru_wa, lru_ba, lru_wx, lru_bx, lru_lambda, proj_nsa, proj_ssd, proj_lru, w_out, ln1_g, ln1_b, router_w, router_b, exp_w_gate, exp_w_up, exp_w_down, ple_w_gate, ple_w_proj, ln2_g, ln2_b):
    B, S, d = x.shape
    T = B * S
    depth = w_in.shape[0]
    assert d == D_MODEL and S % SEL_TK == 0 and S >= WIN_KEYS and S // SEL_BLOCK <= SEL_LANES
    nr = S // CMP_STRIDE
    ov = _overlap_matrix(nr)
    emat = _expand_matrix(S)
    rw_t = router_w.T.astype(BF16)
    rb = router_b.reshape(N_EXPERTS, 1).astype(F32)

    xf = x.reshape(T, d)
    xb = xf.astype(BF16)
    for i in range(depth):
        proj = _in_proj(xb, _prep_w_in(w_in[i]))

        kv0 = C_KV
        k_cmp = proj[:, kv0:kv0 + NSA_KV_W]
        v_cmp = proj[:, kv0 + NSA_KV_W:kv0 + 2 * NSA_KV_W]
        pe_rows = lambda pe: jnp.broadcast_to(pe.reshape(1, -1), (8, CMP_BLOCK * NSA_HEAD_DIM)).astype(BF16)
        kc, vc = _nsa_compress(_cmp_rows(k_cmp, B, S), _cmp_rows(v_cmp, B, S),
                               pe_rows(nsa_pe_k[i]), pe_rows(nsa_pe_v[i]),
                               nsa_w1_k[i].astype(BF16), nsa_w1_v[i].astype(BF16),
                               _pad_w2(nsa_w2_k[i]), _pad_w2(nsa_w2_v[i]))
        ocmp, sel = _nsa_cmp_attn(proj, kc, vc, ov, B, S)
        o_nsa = _nsa_sel_win(proj, sel, emat, ocmp, B, S)

        dt_col = proj[:, C_SMALL + GATE_W:C_SMALL + GATE_W + SSD_HEADS].astype(F32)
        o_ssd = _ssd_mixer(proj, dt_col, dt_col.T, ssd_conv_w[i], ssd_conv_b[i], ssd_dt_bias[i],
                           ssd_a_log[i], ssd_d[i], ssd_norm_w[i], B, S)
        o_lru = _lru_mixer(proj, lru_conv_w[i], lru_conv_b[i], lru_wa[i], lru_ba[i], lru_wx[i],
                           lru_bx[i], lru_lambda[i], B, S)

        xf, xb = _merge(xf, proj, o_nsa, o_ssd, o_lru, _pad_proj_nsa(proj_nsa[i]),
                        proj_ssd[i].astype(BF16), proj_lru[i].astype(BF16), w_out[i].astype(BF16),
                        ln1_g[i], ln1_b[i])
        xf, xb = _moe_ple(xb, xf, p[i].reshape(T, PLE_DIM), rw_t, rb,
                          ple_w_gate[i].astype(BF16), ple_w_proj[i].astype(BF16),
                          exp_w_gate[i].astype(BF16), exp_w_up[i].astype(BF16),
                          exp_w_down[i].astype(BF16), ln2_g[i], ln2_b[i])
    return xf.reshape(B, S, d)
```

```python
import functools
import math

import jax
import jax.numpy as jnp
from jax import lax
from jax.experimental import pallas as pl
from jax.experimental.pallas import tpu as pltpu

F32 = jnp.float32
BF16 = jnp.bfloat16

D_MODEL = 1024
PLE_DIM = 256
NSA_HEADS = 8
NSA_KV_GROUPS = 2
NSA_HEAD_DIM = 64
NSA_HPG = NSA_HEADS // NSA_KV_GROUPS
NSA_Q_W = NSA_HEADS * NSA_HEAD_DIM
NSA_KV_W = NSA_KV_GROUPS * NSA_HEAD_DIM
CMP_BLOCK = 32
CMP_STRIDE = 16
CMP_HIDDEN = 256
SEL_BLOCK = 64
SEL_TOPN = 16
WINDOW = 512
FORCE_SCORE = 1e4
MASK_VALUE = -1e30
LOG2E = 1.4426950408889634
SSD_HEADS = 8
SSD_HEAD_DIM = 64
SSD_INNER = SSD_HEADS * SSD_HEAD_DIM
SSD_GROUPS = 2
SSD_STATE = 64
SSD_CHUNK = 128
SSD_XBC_W = SSD_INNER + 2 * SSD_GROUPS * SSD_STATE
CONV_WIDTH = 4
LRU_WIDTH = 512
LRU_BLOCKS = 8
LRU_BLOCK_DIM = LRU_WIDTH // LRU_BLOCKS
LRU_C = 8.0
N_EXPERTS = 16
N_EXPERT_GROUPS = 4
EXPERTS_PER_GROUP = N_EXPERTS // N_EXPERT_GROUPS
TOP_K = 2
D_EXPERT = 256
DEPTH = 2
ALPHA = (2 * DEPTH) ** 0.25
LN_EPS = 1e-5
RMS_EPS = 1e-5
IN_SIZES = (NSA_Q_W, NSA_KV_W, NSA_KV_W, NSA_KV_W, NSA_KV_W, NSA_KV_W, NSA_KV_W, NSA_HEADS * 3,
            SSD_INNER, SSD_XBC_W, SSD_HEADS, LRU_WIDTH, LRU_WIDTH, 3 * D_MODEL)

LANES = 128
SEL_LANES = 128
HEAD_SLOT = 128

C_MERGE = 0
C_QEXT = 3072
C_SSDZ = 4096
C_XBC = 4608
C_KV = 5376
C_LRUX = 6144
C_LRUY = 6656
C_SMALL = 7168
PROJ_W = 7680
GATE_W = NSA_HEADS * 3

VMEM_LIMIT = 56 * 1024 * 1024


def _cparams(sem):
    return pltpu.CompilerParams(dimension_semantics=sem, vmem_limit_bytes=VMEM_LIMIT)


def _sigmoid(x):
    return 1.0 / (1.0 + jnp.exp(-x))


def _softplus(x):
    return jnp.maximum(x, 0.0) + jnp.log(1.0 + jnp.exp(-jnp.abs(x)))


def _gelu_tanh(x):
    c = math.sqrt(2.0 / math.pi)
    return 0.5 * x * (1.0 + jnp.tanh(c * (x + 0.044715 * (x * x * x))))


def _dot(a, b):
    return jnp.dot(a, b, preferred_element_type=F32)


def _dot_nt(a, b):
    return lax.dot_general(a, b, (((1,), (1,)), ((), ())), preferred_element_type=F32)


def _dot_tn(a, b):
    return lax.dot_general(a, b, (((0,), (0,)), ((), ())), preferred_element_type=F32)


def _dot_f32(a, b):
    return jnp.dot(a, b, preferred_element_type=F32, precision=lax.Precision.HIGHEST)


def _layer_norm(v, g, b):
    mu = jnp.mean(v, axis=-1, keepdims=True)
    vc = v - mu
    var = jnp.mean(vc * vc, axis=-1, keepdims=True)
    return vc * lax.rsqrt(var + LN_EPS) * g + b


def _matmul_kernel(x_ref, w_ref, o_ref):
    o_ref[...] = _dot(x_ref[...], w_ref[...]).astype(o_ref.dtype)


def _in_proj(xb, w):
    T, K = xb.shape
    N = w.shape[1]
    tm = min(1024, T)
    tn = 512
    return pl.pallas_call(
        _matmul_kernel,
        grid=(T // tm, N // tn),
        in_specs=[pl.BlockSpec((tm, K), lambda i, j: (i, 0)),
                  pl.BlockSpec((K, tn), lambda i, j: (0, j))],
        out_specs=pl.BlockSpec((tm, tn), lambda i, j: (i, j)),
        out_shape=jax.ShapeDtypeStruct((T, N), BF16),
        compiler_params=_cparams(("parallel", "arbitrary")),
        name="in_proj",
    )(xb, w)


def _prep_w_in(w):
    offs = [0]
    for s in IN_SIZES:
        offs.append(offs[-1] + s)
    piece = lambda k: w[:, offs[k]:offs[k + 1]]
    d = w.shape[0]
    q = piece(0).reshape(d, NSA_KV_GROUPS, NSA_HPG, NSA_HEAD_DIM) * (NSA_HEAD_DIM ** -0.5 * LOG2E)
    q_ext = jnp.zeros((d, NSA_KV_GROUPS, NSA_HPG, HEAD_SLOT), F32)
    for g in range(NSA_KV_GROUPS):
        q_ext = q_ext.at[:, g, :, g * NSA_HEAD_DIM:(g + 1) * NSA_HEAD_DIM].set(q[:, g])
    q_ext = q_ext.reshape(d, NSA_HEADS * HEAD_SLOT)
    small = jnp.concatenate([piece(7), piece(10)], axis=1)
    small = jnp.pad(small, ((0, 0), (0, PROJ_W - C_SMALL - small.shape[1])))
    out = jnp.concatenate([piece(13), q_ext, piece(8), piece(9),
                           piece(1), piece(2), piece(3), piece(4), piece(5), piece(6),
                           piece(11), piece(12), small], axis=1)
    assert out.shape[1] == PROJ_W
    return out.astype(BF16)


def _compress_kernel(rk_ref, rv_ref, pek_ref, pev_ref, w1k_ref, w1v_ref, w2k_ref, w2v_ref,
                     kc_ref, vc_ref):
    half = CMP_STRIDE * NSA_HEAD_DIM
    for r_ref, pe_ref, w1_ref, w2_ref, o_ref in ((rk_ref, pek_ref, w1k_ref, w2k_ref, kc_ref),
                                                 (rv_ref, pev_ref, w1v_ref, w2v_ref, vc_ref)):
        r = r_ref[0, 0]
        nr = r.shape[0]
        u = _dot(r, w1_ref[0:half, :])
        v = _dot(r, w1_ref[half:2 * half, :])
        c = _dot(pe_ref[...], w1_ref[...])[0:1]
        hid = u + pltpu.roll(v, nr - 1, axis=0) + c
        act = _gelu_tanh(hid)
        o_ref[0, 0] = _dot(act.astype(BF16), w2_ref[0]).astype(o_ref.dtype)


def _nsa_compress(rk, rv, pek, pev, w1k, w1v, w2k, w2v):
    B, G, NR, W = rk.shape
    blk_r = pl.BlockSpec((1, 1, NR, W), lambda b, g: (b, g, 0, 0))
    full2 = lambda a: pl.BlockSpec(a.shape, lambda b, g: (0, 0))
    blk_w2 = pl.BlockSpec((1, CMP_HIDDEN, HEAD_SLOT), lambda b, g: (g, 0, 0))
    blk_o = pl.BlockSpec((1, 1, NR, HEAD_SLOT), lambda b, g: (b, g, 0, 0))
    shp = jax.ShapeDtypeStruct((B, G, NR, HEAD_SLOT), BF16)
    return pl.pallas_call(
        _compress_kernel,
        grid=(B, G),
        in_specs=[blk_r, blk_r, full2(pek), full2(pev), full2(w1k), full2(w1v), blk_w2, blk_w2],
        out_specs=[blk_o, blk_o],
        out_shape=[shp, shp],
        compiler_params=_cparams(("parallel", "parallel")),
        name="nsa_compress",
    )(rk, rv, pek, pev, w1k, w1v, w2k, w2v)


def _stack_heads(q):
    return jnp.concatenate([q[:, j * HEAD_SLOT:(j + 1) * HEAD_SLOT] for j in range(NSA_HPG)], axis=0)


def _cmp_attn_kernel(q_ref, kc_ref, vct_ref, ovt_ref, gate_ref, ocmpt_ref, sel_ref, gt_ref, *, tq):
    g = pl.program_id(1)
    i = pl.program_id(2)
    q2 = _stack_heads(q_ref[...])
    kc = kc_ref[0, 0]
    vct = vct_ref[0, 0]
    nc = kc.shape[0]
    sts = [_dot_nt(kc, q2[j * tq:(j + 1) * tq]) for j in range(NSA_HPG)]
    n = lax.broadcasted_iota(jnp.int32, (nc, tq), 0)
    t = i * tq + lax.broadcasted_iota(jnp.int32, (nc, tq), 1)
    mask = n * CMP_STRIDE + (CMP_BLOCK - 1) <= t
    gt_ref[...] = _sigmoid(gate_ref[...].astype(F32)).T
    ps = None
    for j in range(NSA_HPG):
        s = jnp.where(mask, sts[j], MASK_VALUE)
        m = jnp.max(s, axis=0, keepdims=True)
        e = jnp.where(mask, jnp.exp2(s - m), 0.0)
        den = jnp.maximum(jnp.sum(e, axis=0, keepdims=True), 1e-30)
        p = e * (1.0 / den)
        ps = p if ps is None else ps + p
        gate = gt_ref[pl.ds(g * (NSA_HPG * 3) + j * 3, 1), :]
        ocmpt_ref[0, 0, 0, :, j * tq:(j + 1) * tq] = (gate * _dot(vct, p.astype(BF16))).astype(ocmpt_ref.dtype)
    ovt = ovt_ref[...]
    hi = ps.astype(BF16)
    r1 = ps - hi.astype(F32)
    mid = r1.astype(BF16)
    lo = (r1 - mid.astype(F32)).astype(BF16)
    imp = _dot(ovt, hi) + _dot(ovt, mid) + _dot(ovt, lo)
    blk = lax.broadcasted_iota(jnp.int32, imp.shape, 0)
    tt = i * tq + lax.broadcasted_iota(jnp.int32, imp.shape, 1)
    cur = tt // SEL_BLOCK
    forced = (blk == 0) | (blk == cur) | (blk == cur - 1)
    causal = blk * SEL_BLOCK <= tt
    v0 = jnp.where(forced, FORCE_SCORE, jnp.where(causal, imp, -1.0))

    blk1 = blk[:, 0:LANES]

    def pick(_, tiles):
        out = []
        for v in tiles:
            mx = jnp.max(v, axis=0, keepdims=True)
            idx = jnp.min(jnp.where(v == mx, blk1, SEL_LANES), axis=0, keepdims=True)
            out.append(jnp.where(blk1 == idx, -jnp.inf, v))
        return tuple(out)

    tiles = lax.fori_loop(0, SEL_TOPN, pick, tuple(v0[:, u * LANES:(u + 1) * LANES] for u in range(tq // LANES)))
    for u, v in enumerate(tiles):
        sel_ref[0, 0, u * LANES:(u + 1) * LANES, :] = jnp.where(v == -jnp.inf, 1.0, 0.0).T.astype(sel_ref.dtype)


def _nsa_cmp_attn(proj, kc, vct, ovt, B, S):
    tq = NSA_TQ
    nq = S // tq
    G = NSA_KV_GROUPS
    NC = kc.shape[2]
    qw = NSA_HPG * HEAD_SLOT
    qblk = C_QEXT // qw
    kern = functools.partial(_cmp_attn_kernel, tq=tq)
    return pl.pallas_call(
        kern,
        grid=(B, G, nq),
        in_specs=[pl.BlockSpec((tq, qw), lambda b, g, i: (b * nq + i, qblk + g)),
                  pl.BlockSpec((1, 1, NC, HEAD_SLOT), lambda b, g, i: (b, g, 0, 0)),
                  pl.BlockSpec((1, 1, HEAD_SLOT, NC), lambda b, g, i: (b, g, 0, 0)),
                  pl.BlockSpec(ovt.shape, lambda b, g, i: (0, 0)),
                  pl.BlockSpec((tq, LANES), lambda b, g, i: (b * nq + i, C_SMALL // LANES))],
        out_specs=[pl.BlockSpec((1, 1, 1, HEAD_SLOT, NSA_HPG * tq), lambda b, g, i: (b, g, i, 0, 0)),
                   pl.BlockSpec((1, 1, tq, SEL_LANES), lambda b, g, i: (b, g, i, 0))],
        out_shape=[jax.ShapeDtypeStruct((B, G, nq, HEAD_SLOT, NSA_HPG * tq), BF16),
                   jax.ShapeDtypeStruct((B, G, S, SEL_LANES), BF16)],
        scratch_shapes=[pltpu.VMEM((LANES, tq), F32)],
        compiler_params=_cparams(("parallel", "parallel", "parallel")),
        name="nsa_cmp_attn",
    )(proj, kc, vct, ovt, proj)


SEL_TK = 512
NSA_TQ = 256
WIN_KEYS = WINDOW + NSA_TQ


VT_ROWS = LANES + 16
HALF = 256
NCH = NSA_HPG * NSA_TQ // HALF


def _sel_win_kernel(q_ref, ksel_ref, vselt_ref, kwin_ref, vwint_ref, sel_ref, et_ref, ocmpt_ref,
                    gate_ref, dbias_ref, wbias_ref, o_ref, s_ref, m_ref, acc_ref, gt_ref):
    g = pl.program_id(1)
    i = pl.program_id(2)
    tq = NSA_TQ
    t0 = i * tq
    q2 = _stack_heads(q_ref[...])
    selm1 = sel_ref[0, 0] - 1.0
    qx = jnp.concatenate([q2, jnp.concatenate([selm1] * NSA_HPG, axis=0)], axis=1)
    qxh = [qx[c * HALF:(c + 1) * HALF] for c in range(NCH)]
    q2h = [q2[c * HALF:(c + 1) * HALF] for c in range(NCH)]
    kpb = SEL_TK // LANES

    wblk = jnp.maximum((t0 - WINDOW) // LANES, 0)
    start = pl.multiple_of(wblk * LANES, LANES)
    kw = kwin_ref[pl.ds(start, WIN_KEYS), :]
    vwt = jnp.concatenate([vwint_ref[0, wblk + u] for u in range(WIN_KEYS // LANES)], axis=1)
    wbias = wbias_ref[jnp.minimum(i, WINDOW // tq)].astype(F32)
    sws = [_dot_nt(kw, q2h[c]) for c in range(NCH)]

    def scores_into(kv):
        off = pl.multiple_of(kv * SEL_TK, SEL_TK)
        kx = jnp.concatenate([ksel_ref[pl.ds(off, SEL_TK), :], et_ref[pl.ds(off, SEL_TK), :]], axis=1)
        for c in range(NCH):
            s_ref[:, c * HALF:(c + 1) * HALF] = _dot_nt(kx, qxh[c])

    def update(vt, s_chunks):
        for c in range(NCH):
            cols = slice(c * HALF, (c + 1) * HALF)
            s = s_chunks[c]
            m_old = m_ref[:, cols]
            m_new = jnp.maximum(m_old, jnp.max(s, axis=0, keepdims=True))
            p = jnp.exp2(s - m_new).astype(BF16)
            acc_ref[:, cols] = jnp.exp2(m_old - m_new) * acc_ref[:, cols] + _dot(vt, p)
            m_ref[:, cols] = m_new

    def sel_values(kv):
        return jnp.concatenate([vselt_ref[0, kv * kpb + u] for u in range(kpb)], axis=1)

    def load_scores():
        return [s_ref[:, c * HALF:(c + 1) * HALF] for c in range(NCH)]

    m_ref[...] = jnp.full(m_ref.shape, MASK_VALUE, F32)
    acc_ref[...] = jnp.zeros(acc_ref.shape, F32)
    kd = t0 // SEL_TK
    scores_into(0)
    ot_win = []
    for c in range(NCH):
        sw = sws[c] + wbias
        mw = jnp.max(sw, axis=0, keepdims=True)
        accw = _dot(vwt, jnp.exp2(sw - mw).astype(BF16))
        ot_win.append(accw[0:LANES] * (1.0 / jnp.maximum(accw[LANES:LANES + 1], 1e-30)))

    def body(kv, carry):
        s_chunks = load_scores()
        scores_into(kv + 1)
        update(sel_values(kv), s_chunks)
        return carry

    lax.fori_loop(0, kd, body, 0)
    dbias = dbias_ref[i % (SEL_TK // tq)].astype(F32)
    update(sel_values(kd), [sc + dbias for sc in load_scores()])
    acc = acc_ref[...]
    ot_slc = acc[0:LANES] * (1.0 / jnp.maximum(acc[LANES:LANES + 1], 1e-30))

    gt_ref[...] = _sigmoid(gate_ref[...].astype(F32)).T
    for j in range(NSA_HPG):
        base = g * (NSA_HPG * 3) + j * 3
        ot = (ocmpt_ref[0, 0, 0, :, j * tq:(j + 1) * tq].astype(F32)
              + gt_ref[pl.ds(base + 1, 1), :] * ot_slc[:, j * tq:(j + 1) * tq]
              + gt_ref[pl.ds(base + 2, 1), :] * ot_win[j])
        o_ref[:, j * HEAD_SLOT:(j + 1) * HEAD_SLOT] = ot.T.astype(o_ref.dtype)


def _values_t(v, B, S):
    vt = v.reshape(B, S // LANES, LANES, LANES).transpose(0, 1, 3, 2)
    ones = jnp.ones((B, S // LANES, VT_ROWS - LANES, LANES), v.dtype)
    return jnp.concatenate([vt, ones], axis=2)


def _diag_bias():
    r = jnp.arange(SEL_TK)[None, :, None]
    c = jnp.arange(NSA_TQ)[None, None, :]
    off = (jnp.arange(SEL_TK // NSA_TQ) * NSA_TQ)[:, None, None]
    return jnp.where(r <= off + c, 0.0, MASK_VALUE).astype(BF16)


def _window_bias():
    r = jnp.arange(WIN_KEYS)[None, :, None]
    c = jnp.arange(NSA_TQ)[None, None, :]
    off = jnp.minimum(jnp.arange(WINDOW // NSA_TQ + 1) * NSA_TQ, WINDOW)[:, None, None]
    diff = off + c - r
    return jnp.where((diff >= 0) & (diff < WINDOW), 0.0, MASK_VALUE).astype(BF16)


def _nsa_sel_win(proj, sel, et, ocmpt, B, S):
    T = B * S
    tq = NSA_TQ
    assert HALF == tq
    dbias = _diag_bias()
    wbias = _window_bias()
    nq = S // tq
    G = NSA_KV_GROUPS
    qw = NSA_HPG * HEAD_SLOT
    qblk = C_QEXT // qw
    kvblk = C_KV // LANES
    vsel_t = _values_t(proj[:, C_KV + 3 * LANES:C_KV + 4 * LANES], B, S)
    vwin_t = _values_t(proj[:, C_KV + 5 * LANES:C_KV + 6 * LANES], B, S)
    kv_spec = lambda k: pl.BlockSpec((S, LANES), lambda b, g, i: (b, kvblk + k))
    vt_spec = pl.BlockSpec((1, S // LANES, VT_ROWS, LANES), lambda b, g, i: (b, 0, 0, 0))
    return pl.pallas_call(
        _sel_win_kernel,
        grid=(B, G, nq),
        in_specs=[pl.BlockSpec((tq, qw), lambda b, g, i: (b * nq + i, qblk + g)),
                  kv_spec(2), vt_spec, kv_spec(4), vt_spec,
                  pl.BlockSpec((1, 1, tq, SEL_LANES), lambda b, g, i: (b, g, i, 0)),
                  pl.BlockSpec(et.shape, lambda b, g, i: (0, 0)),
                  pl.BlockSpec((1, 1, 1, HEAD_SLOT, NSA_HPG * tq), lambda b, g, i: (b, g, i, 0, 0)),
                  pl.BlockSpec((tq, LANES), lambda b, g, i: (b * nq + i, C_SMALL // LANES)),
                  pl.BlockSpec(dbias.shape, lambda b, g, i: (0, 0, 0)),
                  pl.BlockSpec(wbias.shape, lambda b, g, i: (0, 0, 0))],
        out_specs=pl.BlockSpec((tq, qw), lambda b, g, i: (b * nq + i, g)),
        out_shape=jax.ShapeDtypeStruct((T, NSA_HEADS * HEAD_SLOT), BF16),
        scratch_shapes=[pltpu.VMEM((SEL_TK, NSA_HPG * tq), F32),
                        pltpu.VMEM((1, NSA_HPG * tq), F32),
                        pltpu.VMEM((VT_ROWS, NSA_HPG * tq), F32),
                        pltpu.VMEM((LANES, tq), F32)],
        compiler_params=_cparams(("parallel", "parallel", "parallel")),
        name="nsa_sel_win",
    )(proj, proj, vsel_t, proj, vwin_t, sel, et, ocmpt, proj, dbias, wbias)


TAIL_ROWS = 8


def _causal_conv(x, tail_ref, w, b):
    L = x.shape[0]
    xx = jnp.concatenate([tail_ref[...], x], axis=0)
    y = b + w[CONV_WIDTH - 1:CONV_WIDTH] * x
    for k in range(1, CONV_WIDTH):
        y = y + w[CONV_WIDTH - 1 - k:CONV_WIDTH - k] * xx[TAIL_ROWS - k:TAIL_ROWS - k + L]
    tail_ref[...] = x[L - TAIL_ROWS:L]
    return y


def _ssd_kernel(z_ref, xbc_ref, dtc_ref, dtr_ref, cw_ref, cb_ref, dtbc_ref, dtbr_ref, alc_ref, alr_ref,
                dsk_ref, nw_ref, o_ref, state_ref, tail_ref, y_ref):
    c = pl.program_id(1)
    L = SSD_CHUNK
    P = SSD_HEAD_DIM
    N = SSD_STATE

    @pl.when(c == 0)
    def _():
        state_ref[...] = jnp.zeros_like(state_ref)
        tail_ref[...] = jnp.zeros_like(tail_ref)

    conv = _causal_conv(xbc_ref[...].astype(F32), tail_ref, cw_ref[...], cb_ref[...])
    xbc = conv * _sigmoid(conv)
    xs = xbc[:, 0:SSD_INNER]
    bm = xbc[:, SSD_INNER:SSD_INNER + SSD_GROUPS * N]
    cm = xbc[:, SSD_INNER + SSD_GROUPS * N:SSD_INNER + 2 * SSD_GROUPS * N]

    dt_c = _softplus(dtc_ref[...] + dtbc_ref[...])
    dt_r = _softplus(dtr_ref[...] + dtbr_ref[...])
    a_c = dt_c * (-jnp.exp(alc_ref[...]))
    a_r = dt_r * (-jnp.exp(alr_ref[...]))
    ii = lax.broadcasted_iota(jnp.int32, (L, L), 0)
    jj = lax.broadcasted_iota(jnp.int32, (L, L), 1)
    tri = ii >= jj
    acum_c = _dot_f32(tri.astype(F32), a_c)
    acum_r = _dot_f32(a_r, (ii <= jj).astype(F32))

    for g in range(SSD_GROUPS):
        bg = bm[:, g * N:(g + 1) * N]
        cg = cm[:, g * N:(g + 1) * N]
        cgb = cg.astype(BF16)
        cb = _dot_nt(cgb, bg.astype(BF16))
        for j in range(SSD_HEADS // SSD_GROUPS):
            h = g * (SSD_HEADS // SSD_GROUPS) + j
            ac = acum_c[:, h:h + 1]
            ar = acum_r[h:h + 1, :]
            a_last = acum_c[L - 1:L, h:h + 1]
            xh_raw = xs[:, h * P:(h + 1) * P]
            xh = (xh_raw * dt_c[:, h:h + 1]).astype(BF16)
            lmat = jnp.where(tri, jnp.exp(ac - ar), 0.0)
            y = _dot((cb * lmat).astype(BF16), xh)
            prev = state_ref[h]
            y = y + _dot(cgb, prev.astype(BF16)) * jnp.exp(ac)
            bd = (bg * jnp.exp(a_last - ac)).astype(BF16)
            state_ref[h] = jnp.exp(a_last) * prev + _dot_tn(bd, xh)
            y_ref[:, h * P:(h + 1) * P] = y + dsk_ref[:, h * P:(h + 1) * P] * xh_raw

    zf = z_ref[...].astype(F32)
    yg = y_ref[...] * (zf * _sigmoid(zf))
    ms = jnp.mean(yg * yg, axis=-1, keepdims=True)
    o_ref[...] = (yg * lax.rsqrt(ms + RMS_EPS) * nw_ref[...]).astype(o_ref.dtype)


def _ssd_mixer(proj, dt_col, dt_row, conv_w, conv_b, dt_bias, a_log, d_skip, norm_w, B, S):
    T = B * S
    L = SSD_CHUNK
    nc = S // L
    H = SSD_HEADS
    full = lambda a: pl.BlockSpec(a.shape, lambda b, c: (0, 0))
    cb2 = conv_b.reshape(1, -1)
    dtb_c = dt_bias.reshape(1, H)
    dtb_r = dt_bias.reshape(H, 1)
    al_c = a_log.reshape(1, H)
    al_r = a_log.reshape(H, 1)
    dsk = jnp.repeat(d_skip, SSD_HEAD_DIM).reshape(1, SSD_INNER)
    nw = norm_w.reshape(1, SSD_INNER)
    return pl.pallas_call(
        _ssd_kernel,
        grid=(B, nc),
        in_specs=[pl.BlockSpec((L, SSD_INNER), lambda b, c: (b * nc + c, C_SSDZ // SSD_INNER)),
                  pl.BlockSpec((L, SSD_XBC_W), lambda b, c: (b * nc + c, C_XBC // SSD_XBC_W)),
                  pl.BlockSpec((L, H), lambda b, c: (b * nc + c, 0)),
                  pl.BlockSpec((H, L), lambda b, c: (0, b * nc + c)),
                  full(conv_w), full(cb2), full(dtb_c), full(dtb_r), full(al_c), full(al_r),
                  full(dsk), full(nw)],
        out_specs=pl.BlockSpec((L, SSD_INNER), lambda b, c: (b * nc + c, 0)),
        out_shape=jax.ShapeDtypeStruct((T, SSD_INNER), BF16),
        scratch_shapes=[pltpu.VMEM((H, SSD_STATE, SSD_HEAD_DIM), F32),
                        pltpu.VMEM((TAIL_ROWS, SSD_XBC_W), F32),
                        pltpu.VMEM((L, SSD_INNER), F32)],
        compiler_params=_cparams(("parallel", "arbitrary")),
        name="ssd_mixer",
    )(proj, proj, dt_col, dt_row, conv_w, cb2, dtb_c, dtb_r, al_c, al_r, dsk, nw)


def _lru_kernel(x_ref, y_ref, cw_ref, cb_ref, wa_ref, ba_ref, wx_ref, bx_ref, lam_ref, o_ref,
                h_ref, tail_ref, *, tc):
    c = pl.program_id(1)

    @pl.when(c == 0)
    def _():
        h_ref[...] = jnp.zeros_like(h_ref)
        tail_ref[...] = jnp.zeros_like(tail_ref)

    xr = _causal_conv(x_ref[...].astype(F32), tail_ref, cw_ref[...], cb_ref[...])
    xrb = xr.astype(BF16)
    r = _sigmoid(_dot(xrb, wa_ref[...]) + ba_ref[...])
    ig = _sigmoid(_dot(xrb, wx_ref[...]) + bx_ref[...])
    log_a = -LRU_C * r * _softplus(-lam_ref[...])
    a = jnp.exp(log_a)
    b = jnp.sqrt(1.0 - jnp.exp(2.0 * log_a)) * (ig * xr)
    row = lax.broadcasted_iota(jnp.int32, a.shape, 0)
    k = 1
    while k < tc:
        keep = row >= k
        a_s = jnp.where(keep, pltpu.roll(a, k, axis=0), 1.0)
        b_s = jnp.where(keep, pltpu.roll(b, k, axis=0), 0.0)
        b = a * b_s + b
        a = a * a_s
        k *= 2
    h = a * h_ref[0:1, :] + b
    h_ref[...] = jnp.broadcast_to(h[tc - 1:tc, :], h_ref.shape)
    o_ref[...] = (h * _gelu_tanh(y_ref[...].astype(F32))).astype(o_ref.dtype)


def _block_diag(w):
    nb, c, d = w.shape
    eye = jnp.eye(nb, dtype=w.dtype)
    return (eye[:, None, :, None] * w[:, :, None, :]).reshape(nb * c, nb * d)


def _lru_mixer(proj, conv_w, conv_b, wa, ba, wx, bx, lam, B, S):
    T = B * S
    tc = min(256, S)
    nt = S // tc
    W = LRU_WIDTH
    wa_bd = _block_diag(wa).astype(BF16)
    wx_bd = _block_diag(wx).astype(BF16)
    row = lambda v: v.reshape(1, W)
    full = lambda a: pl.BlockSpec(a.shape, lambda b, c: (0, 0))
    args = (conv_w, row(conv_b), wa_bd, row(ba), wx_bd, row(bx), row(lam))
    return pl.pallas_call(
        functools.partial(_lru_kernel, tc=tc),
        grid=(B, nt),
        in_specs=[pl.BlockSpec((tc, W), lambda b, c: (b * nt + c, C_LRUX // W)),
                  pl.BlockSpec((tc, W), lambda b, c: (b * nt + c, C_LRUY // W))]
                 + [full(a) for a in args],
        out_specs=pl.BlockSpec((tc, W), lambda b, c: (b * nt + c, 0)),
        out_shape=jax.ShapeDtypeStruct((T, W), BF16),
        scratch_shapes=[pltpu.VMEM((TAIL_ROWS, W), F32), pltpu.VMEM((TAIL_ROWS, W), F32)],
        compiler_params=_cparams(("parallel", "arbitrary")),
        name="lru_mixer",
    )(proj, proj, *args)


def _merge_kernel(x_ref, mg_ref, on_ref, os_ref, ol_ref, pn_ref, ps_ref, pl_ref, wo_ref, g_ref, b_ref,
                  of_ref, ob_ref):
    d = D_MODEL
    gate = _sigmoid(mg_ref[...].astype(F32))
    mixed = (gate[:, 0:d] * _dot(on_ref[...], pn_ref[...])
             + gate[:, d:2 * d] * _dot(os_ref[...], ps_ref[...])
             + gate[:, 2 * d:3 * d] * _dot(ol_ref[...], pl_ref[...]))
    v = ALPHA * x_ref[...] + _dot(mixed.astype(BF16), wo_ref[...])
    out = _layer_norm(v, g_ref[...], b_ref[...])
    of_ref[...] = out
    ob_ref[...] = out.astype(BF16)


def _merge(x, proj, o_nsa, o_ssd, o_lru, pn, ps, plru, wo, g, b):
    T = x.shape[0]
    tm = min(512, T)
    d = D_MODEL
    rowblk = lambda w: pl.BlockSpec((tm, w), lambda i: (i, 0))
    full = lambda a: pl.BlockSpec(a.shape, lambda i: (0, 0))
    g2, b2 = g.reshape(1, d), b.reshape(1, d)
    return pl.pallas_call(
        _merge_kernel,
        grid=(T // tm,),
        in_specs=[rowblk(d), pl.BlockSpec((tm, 3 * d), lambda i: (i, C_MERGE // (3 * d))),
                  rowblk(o_nsa.shape[1]), rowblk(o_ssd.shape[1]), rowblk(o_lru.shape[1]),
                  full(pn), full(ps), full(plru), full(wo), full(g2), full(b2)],
        out_specs=[rowblk(d), rowblk(d)],
        out_shape=[jax.ShapeDtypeStruct((T, d), F32), jax.ShapeDtypeStruct((T, d), BF16)],
        compiler_params=_cparams(("parallel",)),
        name="merge_ln",
    )(x, proj, o_nsa, o_ssd, o_lru, pn, ps, plru, wo, g2, b2)


def _route(sel, aff):
    epg = EXPERTS_PER_GROUP
    scores = []
    for gi in range(N_EXPERT_GROUPS):
        v = sel[gi * epg:(gi + 1) * epg]
        pair = None
        for a in range(epg):
            for b in range(a + 1, epg):
                sab = v[a] + v[b]
                pair = sab if pair is None else jnp.maximum(pair, sab)
        scores.append(pair)
    best = jnp.zeros_like(scores[0], dtype=jnp.int32)
    best_s = scores[0]
    for gi in range(1, N_EXPERT_GROUPS):
        better = scores[gi] > best_s
        best = jnp.where(better, gi, best)
        best_s = jnp.where(better, scores[gi], best_s)
    chosen = []
    for k in range(N_EXPERTS):
        gi = k // epg
        rank = jnp.zeros_like(best)
        for o in range(gi * epg, (gi + 1) * epg):
            if o == k:
                continue
            ahead = (sel[o] > sel[k]) | ((sel[o] == sel[k]) & (o < k))
            rank = rank + ahead.astype(jnp.int32)
        chosen.append((best == gi) & (rank < TOP_K))
    wsum = None
    for k in range(N_EXPERTS):
        wk = jnp.where(chosen[k], aff[k], 0.0)
        wsum = wk if wsum is None else wsum + wk
    inv = 1.0 / wsum
    return [jnp.where(chosen[k], aff[k], 0.0) * inv for k in range(N_EXPERTS)]


EXPERT_PAIRS = N_EXPERTS // 2
PAIRS_PER_ITER = 2


def _moe_kernel(xb_ref, xf_ref, p_ref, rw_ref, rb_ref, pg_ref, pp_ref, wgu_ref, wd_ref,
                g_ref, b_ref, of_ref, ob_ref, acc_ref, gates_ref):
    xb = xb_ref[...]
    tm = xb.shape[0]
    logits = _dot_nt(rw_ref[...], xb)
    aff = _sigmoid(logits)
    sel = aff + rb_ref[...]
    gate_rows = _route([sel[k:k + 1, :] for k in range(N_EXPERTS)],
                       [aff[k:k + 1, :] for k in range(N_EXPERTS)])
    gt = jnp.concatenate(gate_rows + [jnp.zeros((LANES - N_EXPERTS, tm), F32)], axis=0)
    gates_ref[...] = gt.T
    acc_ref[...] = _sigmoid(_dot(xb, pg_ref[...])) * _dot(p_ref[...].astype(BF16), pp_ref[...])
    lane = lax.broadcasted_iota(jnp.int32, (tm, LANES), 1)
    pw = 2 * D_EXPERT

    def gate_cols(k):
        gates = gates_ref[...]
        cols = [jnp.broadcast_to(jnp.sum(jnp.where(lane == 2 * k + u, gates, 0.0), axis=-1, keepdims=True),
                                 (tm, D_EXPERT)) for u in range(2)]
        return jnp.concatenate(cols, axis=1)

    def step(it, carry):
        ks = [it * PAIRS_PER_ITER + u for u in range(PAIRS_PER_ITER)]
        hgus = [_dot(xb, wgu_ref[k]) for k in ks]
        for k, hgu in zip(ks, hgus):
            hg = hgu[:, 0:pw]
            h = (hg * _sigmoid(hg)) * hgu[:, pw:2 * pw] * gate_cols(k)
            acc_ref[...] += _dot(h.astype(BF16), wd_ref[k])
        return carry

    lax.fori_loop(0, EXPERT_PAIRS // PAIRS_PER_ITER, step, 0)
    out = _layer_norm(ALPHA * xf_ref[...] + acc_ref[...], g_ref[...], b_ref[...])
    of_ref[...] = out
    ob_ref[...] = out.astype(BF16)


def _pair_expert_weights(wg, wu, wd):
    e, d, f = wg.shape
    g2 = wg.reshape(e // 2, 2, d, f).transpose(0, 2, 1, 3).reshape(e // 2, d, 2 * f)
    u2 = wu.reshape(e // 2, 2, d, f).transpose(0, 2, 1, 3).reshape(e // 2, d, 2 * f)
    return jnp.concatenate([g2, u2], axis=2).astype(BF16), wd.reshape(e // 2, 2 * f, d).astype(BF16)


def _moe_ple(xb, xf, p, rw_t, rb, pg, pp, wgu, wd, g, b):
    T = xb.shape[0]
    tm = min(512, T)
    d = D_MODEL
    rowblk = lambda w: pl.BlockSpec((tm, w), lambda i: (i, 0))
    once = pl.Buffered(1)
    full = lambda a: pl.BlockSpec(a.shape, lambda i: (0,) * a.ndim, pipeline_mode=once)
    g2, b2 = g.reshape(1, d), b.reshape(1, d)
    return pl.pallas_call(
        _moe_kernel,
        grid=(T // tm,),
        in_specs=[rowblk(d), rowblk(d), rowblk(PLE_DIM), full(rw_t), full(rb), full(pg), full(pp),
                  full(wgu), full(wd), full(g2), full(b2)],
        out_specs=[rowblk(d), rowblk(d)],
        out_shape=[jax.ShapeDtypeStruct((T, d), F32), jax.ShapeDtypeStruct((T, d), BF16)],
        scratch_shapes=[pltpu.VMEM((tm, d), F32), pltpu.VMEM((tm, LANES), F32)],
        compiler_params=_cparams(("parallel",)),
        name="moe_ple_ln",
    )(xb, xf, p, rw_t, rb, pg, pp, wgu, wd, g2, b2)


def _overlap_matrix(nc):
    n = jnp.arange(nc)[None, :]
    m = jnp.arange(SEL_LANES)[:, None]
    ratio = SEL_BLOCK // CMP_STRIDE
    ov = jnp.zeros((SEL_LANES, nc), F32)
    for k in range(CMP_BLOCK // CMP_STRIDE):
        ov = ov + ((n + k) // ratio == m).astype(F32)
    return ov.astype(BF16)


def _expand_matrix(S):
    c = jnp.arange(S)[:, None]
    m = jnp.arange(SEL_LANES)[None, :]
    return jnp.where(c // SEL_BLOCK == m, -MASK_VALUE, 0.0).astype(BF16)


def _pad_w2(w2):
    out = jnp.zeros((NSA_KV_GROUPS, CMP_HIDDEN, HEAD_SLOT), F32)
    for g in range(NSA_KV_GROUPS):
        out = out.at[g, :, g * NSA_HEAD_DIM:(g + 1) * NSA_HEAD_DIM].set(w2)
    return out.astype(BF16)


def _pad_proj_nsa(w):
    d = w.shape[1]
    w4 = w.reshape(NSA_KV_GROUPS, NSA_HPG, NSA_HEAD_DIM, d)
    out = jnp.zeros((NSA_KV_GROUPS, NSA_HPG, HEAD_SLOT, d), F32)
    for g in range(NSA_KV_GROUPS):
        out = out.at[g, :, g * NSA_HEAD_DIM:(g + 1) * NSA_HEAD_DIM].set(w4[g])
    return out.reshape(NSA_HEADS * HEAD_SLOT, d).astype(BF16)


def _cmp_rows(kv, B, S):
    nr = S // CMP_STRIDE
    kv = kv.reshape(B, nr, CMP_STRIDE, NSA_KV_GROUPS, NSA_HEAD_DIM)
    return kv.transpose(0, 3, 1, 2, 4).reshape(B, NSA_KV_GROUPS, nr, CMP_STRIDE * NSA_HEAD_DIM)


def kernel(x, p, w_in, nsa_pe_k, nsa_w1_k, nsa_w2_k, nsa_pe_v, nsa_w1_v, nsa_w2_v, ssd_conv_w, ssd_conv_b, ssd_dt_bias, ssd_a_log, ssd_d, ssd_norm_w, lru_conv_w, lru_conv_b, lru_wa, lru_ba, lru_wx, lru_bx, lru_lambda, proj_nsa, proj_ssd, proj_lru, w_out, ln1_g, ln1_b, router_w, router_b, exp_w_gate, exp_w_up, exp_w_down, ple_w_gate, ple_w_proj, ln2_g, ln2_b):
    B, S, d = x.shape
    T = B * S
    depth = w_in.shape[0]
    assert d == D_MODEL and S % SEL_TK == 0 and S >= WIN_KEYS and S // SEL_BLOCK <= SEL_LANES
    nr = S // CMP_STRIDE
    ov = _overlap_matrix(nr)
    emat = _expand_matrix(S)
    rw_t = router_w.T.astype(BF16)
    rb = router_b.reshape(N_EXPERTS, 1).astype(F32)

    xf = x.reshape(T, d)
    xb = xf.astype(BF16)
    for i in range(depth):
        proj = _in_proj(xb, _prep_w_in(w_in[i]))

        kv0 = C_KV
        k_cmp = proj[:, kv0:kv0 + NSA_KV_W]
        v_cmp = proj[:, kv0 + NSA_KV_W:kv0 + 2 * NSA_KV_W]
        pe_rows = lambda pe: jnp.broadcast_to(pe.reshape(1, -1), (8, CMP_BLOCK * NSA_HEAD_DIM)).astype(BF16)
        kc, vc = _nsa_compress(_cmp_rows(k_cmp, B, S), _cmp_rows(v_cmp, B, S),
                               pe_rows(nsa_pe_k[i]), pe_rows(nsa_pe_v[i]),
                               nsa_w1_k[i].astype(BF16), nsa_w1_v[i].astype(BF16),
                               _pad_w2(nsa_w2_k[i]), _pad_w2(nsa_w2_v[i]))
        ocmpt, sel = _nsa_cmp_attn(proj, kc, vc.transpose(0, 1, 3, 2), ov, B, S)
        o_nsa = _nsa_sel_win(proj, sel, emat, ocmpt, B, S)

        dt_col = proj[:, C_SMALL + GATE_W:C_SMALL + GATE_W + SSD_HEADS].astype(F32)
        o_ssd = _ssd_mixer(proj, dt_col, dt_col.T, ssd_conv_w[i], ssd_conv_b[i], ssd_dt_bias[i],
                           ssd_a_log[i], ssd_d[i], ssd_norm_w[i], B, S)
        o_lru = _lru_mixer(proj, lru_conv_w[i], lru_conv_b[i], lru_wa[i], lru_ba[i], lru_wx[i],
                           lru_bx[i], lru_lambda[i], B, S)

        xf, xb = _merge(xf, proj, o_nsa, o_ssd, o_lru, _pad_proj_nsa(proj_nsa[i]),
                        proj_ssd[i].astype(BF16), proj_lru[i].astype(BF16), w_out[i].astype(BF16),
                        ln1_g[i], ln1_b[i])
        xf, xb = _moe_ple(xb, xf, p[i].reshape(T, PLE_DIM), rw_t, rb,
                          ple_w_gate[i].astype(BF16), ple_w_proj[i].astype(BF16),
                          *_pair_expert_weights(exp_w_gate[i], exp_w_up[i], exp_w_down[i]),
                          ln2_g[i], ln2_b[i])
    return xf.reshape(B, S, d)
```

```python
import functools
import math

import jax
import jax.numpy as jnp
from jax import lax
from jax.experimental import pallas as pl
from jax.experimental.pallas import tpu as pltpu

F32 = jnp.float32
BF16 = jnp.bfloat16

D_MODEL = 1024
PLE_DIM = 256
NSA_HEADS = 8
NSA_KV_GROUPS = 2
NSA_HEAD_DIM = 64
NSA_HPG = NSA_HEADS // NSA_KV_GROUPS
NSA_Q_W = NSA_HEADS * NSA_HEAD_DIM
NSA_KV_W = NSA_KV_GROUPS * NSA_HEAD_DIM
CMP_BLOCK = 32
CMP_STRIDE = 16
CMP_HIDDEN = 256
SEL_BLOCK = 64
SEL_TOPN = 16
WINDOW = 512
FORCE_SCORE = 1e4
MASK_VALUE = -1e30
LOG2E = 1.4426950408889634
SSD_HEADS = 8
SSD_HEAD_DIM = 64
SSD_INNER = SSD_HEADS * SSD_HEAD_DIM
SSD_GROUPS = 2
SSD_STATE = 64
SSD_CHUNK = 128
SSD_XBC_W = SSD_INNER + 2 * SSD_GROUPS * SSD_STATE
CONV_WIDTH = 4
LRU_WIDTH = 512
LRU_BLOCKS = 8
LRU_BLOCK_DIM = LRU_WIDTH // LRU_BLOCKS
LRU_C = 8.0
N_EXPERTS = 16
N_EXPERT_GROUPS = 4
EXPERTS_PER_GROUP = N_EXPERTS // N_EXPERT_GROUPS
TOP_K = 2
D_EXPERT = 256
DEPTH = 2
ALPHA = (2 * DEPTH) ** 0.25
LN_EPS = 1e-5
RMS_EPS = 1e-5
IN_SIZES = (NSA_Q_W, NSA_KV_W, NSA_KV_W, NSA_KV_W, NSA_KV_W, NSA_KV_W, NSA_KV_W, NSA_HEADS * 3,
            SSD_INNER, SSD_XBC_W, SSD_HEADS, LRU_WIDTH, LRU_WIDTH, 3 * D_MODEL)

LANES = 128
SEL_LANES = 128
HEAD_SLOT = 128

C_MERGE = 0
C_QEXT = 3072
C_SSDZ = 4096
C_LRUX = 4608
C_LRUY = 5120
C_KSEL = 5632
C_KWIN = 5760
C_SMALL = 5888
C_XBC = 6144
PROJ_W = 6912
PROJ_TN = 2304
GATE_W = NSA_HEADS * 3

VMEM_LIMIT = 56 * 1024 * 1024


def _cparams(sem):
    return pltpu.CompilerParams(dimension_semantics=sem, vmem_limit_bytes=VMEM_LIMIT)


def _sigmoid(x):
    return 1.0 / (1.0 + jnp.exp(-x))


def _softplus(x):
    return jnp.maximum(x, 0.0) + jnp.log(1.0 + jnp.exp(-jnp.abs(x)))


def _gelu_tanh(x):
    c = math.sqrt(2.0 / math.pi)
    return 0.5 * x * (1.0 + jnp.tanh(c * (x + 0.044715 * (x * x * x))))


def _dot(a, b):
    return jnp.dot(a, b, preferred_element_type=F32)


def _dot_nt(a, b):
    return lax.dot_general(a, b, (((1,), (1,)), ((), ())), preferred_element_type=F32)


def _dot_tn(a, b):
    return lax.dot_general(a, b, (((0,), (0,)), ((), ())), preferred_element_type=F32)


def _dot_f32(a, b):
    return jnp.dot(a, b, preferred_element_type=F32, precision=lax.Precision.HIGHEST)


def _layer_norm(v, g, b):
    mu = jnp.mean(v, axis=-1, keepdims=True)
    vc = v - mu
    var = jnp.mean(vc * vc, axis=-1, keepdims=True)
    return vc * lax.rsqrt(var + LN_EPS) * g + b


def _matmul_kernel(x_ref, w_ref, o_ref):
    o_ref[...] = _dot(x_ref[...], w_ref[...]).astype(o_ref.dtype)


def _in_proj(xb, w):
    T, K = xb.shape
    N = w.shape[1]
    tm = min(1024, T)
    tn = PROJ_TN
    return pl.pallas_call(
        _matmul_kernel,
        grid=(N // tn, T // tm),
        in_specs=[pl.BlockSpec((tm, K), lambda j, i: (i, 0)),
                  pl.BlockSpec((K, tn), lambda j, i: (0, j))],
        out_specs=pl.BlockSpec((tm, tn), lambda j, i: (i, j)),
        out_shape=jax.ShapeDtypeStruct((T, N), BF16),
        compiler_params=_cparams(("parallel", "parallel")),
        name="in_proj",
    )(xb, w)


def _split_w_in(w):
    offs = [0]
    for s in IN_SIZES:
        offs.append(offs[-1] + s)
    return [w[:, offs[k]:offs[k + 1]] for k in range(len(IN_SIZES))]


def _prep_w_in(w):
    pc = _split_w_in(w)
    d = w.shape[0]
    q = pc[0].reshape(d, NSA_KV_GROUPS, NSA_HPG, NSA_HEAD_DIM) * (NSA_HEAD_DIM ** -0.5 * LOG2E)
    q_ext = jnp.zeros((d, NSA_KV_GROUPS, NSA_HPG, HEAD_SLOT), F32)
    for g in range(NSA_KV_GROUPS):
        q_ext = q_ext.at[:, g, :, g * NSA_HEAD_DIM:(g + 1) * NSA_HEAD_DIM].set(q[:, g])
    q_ext = q_ext.reshape(d, NSA_HEADS * HEAD_SLOT)
    small = jnp.pad(pc[7], ((0, 0), (0, C_XBC - C_SMALL - GATE_W)))
    out = jnp.concatenate([pc[13], q_ext, pc[8], pc[11], pc[12], pc[3], pc[5], small, pc[9]], axis=1)
    assert out.shape[1] == PROJ_W
    return out.astype(BF16)


VT_ROWS = LANES + 16


def _aux_proj_kernel(x_ref, wc_ref, wvt_ref, wdt_ref, wdtt_ref, cmp_ref, vst_ref, vwt_ref, dtc_ref, dtr_ref):
    x = x_ref[...]
    tm = x.shape[0]
    cmp_ref[...] = _dot(x, wc_ref[...]).astype(cmp_ref.dtype)
    vt = _dot_nt(wvt_ref[...], x)
    ones = jnp.ones((VT_ROWS - LANES, LANES), vst_ref.dtype)
    for u in range(tm // LANES):
        for k, o_ref in enumerate((vst_ref, vwt_ref)):
            o_ref[u, 0:LANES, :] = vt[k * LANES:(k + 1) * LANES, u * LANES:(u + 1) * LANES].astype(o_ref.dtype)
            o_ref[u, LANES:VT_ROWS, :] = ones
    xf = x.astype(F32)
    dtc_ref[...] = _dot(xf, wdt_ref[...])
    dtr_ref[...] = _dot_nt(wdtt_ref[...], xf)


def _aux_proj(xb, w):
    T, K = xb.shape
    pc = _split_w_in(w)
    wc = jnp.concatenate([pc[1], pc[2]], axis=1).astype(BF16)
    wvt = jnp.concatenate([pc[4], pc[6]], axis=1).T.astype(BF16)
    wdt = pc[10]
    wdtt = pc[10].T
    tm = min(1024, T)
    full = lambda a: pl.BlockSpec(a.shape, lambda i: (0, 0))
    vt_shape = jax.ShapeDtypeStruct((T // LANES, VT_ROWS, LANES), BF16)
    vt_spec = pl.BlockSpec((tm // LANES, VT_ROWS, LANES), lambda i: (i, 0, 0))
    return pl.pallas_call(
        _aux_proj_kernel,
        grid=(T // tm,),
        in_specs=[pl.BlockSpec((tm, K), lambda i: (i, 0)), full(wc), full(wvt), full(wdt), full(wdtt)],
        out_specs=[pl.BlockSpec((tm, 2 * NSA_KV_W), lambda i: (i, 0)), vt_spec, vt_spec,
                   pl.BlockSpec((tm, SSD_HEADS), lambda i: (i, 0)),
                   pl.BlockSpec((SSD_HEADS, tm), lambda i: (0, i))],
        out_shape=[jax.ShapeDtypeStruct((T, 2 * NSA_KV_W), BF16), vt_shape, vt_shape,
                   jax.ShapeDtypeStruct((T, SSD_HEADS), F32), jax.ShapeDtypeStruct((SSD_HEADS, T), F32)],
        compiler_params=_cparams(("parallel",)),
        name="aux_proj",
    )(xb, wc, wvt, wdt, wdtt)


CMP_PIECES = 2 * NSA_KV_GROUPS


def _compress_kernel(r_ref, pe_ref, w1_ref, w2k_ref, w2vt_ref, kc_ref, vct_ref):
    r = r_ref[0]
    nr = r.shape[0]
    u = _dot(r, w1_ref[0])
    v = _dot(r, w1_ref[1])
    c = _dot(pe_ref[0], w1_ref[0]) + _dot(pe_ref[1], w1_ref[1])
    hid = u + pltpu.roll(v, nr - 1, axis=0)
    hid = (hid.reshape(nr // 8, 8, hid.shape[1]) + c[None]).reshape(nr, hid.shape[1])
    act = _gelu_tanh(hid).astype(BF16)
    for g in range(NSA_KV_GROUPS):
        kp, vp = g, NSA_KV_GROUPS + g
        kc_ref[0, g] = _dot(act[:, kp * CMP_HIDDEN:(kp + 1) * CMP_HIDDEN], w2k_ref[g]).astype(kc_ref.dtype)
        vct = _dot_nt(w2vt_ref[g], act[:, vp * CMP_HIDDEN:(vp + 1) * CMP_HIDDEN])
        vct_ref[0, g] = vct.astype(vct_ref.dtype)


def _cmp_first_layer(w1k, w1v, pek, pev):
    half = CMP_BLOCK // 2
    eye = jnp.eye(CMP_PIECES, dtype=F32)
    w = jnp.stack([w1k, w1k, w1v, w1v]).reshape(CMP_PIECES, 2, half, NSA_HEAD_DIM, CMP_HIDDEN)
    big = jnp.einsum("pald,pq->alpdq", w.reshape(CMP_PIECES, 2, half, NSA_HEAD_DIM * CMP_HIDDEN), eye)
    big = big.reshape(2, half, CMP_PIECES, NSA_HEAD_DIM, CMP_HIDDEN, CMP_PIECES)
    big = big.transpose(0, 1, 2, 3, 5, 4).reshape(2, half * CMP_PIECES * NSA_HEAD_DIM, CMP_PIECES * CMP_HIDDEN)
    pe = jnp.stack([pek, pek, pev, pev]).reshape(CMP_PIECES, 2, half, NSA_HEAD_DIM)
    pe = pe.transpose(1, 2, 0, 3).reshape(2, 1, half * CMP_PIECES * NSA_HEAD_DIM)
    return big.astype(BF16), jnp.broadcast_to(pe, (2, 8, pe.shape[-1])).astype(BF16)


def _nsa_compress(cmp, w1big, pebig, w2k, w2vt, B, S):
    NR = S // CMP_STRIDE
    W = CMP_STRIDE * 2 * NSA_KV_W
    r = cmp.reshape(B, NR, W)
    once = pl.Buffered(1)
    full = lambda a: pl.BlockSpec(a.shape, lambda b: (0,) * a.ndim, pipeline_mode=once)
    return pl.pallas_call(
        _compress_kernel,
        grid=(B,),
        in_specs=[pl.BlockSpec((1, NR, W), lambda b: (b, 0, 0)), full(pebig), full(w1big), full(w2k), full(w2vt)],
        out_specs=[pl.BlockSpec((1, NSA_KV_GROUPS, NR, HEAD_SLOT), lambda b: (b, 0, 0, 0)),
                   pl.BlockSpec((1, NSA_KV_GROUPS, HEAD_SLOT, NR), lambda b: (b, 0, 0, 0))],
        out_shape=[jax.ShapeDtypeStruct((B, NSA_KV_GROUPS, NR, HEAD_SLOT), BF16),
                   jax.ShapeDtypeStruct((B, NSA_KV_GROUPS, HEAD_SLOT, NR), BF16)],
        compiler_params=_cparams(("parallel",)),
        name="nsa_compress",
    )(r, pebig, w1big, w2k, w2vt)


def _stack_heads(q):
    return jnp.concatenate([q[:, j * HEAD_SLOT:(j + 1) * HEAD_SLOT] for j in range(NSA_HPG)], axis=0)


def _cmp_attn_kernel(q_ref, kc_ref, vct_ref, ovt_ref, gate_ref, ocmpt_ref, sel_ref, gt_ref, *, tq):
    g = pl.program_id(1)
    i = pl.program_id(2)
    q2 = _stack_heads(q_ref[...])
    kc = kc_ref[0, 0]
    vct = vct_ref[0, 0]
    nc = kc.shape[0]
    sts = [_dot_nt(kc, q2[j * tq:(j + 1) * tq]) for j in range(NSA_HPG)]
    n = lax.broadcasted_iota(jnp.int32, (nc, tq), 0)
    t = i * tq + lax.broadcasted_iota(jnp.int32, (nc, tq), 1)
    mask = n * CMP_STRIDE + (CMP_BLOCK - 1) <= t
    gt_ref[...] = _sigmoid(gate_ref[...].astype(F32)).T
    ps = None
    for j in range(NSA_HPG):
        s = jnp.where(mask, sts[j], MASK_VALUE)
        m = jnp.max(s, axis=0, keepdims=True)
        e = jnp.where(mask, jnp.exp2(s - m), 0.0)
        den = jnp.maximum(jnp.sum(e, axis=0, keepdims=True), 1e-30)
        p = e * (1.0 / den)
        ps = p if ps is None else ps + p
        gate = gt_ref[pl.ds(g * (NSA_HPG * 3) + j * 3, 1), :]
        ocmpt_ref[0, 0, 0, :, j * tq:(j + 1) * tq] = (gate * _dot(vct, p.astype(BF16))).astype(ocmpt_ref.dtype)
    ovt = ovt_ref[...]
    hi = ps.astype(BF16)
    r1 = ps - hi.astype(F32)
    mid = r1.astype(BF16)
    lo = (r1 - mid.astype(F32)).astype(BF16)
    imp = _dot(ovt, hi) + _dot(ovt, mid) + _dot(ovt, lo)
    blk = lax.broadcasted_iota(jnp.int32, imp.shape, 0)
    tt = i * tq + lax.broadcasted_iota(jnp.int32, imp.shape, 1)
    cur = tt // SEL_BLOCK
    forced = (blk == 0) | (blk == cur) | (blk == cur - 1)
    causal = blk * SEL_BLOCK <= tt
    v0 = jnp.where(forced, FORCE_SCORE, jnp.where(causal, imp, -1.0))

    blk1 = blk[:, 0:LANES]

    def pick(_, tiles):
        out = []
        for v in tiles:
            mx = jnp.max(v, axis=0, keepdims=True)
            idx = jnp.min(jnp.where(v == mx, blk1, SEL_LANES), axis=0, keepdims=True)
            out.append(jnp.where(blk1 == idx, -jnp.inf, v))
        return tuple(out)

    tiles = lax.fori_loop(0, SEL_TOPN, pick, tuple(v0[:, u * LANES:(u + 1) * LANES] for u in range(tq // LANES)))
    for u, v in enumerate(tiles):
        sel_ref[0, 0, u * LANES:(u + 1) * LANES, :] = jnp.where(v == -jnp.inf, 1.0, 0.0).T.astype(sel_ref.dtype)


def _nsa_cmp_attn(proj, kc, vct, ovt, B, S):
    tq = NSA_TQ
    nq = S // tq
    G = NSA_KV_GROUPS
    NC = kc.shape[2]
    qw = NSA_HPG * HEAD_SLOT
    qblk = C_QEXT // qw
    kern = functools.partial(_cmp_attn_kernel, tq=tq)
    return pl.pallas_call(
        kern,
        grid=(B, G, nq),
        in_specs=[pl.BlockSpec((tq, qw), lambda b, g, i: (b * nq + i, qblk + g)),
                  pl.BlockSpec((1, 1, NC, HEAD_SLOT), lambda b, g, i: (b, g, 0, 0)),
                  pl.BlockSpec((1, 1, HEAD_SLOT, NC), lambda b, g, i: (b, g, 0, 0)),
                  pl.BlockSpec(ovt.shape, lambda b, g, i: (0, 0)),
                  pl.BlockSpec((tq, LANES), lambda b, g, i: (b * nq + i, C_SMALL // LANES))],
        out_specs=[pl.BlockSpec((1, 1, 1, HEAD_SLOT, NSA_HPG * tq), lambda b, g, i: (b, g, i, 0, 0)),
                   pl.BlockSpec((1, 1, tq, SEL_LANES), lambda b, g, i: (b, g, i, 0))],
        out_shape=[jax.ShapeDtypeStruct((B, G, nq, HEAD_SLOT, NSA_HPG * tq), BF16),
                   jax.ShapeDtypeStruct((B, G, S, SEL_LANES), BF16)],
        scratch_shapes=[pltpu.VMEM((LANES, tq), F32)],
        compiler_params=_cparams(("parallel", "parallel", "parallel")),
        name="nsa_cmp_attn",
    )(proj, kc, vct, ovt, proj)


SEL_TK = 512
NSA_TQ = 256
WIN_KEYS = WINDOW + NSA_TQ


HALF = 256
NCH = NSA_HPG * NSA_TQ // HALF


def _sel_win_kernel(q_ref, ksel_ref, vselt_ref, kwin_ref, vwint_ref, sel_ref, et_ref, ocmpt_ref,
                    gate_ref, dbias_ref, wbias_ref, o_ref, s_ref, m_ref, acc_ref, gt_ref):
    g = pl.program_id(1)
    i = pl.program_id(2)
    tq = NSA_TQ
    t0 = i * tq
    q2 = _stack_heads(q_ref[...])
    selm1 = sel_ref[0, 0] - 1.0
    qx = jnp.concatenate([q2, jnp.concatenate([selm1] * NSA_HPG, axis=0)], axis=1)
    qxh = [qx[c * HALF:(c + 1) * HALF] for c in range(NCH)]
    q2h = [q2[c * HALF:(c + 1) * HALF] for c in range(NCH)]
    kpb = SEL_TK // LANES

    wblk = jnp.maximum((t0 - WINDOW) // LANES, 0)
    start = pl.multiple_of(wblk * LANES, LANES)
    kw = kwin_ref[pl.ds(start, WIN_KEYS), :]
    vwt = jnp.concatenate([vwint_ref[0, wblk + u] for u in range(WIN_KEYS // LANES)], axis=1)
    wbias = wbias_ref[jnp.minimum(i, WINDOW // tq)].astype(F32)
    sws = [_dot_nt(kw, q2h[c]) for c in range(NCH)]

    def scores_into(kv):
        off = pl.multiple_of(kv * SEL_TK, SEL_TK)
        kx = jnp.concatenate([ksel_ref[pl.ds(off, SEL_TK), :], et_ref[pl.ds(off, SEL_TK), :]], axis=1)
        for c in range(NCH):
            s_ref[:, c * HALF:(c + 1) * HALF] = _dot_nt(kx, qxh[c])

    def update(vt, s_chunks):
        for c in range(NCH):
            cols = slice(c * HALF, (c + 1) * HALF)
            s = s_chunks[c]
            m_old = m_ref[:, cols]
            m_new = jnp.maximum(m_old, jnp.max(s, axis=0, keepdims=True))
            p = jnp.exp2(s - m_new).astype(BF16)
            acc_ref[:, cols] = jnp.exp2(m_old - m_new) * acc_ref[:, cols] + _dot(vt, p)
            m_ref[:, cols] = m_new

    def sel_values(kv):
        return jnp.concatenate([vselt_ref[0, kv * kpb + u] for u in range(kpb)], axis=1)

    def load_scores():
        return [s_ref[:, c * HALF:(c + 1) * HALF] for c in range(NCH)]

    m_ref[...] = jnp.full(m_ref.shape, MASK_VALUE, F32)
    acc_ref[...] = jnp.zeros(acc_ref.shape, F32)
    kd = t0 // SEL_TK
    scores_into(0)
    ot_win = []
    for c in range(NCH):
        sw = sws[c] + wbias
        mw = jnp.max(sw, axis=0, keepdims=True)
        accw = _dot(vwt, jnp.exp2(sw - mw).astype(BF16))
        ot_win.append(accw[0:LANES] * (1.0 / jnp.maximum(accw[LANES:LANES + 1], 1e-30)))

    def body(kv, carry):
        s_chunks = load_scores()
        scores_into(kv + 1)
        update(sel_values(kv), s_chunks)
        return carry

    lax.fori_loop(0, kd, body, 0)
    dbias = dbias_ref[i % (SEL_TK // tq)].astype(F32)
    update(sel_values(kd), [sc + dbias for sc in load_scores()])
    acc = acc_ref[...]
    ot_slc = acc[0:LANES] * (1.0 / jnp.maximum(acc[LANES:LANES + 1], 1e-30))

    gt_ref[...] = _sigmoid(gate_ref[...].astype(F32)).T
    for j in range(NSA_HPG):
        base = g * (NSA_HPG * 3) + j * 3
        ot = (ocmpt_ref[0, 0, 0, :, j * tq:(j + 1) * tq].astype(F32)
              + gt_ref[pl.ds(base + 1, 1), :] * ot_slc[:, j * tq:(j + 1) * tq]
              + gt_ref[pl.ds(base + 2, 1), :] * ot_win[j])
        o_ref[:, j * HEAD_SLOT:(j + 1) * HEAD_SLOT] = ot.T.astype(o_ref.dtype)


def _diag_bias():
    r = jnp.arange(SEL_TK)[None, :, None]
    c = jnp.arange(NSA_TQ)[None, None, :]
    off = (jnp.arange(SEL_TK // NSA_TQ) * NSA_TQ)[:, None, None]
    return jnp.where(r <= off + c, 0.0, MASK_VALUE).astype(BF16)


def _window_bias():
    r = jnp.arange(WIN_KEYS)[None, :, None]
    c = jnp.arange(NSA_TQ)[None, None, :]
    off = jnp.minimum(jnp.arange(WINDOW // NSA_TQ + 1) * NSA_TQ, WINDOW)[:, None, None]
    diff = off + c - r
    return jnp.where((diff >= 0) & (diff < WINDOW), 0.0, MASK_VALUE).astype(BF16)


def _nsa_sel_win(proj, vsel_t, vwin_t, sel, et, ocmpt, B, S):
    T = B * S
    tq = NSA_TQ
    assert HALF == tq
    dbias = _diag_bias()
    wbias = _window_bias()
    nq = S // tq
    G = NSA_KV_GROUPS
    qw = NSA_HPG * HEAD_SLOT
    qblk = C_QEXT // qw
    vsel_t = vsel_t.reshape(B, S // LANES, VT_ROWS, LANES)
    vwin_t = vwin_t.reshape(B, S // LANES, VT_ROWS, LANES)
    kv_spec = lambda c: pl.BlockSpec((S, LANES), lambda b, g, i: (b, c // LANES))
    vt_spec = pl.BlockSpec((1, S // LANES, VT_ROWS, LANES), lambda b, g, i: (b, 0, 0, 0))
    return pl.pallas_call(
        _sel_win_kernel,
        grid=(B, G, nq),
        in_specs=[pl.BlockSpec((tq, qw), lambda b, g, i: (b * nq + i, qblk + g)),
                  kv_spec(C_KSEL), vt_spec, kv_spec(C_KWIN), vt_spec,
                  pl.BlockSpec((1, 1, tq, SEL_LANES), lambda b, g, i: (b, g, i, 0)),
                  pl.BlockSpec(et.shape, lambda b, g, i: (0, 0)),
                  pl.BlockSpec((1, 1, 1, HEAD_SLOT, NSA_HPG * tq), lambda b, g, i: (b, g, i, 0, 0)),
                  pl.BlockSpec((tq, LANES), lambda b, g, i: (b * nq + i, C_SMALL // LANES)),
                  pl.BlockSpec(dbias.shape, lambda b, g, i: (0, 0, 0)),
                  pl.BlockSpec(wbias.shape, lambda b, g, i: (0, 0, 0))],
        out_specs=pl.BlockSpec((tq, qw), lambda b, g, i: (b * nq + i, g)),
        out_shape=jax.ShapeDtypeStruct((T, NSA_HEADS * HEAD_SLOT), BF16),
        scratch_shapes=[pltpu.VMEM((SEL_TK, NSA_HPG * tq), F32),
                        pltpu.VMEM((1, NSA_HPG * tq), F32),
                        pltpu.VMEM((VT_ROWS, NSA_HPG * tq), F32),
                        pltpu.VMEM((LANES, tq), F32)],
        compiler_params=_cparams(("parallel", "parallel", "parallel")),
        name="nsa_sel_win",
    )(proj, proj, vsel_t, proj, vwin_t, sel, et, ocmpt, proj, dbias, wbias)


TAIL_ROWS = 8


def _causal_conv(x, tail_ref, w, b):
    L = x.shape[0]
    xx = jnp.concatenate([tail_ref[...], x], axis=0)
    y = b + w[CONV_WIDTH - 1:CONV_WIDTH] * x
    for k in range(1, CONV_WIDTH):
        y = y + w[CONV_WIDTH - 1 - k:CONV_WIDTH - k] * xx[TAIL_ROWS - k:TAIL_ROWS - k + L]
    tail_ref[...] = x[L - TAIL_ROWS:L]
    return y


def _ssd_kernel(z_ref, xbc_ref, dtc_ref, dtr_ref, cw_ref, cb_ref, dtbc_ref, dtbr_ref, alc_ref, alr_ref,
                dsk_ref, nw_ref, o_ref, state_ref, tail_ref, y_ref):
    c = pl.program_id(1)
    L = SSD_CHUNK
    P = SSD_HEAD_DIM
    N = SSD_STATE

    @pl.when(c == 0)
    def _():
        state_ref[...] = jnp.zeros_like(state_ref)
        tail_ref[...] = jnp.zeros_like(tail_ref)

    conv = _causal_conv(xbc_ref[...].astype(F32), tail_ref, cw_ref[...], cb_ref[...])
    xbc = conv * _sigmoid(conv)
    xs = xbc[:, 0:SSD_INNER]
    bm = xbc[:, SSD_INNER:SSD_INNER + SSD_GROUPS * N]
    cm = xbc[:, SSD_INNER + SSD_GROUPS * N:SSD_INNER + 2 * SSD_GROUPS * N]

    dt_c = _softplus(dtc_ref[...] + dtbc_ref[...])
    dt_r = _softplus(dtr_ref[...] + dtbr_ref[...])
    a_c = dt_c * (-jnp.exp(alc_ref[...]))
    a_r = dt_r * (-jnp.exp(alr_ref[...]))
    ii = lax.broadcasted_iota(jnp.int32, (L, L), 0)
    jj = lax.broadcasted_iota(jnp.int32, (L, L), 1)
    tri = ii >= jj
    acum_c = _dot_f32(tri.astype(F32), a_c)
    acum_r = _dot_f32(a_r, (ii <= jj).astype(F32))

    for g in range(SSD_GROUPS):
        bg = bm[:, g * N:(g + 1) * N]
        cg = cm[:, g * N:(g + 1) * N]
        cgb = cg.astype(BF16)
        cb = _dot_nt(cgb, bg.astype(BF16))
        for j in range(SSD_HEADS // SSD_GROUPS):
            h = g * (SSD_HEADS // SSD_GROUPS) + j
            ac = acum_c[:, h:h + 1]
            ar = acum_r[h:h + 1, :]
            a_last = acum_c[L - 1:L, h:h + 1]
            xh_raw = xs[:, h * P:(h + 1) * P]
            xh = (xh_raw * dt_c[:, h:h + 1]).astype(BF16)
            lmat = jnp.where(tri, jnp.exp(ac - ar), 0.0)
            y = _dot((cb * lmat).astype(BF16), xh)
            prev = state_ref[h]
            y = y + _dot(cgb, prev.astype(BF16)) * jnp.exp(ac)
            bd = (bg * jnp.exp(a_last - ac)).astype(BF16)
            state_ref[h] = jnp.exp(a_last) * prev + _dot_tn(bd, xh)
            y_ref[:, h * P:(h + 1) * P] = y + dsk_ref[:, h * P:(h + 1) * P] * xh_raw

    zf = z_ref[...].astype(F32)
    yg = y_ref[...] * (zf * _sigmoid(zf))
    ms = jnp.mean(yg * yg, axis=-1, keepdims=True)
    o_ref[...] = (yg * lax.rsqrt(ms + RMS_EPS) * nw_ref[...]).astype(o_ref.dtype)


def _ssd_mixer(proj, dt_col, dt_row, conv_w, conv_b, dt_bias, a_log, d_skip, norm_w, B, S):
    T = B * S
    L = SSD_CHUNK
    nc = S // L
    H = SSD_HEADS
    full = lambda a: pl.BlockSpec(a.shape, lambda b, c: (0, 0))
    cb2 = conv_b.reshape(1, -1)
    dtb_c = dt_bias.reshape(1, H)
    dtb_r = dt_bias.reshape(H, 1)
    al_c = a_log.reshape(1, H)
    al_r = a_log.reshape(H, 1)
    dsk = jnp.repeat(d_skip, SSD_HEAD_DIM).reshape(1, SSD_INNER)
    nw = norm_w.reshape(1, SSD_INNER)
    return pl.pallas_call(
        _ssd_kernel,
        grid=(B, nc),
        in_specs=[pl.BlockSpec((L, SSD_INNER), lambda b, c: (b * nc + c, C_SSDZ // SSD_INNER)),
                  pl.BlockSpec((L, SSD_XBC_W), lambda b, c: (b * nc + c, C_XBC // SSD_XBC_W)),
                  pl.BlockSpec((L, H), lambda b, c: (b * nc + c, 0)),
                  pl.BlockSpec((H, L), lambda b, c: (0, b * nc + c)),
                  full(conv_w), full(cb2), full(dtb_c), full(dtb_r), full(al_c), full(al_r),
                  full(dsk), full(nw)],
        out_specs=pl.BlockSpec((L, SSD_INNER), lambda b, c: (b * nc + c, 0)),
        out_shape=jax.ShapeDtypeStruct((T, SSD_INNER), BF16),
        scratch_shapes=[pltpu.VMEM((H, SSD_STATE, SSD_HEAD_DIM), F32),
                        pltpu.VMEM((TAIL_ROWS, SSD_XBC_W), F32),
                        pltpu.VMEM((L, SSD_INNER), F32)],
        compiler_params=_cparams(("parallel", "arbitrary")),
        name="ssd_mixer",
    )(proj, proj, dt_col, dt_row, conv_w, cb2, dtb_c, dtb_r, al_c, al_r, dsk, nw)


def _lru_kernel(x_ref, y_ref, cw_ref, cb_ref, wa_ref, ba_ref, wx_ref, bx_ref, lam_ref, o_ref,
                h_ref, tail_ref, *, tc):
    c = pl.program_id(1)

    @pl.when(c == 0)
    def _():
        h_ref[...] = jnp.zeros_like(h_ref)
        tail_ref[...] = jnp.zeros_like(tail_ref)

    xr = _causal_conv(x_ref[...].astype(F32), tail_ref, cw_ref[...], cb_ref[...])
    xrb = xr.astype(BF16)
    r = _sigmoid(_dot(xrb, wa_ref[...]) + ba_ref[...])
    ig = _sigmoid(_dot(xrb, wx_ref[...]) + bx_ref[...])
    log_a = -LRU_C * r * _softplus(-lam_ref[...])
    a = jnp.exp(log_a)
    b = jnp.sqrt(1.0 - jnp.exp(2.0 * log_a)) * (ig * xr)
    row = lax.broadcasted_iota(jnp.int32, a.shape, 0)
    k = 1
    while k < tc:
        keep = row >= k
        a_s = jnp.where(keep, pltpu.roll(a, k, axis=0), 1.0)
        b_s = jnp.where(keep, pltpu.roll(b, k, axis=0), 0.0)
        b = a * b_s + b
        a = a * a_s
        k *= 2
    h = a * h_ref[0:1, :] + b
    h_ref[...] = jnp.broadcast_to(h[tc - 1:tc, :], h_ref.shape)
    o_ref[...] = (h * _gelu_tanh(y_ref[...].astype(F32))).astype(o_ref.dtype)


def _block_diag(w):
    nb, c, d = w.shape
    eye = jnp.eye(nb, dtype=w.dtype)
    return (eye[:, None, :, None] * w[:, :, None, :]).reshape(nb * c, nb * d)


def _lru_mixer(proj, conv_w, conv_b, wa, ba, wx, bx, lam, B, S):
    T = B * S
    tc = min(256, S)
    nt = S // tc
    W = LRU_WIDTH
    wa_bd = _block_diag(wa).astype(BF16)
    wx_bd = _block_diag(wx).astype(BF16)
    row = lambda v: v.reshape(1, W)
    full = lambda a: pl.BlockSpec(a.shape, lambda b, c: (0, 0))
    args = (conv_w, row(conv_b), wa_bd, row(ba), wx_bd, row(bx), row(lam))
    return pl.pallas_call(
        functools.partial(_lru_kernel, tc=tc),
        grid=(B, nt),
        in_specs=[pl.BlockSpec((tc, W), lambda b, c: (b * nt + c, C_LRUX // W)),
                  pl.BlockSpec((tc, W), lambda b, c: (b * nt + c, C_LRUY // W))]
                 + [full(a) for a in args],
        out_specs=pl.BlockSpec((tc, W), lambda b, c: (b * nt + c, 0)),
        out_shape=jax.ShapeDtypeStruct((T, W), BF16),
        scratch_shapes=[pltpu.VMEM((TAIL_ROWS, W), F32), pltpu.VMEM((TAIL_ROWS, W), F32)],
        compiler_params=_cparams(("parallel", "arbitrary")),
        name="lru_mixer",
    )(proj, proj, *args)


def _merge_kernel(x_ref, mg_ref, on_ref, os_ref, ol_ref, pn_ref, ps_ref, pl_ref, wo_ref, g_ref, b_ref,
                  of_ref, ob_ref):
    d = D_MODEL
    gate = _sigmoid(mg_ref[...].astype(F32))
    mixed = (gate[:, 0:d] * _dot(on_ref[...], pn_ref[...])
             + gate[:, d:2 * d] * _dot(os_ref[...], ps_ref[...])
             + gate[:, 2 * d:3 * d] * _dot(ol_ref[...], pl_ref[...]))
    v = ALPHA * x_ref[...] + _dot(mixed.astype(BF16), wo_ref[...])
    out = _layer_norm(v, g_ref[...], b_ref[...])
    of_ref[...] = out
    ob_ref[...] = out.astype(BF16)


def _merge(x, proj, o_nsa, o_ssd, o_lru, pn, ps, plru, wo, g, b):
    T = x.shape[0]
    tm = min(512, T)
    d = D_MODEL
    rowblk = lambda w: pl.BlockSpec((tm, w), lambda i: (i, 0))
    full = lambda a: pl.BlockSpec(a.shape, lambda i: (0, 0))
    g2, b2 = g.reshape(1, d), b.reshape(1, d)
    return pl.pallas_call(
        _merge_kernel,
        grid=(T // tm,),
        in_specs=[rowblk(d), pl.BlockSpec((tm, 3 * d), lambda i: (i, C_MERGE // (3 * d))),
                  rowblk(o_nsa.shape[1]), rowblk(o_ssd.shape[1]), rowblk(o_lru.shape[1]),
                  full(pn), full(ps), full(plru), full(wo), full(g2), full(b2)],
        out_specs=[rowblk(d), rowblk(d)],
        out_shape=[jax.ShapeDtypeStruct((T, d), F32), jax.ShapeDtypeStruct((T, d), BF16)],
        compiler_params=_cparams(("parallel",)),
        name="merge_ln",
    )(x, proj, o_nsa, o_ssd, o_lru, pn, ps, plru, wo, g2, b2)


def _route(sel, aff):
    epg = EXPERTS_PER_GROUP
    scores = []
    for gi in range(N_EXPERT_GROUPS):
        v = sel[gi * epg:(gi + 1) * epg]
        pair = None
        for a in range(epg):
            for b in range(a + 1, epg):
                sab = v[a] + v[b]
                pair = sab if pair is None else jnp.maximum(pair, sab)
        scores.append(pair)
    best = jnp.zeros_like(scores[0], dtype=jnp.int32)
    best_s = scores[0]
    for gi in range(1, N_EXPERT_GROUPS):
        better = scores[gi] > best_s
        best = jnp.where(better, gi, best)
        best_s = jnp.where(better, scores[gi], best_s)
    chosen = []
    for k in range(N_EXPERTS):
        gi = k // epg
        rank = jnp.zeros_like(best)
        for o in range(gi * epg, (gi + 1) * epg):
            if o == k:
                continue
            ahead = (sel[o] > sel[k]) | ((sel[o] == sel[k]) & (o < k))
            rank = rank + ahead.astype(jnp.int32)
        chosen.append((best == gi) & (rank < TOP_K))
    wsum = None
    for k in range(N_EXPERTS):
        wk = jnp.where(chosen[k], aff[k], 0.0)
        wsum = wk if wsum is None else wsum + wk
    inv = 1.0 / wsum
    return [jnp.where(chosen[k], aff[k], 0.0) * inv for k in range(N_EXPERTS)]


EXPERT_PAIRS = N_EXPERTS // 2
PAIRS_PER_ITER = 2


def _moe_kernel(xb_ref, xf_ref, p_ref, rw_ref, rb_ref, pg_ref, pp_ref, wg_ref, wu_ref, wd_ref,
                g_ref, b_ref, of_ref, ob_ref, acc_ref, gates_ref):
    xb = xb_ref[...]
    tm = xb.shape[0]
    logits = _dot_nt(rw_ref[...], xb)
    aff = _sigmoid(logits)
    sel = aff + rb_ref[...]
    gate_rows = _route([sel[k:k + 1, :] for k in range(N_EXPERTS)],
                       [aff[k:k + 1, :] for k in range(N_EXPERTS)])
    gt = jnp.concatenate(gate_rows + [jnp.zeros((LANES - N_EXPERTS, tm), F32)], axis=0)
    gates_ref[...] = gt.T
    acc_ref[...] = _sigmoid(_dot(xb, pg_ref[...])) * _dot(p_ref[...].astype(BF16), pp_ref[...])
    lane = lax.broadcasted_iota(jnp.int32, (tm, LANES), 1)

    def gate_cols(k):
        gates = gates_ref[...]
        cols = [jnp.broadcast_to(jnp.sum(jnp.where(lane == 2 * k + u, gates, 0.0), axis=-1, keepdims=True),
                                 (tm, D_EXPERT)) for u in range(2)]
        return jnp.concatenate(cols, axis=1)

    def step(it, carry):
        ks = [it * PAIRS_PER_ITER + u for u in range(PAIRS_PER_ITER)]
        hgs = [[_dot(xb, wg_ref[2 * k + u]) for u in range(2)] for k in ks]
        hus = [[_dot(xb, wu_ref[2 * k + u]) for u in range(2)] for k in ks]
        for k, hg2, hu2 in zip(ks, hgs, hus):
            hg = jnp.concatenate(hg2, axis=1)
            h = (hg * _sigmoid(hg)) * jnp.concatenate(hu2, axis=1) * gate_cols(k)
            acc_ref[...] += _dot(h.astype(BF16), wd_ref[k])
        return carry

    lax.fori_loop(0, EXPERT_PAIRS // PAIRS_PER_ITER, step, 0)
    out = _layer_norm(ALPHA * xf_ref[...] + acc_ref[...], g_ref[...], b_ref[...])
    of_ref[...] = out
    ob_ref[...] = out.astype(BF16)


def _moe_ple(xb, xf, p, rw_t, rb, pg, pp, wg, wu, wd, g, b):
    T = xb.shape[0]
    tm = min(512, T)
    d = D_MODEL
    rowblk = lambda w: pl.BlockSpec((tm, w), lambda i: (i, 0))
    once = pl.Buffered(1)
    full = lambda a: pl.BlockSpec(a.shape, lambda i: (0,) * a.ndim, pipeline_mode=once)
    g2, b2 = g.reshape(1, d), b.reshape(1, d)
    return pl.pallas_call(
        _moe_kernel,
        grid=(T // tm,),
        in_specs=[rowblk(d), rowblk(d), rowblk(PLE_DIM), full(rw_t), full(rb), full(pg), full(pp),
                  full(wg), full(wu), full(wd), full(g2), full(b2)],
        out_specs=[rowblk(d), rowblk(d)],
        out_shape=[jax.ShapeDtypeStruct((T, d), F32), jax.ShapeDtypeStruct((T, d), BF16)],
        scratch_shapes=[pltpu.VMEM((tm, d), F32), pltpu.VMEM((tm, LANES), F32)],
        compiler_params=_cparams(("parallel",)),
        name="moe_ple_ln",
    )(xb, xf, p, rw_t, rb, pg, pp, wg, wu, wd, g2, b2)


def _overlap_matrix(nc):
    n = jnp.arange(nc)[None, :]
    m = jnp.arange(SEL_LANES)[:, None]
    ratio = SEL_BLOCK // CMP_STRIDE
    ov = jnp.zeros((SEL_LANES, nc), F32)
    for k in range(CMP_BLOCK // CMP_STRIDE):
        ov = ov + ((n + k) // ratio == m).astype(F32)
    return ov.astype(BF16)


def _expand_matrix(S):
    c = jnp.arange(S)[:, None]
    m = jnp.arange(SEL_LANES)[None, :]
    return jnp.where(c // SEL_BLOCK == m, -MASK_VALUE, 0.0).astype(BF16)


def _pad_w2(w2):
    out = jnp.zeros((NSA_KV_GROUPS, CMP_HIDDEN, HEAD_SLOT), F32)
    for g in range(NSA_KV_GROUPS):
        out = out.at[g, :, g * NSA_HEAD_DIM:(g + 1) * NSA_HEAD_DIM].set(w2)
    return out.astype(BF16)


def _pad_proj_nsa(w):
    d = w.shape[1]
    w4 = w.reshape(NSA_KV_GROUPS, NSA_HPG, NSA_HEAD_DIM, d)
    out = jnp.zeros((NSA_KV_GROUPS, NSA_HPG, HEAD_SLOT, d), F32)
    for g in range(NSA_KV_GROUPS):
        out = out.at[g, :, g * NSA_HEAD_DIM:(g + 1) * NSA_HEAD_DIM].set(w4[g])
    return out.reshape(NSA_HEADS * HEAD_SLOT, d).astype(BF16)


def kernel(x, p, w_in, nsa_pe_k, nsa_w1_k, nsa_w2_k, nsa_pe_v, nsa_w1_v, nsa_w2_v, ssd_conv_w, ssd_conv_b, ssd_dt_bias, ssd_a_log, ssd_d, ssd_norm_w, lru_conv_w, lru_conv_b, lru_wa, lru_ba, lru_wx, lru_bx, lru_lambda, proj_nsa, proj_ssd, proj_lru, w_out, ln1_g, ln1_b, router_w, router_b, exp_w_gate, exp_w_up, exp_w_down, ple_w_gate, ple_w_proj, ln2_g, ln2_b):
    B, S, d = x.shape
    T = B * S
    depth = w_in.shape[0]
    assert d == D_MODEL and S % SEL_TK == 0 and S >= WIN_KEYS and S // SEL_BLOCK <= SEL_LANES
    nr = S // CMP_STRIDE
    ov = _overlap_matrix(nr)
    emat = _expand_matrix(S)
    rw_t = router_w.T.astype(BF16)
    rb = router_b.reshape(N_EXPERTS, 1).astype(F32)

    xf = x.reshape(T, d)
    xb = xf.astype(BF16)
    for i in range(depth):
        proj = _in_proj(xb, _prep_w_in(w_in[i]))
        cmp, vsel_t, vwin_t, dt_col, dt_row = _aux_proj(xb, w_in[i])

        w1big, pebig = _cmp_first_layer(nsa_w1_k[i], nsa_w1_v[i], nsa_pe_k[i], nsa_pe_v[i])
        kc, vct = _nsa_compress(cmp, w1big, pebig, _pad_w2(nsa_w2_k[i]),
                                _pad_w2(nsa_w2_v[i]).transpose(0, 2, 1), B, S)
        ocmpt, sel = _nsa_cmp_attn(proj, kc, vct, ov, B, S)
        o_nsa = _nsa_sel_win(proj, vsel_t, vwin_t, sel, emat, ocmpt, B, S)

        o_ssd = _ssd_mixer(proj, dt_col, dt_row, ssd_conv_w[i], ssd_conv_b[i], ssd_dt_bias[i],
                           ssd_a_log[i], ssd_d[i], ssd_norm_w[i], B, S)
        o_lru = _lru_mixer(proj, lru_conv_w[i], lru_conv_b[i], lru_wa[i], lru_ba[i], lru_wx[i],
                           lru_bx[i], lru_lambda[i], B, S)

        xf, xb = _merge(xf, proj, o_nsa, o_ssd, o_lru, _pad_proj_nsa(proj_nsa[i]),
                        proj_ssd[i].astype(BF16), proj_lru[i].astype(BF16), w_out[i].astype(BF16),
                        ln1_g[i], ln1_b[i])
        xf, xb = _moe_ple(xb, xf, p[i].reshape(T, PLE_DIM), rw_t, rb,
                          ple_w_gate[i].astype(BF16), ple_w_proj[i].astype(BF16),
                          exp_w_gate[i].astype(BF16), exp_w_up[i].astype(BF16),
                          exp_w_down[i].reshape(EXPERT_PAIRS, 2 * D_EXPERT, d).astype(BF16),
                          ln2_g[i], ln2_b[i])
    return xf.reshape(B, S, d)
```

```python
import functools
import math

import jax
import jax.numpy as jnp
from jax import lax
from jax.experimental import pallas as pl
from jax.experimental.pallas import tpu as pltpu

F32 = jnp.float32
BF16 = jnp.bfloat16

D_MODEL = 1024
PLE_DIM = 256
NSA_HEADS = 8
NSA_KV_GROUPS = 2
NSA_HEAD_DIM = 64
NSA_HPG = NSA_HEADS // NSA_KV_GROUPS
NSA_Q_W = NSA_HEADS * NSA_HEAD_DIM
NSA_KV_W = NSA_KV_GROUPS * NSA_HEAD_DIM
CMP_BLOCK = 32
CMP_STRIDE = 16
CMP_HIDDEN = 256
SEL_BLOCK = 64
SEL_TOPN = 16
WINDOW = 512
FORCE_SCORE = 1e4
MASK_VALUE = -1e30
LOG2E = 1.4426950408889634
SSD_HEADS = 8
SSD_HEAD_DIM = 64
SSD_INNER = SSD_HEADS * SSD_HEAD_DIM
SSD_GROUPS = 2
SSD_STATE = 64
SSD_CHUNK = 128
SSD_XBC_W = SSD_INNER + 2 * SSD_GROUPS * SSD_STATE
CONV_WIDTH = 4
LRU_WIDTH = 512
LRU_BLOCKS = 8
LRU_BLOCK_DIM = LRU_WIDTH // LRU_BLOCKS
LRU_C = 8.0
N_EXPERTS = 16
N_EXPERT_GROUPS = 4
EXPERTS_PER_GROUP = N_EXPERTS // N_EXPERT_GROUPS
TOP_K = 2
D_EXPERT = 256
DEPTH = 2
ALPHA = (2 * DEPTH) ** 0.25
LN_EPS = 1e-5
RMS_EPS = 1e-5
IN_SIZES = (NSA_Q_W, NSA_KV_W, NSA_KV_W, NSA_KV_W, NSA_KV_W, NSA_KV_W, NSA_KV_W, NSA_HEADS * 3,
            SSD_INNER, SSD_XBC_W, SSD_HEADS, LRU_WIDTH, LRU_WIDTH, 3 * D_MODEL)

LANES = 128
SEL_LANES = 128
HEAD_SLOT = 128

C_MERGE = 0
C_QEXT = 3072
C_SSDZ = 4096
C_LRUX = 4608
C_LRUY = 5120
C_KSEL = 5632
C_KWIN = 5760
C_SMALL = 5888
C_XBC = 6144
PROJ_W = 6912
PROJ_TN = 2304
GATE_W = NSA_HEADS * 3

VMEM_LIMIT = 56 * 1024 * 1024


def _cparams(sem):
    return pltpu.CompilerParams(dimension_semantics=sem, vmem_limit_bytes=VMEM_LIMIT)


def _sigmoid(x):
    return 1.0 / (1.0 + jnp.exp(-x))


def _softplus(x):
    return jnp.maximum(x, 0.0) + jnp.log(1.0 + jnp.exp(-jnp.abs(x)))


def _gelu_tanh(x):
    c = math.sqrt(2.0 / math.pi)
    return 0.5 * x * (1.0 + jnp.tanh(c * (x + 0.044715 * (x * x * x))))


def _dot(a, b):
    return jnp.dot(a, b, preferred_element_type=F32)


def _dot_nt(a, b):
    return lax.dot_general(a, b, (((1,), (1,)), ((), ())), preferred_element_type=F32)


def _dot_tn(a, b):
    return lax.dot_general(a, b, (((0,), (0,)), ((), ())), preferred_element_type=F32)


def _dot_f32(a, b):
    return jnp.dot(a, b, preferred_element_type=F32, precision=lax.Precision.HIGHEST)


def _layer_norm(v, g, b):
    mu = jnp.mean(v, axis=-1, keepdims=True)
    vc = v - mu
    var = jnp.mean(vc * vc, axis=-1, keepdims=True)
    return vc * lax.rsqrt(var + LN_EPS) * g + b


def _matmul_kernel(x_ref, w_ref, o_ref):
    o_ref[...] = _dot(x_ref[...].astype(BF16), w_ref[...]).astype(o_ref.dtype)


def _in_proj(xb, w):
    T, K = xb.shape
    N = w.shape[1]
    tm = min(1024, T)
    tn = PROJ_TN
    return pl.pallas_call(
        _matmul_kernel,
        grid=(N // tn, T // tm),
        in_specs=[pl.BlockSpec((tm, K), lambda j, i: (i, 0)),
                  pl.BlockSpec((K, tn), lambda j, i: (0, j))],
        out_specs=pl.BlockSpec((tm, tn), lambda j, i: (i, j)),
        out_shape=jax.ShapeDtypeStruct((T, N), BF16),
        compiler_params=_cparams(("parallel", "parallel")),
        name="in_proj",
    )(xb, w)


def _split_w_in(w):
    offs = [0]
    for s in IN_SIZES:
        offs.append(offs[-1] + s)
    return [w[:, offs[k]:offs[k + 1]] for k in range(len(IN_SIZES))]


def _prep_w_in(w):
    pc = _split_w_in(w)
    d = w.shape[0]
    q = pc[0].reshape(d, NSA_KV_GROUPS, NSA_HPG, NSA_HEAD_DIM) * (NSA_HEAD_DIM ** -0.5 * LOG2E)
    q_ext = jnp.zeros((d, NSA_KV_GROUPS, NSA_HPG, HEAD_SLOT), F32)
    for g in range(NSA_KV_GROUPS):
        q_ext = q_ext.at[:, g, :, g * NSA_HEAD_DIM:(g + 1) * NSA_HEAD_DIM].set(q[:, g])
    q_ext = q_ext.reshape(d, NSA_HEADS * HEAD_SLOT)
    small = jnp.pad(pc[7], ((0, 0), (0, C_XBC - C_SMALL - GATE_W)))
    out = jnp.concatenate([pc[13], q_ext, pc[8], pc[11], pc[12], pc[3], pc[5], small, pc[9]], axis=1)
    assert out.shape[1] == PROJ_W
    return out.astype(BF16)


VT_ROWS = LANES + 16


def _aux_proj_kernel(x_ref, wc_ref, wvt_ref, wdt_ref, wdtt_ref, cmp_ref, vst_ref, vwt_ref, dtc_ref, dtr_ref):
    x = x_ref[...].astype(BF16)
    tm = x.shape[0]
    cmp_ref[...] = _dot(x, wc_ref[...]).astype(cmp_ref.dtype)
    vt = _dot_nt(wvt_ref[...], x)
    ones = jnp.ones((VT_ROWS - LANES, LANES), vst_ref.dtype)
    for u in range(tm // LANES):
        for k, o_ref in enumerate((vst_ref, vwt_ref)):
            o_ref[u, 0:LANES, :] = vt[k * LANES:(k + 1) * LANES, u * LANES:(u + 1) * LANES].astype(o_ref.dtype)
            o_ref[u, LANES:VT_ROWS, :] = ones
    xf = x.astype(F32)
    dtc_ref[...] = _dot(xf, wdt_ref[...])
    dtr_ref[...] = _dot_nt(wdtt_ref[...], xf)


def _aux_proj(xb, w):
    T, K = xb.shape
    pc = _split_w_in(w)
    wc = jnp.concatenate([pc[1], pc[2]], axis=1).astype(BF16)
    wvt = jnp.concatenate([pc[4], pc[6]], axis=1).T.astype(BF16)
    wdt = pc[10]
    wdtt = pc[10].T
    tm = min(1024, T)
    full = lambda a: pl.BlockSpec(a.shape, lambda i: (0, 0))
    vt_shape = jax.ShapeDtypeStruct((T // LANES, VT_ROWS, LANES), BF16)
    vt_spec = pl.BlockSpec((tm // LANES, VT_ROWS, LANES), lambda i: (i, 0, 0))
    return pl.pallas_call(
        _aux_proj_kernel,
        grid=(T // tm,),
        in_specs=[pl.BlockSpec((tm, K), lambda i: (i, 0)), full(wc), full(wvt), full(wdt), full(wdtt)],
        out_specs=[pl.BlockSpec((tm, 2 * NSA_KV_W), lambda i: (i, 0)), vt_spec, vt_spec,
                   pl.BlockSpec((tm, SSD_HEADS), lambda i: (i, 0)),
                   pl.BlockSpec((SSD_HEADS, tm), lambda i: (0, i))],
        out_shape=[jax.ShapeDtypeStruct((T, 2 * NSA_KV_W), BF16), vt_shape, vt_shape,
                   jax.ShapeDtypeStruct((T, SSD_HEADS), F32), jax.ShapeDtypeStruct((SSD_HEADS, T), F32)],
        compiler_params=_cparams(("parallel",)),
        name="aux_proj",
    )(xb, wc, wvt, wdt, wdtt)


CMP_PIECES = 2 * NSA_KV_GROUPS


def _compress_kernel(r_ref, pe_ref, w1_ref, w2k_ref, w2vt_ref, kc_ref, vct_ref):
    r = r_ref[0]
    nr = r.shape[0]
    u = _dot(r, w1_ref[0])
    v = _dot(r, w1_ref[1])
    c = _dot(pe_ref[0], w1_ref[0]) + _dot(pe_ref[1], w1_ref[1])
    hid = u + pltpu.roll(v, nr - 1, axis=0)
    hid = (hid.reshape(nr // 8, 8, hid.shape[1]) + c[None]).reshape(nr, hid.shape[1])
    act = _gelu_tanh(hid).astype(BF16)
    for g in range(NSA_KV_GROUPS):
        kp, vp = g, NSA_KV_GROUPS + g
        kc_ref[0, g] = _dot(act[:, kp * CMP_HIDDEN:(kp + 1) * CMP_HIDDEN], w2k_ref[g]).astype(kc_ref.dtype)
        vct = _dot_nt(w2vt_ref[g], act[:, vp * CMP_HIDDEN:(vp + 1) * CMP_HIDDEN])
        vct_ref[0, g] = vct.astype(vct_ref.dtype)


def _cmp_first_layer(w1k, w1v, pek, pev):
    half = CMP_BLOCK // 2
    big = jnp.zeros((2, half, CMP_PIECES, NSA_HEAD_DIM, CMP_PIECES, CMP_HIDDEN), BF16)
    pe = jnp.zeros((2, half, CMP_PIECES, NSA_HEAD_DIM), BF16)
    for p, (w1, pe1) in enumerate(((w1k, pek), (w1k, pek), (w1v, pev), (w1v, pev))):
        big = big.at[:, :, p, :, p, :].set(w1.reshape(2, half, NSA_HEAD_DIM, CMP_HIDDEN).astype(BF16))
        pe = pe.at[:, :, p, :].set(pe1.reshape(2, half, NSA_HEAD_DIM).astype(BF16))
    rows = half * CMP_PIECES * NSA_HEAD_DIM
    return big.reshape(2, rows, CMP_PIECES * CMP_HIDDEN), jnp.broadcast_to(pe.reshape(2, 1, rows), (2, 8, rows))


def _nsa_compress(cmp, w1big, pebig, w2k, w2vt, B, S):
    NR = S // CMP_STRIDE
    W = CMP_STRIDE * 2 * NSA_KV_W
    r = cmp.reshape(B, NR, W)
    once = pl.Buffered(1)
    full = lambda a: pl.BlockSpec(a.shape, lambda b: (0,) * a.ndim, pipeline_mode=once)
    return pl.pallas_call(
        _compress_kernel,
        grid=(B,),
        in_specs=[pl.BlockSpec((1, NR, W), lambda b: (b, 0, 0)), full(pebig), full(w1big), full(w2k), full(w2vt)],
        out_specs=[pl.BlockSpec((1, NSA_KV_GROUPS, NR, HEAD_SLOT), lambda b: (b, 0, 0, 0)),
                   pl.BlockSpec((1, NSA_KV_GROUPS, HEAD_SLOT, NR), lambda b: (b, 0, 0, 0))],
        out_shape=[jax.ShapeDtypeStruct((B, NSA_KV_GROUPS, NR, HEAD_SLOT), BF16),
                   jax.ShapeDtypeStruct((B, NSA_KV_GROUPS, HEAD_SLOT, NR), BF16)],
        compiler_params=_cparams(("parallel",)),
        name="nsa_compress",
    )(r, pebig, w1big, w2k, w2vt)


def _stack_heads(q):
    return jnp.concatenate([q[:, j * HEAD_SLOT:(j + 1) * HEAD_SLOT] for j in range(NSA_HPG)], axis=0)


def _cmp_attn_kernel(q_ref, kc_ref, vct_ref, ovt_ref, gate_ref, ocmpt_ref, sel_ref, gt_ref, *, tq):
    g = pl.program_id(1)
    i = pl.program_id(2)
    q2 = _stack_heads(q_ref[...])
    kc = kc_ref[0, 0]
    vct = vct_ref[0, 0]
    nc = kc.shape[0]
    sts = [_dot_nt(kc, q2[j * tq:(j + 1) * tq]) for j in range(NSA_HPG)]
    n = lax.broadcasted_iota(jnp.int32, (nc, tq), 0)
    t = i * tq + lax.broadcasted_iota(jnp.int32, (nc, tq), 1)
    mask = n * CMP_STRIDE + (CMP_BLOCK - 1) <= t
    gt_ref[...] = _sigmoid(gate_ref[...].astype(F32)).T
    ps = None
    for j in range(NSA_HPG):
        s = jnp.where(mask, sts[j], MASK_VALUE)
        m = jnp.max(s, axis=0, keepdims=True)
        e = jnp.where(mask, jnp.exp2(s - m), 0.0)
        den = jnp.maximum(jnp.sum(e, axis=0, keepdims=True), 1e-30)
        p = e * (1.0 / den)
        ps = p if ps is None else ps + p
        gate = gt_ref[pl.ds(g * (NSA_HPG * 3) + j * 3, 1), :]
        ocmpt_ref[0, 0, 0, :, j * tq:(j + 1) * tq] = (gate * _dot(vct, p.astype(BF16))).astype(ocmpt_ref.dtype)
    ovt = ovt_ref[...]
    hi = ps.astype(BF16)
    r1 = ps - hi.astype(F32)
    mid = r1.astype(BF16)
    lo = (r1 - mid.astype(F32)).astype(BF16)
    imp = _dot(ovt, hi) + _dot(ovt, mid) + _dot(ovt, lo)
    blk = lax.broadcasted_iota(jnp.int32, imp.shape, 0)
    tt = i * tq + lax.broadcasted_iota(jnp.int32, imp.shape, 1)
    cur = tt // SEL_BLOCK
    forced = (blk == 0) | (blk == cur) | (blk == cur - 1)
    causal = blk * SEL_BLOCK <= tt
    v0 = jnp.where(forced, FORCE_SCORE, jnp.where(causal, imp, -1.0))

    blk1 = blk[:, 0:LANES]

    def pick(_, tiles):
        out = []
        for v in tiles:
            mx = jnp.max(v, axis=0, keepdims=True)
            idx = jnp.min(jnp.where(v == mx, blk1, SEL_LANES), axis=0, keepdims=True)
            out.append(jnp.where(blk1 == idx, -jnp.inf, v))
        return tuple(out)

    tiles = lax.fori_loop(0, SEL_TOPN, pick, tuple(v0[:, u * LANES:(u + 1) * LANES] for u in range(tq // LANES)))
    for u, v in enumerate(tiles):
        sel_ref[0, 0, u * LANES:(u + 1) * LANES, :] = jnp.where(v == -jnp.inf, 1.0, 0.0).T.astype(sel_ref.dtype)


def _nsa_cmp_attn(proj, kc, vct, ovt, B, S):
    tq = NSA_TQ
    nq = S // tq
    G = NSA_KV_GROUPS
    NC = kc.shape[2]
    qw = NSA_HPG * HEAD_SLOT
    qblk = C_QEXT // qw
    kern = functools.partial(_cmp_attn_kernel, tq=tq)
    return pl.pallas_call(
        kern,
        grid=(B, G, nq),
        in_specs=[pl.BlockSpec((tq, qw), lambda b, g, i: (b * nq + i, qblk + g)),
                  pl.BlockSpec((1, 1, NC, HEAD_SLOT), lambda b, g, i: (b, g, 0, 0)),
                  pl.BlockSpec((1, 1, HEAD_SLOT, NC), lambda b, g, i: (b, g, 0, 0)),
                  pl.BlockSpec(ovt.shape, lambda b, g, i: (0, 0)),
                  pl.BlockSpec((tq, LANES), lambda b, g, i: (b * nq + i, C_SMALL // LANES))],
        out_specs=[pl.BlockSpec((1, 1, 1, HEAD_SLOT, NSA_HPG * tq), lambda b, g, i: (b, g, i, 0, 0)),
                   pl.BlockSpec((1, 1, tq, SEL_LANES), lambda b, g, i: (b, g, i, 0))],
        out_shape=[jax.ShapeDtypeStruct((B, G, nq, HEAD_SLOT, NSA_HPG * tq), BF16),
                   jax.ShapeDtypeStruct((B, G, S, SEL_LANES), BF16)],
        scratch_shapes=[pltpu.VMEM((LANES, tq), F32)],
        compiler_params=_cparams(("parallel", "parallel", "parallel")),
        name="nsa_cmp_attn",
    )(proj, kc, vct, ovt, proj)


SEL_TK = 512
NSA_TQ = 256
WIN_KEYS = WINDOW + NSA_TQ


HALF = 256
NCH = NSA_HPG * NSA_TQ // HALF


def _sel_win_kernel(q_ref, ksel_ref, vselt_ref, kwin_ref, vwint_ref, sel_ref, et_ref, ocmpt_ref,
                    gate_ref, dbias_ref, wbias_ref, o_ref, s_ref, m_ref, acc_ref, gt_ref):
    g = pl.program_id(1)
    i = pl.program_id(2)
    tq = NSA_TQ
    t0 = i * tq
    q2 = _stack_heads(q_ref[...])
    selm1 = sel_ref[0, 0] - 1.0
    qx = jnp.concatenate([q2, jnp.concatenate([selm1] * NSA_HPG, axis=0)], axis=1)
    qxh = [qx[c * HALF:(c + 1) * HALF] for c in range(NCH)]
    q2h = [q2[c * HALF:(c + 1) * HALF] for c in range(NCH)]
    kpb = SEL_TK // LANES

    wblk = jnp.maximum((t0 - WINDOW) // LANES, 0)
    start = pl.multiple_of(wblk * LANES, LANES)
    kw = kwin_ref[pl.ds(start, WIN_KEYS), :]
    vwt = jnp.concatenate([vwint_ref[0, wblk + u] for u in range(WIN_KEYS // LANES)], axis=1)
    wbias = wbias_ref[jnp.minimum(i, WINDOW // tq)].astype(F32)
    sw = _dot_nt(kw, q2)

    def scores_into(kv):
        off = pl.multiple_of(kv * SEL_TK, SEL_TK)
        kx = jnp.concatenate([ksel_ref[pl.ds(off, SEL_TK), :], et_ref[pl.ds(off, SEL_TK), :]], axis=1)
        for c in range(NCH):
            s_ref[:, c * HALF:(c + 1) * HALF] = _dot_nt(kx, qxh[c])

    def update(vt, s_chunks):
        for c in range(NCH):
            cols = slice(c * HALF, (c + 1) * HALF)
            s = s_chunks[c]
            m_old = m_ref[:, cols]
            m_new = jnp.maximum(m_old, jnp.max(s, axis=0, keepdims=True))
            p = jnp.exp2(s - m_new).astype(BF16)
            acc_ref[:, cols] = jnp.exp2(m_old - m_new) * acc_ref[:, cols] + _dot(vt, p)
            m_ref[:, cols] = m_new

    def sel_values(kv):
        return jnp.concatenate([vselt_ref[0, kv * kpb + u] for u in range(kpb)], axis=1)

    def load_scores():
        return [s_ref[:, c * HALF:(c + 1) * HALF] for c in range(NCH)]

    m_ref[...] = jnp.full(m_ref.shape, MASK_VALUE, F32)
    acc_ref[...] = jnp.zeros(acc_ref.shape, F32)
    kd = t0 // SEL_TK
    scores_into(0)
    sw = sw + jnp.concatenate([wbias] * NSA_HPG, axis=1)
    mw = jnp.max(sw, axis=0, keepdims=True)
    accw = _dot(vwt, jnp.exp2(sw - mw).astype(BF16))
    ot_win = accw[0:LANES] * (1.0 / jnp.maximum(accw[LANES:LANES + 1], 1e-30))

    def body(kv, carry):
        s_chunks = load_scores()
        scores_into(kv + 1)
        update(sel_values(kv), s_chunks)
        return carry

    lax.fori_loop(0, kd, body, 0)
    dbias = dbias_ref[i % (SEL_TK // tq)].astype(F32)
    update(sel_values(kd), [sc + dbias for sc in load_scores()])
    acc = acc_ref[...]
    ot_slc = acc[0:LANES] * (1.0 / jnp.maximum(acc[LANES:LANES + 1], 1e-30))

    gt_ref[...] = _sigmoid(gate_ref[...].astype(F32)).T
    for j in range(NSA_HPG):
        base = g * (NSA_HPG * 3) + j * 3
        ot = (ocmpt_ref[0, 0, 0, :, j * tq:(j + 1) * tq].astype(F32)
              + gt_ref[pl.ds(base + 1, 1), :] * ot_slc[:, j * tq:(j + 1) * tq]
              + gt_ref[pl.ds(base + 2, 1), :] * ot_win[:, j * tq:(j + 1) * tq])
        o_ref[:, j * HEAD_SLOT:(j + 1) * HEAD_SLOT] = ot.T.astype(o_ref.dtype)


def _diag_bias():
    r = jnp.arange(SEL_TK)[None, :, None]
    c = jnp.arange(NSA_TQ)[None, None, :]
    off = (jnp.arange(SEL_TK // NSA_TQ) * NSA_TQ)[:, None, None]
    return jnp.where(r <= off + c, 0.0, MASK_VALUE).astype(BF16)


def _window_bias():
    r = jnp.arange(WIN_KEYS)[None, :, None]
    c = jnp.arange(NSA_TQ)[None, None, :]
    off = jnp.minimum(jnp.arange(WINDOW // NSA_TQ + 1) * NSA_TQ, WINDOW)[:, None, None]
    diff = off + c - r
    return jnp.where((diff >= 0) & (diff < WINDOW), 0.0, MASK_VALUE).astype(BF16)


def _nsa_sel_win(proj, vsel_t, vwin_t, sel, et, ocmpt, B, S):
    T = B * S
    tq = NSA_TQ
    assert HALF == tq
    dbias = _diag_bias()
    wbias = _window_bias()
    nq = S // tq
    G = NSA_KV_GROUPS
    qw = NSA_HPG * HEAD_SLOT
    qblk = C_QEXT // qw
    vsel_t = vsel_t.reshape(B, S // LANES, VT_ROWS, LANES)
    vwin_t = vwin_t.reshape(B, S // LANES, VT_ROWS, LANES)
    kv_spec = lambda c: pl.BlockSpec((S, LANES), lambda b, g, i: (b, c // LANES))
    vt_spec = pl.BlockSpec((1, S // LANES, VT_ROWS, LANES), lambda b, g, i: (b, 0, 0, 0))
    return pl.pallas_call(
        _sel_win_kernel,
        grid=(B, G, nq),
        in_specs=[pl.BlockSpec((tq, qw), lambda b, g, i: (b * nq + i, qblk + g)),
                  kv_spec(C_KSEL), vt_spec, kv_spec(C_KWIN), vt_spec,
                  pl.BlockSpec((1, 1, tq, SEL_LANES), lambda b, g, i: (b, g, i, 0)),
                  pl.BlockSpec(et.shape, lambda b, g, i: (0, 0)),
                  pl.BlockSpec((1, 1, 1, HEAD_SLOT, NSA_HPG * tq), lambda b, g, i: (b, g, i, 0, 0)),
                  pl.BlockSpec((tq, LANES), lambda b, g, i: (b * nq + i, C_SMALL // LANES)),
                  pl.BlockSpec(dbias.shape, lambda b, g, i: (0, 0, 0)),
                  pl.BlockSpec(wbias.shape, lambda b, g, i: (0, 0, 0))],
        out_specs=pl.BlockSpec((tq, qw), lambda b, g, i: (b * nq + i, g)),
        out_shape=jax.ShapeDtypeStruct((T, NSA_HEADS * HEAD_SLOT), BF16),
        scratch_shapes=[pltpu.VMEM((SEL_TK, NSA_HPG * tq), F32),
                        pltpu.VMEM((1, NSA_HPG * tq), F32),
                        pltpu.VMEM((VT_ROWS, NSA_HPG * tq), F32),
                        pltpu.VMEM((LANES, tq), F32)],
        compiler_params=_cparams(("parallel", "parallel", "parallel")),
        name="nsa_sel_win",
    )(proj, proj, vsel_t, proj, vwin_t, sel, et, ocmpt, proj, dbias, wbias)


TAIL_ROWS = 8


def _causal_conv(x, tail_ref, w, b):
    L, C = x.shape
    nv = L // TAIL_ROWS
    xx = jnp.concatenate([tail_ref[...], x], axis=0).reshape(nv + 1, TAIL_ROWS, C)
    sub = lax.broadcasted_iota(jnp.int32, (nv, TAIL_ROWS, C), 1)
    y = b + w[CONV_WIDTH - 1:CONV_WIDTH] * x
    for k in range(1, CONV_WIDTH):
        rot = pltpu.roll(xx, k, axis=1)
        shifted = jnp.where(sub >= k, rot[1:], rot[:-1]).reshape(L, C)
        y = y + w[CONV_WIDTH - 1 - k:CONV_WIDTH - k] * shifted
    tail_ref[...] = x[L - TAIL_ROWS:L]
    return y


SSD_CPS = 2


def _ssd_kernel(z_ref, xbc_ref, dtc_ref, dtr_ref, cw_ref, cb_ref, dtbc_ref, dtbr_ref, alc_ref, alr_ref,
                dsk_ref, nw_ref, o_ref, state_ref, tail_ref, y_ref):
    c = pl.program_id(1)
    L = SSD_CHUNK
    P = SSD_HEAD_DIM
    N = SSD_STATE
    hpg = SSD_HEADS // SSD_GROUPS

    @pl.when(c == 0)
    def _():
        state_ref[...] = jnp.zeros_like(state_ref)
        tail_ref[...] = jnp.zeros_like(tail_ref)

    conv = _causal_conv(xbc_ref[...].astype(F32), tail_ref, cw_ref[...], cb_ref[...])
    xbc = conv * _sigmoid(conv)
    dt_c = _softplus(dtc_ref[...] + dtbc_ref[...])
    dt_r = _softplus(dtr_ref[...] + dtbr_ref[...])
    a_c = dt_c * (-jnp.exp(alc_ref[...]))
    a_r = dt_r * (-jnp.exp(alr_ref[...]))
    ii = lax.broadcasted_iota(jnp.int32, (L, L), 0)
    jj = lax.broadcasted_iota(jnp.int32, (L, L), 1)
    tri = ii >= jj
    tril = tri.astype(F32)
    triu = (ii <= jj).astype(F32)

    for cc in range(SSD_CPS):
        rs = slice(cc * L, (cc + 1) * L)
        xs = xbc[rs, 0:SSD_INNER]
        bm = xbc[rs, SSD_INNER:SSD_INNER + SSD_GROUPS * N]
        cm = xbc[rs, SSD_INNER + SSD_GROUPS * N:SSD_INNER + 2 * SSD_GROUPS * N]
        acum_c = _dot_f32(tril, a_c[rs])
        acum_r = _dot_f32(a_r[:, rs], triu)
        for g in range(SSD_GROUPS):
            bg = bm[:, g * N:(g + 1) * N]
            cgb = cm[:, g * N:(g + 1) * N].astype(BF16)
            cb = _dot_nt(cgb, bg.astype(BF16))
            for j in range(hpg):
                h = g * hpg + j
                acb = jnp.broadcast_to(acum_c[:, h:h + 1], (L, L))
                a_last = acum_c[L - 1:L, h:h + 1]
                xh_raw = xs[:, h * P:(h + 1) * P]
                xh = (xh_raw * jnp.broadcast_to(dt_c[rs, h:h + 1], (L, P))).astype(BF16)
                lmat = jnp.where(tri, jnp.exp(acb - acum_r[h:h + 1, :]), 0.0)
                y = _dot((cb * lmat).astype(BF16), xh)
                prev = state_ref[h]
                y = y + _dot(cgb, prev.astype(BF16)) * jnp.exp(acb[:, 0:P])
                bd = (bg * jnp.exp(a_last - acb[:, 0:N])).astype(BF16)
                state_ref[h] = jnp.exp(a_last) * prev + _dot_tn(bd, xh)
                y_ref[rs, h * P:(h + 1) * P] = y + dsk_ref[:, h * P:(h + 1) * P] * xh_raw

    zf = z_ref[...].astype(F32)
    yg = y_ref[...] * (zf * _sigmoid(zf))
    ms = jnp.mean(yg * yg, axis=-1, keepdims=True)
    o_ref[...] = (yg * lax.rsqrt(ms + RMS_EPS) * nw_ref[...]).astype(o_ref.dtype)


def _ssd_mixer(proj, dt_col, dt_row, conv_w, conv_b, dt_bias, a_log, d_skip, norm_w, B, S):
    T = B * S
    L = SSD_CPS * SSD_CHUNK
    nc = S // L
    H = SSD_HEADS
    full = lambda a: pl.BlockSpec(a.shape, lambda b, c: (0, 0))
    cb2 = conv_b.reshape(1, -1)
    dtb_c = dt_bias.reshape(1, H)
    dtb_r = dt_bias.reshape(H, 1)
    al_c = a_log.reshape(1, H)
    al_r = a_log.reshape(H, 1)
    dsk = jnp.repeat(d_skip, SSD_HEAD_DIM).reshape(1, SSD_INNER)
    nw = norm_w.reshape(1, SSD_INNER)
    return pl.pallas_call(
        _ssd_kernel,
        grid=(B, nc),
        in_specs=[pl.BlockSpec((L, SSD_INNER), lambda b, c: (b * nc + c, C_SSDZ // SSD_INNER)),
                  pl.BlockSpec((L, SSD_XBC_W), lambda b, c: (b * nc + c, C_XBC // SSD_XBC_W)),
                  pl.BlockSpec((L, H), lambda b, c: (b * nc + c, 0)),
                  pl.BlockSpec((H, L), lambda b, c: (0, b * nc + c)),
                  full(conv_w), full(cb2), full(dtb_c), full(dtb_r), full(al_c), full(al_r),
                  full(dsk), full(nw)],
        out_specs=pl.BlockSpec((L, SSD_INNER), lambda b, c: (b * nc + c, 0)),
        out_shape=jax.ShapeDtypeStruct((T, SSD_INNER), BF16),
        scratch_shapes=[pltpu.VMEM((H, SSD_STATE, SSD_HEAD_DIM), F32),
                        pltpu.VMEM((TAIL_ROWS, SSD_XBC_W), F32),
                        pltpu.VMEM((L, SSD_INNER), F32)],
        compiler_params=_cparams(("parallel", "arbitrary")),
        name="ssd_mixer",
    )(proj, proj, dt_col, dt_row, conv_w, cb2, dtb_c, dtb_r, al_c, al_r, dsk, nw)


SCAN_GROUP = 8


def _lru_kernel(x_ref, y_ref, cw_ref, cb_ref, wa_ref, ba_ref, wx_ref, bx_ref, lam_ref, o_ref,
                h_ref, tail_ref, *, tc):
    c = pl.program_id(1)

    @pl.when(c == 0)
    def _():
        h_ref[...] = jnp.zeros_like(h_ref)
        tail_ref[...] = jnp.zeros_like(tail_ref)

    xr = _causal_conv(x_ref[...].astype(F32), tail_ref, cw_ref[...], cb_ref[...])
    xrb = xr.astype(BF16)
    r = _sigmoid(_dot(xrb, wa_ref[...]) + ba_ref[...])
    ig = _sigmoid(_dot(xrb, wx_ref[...]) + bx_ref[...])
    log_a = -LRU_C * r * _softplus(-lam_ref[...])
    a = jnp.exp(log_a)
    b = jnp.sqrt(1.0 - jnp.exp(2.0 * log_a)) * (ig * xr)
    ng = tc // SCAN_GROUP
    a = a.reshape(ng, SCAN_GROUP, a.shape[1])
    b = b.reshape(ng, SCAN_GROUP, b.shape[1])
    sub = lax.broadcasted_iota(jnp.int32, a.shape, 1)
    k = 1
    while k < SCAN_GROUP:
        keep = sub >= k
        a_s = jnp.where(keep, pltpu.roll(a, k, axis=1), 1.0)
        b_s = jnp.where(keep, pltpu.roll(b, k, axis=1), 0.0)
        b = a * b_s + b
        a = a * a_s
        k *= 2
    carry = h_ref[0:1, :]
    groups = []
    for rg in range(ng):
        hg = a[rg] * carry + b[rg]
        groups.append(hg)
        carry = hg[SCAN_GROUP - 1:SCAN_GROUP]
    h = jnp.concatenate(groups, axis=0)
    h_ref[...] = jnp.broadcast_to(carry, h_ref.shape)
    o_ref[...] = (h * _gelu_tanh(y_ref[...].astype(F32))).astype(o_ref.dtype)


def _block_diag(w):
    nb, c, d = w.shape
    eye = jnp.eye(nb, dtype=w.dtype)
    return (eye[:, None, :, None] * w[:, :, None, :]).reshape(nb * c, nb * d)


def _lru_mixer(proj, conv_w, conv_b, wa, ba, wx, bx, lam, B, S):
    T = B * S
    tc = min(256, S)
    nt = S // tc
    W = LRU_WIDTH
    wa_bd = _block_diag(wa).astype(BF16)
    wx_bd = _block_diag(wx).astype(BF16)
    row = lambda v: v.reshape(1, W)
    full = lambda a: pl.BlockSpec(a.shape, lambda b, c: (0, 0))
    args = (conv_w, row(conv_b), wa_bd, row(ba), wx_bd, row(bx), row(lam))
    return pl.pallas_call(
        functools.partial(_lru_kernel, tc=tc),
        grid=(B, nt),
        in_specs=[pl.BlockSpec((tc, W), lambda b, c: (b * nt + c, C_LRUX // W)),
                  pl.BlockSpec((tc, W), lambda b, c: (b * nt + c, C_LRUY // W))]
                 + [full(a) for a in args],
        out_specs=pl.BlockSpec((tc, W), lambda b, c: (b * nt + c, 0)),
        out_shape=jax.ShapeDtypeStruct((T, W), BF16),
        scratch_shapes=[pltpu.VMEM((TAIL_ROWS, W), F32), pltpu.VMEM((TAIL_ROWS, W), F32)],
        compiler_params=_cparams(("parallel", "arbitrary")),
        name="lru_mixer",
    )(proj, proj, *args)


def _merge_kernel(x_ref, mg_ref, on_ref, os_ref, ol_ref, pn_ref, ps_ref, pl_ref, wo_ref, g_ref, b_ref,
                  of_ref, ob_ref):
    d = D_MODEL
    gate = _sigmoid(mg_ref[...].astype(F32))
    mixed = (gate[:, 0:d] * _dot(on_ref[...], pn_ref[...])
             + gate[:, d:2 * d] * _dot(os_ref[...], ps_ref[...])
             + gate[:, 2 * d:3 * d] * _dot(ol_ref[...], pl_ref[...]))
    v = ALPHA * x_ref[...] + _dot(mixed.astype(BF16), wo_ref[...])
    out = _layer_norm(v, g_ref[...], b_ref[...])
    of_ref[...] = out
    ob_ref[...] = out.astype(BF16)


def _merge(x, proj, o_nsa, o_ssd, o_lru, pn, ps, plru, wo, g, b):
    T = x.shape[0]
    tm = min(512, T)
    d = D_MODEL
    rowblk = lambda w: pl.BlockSpec((tm, w), lambda i: (i, 0))
    full = lambda a: pl.BlockSpec(a.shape, lambda i: (0, 0))
    g2, b2 = g.reshape(1, d), b.reshape(1, d)
    return pl.pallas_call(
        _merge_kernel,
        grid=(T // tm,),
        in_specs=[rowblk(d), pl.BlockSpec((tm, 3 * d), lambda i: (i, C_MERGE // (3 * d))),
                  rowblk(o_nsa.shape[1]), rowblk(o_ssd.shape[1]), rowblk(o_lru.shape[1]),
                  full(pn), full(ps), full(plru), full(wo), full(g2), full(b2)],
        out_specs=[rowblk(d), rowblk(d)],
        out_shape=[jax.ShapeDtypeStruct((T, d), F32), jax.ShapeDtypeStruct((T, d), BF16)],
        compiler_params=_cparams(("parallel",)),
        name="merge_ln",
    )(x, proj, o_nsa, o_ssd, o_lru, pn, ps, plru, wo, g2, b2)


def _route(sel, aff):
    epg = EXPERTS_PER_GROUP
    scores = []
    for gi in range(N_EXPERT_GROUPS):
        v = sel[gi * epg:(gi + 1) * epg]
        pair = None
        for a in range(epg):
            for b in range(a + 1, epg):
                sab = v[a] + v[b]
                pair = sab if pair is None else jnp.maximum(pair, sab)
        scores.append(pair)
    best = jnp.zeros_like(scores[0], dtype=jnp.int32)
    best_s = scores[0]
    for gi in range(1, N_EXPERT_GROUPS):
        better = scores[gi] > best_s
        best = jnp.where(better, gi, best)
        best_s = jnp.where(better, scores[gi], best_s)
    chosen = []
    for k in range(N_EXPERTS):
        gi = k // epg
        rank = jnp.zeros_like(best)
        for o in range(gi * epg, (gi + 1) * epg):
            if o == k:
                continue
            ahead = (sel[o] > sel[k]) | ((sel[o] == sel[k]) & (o < k))
            rank = rank + ahead.astype(jnp.int32)
        chosen.append((best == gi) & (rank < TOP_K))
    wsum = None
    for k in range(N_EXPERTS):
        wk = jnp.where(chosen[k], aff[k], 0.0)
        wsum = wk if wsum is None else wsum + wk
    inv = 1.0 / wsum
    return [jnp.where(chosen[k], aff[k], 0.0) * inv for k in range(N_EXPERTS)]


EXPERT_PAIRS = N_EXPERTS // 2
PAIRS_PER_ITER = 2


def _moe_kernel(xb_ref, xf_ref, p_ref, rw_ref, rb_ref, pg_ref, pp_ref, wg_ref, wu_ref, wd_ref,
                g_ref, b_ref, of_ref, ob_ref, acc_ref, gates_ref):
    xb = xb_ref[...]
    tm = xb.shape[0]
    logits = _dot_nt(rw_ref[...], xb)
    aff = _sigmoid(logits)
    sel = aff + rb_ref[...]
    gate_rows = _route([sel[k:k + 1, :] for k in range(N_EXPERTS)],
                       [aff[k:k + 1, :] for k in range(N_EXPERTS)])
    gt = jnp.concatenate(gate_rows + [jnp.zeros((LANES - N_EXPERTS, tm), F32)], axis=0)
    gates_ref[...] = gt.T
    acc_ref[...] = _sigmoid(_dot(xb, pg_ref[...])) * _dot(p_ref[...].astype(BF16), pp_ref[...])
    lane = lax.broadcasted_iota(jnp.int32, (tm, LANES), 1)

    def gate_cols(k):
        gates = gates_ref[...]
        cols = [jnp.broadcast_to(jnp.sum(jnp.where(lane == 2 * k + u, gates, 0.0), axis=-1, keepdims=True),
                                 (tm, D_EXPERT)) for u in range(2)]
        return jnp.concatenate(cols, axis=1)

    def step(it, carry):
        ks = [it * PAIRS_PER_ITER + u for u in range(PAIRS_PER_ITER)]
        hgs = [[_dot(xb, wg_ref[2 * k + u]) for u in range(2)] for k in ks]
        hus = [[_dot(xb, wu_ref[2 * k + u]) for u in range(2)] for k in ks]
        for k, hg2, hu2 in zip(ks, hgs, hus):
            hg = jnp.concatenate(hg2, axis=1)
            h = (hg * _sigmoid(hg)) * jnp.concatenate(hu2, axis=1) * gate_cols(k)
            acc_ref[...] += _dot(h.astype(BF16), wd_ref[k])
        return carry

    lax.fori_loop(0, EXPERT_PAIRS // PAIRS_PER_ITER, step, 0)
    out = _layer_norm(ALPHA * xf_ref[...] + acc_ref[...], g_ref[...], b_ref[...])
    of_ref[...] = out
    ob_ref[...] = out.astype(BF16)


def _moe_ple(xb, xf, p, layer, rw_t, rb, pg, pp, wg, wu, wd, g, b):
    T = xb.shape[0]
    tm = min(512, T)
    d = D_MODEL
    rowblk = lambda w: pl.BlockSpec((tm, w), lambda i: (i, 0))
    once = pl.Buffered(1)
    full = lambda a: pl.BlockSpec(a.shape, lambda i: (0,) * a.ndim, pipeline_mode=once)
    g2, b2 = g.reshape(1, d), b.reshape(1, d)
    return pl.pallas_call(
        _moe_kernel,
        grid=(T // tm,),
        in_specs=[rowblk(d), rowblk(d), pl.BlockSpec((None, tm, PLE_DIM), lambda i: (layer, i, 0)),
                  full(rw_t), full(rb), full(pg), full(pp),
                  full(wg), full(wu), full(wd), full(g2), full(b2)],
        out_specs=[rowblk(d), rowblk(d)],
        out_shape=[jax.ShapeDtypeStruct((T, d), F32), jax.ShapeDtypeStruct((T, d), BF16)],
        scratch_shapes=[pltpu.VMEM((tm, d), F32), pltpu.VMEM((tm, LANES), F32)],
        compiler_params=_cparams(("parallel",)),
        name="moe_ple_ln",
    )(xb, xf, p, rw_t, rb, pg, pp, wg, wu, wd, g2, b2)


def _overlap_matrix(nc):
    n = jnp.arange(nc)[None, :]
    m = jnp.arange(SEL_LANES)[:, None]
    ratio = SEL_BLOCK // CMP_STRIDE
    ov = jnp.zeros((SEL_LANES, nc), F32)
    for k in range(CMP_BLOCK // CMP_STRIDE):
        ov = ov + ((n + k) // ratio == m).astype(F32)
    return ov.astype(BF16)


def _expand_matrix(S):
    c = jnp.arange(S)[:, None]
    m = jnp.arange(SEL_LANES)[None, :]
    return jnp.where(c // SEL_BLOCK == m, -MASK_VALUE, 0.0).astype(BF16)


def _pad_w2(w2):
    out = jnp.zeros((NSA_KV_GROUPS, CMP_HIDDEN, HEAD_SLOT), F32)
    for g in range(NSA_KV_GROUPS):
        out = out.at[g, :, g * NSA_HEAD_DIM:(g + 1) * NSA_HEAD_DIM].set(w2)
    return out.astype(BF16)


def _pad_proj_nsa(w):
    d = w.shape[1]
    w4 = w.reshape(NSA_KV_GROUPS, NSA_HPG, NSA_HEAD_DIM, d)
    out = jnp.zeros((NSA_KV_GROUPS, NSA_HPG, HEAD_SLOT, d), F32)
    for g in range(NSA_KV_GROUPS):
        out = out.at[g, :, g * NSA_HEAD_DIM:(g + 1) * NSA_HEAD_DIM].set(w4[g])
    return out.reshape(NSA_HEADS * HEAD_SLOT, d).astype(BF16)


def kernel(x, p, w_in, nsa_pe_k, nsa_w1_k, nsa_w2_k, nsa_pe_v, nsa_w1_v, nsa_w2_v, ssd_conv_w, ssd_conv_b, ssd_dt_bias, ssd_a_log, ssd_d, ssd_norm_w, lru_conv_w, lru_conv_b, lru_wa, lru_ba, lru_wx, lru_bx, lru_lambda, proj_nsa, proj_ssd, proj_lru, w_out, ln1_g, ln1_b, router_w, router_b, exp_w_gate, exp_w_up, exp_w_down, ple_w_gate, ple_w_proj, ln2_g, ln2_b):
    B, S, d = x.shape
    T = B * S
    depth = w_in.shape[0]
    assert d == D_MODEL and S % SEL_TK == 0 and S >= WIN_KEYS and S // SEL_BLOCK <= SEL_LANES
    nr = S // CMP_STRIDE
    ov = _overlap_matrix(nr)
    emat = _expand_matrix(S)
    rw_t = router_w.T.astype(BF16)
    rb = router_b.reshape(N_EXPERTS, 1).astype(F32)

    xf = x.reshape(T, d)
    xb = xf
    for i in range(depth):
        proj = _in_proj(xb, _prep_w_in(w_in[i]))
        cmp, vsel_t, vwin_t, dt_col, dt_row = _aux_proj(xb, w_in[i])

        w1big, pebig = _cmp_first_layer(nsa_w1_k[i], nsa_w1_v[i], nsa_pe_k[i], nsa_pe_v[i])
        kc, vct = _nsa_compress(cmp, w1big, pebig, _pad_w2(nsa_w2_k[i]),
                                _pad_w2(nsa_w2_v[i]).transpose(0, 2, 1), B, S)
        ocmpt, sel = _nsa_cmp_attn(proj, kc, vct, ov, B, S)
        o_nsa = _nsa_sel_win(proj, vsel_t, vwin_t, sel, emat, ocmpt, B, S)

        o_ssd = _ssd_mixer(proj, dt_col, dt_row, ssd_conv_w[i], ssd_conv_b[i], ssd_dt_bias[i],
                           ssd_a_log[i], ssd_d[i], ssd_norm_w[i], B, S)
        o_lru = _lru_mixer(proj, lru_conv_w[i], lru_conv_b[i], lru_wa[i], lru_ba[i], lru_wx[i],
                           lru_bx[i], lru_lambda[i], B, S)

        xf, xb = _merge(xf, proj, o_nsa, o_ssd, o_lru, _pad_proj_nsa(proj_nsa[i]),
                        proj_ssd[i].astype(BF16), proj_lru[i].astype(BF16), w_out[i].astype(BF16),
                        ln1_g[i], ln1_b[i])
        xf, xb = _moe_ple(xb, xf, p.reshape(depth, T, PLE_DIM), i, rw_t, rb,
                          ple_w_gate[i].astype(BF16), ple_w_proj[i].astype(BF16),
                          exp_w_gate[i].astype(BF16), exp_w_up[i].astype(BF16),
                          exp_w_down[i].reshape(EXPERT_PAIRS, 2 * D_EXPERT, d).astype(BF16),
                          ln2_g[i], ln2_b[i])
    return xf.reshape(B, S, d)
```

```python
import functools
import math

import jax
import jax.numpy as jnp
from jax import lax
from jax.experimental import pallas as pl
from jax.experimental.pallas import tpu as pltpu

F32 = jnp.float32
BF16 = jnp.bfloat16

D_MODEL = 1024
PLE_DIM = 256
NSA_HEADS = 8
NSA_KV_GROUPS = 2
NSA_HEAD_DIM = 64
NSA_HPG = NSA_HEADS // NSA_KV_GROUPS
NSA_Q_W = NSA_HEADS * NSA_HEAD_DIM
NSA_KV_W = NSA_KV_GROUPS * NSA_HEAD_DIM
CMP_BLOCK = 32
CMP_STRIDE = 16
CMP_HIDDEN = 256
SEL_BLOCK = 64
SEL_TOPN = 16
WINDOW = 512
FORCE_SCORE = 1e4
MASK_VALUE = -1e30
LOG2E = 1.4426950408889634
SSD_HEADS = 8
SSD_HEAD_DIM = 64
SSD_INNER = SSD_HEADS * SSD_HEAD_DIM
SSD_GROUPS = 2
SSD_STATE = 64
SSD_CHUNK = 128
SSD_XBC_W = SSD_INNER + 2 * SSD_GROUPS * SSD_STATE
CONV_WIDTH = 4
LRU_WIDTH = 512
LRU_BLOCKS = 8
LRU_BLOCK_DIM = LRU_WIDTH // LRU_BLOCKS
LRU_C = 8.0
N_EXPERTS = 16
N_EXPERT_GROUPS = 4
EXPERTS_PER_GROUP = N_EXPERTS // N_EXPERT_GROUPS
TOP_K = 2
D_EXPERT = 256
DEPTH = 2
ALPHA = (2 * DEPTH) ** 0.25
LN_EPS = 1e-5
RMS_EPS = 1e-5
IN_SIZES = (NSA_Q_W, NSA_KV_W, NSA_KV_W, NSA_KV_W, NSA_KV_W, NSA_KV_W, NSA_KV_W, NSA_HEADS * 3,
            SSD_INNER, SSD_XBC_W, SSD_HEADS, LRU_WIDTH, LRU_WIDTH, 3 * D_MODEL)

LANES = 128
SEL_LANES = 128
HEAD_SLOT = 128

C_MERGE = 0
C_QEXT = 3072
C_SSDZ = 4096
C_LRUX = 4608
C_LRUY = 5120
C_KSEL = 5632
C_KWIN = 5760
C_SMALL = 5888
C_XBC = 6144
PROJ_W = 6912
PROJ_TN = 2304
GATE_W = NSA_HEADS * 3

VMEM_LIMIT = 56 * 1024 * 1024


def _cparams(sem):
    return pltpu.CompilerParams(dimension_semantics=sem, vmem_limit_bytes=VMEM_LIMIT)


def _sigmoid(x):
    return 1.0 / (1.0 + jnp.exp(-x))


def _softplus(x):
    return jnp.maximum(x, 0.0) + jnp.log(1.0 + jnp.exp(-jnp.abs(x)))


def _gelu_tanh(x):
    c = math.sqrt(2.0 / math.pi)
    return 0.5 * x * (1.0 + jnp.tanh(c * (x + 0.044715 * (x * x * x))))


def _dot(a, b):
    return jnp.dot(a, b, preferred_element_type=F32)


def _dot_nt(a, b):
    return lax.dot_general(a, b, (((1,), (1,)), ((), ())), preferred_element_type=F32)


def _dot_tn(a, b):
    return lax.dot_general(a, b, (((0,), (0,)), ((), ())), preferred_element_type=F32)


def _dot_f32(a, b):
    return jnp.dot(a, b, preferred_element_type=F32, precision=lax.Precision.HIGHEST)


def _layer_norm(v, g, b):
    mu = jnp.mean(v, axis=-1, keepdims=True)
    vc = v - mu
    var = jnp.mean(vc * vc, axis=-1, keepdims=True)
    return vc * lax.rsqrt(var + LN_EPS) * g + b


def _matmul_kernel(x_ref, w_ref, o_ref):
    o_ref[...] = _dot(x_ref[...].astype(BF16), w_ref[...]).astype(o_ref.dtype)


def _in_proj(xb, w):
    T, K = xb.shape
    N = w.shape[1]
    tm = min(1024, T)
    tn = PROJ_TN
    return pl.pallas_call(
        _matmul_kernel,
        grid=(N // tn, T // tm),
        in_specs=[pl.BlockSpec((tm, K), lambda j, i: (i, 0)),
                  pl.BlockSpec((K, tn), lambda j, i: (0, j))],
        out_specs=pl.BlockSpec((tm, tn), lambda j, i: (i, j)),
        out_shape=jax.ShapeDtypeStruct((T, N), BF16),
        compiler_params=_cparams(("parallel", "parallel")),
        name="in_proj",
    )(xb, w)


def _split_w_in(w):
    offs = [0]
    for s in IN_SIZES:
        offs.append(offs[-1] + s)
    return [w[:, offs[k]:offs[k + 1]] for k in range(len(IN_SIZES))]


def _prep_w_in(w):
    pc = _split_w_in(w)
    d = w.shape[0]
    q = pc[0].reshape(d, NSA_KV_GROUPS, NSA_HPG, NSA_HEAD_DIM) * (NSA_HEAD_DIM ** -0.5 * LOG2E)
    q_ext = jnp.zeros((d, NSA_KV_GROUPS, NSA_HPG, HEAD_SLOT), F32)
    for g in range(NSA_KV_GROUPS):
        q_ext = q_ext.at[:, g, :, g * NSA_HEAD_DIM:(g + 1) * NSA_HEAD_DIM].set(q[:, g])
    q_ext = q_ext.reshape(d, NSA_HEADS * HEAD_SLOT)
    small = jnp.pad(pc[7], ((0, 0), (0, C_XBC - C_SMALL - GATE_W)))
    out = jnp.concatenate([pc[13], q_ext, pc[8], pc[11], pc[12], pc[3], pc[5], small, pc[9]], axis=1)
    assert out.shape[1] == PROJ_W
    return out.astype(BF16)


VT_ROWS = NSA_HEAD_DIM + 16


def _aux_proj_kernel(x_ref, wc_ref, wvt_ref, wdt_ref, wdtt_ref, cmp_ref, vst_ref, vwt_ref, dtc_ref, dtr_ref):
    x = x_ref[...].astype(BF16)
    tm = x.shape[0]
    cmp_ref[...] = _dot(x, wc_ref[...]).astype(cmp_ref.dtype)
    vt = _dot_nt(wvt_ref[...], x)
    dv = NSA_HEAD_DIM
    ones = jnp.ones((VT_ROWS - dv, LANES), vst_ref.dtype)
    for u in range(tm // LANES):
        for k, o_ref in enumerate((vst_ref, vwt_ref)):
            for g in range(NSA_KV_GROUPS):
                r0 = (k * NSA_KV_GROUPS + g) * dv
                o_ref[g, u, 0:dv, :] = vt[r0:r0 + dv, u * LANES:(u + 1) * LANES].astype(o_ref.dtype)
                o_ref[g, u, dv:VT_ROWS, :] = ones
    xf = x.astype(F32)
    dtc_ref[...] = _dot(xf, wdt_ref[...])
    dtr_ref[...] = _dot_nt(wdtt_ref[...], xf)


def _aux_proj(xb, w):
    T, K = xb.shape
    pc = _split_w_in(w)
    wc = jnp.concatenate([pc[1], pc[2]], axis=1).astype(BF16)
    wvt = jnp.concatenate([pc[4], pc[6]], axis=1).T.astype(BF16)
    wdt = pc[10]
    wdtt = pc[10].T
    tm = min(1024, T)
    full = lambda a: pl.BlockSpec(a.shape, lambda i: (0, 0))
    vt_shape = jax.ShapeDtypeStruct((NSA_KV_GROUPS, T // LANES, VT_ROWS, LANES), BF16)
    vt_spec = pl.BlockSpec((NSA_KV_GROUPS, tm // LANES, VT_ROWS, LANES), lambda i: (0, i, 0, 0))
    return pl.pallas_call(
        _aux_proj_kernel,
        grid=(T // tm,),
        in_specs=[pl.BlockSpec((tm, K), lambda i: (i, 0)), full(wc), full(wvt), full(wdt), full(wdtt)],
        out_specs=[pl.BlockSpec((tm, 2 * NSA_KV_W), lambda i: (i, 0)), vt_spec, vt_spec,
                   pl.BlockSpec((tm, SSD_HEADS), lambda i: (i, 0)),
                   pl.BlockSpec((SSD_HEADS, tm), lambda i: (0, i))],
        out_shape=[jax.ShapeDtypeStruct((T, 2 * NSA_KV_W), BF16), vt_shape, vt_shape,
                   jax.ShapeDtypeStruct((T, SSD_HEADS), F32), jax.ShapeDtypeStruct((SSD_HEADS, T), F32)],
        compiler_params=_cparams(("parallel",)),
        name="aux_proj",
    )(xb, wc, wvt, wdt, wdtt)


CMP_PIECES = 2 * NSA_KV_GROUPS


def _compress_kernel(r_ref, pe_ref, w1_ref, w2k_ref, w2vt_ref, kc_ref, vct_ref):
    r = r_ref[0]
    nr = r.shape[0]
    u = _dot(r, w1_ref[0])
    v = _dot(r, w1_ref[1])
    c = _dot(pe_ref[0], w1_ref[0]) + _dot(pe_ref[1], w1_ref[1])
    hid = u + pltpu.roll(v, nr - 1, axis=0)
    hid = (hid.reshape(nr // 8, 8, hid.shape[1]) + c[None]).reshape(nr, hid.shape[1])
    act = _gelu_tanh(hid).astype(BF16)
    for g in range(NSA_KV_GROUPS):
        kp, vp = g, NSA_KV_GROUPS + g
        kc_ref[0, g] = _dot(act[:, kp * CMP_HIDDEN:(kp + 1) * CMP_HIDDEN], w2k_ref[g]).astype(kc_ref.dtype)
        vct = _dot_nt(w2vt_ref[...], act[:, vp * CMP_HIDDEN:(vp + 1) * CMP_HIDDEN])
        vct_ref[0, g] = vct.astype(vct_ref.dtype)


def _cmp_first_layer(w1k, w1v, pek, pev):
    half = CMP_BLOCK // 2
    blocks, pes = [], []
    for p, (w1, pe1) in enumerate(((w1k, pek), (w1k, pek), (w1v, pev), (w1v, pev))):
        wp = w1.reshape(2, half, NSA_HEAD_DIM, CMP_HIDDEN).astype(BF16)
        blocks.append(jnp.pad(wp, ((0, 0), (0, 0), (0, 0), (p * CMP_HIDDEN, (CMP_PIECES - 1 - p) * CMP_HIDDEN))))
        pes.append(pe1.reshape(2, half, NSA_HEAD_DIM).astype(BF16))
    rows = half * CMP_PIECES * NSA_HEAD_DIM
    big = jnp.concatenate(blocks, axis=2).reshape(2, rows, CMP_PIECES * CMP_HIDDEN)
    pe = jnp.concatenate(pes, axis=2).reshape(2, 1, rows)
    return big, jnp.broadcast_to(pe, (2, 8, rows))


def _nsa_compress(cmp, w1big, pebig, w2k, w2vt, B, S):
    NR = S // CMP_STRIDE
    W = CMP_STRIDE * 2 * NSA_KV_W
    r = cmp.reshape(B, NR, W)
    once = pl.Buffered(1)
    full = lambda a: pl.BlockSpec(a.shape, lambda b: (0,) * a.ndim, pipeline_mode=once)
    return pl.pallas_call(
        _compress_kernel,
        grid=(B,),
        in_specs=[pl.BlockSpec((1, NR, W), lambda b: (b, 0, 0)), full(pebig), full(w1big), full(w2k), full(w2vt)],
        out_specs=[pl.BlockSpec((1, NSA_KV_GROUPS, NR, HEAD_SLOT), lambda b: (b, 0, 0, 0)),
                   pl.BlockSpec((1, NSA_KV_GROUPS, NSA_HEAD_DIM, NR), lambda b: (b, 0, 0, 0))],
        out_shape=[jax.ShapeDtypeStruct((B, NSA_KV_GROUPS, NR, HEAD_SLOT), BF16),
                   jax.ShapeDtypeStruct((B, NSA_KV_GROUPS, NSA_HEAD_DIM, NR), BF16)],
        compiler_params=_cparams(("parallel",)),
        name="nsa_compress",
    )(r, pebig, w1big, w2k, w2vt)


def _stack_heads(q):
    return jnp.concatenate([q[:, j * HEAD_SLOT:(j + 1) * HEAD_SLOT] for j in range(NSA_HPG)], axis=0)


CMP_VARIANTS = 4


def _cmp_attn_kernel(q_ref, kc_ref, vct_ref, ovt_ref, gate_ref, ocmpt_ref, sel_ref, gt_ref, imp_ref, *, tq):
    g = pl.program_id(1)
    i = pl.program_id(2)
    q2 = _stack_heads(q_ref[...])
    nc_all = kc_ref.shape[2]
    gt_ref[...] = _sigmoid(gate_ref[...].astype(F32)).T

    def attend(nc):
        kc = kc_ref[0, 0, 0:nc, :]
        vct = vct_ref[0, 0, :, 0:nc]
        sts = [_dot_nt(kc, q2[j * tq:(j + 1) * tq]) for j in range(NSA_HPG)]
        n = lax.broadcasted_iota(jnp.int32, (nc, tq), 0)
        t = i * tq + lax.broadcasted_iota(jnp.int32, (nc, tq), 1)
        mask = n * CMP_STRIDE + (CMP_BLOCK - 1) <= t
        ps = None
        for j in range(NSA_HPG):
            s = jnp.where(mask, sts[j], MASK_VALUE)
            m = jnp.max(s, axis=0, keepdims=True)
            e = jnp.where(mask, jnp.exp2(s - m), 0.0)
            den = jnp.maximum(jnp.sum(e, axis=0, keepdims=True), 1e-30)
            p = e * (1.0 / den)
            ps = p if ps is None else ps + p
            gate = gt_ref[pl.ds(g * (NSA_HPG * 3) + j * 3, 1), :]
            ocmpt_ref[0, 0, 0, :, j * tq:(j + 1) * tq] = (gate * _dot(vct, p.astype(BF16))).astype(ocmpt_ref.dtype)
        ovt = ovt_ref[:, 0:nc]
        hi = ps.astype(BF16)
        r1 = ps - hi.astype(F32)
        mid = r1.astype(BF16)
        lo = (r1 - mid.astype(F32)).astype(BF16)
        imp_ref[...] = _dot(ovt, hi) + _dot(ovt, mid) + _dot(ovt, lo)

    step = nc_all // CMP_VARIANTS
    variant = jnp.minimum(((i + 1) * (tq // CMP_STRIDE) - 1) // step, CMP_VARIANTS - 1)
    for v in range(CMP_VARIANTS):
        pl.when(variant == v)(functools.partial(attend, (v + 1) * step))

    imp = imp_ref[...]
    blk = lax.broadcasted_iota(jnp.int32, imp.shape, 0)
    tt = i * tq + lax.broadcasted_iota(jnp.int32, imp.shape, 1)
    cur = tt // SEL_BLOCK
    forced = (blk == 0) | (blk == cur) | (blk == cur - 1)
    causal = blk * SEL_BLOCK <= tt
    v0 = jnp.where(forced, FORCE_SCORE, jnp.where(causal, imp, -1.0))

    blk1 = blk[:, 0:LANES]

    def pick(_, tiles):
        out = []
        for v in tiles:
            mx = jnp.max(v, axis=0, keepdims=True)
            idx = jnp.min(jnp.where(v == mx, blk1, SEL_LANES), axis=0, keepdims=True)
            out.append(jnp.where(blk1 == idx, -jnp.inf, v))
        return tuple(out)

    tiles = lax.fori_loop(0, SEL_TOPN, pick, tuple(v0[:, u * LANES:(u + 1) * LANES] for u in range(tq // LANES)))
    for u, v in enumerate(tiles):
        sel_ref[0, 0, u * LANES:(u + 1) * LANES, :] = jnp.where(v == -jnp.inf, 1.0, 0.0).T.astype(sel_ref.dtype)


def _nsa_cmp_attn(proj, kc, vct, ovt, B, S):
    tq = NSA_TQ
    nq = S // tq
    G = NSA_KV_GROUPS
    NC = kc.shape[2]
    qw = NSA_HPG * HEAD_SLOT
    qblk = C_QEXT // qw
    kern = functools.partial(_cmp_attn_kernel, tq=tq)
    return pl.pallas_call(
        kern,
        grid=(B, G, nq),
        in_specs=[pl.BlockSpec((tq, qw), lambda b, g, i: (b * nq + i, qblk + g)),
                  pl.BlockSpec((1, 1, NC, HEAD_SLOT), lambda b, g, i: (b, g, 0, 0)),
                  pl.BlockSpec((1, 1, NSA_HEAD_DIM, NC), lambda b, g, i: (b, g, 0, 0)),
                  pl.BlockSpec(ovt.shape, lambda b, g, i: (0, 0)),
                  pl.BlockSpec((tq, LANES), lambda b, g, i: (b * nq + i, C_SMALL // LANES))],
        out_specs=[pl.BlockSpec((1, 1, 1, NSA_HEAD_DIM, NSA_HPG * tq), lambda b, g, i: (b, g, i, 0, 0)),
                   pl.BlockSpec((1, 1, tq, SEL_LANES), lambda b, g, i: (b, g, i, 0))],
        out_shape=[jax.ShapeDtypeStruct((B, G, nq, NSA_HEAD_DIM, NSA_HPG * tq), BF16),
                   jax.ShapeDtypeStruct((B, G, S, SEL_LANES), BF16)],
        scratch_shapes=[pltpu.VMEM((LANES, tq), F32), pltpu.VMEM((SEL_LANES, tq), F32)],
        compiler_params=_cparams(("parallel", "parallel", "parallel")),
        name="nsa_cmp_attn",
    )(proj, kc, vct, ovt, proj)


SEL_TK = 512
NSA_TQ = 256
WIN_KEYS = WINDOW + NSA_TQ


HALF = 256
NCH = NSA_HPG * NSA_TQ // HALF


def _sel_win_kernel(q_ref, ksel_ref, vselt_ref, kwin_ref, vwint_ref, sel_ref, et_ref, ocmpt_ref,
                    gate_ref, dbias_ref, wbias_ref, o_ref, s_ref, m_ref, acc_ref, gt_ref):
    g = pl.program_id(1)
    i = pl.program_id(2)
    tq = NSA_TQ
    t0 = i * tq
    q2 = _stack_heads(q_ref[...])
    selm1 = sel_ref[0, 0] - 1.0
    qx = jnp.concatenate([q2, jnp.concatenate([selm1] * NSA_HPG, axis=0)], axis=1)
    qxh = [qx[c * HALF:(c + 1) * HALF] for c in range(NCH)]
    q2h = [q2[c * HALF:(c + 1) * HALF] for c in range(NCH)]
    kpb = SEL_TK // LANES
    dv = NSA_HEAD_DIM

    wblk = jnp.maximum((t0 - WINDOW) // LANES, 0)
    start = pl.multiple_of(wblk * LANES, LANES)
    kw = kwin_ref[pl.ds(start, WIN_KEYS), :]
    vwt = jnp.concatenate([vwint_ref[wblk + u] for u in range(WIN_KEYS // LANES)], axis=1)
    wbias = wbias_ref[jnp.minimum(i, WINDOW // tq)].astype(F32)
    sw = _dot_nt(kw, q2)

    def scores_into(kv):
        off = pl.multiple_of(kv * SEL_TK, SEL_TK)
        kx = jnp.concatenate([ksel_ref[pl.ds(off, SEL_TK), :], et_ref[pl.ds(off, SEL_TK), :]], axis=1)
        for c in range(NCH):
            s_ref[:, c * HALF:(c + 1) * HALF] = _dot_nt(kx, qxh[c])

    def update(vt, s_chunks):
        for c in range(NCH):
            cols = slice(c * HALF, (c + 1) * HALF)
            s = s_chunks[c]
            m_old = m_ref[:, cols]
            m_new = jnp.maximum(m_old, jnp.max(s, axis=0, keepdims=True))
            p = jnp.exp2(s - m_new).astype(BF16)
            acc_ref[:, cols] = jnp.exp2(m_old - m_new) * acc_ref[:, cols] + _dot(vt, p)
            m_ref[:, cols] = m_new

    def sel_values(kv):
        return jnp.concatenate([vselt_ref[kv * kpb + u] for u in range(kpb)], axis=1)

    def load_scores():
        return [s_ref[:, c * HALF:(c + 1) * HALF] for c in range(NCH)]

    m_ref[...] = jnp.full(m_ref.shape, MASK_VALUE, F32)
    acc_ref[...] = jnp.zeros(acc_ref.shape, F32)
    kd = t0 // SEL_TK
    scores_into(0)
    sw = sw + jnp.concatenate([wbias] * NSA_HPG, axis=1)
    mw = jnp.max(sw, axis=0, keepdims=True)
    accw = _dot(vwt, jnp.exp2(sw - mw).astype(BF16))
    ot_win = accw[0:dv] * (1.0 / jnp.maximum(accw[dv:dv + 1], 1e-30))

    def body(kv, carry):
        s_chunks = load_scores()
        scores_into(kv + 1)
        update(sel_values(kv), s_chunks)
        return carry

    lax.fori_loop(0, kd, body, 0)
    dbias = dbias_ref[i % (SEL_TK // tq)].astype(F32)
    update(sel_values(kd), [sc + dbias for sc in load_scores()])
    acc = acc_ref[...]
    ot_slc = acc[0:dv] * (1.0 / jnp.maximum(acc[dv:dv + 1], 1e-30))

    gt_ref[...] = _sigmoid(gate_ref[...].astype(F32)).T
    ots = []
    for j in range(NSA_HPG):
        base = g * (NSA_HPG * 3) + j * 3
        ots.append(ocmpt_ref[0, 0, 0, :, j * tq:(j + 1) * tq].astype(F32)
                   + gt_ref[pl.ds(base + 1, 1), :] * ot_slc[:, j * tq:(j + 1) * tq]
                   + gt_ref[pl.ds(base + 2, 1), :] * ot_win[:, j * tq:(j + 1) * tq])
    for jp in range(NSA_HPG // 2):
        pair = jnp.concatenate([ots[2 * jp], ots[2 * jp + 1]], axis=0)
        o_ref[:, jp * LANES:(jp + 1) * LANES] = pair.T.astype(o_ref.dtype)


def _diag_bias():
    r = jnp.arange(SEL_TK)[None, :, None]
    c = jnp.arange(NSA_TQ)[None, None, :]
    off = (jnp.arange(SEL_TK // NSA_TQ) * NSA_TQ)[:, None, None]
    return jnp.where(r <= off + c, 0.0, MASK_VALUE).astype(BF16)


def _window_bias():
    r = jnp.arange(WIN_KEYS)[None, :, None]
    c = jnp.arange(NSA_TQ)[None, None, :]
    off = jnp.minimum(jnp.arange(WINDOW // NSA_TQ + 1) * NSA_TQ, WINDOW)[:, None, None]
    diff = off + c - r
    return jnp.where((diff >= 0) & (diff < WINDOW), 0.0, MASK_VALUE).astype(BF16)


def _nsa_sel_win(proj, vsel_t, vwin_t, sel, et, ocmpt, B, S):
    T = B * S
    tq = NSA_TQ
    assert HALF == tq
    dbias = _diag_bias()
    wbias = _window_bias()
    nq = S // tq
    G = NSA_KV_GROUPS
    qw = NSA_HPG * HEAD_SLOT
    qblk = C_QEXT // qw
    vsel_t = vsel_t.reshape(G, B, S // LANES, VT_ROWS, LANES)
    vwin_t = vwin_t.reshape(G, B, S // LANES, VT_ROWS, LANES)
    ow = NSA_HPG * NSA_HEAD_DIM
    kv_spec = lambda c: pl.BlockSpec((S, LANES), lambda b, g, i: (b, c // LANES))
    vt_spec = pl.BlockSpec((None, None, S // LANES, VT_ROWS, LANES), lambda b, g, i: (g, b, 0, 0, 0))
    return pl.pallas_call(
        _sel_win_kernel,
        grid=(B, G, nq),
        in_specs=[pl.BlockSpec((tq, qw), lambda b, g, i: (b * nq + i, qblk + g)),
                  kv_spec(C_KSEL), vt_spec, kv_spec(C_KWIN), vt_spec,
                  pl.BlockSpec((1, 1, tq, SEL_LANES), lambda b, g, i: (b, g, i, 0)),
                  pl.BlockSpec(et.shape, lambda b, g, i: (0, 0)),
                  pl.BlockSpec((1, 1, 1, NSA_HEAD_DIM, NSA_HPG * tq), lambda b, g, i: (b, g, i, 0, 0)),
                  pl.BlockSpec((tq, LANES), lambda b, g, i: (b * nq + i, C_SMALL // LANES)),
                  pl.BlockSpec(dbias.shape, lambda b, g, i: (0, 0, 0)),
                  pl.BlockSpec(wbias.shape, lambda b, g, i: (0, 0, 0))],
        out_specs=pl.BlockSpec((tq, ow), lambda b, g, i: (b * nq + i, g)),
        out_shape=jax.ShapeDtypeStruct((T, NSA_Q_W), BF16),
        scratch_shapes=[pltpu.VMEM((SEL_TK, NSA_HPG * tq), F32),
                        pltpu.VMEM((1, NSA_HPG * tq), F32),
                        pltpu.VMEM((VT_ROWS, NSA_HPG * tq), F32),
                        pltpu.VMEM((LANES, tq), F32)],
        compiler_params=_cparams(("parallel", "parallel", "parallel")),
        name="nsa_sel_win",
    )(proj, proj, vsel_t, proj, vwin_t, sel, et, ocmpt, proj, dbias, wbias)


TAIL_ROWS = 8


def _causal_conv(x, tail_ref, w, b):
    L, C = x.shape
    nv = L // TAIL_ROWS
    xx = jnp.concatenate([tail_ref[...], x], axis=0).reshape(nv + 1, TAIL_ROWS, C)
    sub = lax.broadcasted_iota(jnp.int32, (nv, TAIL_ROWS, C), 1)
    y = b + w[CONV_WIDTH - 1:CONV_WIDTH] * x
    for k in range(1, CONV_WIDTH):
        rot = pltpu.roll(xx, k, axis=1)
        shifted = jnp.where(sub >= k, rot[1:], rot[:-1]).reshape(L, C)
        y = y + w[CONV_WIDTH - 1 - k:CONV_WIDTH - k] * shifted
    tail_ref[...] = x[L - TAIL_ROWS:L]
    return y


SSD_CPS = 2


def _ssd_kernel(z_ref, xbc_ref, dtc_ref, dtr_ref, cw_ref, cb_ref, dtbc_ref, dtbr_ref, alc_ref, alr_ref,
                dsk_ref, nw_ref, o_ref, state_ref, tail_ref, y_ref):
    c = pl.program_id(1)
    L = SSD_CHUNK
    P = SSD_HEAD_DIM
    N = SSD_STATE
    hpg = SSD_HEADS // SSD_GROUPS

    @pl.when(c == 0)
    def _():
        state_ref[...] = jnp.zeros_like(state_ref)
        tail_ref[...] = jnp.zeros_like(tail_ref)

    conv = _causal_conv(xbc_ref[...].astype(F32), tail_ref, cw_ref[...], cb_ref[...])
    xbc = conv * _sigmoid(conv)
    dt_c = _softplus(dtc_ref[...] + dtbc_ref[...])
    dt_r = _softplus(dtr_ref[...] + dtbr_ref[...])
    a_c = dt_c * (-jnp.exp(alc_ref[...]))
    a_r = dt_r * (-jnp.exp(alr_ref[...]))
    ii = lax.broadcasted_iota(jnp.int32, (L, L), 0)
    jj = lax.broadcasted_iota(jnp.int32, (L, L), 1)
    tri = ii >= jj
    tril = tri.astype(F32)
    triu = (ii <= jj).astype(F32)

    for cc in range(SSD_CPS):
        rs = slice(cc * L, (cc + 1) * L)
        xs = xbc[rs, 0:SSD_INNER]
        bm = xbc[rs, SSD_INNER:SSD_INNER + SSD_GROUPS * N]
        cm = xbc[rs, SSD_INNER + SSD_GROUPS * N:SSD_INNER + 2 * SSD_GROUPS * N]
        acum_c = _dot_f32(tril, a_c[rs])
        acum_r = _dot_f32(a_r[:, rs], triu)
        for g in range(SSD_GROUPS):
            bg = bm[:, g * N:(g + 1) * N]
            cgb = cm[:, g * N:(g + 1) * N].astype(BF16)
            cb = _dot_nt(cgb, bg.astype(BF16))
            for j in range(hpg):
                h = g * hpg + j
                acb = jnp.broadcast_to(acum_c[:, h:h + 1], (L, L))
                a_last = acum_c[L - 1:L, h:h + 1]
                xh_raw = xs[:, h * P:(h + 1) * P]
                xh = (xh_raw * jnp.broadcast_to(dt_c[rs, h:h + 1], (L, P))).astype(BF16)
                lmat = jnp.where(tri, jnp.exp(acb - acum_r[h:h + 1, :]), 0.0)
                y = _dot((cb * lmat).astype(BF16), xh)
                prev = state_ref[h]
                y = y + _dot(cgb, prev.astype(BF16)) * jnp.exp(acb[:, 0:P])
                bd = (bg * jnp.exp(a_last - acb[:, 0:N])).astype(BF16)
                state_ref[h] = jnp.exp(a_last) * prev + _dot_tn(bd, xh)
                y_ref[rs, h * P:(h + 1) * P] = y + dsk_ref[:, h * P:(h + 1) * P] * xh_raw

    zf = z_ref[...].astype(F32)
    yg = y_ref[...] * (zf * _sigmoid(zf))
    ms = jnp.mean(yg * yg, axis=-1, keepdims=True)
    o_ref[...] = (yg * lax.rsqrt(ms + RMS_EPS) * nw_ref[...]).astype(o_ref.dtype)


def _ssd_mixer(proj, dt_col, dt_row, conv_w, conv_b, dt_bias, a_log, d_skip, norm_w, B, S):
    T = B * S
    L = SSD_CPS * SSD_CHUNK
    nc = S // L
    H = SSD_HEADS
    full = lambda a: pl.BlockSpec(a.shape, lambda b, c: (0, 0))
    cb2 = conv_b.reshape(1, -1)
    dtb_c = dt_bias.reshape(1, H)
    dtb_r = dt_bias.reshape(H, 1)
    al_c = a_log.reshape(1, H)
    al_r = a_log.reshape(H, 1)
    dsk = jnp.repeat(d_skip, SSD_HEAD_DIM).reshape(1, SSD_INNER)
    nw = norm_w.reshape(1, SSD_INNER)
    return pl.pallas_call(
        _ssd_kernel,
        grid=(B, nc),
        in_specs=[pl.BlockSpec((L, SSD_INNER), lambda b, c: (b * nc + c, C_SSDZ // SSD_INNER)),
                  pl.BlockSpec((L, SSD_XBC_W), lambda b, c: (b * nc + c, C_XBC // SSD_XBC_W)),
                  pl.BlockSpec((L, H), lambda b, c: (b * nc + c, 0)),
                  pl.BlockSpec((H, L), lambda b, c: (0, b * nc + c)),
                  full(conv_w), full(cb2), full(dtb_c), full(dtb_r), full(al_c), full(al_r),
                  full(dsk), full(nw)],
        out_specs=pl.BlockSpec((L, SSD_INNER), lambda b, c: (b * nc + c, 0)),
        out_shape=jax.ShapeDtypeStruct((T, SSD_INNER), BF16),
        scratch_shapes=[pltpu.VMEM((H, SSD_STATE, SSD_HEAD_DIM), F32),
                        pltpu.VMEM((TAIL_ROWS, SSD_XBC_W), F32),
                        pltpu.VMEM((L, SSD_INNER), F32)],
        compiler_params=_cparams(("parallel", "arbitrary")),
        name="ssd_mixer",
    )(proj, proj, dt_col, dt_row, conv_w, cb2, dtb_c, dtb_r, al_c, al_r, dsk, nw)


SCAN_GROUP = 8


def _lru_kernel(x_ref, y_ref, cw_ref, cb_ref, wa_ref, ba_ref, wx_ref, bx_ref, lam_ref, o_ref,
                h_ref, tail_ref, *, tc):
    c = pl.program_id(1)

    @pl.when(c == 0)
    def _():
        h_ref[...] = jnp.zeros_like(h_ref)
        tail_ref[...] = jnp.zeros_like(tail_ref)

    xr = _causal_conv(x_ref[...].astype(F32), tail_ref, cw_ref[...], cb_ref[...])
    xrb = xr.astype(BF16)
    r = _sigmoid(_dot(xrb, wa_ref[...]) + ba_ref[...])
    ig = _sigmoid(_dot(xrb, wx_ref[...]) + bx_ref[...])
    log_a = -LRU_C * r * _softplus(-lam_ref[...])
    a = jnp.exp(log_a)
    b = jnp.sqrt(1.0 - jnp.exp(2.0 * log_a)) * (ig * xr)
    ng = tc // SCAN_GROUP
    a = a.reshape(ng, SCAN_GROUP, a.shape[1])
    b = b.reshape(ng, SCAN_GROUP, b.shape[1])
    sub = lax.broadcasted_iota(jnp.int32, a.shape, 1)
    k = 1
    while k < SCAN_GROUP:
        keep = sub >= k
        a_s = jnp.where(keep, pltpu.roll(a, k, axis=1), 1.0)
        b_s = jnp.where(keep, pltpu.roll(b, k, axis=1), 0.0)
        b = a * b_s + b
        a = a * a_s
        k *= 2
    carry = h_ref[0:1, :]
    groups = []
    for rg in range(ng):
        hg = a[rg] * carry + b[rg]
        groups.append(hg)
        carry = hg[SCAN_GROUP - 1:SCAN_GROUP]
    h = jnp.concatenate(groups, axis=0)
    h_ref[...] = jnp.broadcast_to(carry, h_ref.shape)
    o_ref[...] = (h * _gelu_tanh(y_ref[...].astype(F32))).astype(o_ref.dtype)


def _block_diag(w):
    nb, c, d = w.shape
    eye = jnp.eye(nb, dtype=w.dtype)
    return (eye[:, None, :, None] * w[:, :, None, :]).reshape(nb * c, nb * d)


def _lru_mixer(proj, conv_w, conv_b, wa, ba, wx, bx, lam, B, S):
    T = B * S
    tc = min(256, S)
    nt = S // tc
    W = LRU_WIDTH
    wa_bd = _block_diag(wa).astype(BF16)
    wx_bd = _block_diag(wx).astype(BF16)
    row = lambda v: v.reshape(1, W)
    full = lambda a: pl.BlockSpec(a.shape, lambda b, c: (0, 0))
    args = (conv_w, row(conv_b), wa_bd, row(ba), wx_bd, row(bx), row(lam))
    return pl.pallas_call(
        functools.partial(_lru_kernel, tc=tc),
        grid=(B, nt),
        in_specs=[pl.BlockSpec((tc, W), lambda b, c: (b * nt + c, C_LRUX // W)),
                  pl.BlockSpec((tc, W), lambda b, c: (b * nt + c, C_LRUY // W))]
                 + [full(a) for a in args],
        out_specs=pl.BlockSpec((tc, W), lambda b, c: (b * nt + c, 0)),
        out_shape=jax.ShapeDtypeStruct((T, W), BF16),
        scratch_shapes=[pltpu.VMEM((TAIL_ROWS, W), F32), pltpu.VMEM((TAIL_ROWS, W), F32)],
        compiler_params=_cparams(("parallel", "arbitrary")),
        name="lru_mixer",
    )(proj, proj, *args)


def _merge_kernel(x_ref, mg_ref, on_ref, os_ref, ol_ref, pn_ref, ps_ref, pl_ref, wo_ref, g_ref, b_ref,
                  of_ref, ob_ref):
    d = D_MODEL
    gate = _sigmoid(mg_ref[...].astype(F32))
    mixed = (gate[:, 0:d] * _dot(on_ref[...], pn_ref[...])
             + gate[:, d:2 * d] * _dot(os_ref[...], ps_ref[...])
             + gate[:, 2 * d:3 * d] * _dot(ol_ref[...], pl_ref[...]))
    v = ALPHA * x_ref[...] + _dot(mixed.astype(BF16), wo_ref[...])
    out = _layer_norm(v, g_ref[...], b_ref[...])
    of_ref[...] = out
    ob_ref[...] = out.astype(BF16)


def _merge(x, proj, o_nsa, o_ssd, o_lru, pn, ps, plru, wo, g, b):
    T = x.shape[0]
    tm = min(512, T)
    d = D_MODEL
    rowblk = lambda w: pl.BlockSpec((tm, w), lambda i: (i, 0))
    full = lambda a: pl.BlockSpec(a.shape, lambda i: (0, 0))
    g2, b2 = g.reshape(1, d), b.reshape(1, d)
    return pl.pallas_call(
        _merge_kernel,
        grid=(T // tm,),
        in_specs=[rowblk(d), pl.BlockSpec((tm, 3 * d), lambda i: (i, C_MERGE // (3 * d))),
                  rowblk(o_nsa.shape[1]), rowblk(o_ssd.shape[1]), rowblk(o_lru.shape[1]),
                  full(pn), full(ps), full(plru), full(wo), full(g2), full(b2)],
        out_specs=[rowblk(d), rowblk(d)],
        out_shape=[jax.ShapeDtypeStruct((T, d), F32), jax.ShapeDtypeStruct((T, d), BF16)],
        compiler_params=_cparams(("parallel",)),
        name="merge_ln",
    )(x, proj, o_nsa, o_ssd, o_lru, pn, ps, plru, wo, g2, b2)


def _route(sel, aff):
    epg = EXPERTS_PER_GROUP
    scores = []
    for gi in range(N_EXPERT_GROUPS):
        v = sel[gi * epg:(gi + 1) * epg]
        pair = None
        for a in range(epg):
            for b in range(a + 1, epg):
                sab = v[a] + v[b]
                pair = sab if pair is None else jnp.maximum(pair, sab)
        scores.append(pair)
    best = jnp.zeros_like(scores[0], dtype=jnp.int32)
    best_s = scores[0]
    for gi in range(1, N_EXPERT_GROUPS):
        better = scores[gi] > best_s
        best = jnp.where(better, gi, best)
        best_s = jnp.where(better, scores[gi], best_s)
    chosen = []
    for k in range(N_EXPERTS):
        gi = k // epg
        rank = jnp.zeros_like(best)
        for o in range(gi * epg, (gi + 1) * epg):
            if o == k:
                continue
            ahead = (sel[o] > sel[k]) | ((sel[o] == sel[k]) & (o < k))
            rank = rank + ahead.astype(jnp.int32)
        chosen.append((best == gi) & (rank < TOP_K))
    wsum = None
    for k in range(N_EXPERTS):
        wk = jnp.where(chosen[k], aff[k], 0.0)
        wsum = wk if wsum is None else wsum + wk
    inv = 1.0 / wsum
    return [jnp.where(chosen[k], aff[k], 0.0) * inv for k in range(N_EXPERTS)]


EXPERT_PAIRS = N_EXPERTS // 2
PAIRS_PER_ITER = 2


def _moe_kernel(xb_ref, xf_ref, p_ref, rw_ref, rb_ref, pg_ref, pp_ref, wg_ref, wu_ref, wd_ref,
                g_ref, b_ref, of_ref, ob_ref, acc_ref, gates_ref):
    xb = xb_ref[...]
    tm = xb.shape[0]
    logits = _dot_nt(rw_ref[...], xb)
    aff = _sigmoid(logits)
    sel = aff + rb_ref[...]
    gate_rows = _route([sel[k:k + 1, :] for k in range(N_EXPERTS)],
                       [aff[k:k + 1, :] for k in range(N_EXPERTS)])
    gt = jnp.concatenate(gate_rows + [jnp.zeros((LANES - N_EXPERTS, tm), F32)], axis=0)
    gates_ref[...] = gt.T
    acc_ref[...] = _sigmoid(_dot(xb, pg_ref[...])) * _dot(p_ref[...].astype(BF16), pp_ref[...])
    lane = lax.broadcasted_iota(jnp.int32, (tm, LANES), 1)

    def gate_cols(k):
        gates = gates_ref[...]
        cols = [jnp.broadcast_to(jnp.sum(jnp.where(lane == 2 * k + u, gates, 0.0), axis=-1, keepdims=True),
                                 (tm, D_EXPERT)) for u in range(2)]
        return jnp.concatenate(cols, axis=1)

    def step(it, carry):
        ks = [it * PAIRS_PER_ITER + u for u in range(PAIRS_PER_ITER)]
        hgs = [[_dot(xb, wg_ref[2 * k + u]) for u in range(2)] for k in ks]
        hus = [[_dot(xb, wu_ref[2 * k + u]) for u in range(2)] for k in ks]
        for k, hg2, hu2 in zip(ks, hgs, hus):
            hg = jnp.concatenate(hg2, axis=1)
            h = (hg * _sigmoid(hg)) * jnp.concatenate(hu2, axis=1) * gate_cols(k)
            acc_ref[...] += _dot(h.astype(BF16), wd_ref[k])
        return carry

    lax.fori_loop(0, EXPERT_PAIRS // PAIRS_PER_ITER, step, 0)
    out = _layer_norm(ALPHA * xf_ref[...] + acc_ref[...], g_ref[...], b_ref[...])
    of_ref[...] = out
    ob_ref[...] = out.astype(BF16)


def _moe_ple(xb, xf, p, layer, rw_t, rb, pg, pp, wg, wu, wd, g, b):
    T = xb.shape[0]
    tm = min(512, T)
    d = D_MODEL
    rowblk = lambda w: pl.BlockSpec((tm, w), lambda i: (i, 0))
    once = pl.Buffered(1)
    full = lambda a: pl.BlockSpec(a.shape, lambda i: (0,) * a.ndim, pipeline_mode=once)
    g2, b2 = g.reshape(1, d), b.reshape(1, d)
    return pl.pallas_call(
        _moe_kernel,
        grid=(T // tm,),
        in_specs=[rowblk(d), rowblk(d), pl.BlockSpec((None, tm, PLE_DIM), lambda i: (layer, i, 0)),
                  full(rw_t), full(rb), full(pg), full(pp),
                  full(wg), full(wu), full(wd), full(g2), full(b2)],
        out_specs=[rowblk(d), rowblk(d)],
        out_shape=[jax.ShapeDtypeStruct((T, d), F32), jax.ShapeDtypeStruct((T, d), BF16)],
        scratch_shapes=[pltpu.VMEM((tm, d), F32), pltpu.VMEM((tm, LANES), F32)],
        compiler_params=_cparams(("parallel",)),
        name="moe_ple_ln",
    )(xb, xf, p, rw_t, rb, pg, pp, wg, wu, wd, g2, b2)


def _overlap_matrix(nc):
    n = jnp.arange(nc)[None, :]
    m = jnp.arange(SEL_LANES)[:, None]
    ratio = SEL_BLOCK // CMP_STRIDE
    ov = jnp.zeros((SEL_LANES, nc), F32)
    for k in range(CMP_BLOCK // CMP_STRIDE):
        ov = ov + ((n + k) // ratio == m).astype(F32)
    return ov.astype(BF16)


def _expand_matrix(S):
    c = jnp.arange(S)[:, None]
    m = jnp.arange(SEL_LANES)[None, :]
    return jnp.where(c // SEL_BLOCK == m, -MASK_VALUE, 0.0).astype(BF16)


def _pad_w2(w2):
    out = jnp.zeros((NSA_KV_GROUPS, CMP_HIDDEN, HEAD_SLOT), F32)
    for g in range(NSA_KV_GROUPS):
        out = out.at[g, :, g * NSA_HEAD_DIM:(g + 1) * NSA_HEAD_DIM].set(w2)
    return out.astype(BF16)


def kernel(x, p, w_in, nsa_pe_k, nsa_w1_k, nsa_w2_k, nsa_pe_v, nsa_w1_v, nsa_w2_v, ssd_conv_w, ssd_conv_b, ssd_dt_bias, ssd_a_log, ssd_d, ssd_norm_w, lru_conv_w, lru_conv_b, lru_wa, lru_ba, lru_wx, lru_bx, lru_lambda, proj_nsa, proj_ssd, proj_lru, w_out, ln1_g, ln1_b, router_w, router_b, exp_w_gate, exp_w_up, exp_w_down, ple_w_gate, ple_w_proj, ln2_g, ln2_b):
    B, S, d = x.shape
    T = B * S
    depth = w_in.shape[0]
    assert d == D_MODEL and S % SEL_TK == 0 and S >= WIN_KEYS and S // SEL_BLOCK <= SEL_LANES
    nr = S // CMP_STRIDE
    ov = _overlap_matrix(nr)
    emat = _expand_matrix(S)
    rw_t = router_w.T.astype(BF16)
    rb = router_b.reshape(N_EXPERTS, 1).astype(F32)

    xf = x.reshape(T, d)
    xb = xf
    for i in range(depth):
        proj = _in_proj(xb, _prep_w_in(w_in[i]))
        cmp, vsel_t, vwin_t, dt_col, dt_row = _aux_proj(xb, w_in[i])

        w1big, pebig = _cmp_first_layer(nsa_w1_k[i], nsa_w1_v[i], nsa_pe_k[i], nsa_pe_v[i])
        kc, vct = _nsa_compress(cmp, w1big, pebig, _pad_w2(nsa_w2_k[i]),
                                nsa_w2_v[i].T.astype(BF16), B, S)
        ocmpt, sel = _nsa_cmp_attn(proj, kc, vct, ov, B, S)
        o_nsa = _nsa_sel_win(proj, vsel_t, vwin_t, sel, emat, ocmpt, B, S)

        o_ssd = _ssd_mixer(proj, dt_col, dt_row, ssd_conv_w[i], ssd_conv_b[i], ssd_dt_bias[i],
                           ssd_a_log[i], ssd_d[i], ssd_norm_w[i], B, S)
        o_lru = _lru_mixer(proj, lru_conv_w[i], lru_conv_b[i], lru_wa[i], lru_ba[i], lru_wx[i],
                           lru_bx[i], lru_lambda[i], B, S)

        xf, xb = _merge(xf, proj, o_nsa, o_ssd, o_lru, proj_nsa[i].astype(BF16),
                        proj_ssd[i].astype(BF16), proj_lru[i].astype(BF16), w_out[i].astype(BF16),
                        ln1_g[i], ln1_b[i])
        xf, xb = _moe_ple(xb, xf, p.reshape(depth, T, PLE_DIM), i, rw_t, rb,
                          ple_w_gate[i].astype(BF16), ple_w_proj[i].astype(BF16),
                          exp_w_gate[i].astype(BF16), exp_w_up[i].astype(BF16),
                          exp_w_down[i].reshape(EXPERT_PAIRS, 2 * D_EXPERT, d).astype(BF16),
                          ln2_g[i], ln2_b[i])
    return xf.reshape(B, S, d)
```

```python
import functools
import math

import jax
import jax.numpy as jnp
from jax import lax
from jax.experimental import pallas as pl
from jax.experimental.pallas import tpu as pltpu

F32 = jnp.float32
BF16 = jnp.bfloat16

D_MODEL = 1024
PLE_DIM = 256
NSA_HEADS = 8
NSA_KV_GROUPS = 2
NSA_HEAD_DIM = 64
NSA_HPG = NSA_HEADS // NSA_KV_GROUPS
NSA_Q_W = NSA_HEADS * NSA_HEAD_DIM
NSA_KV_W = NSA_KV_GROUPS * NSA_HEAD_DIM
CMP_BLOCK = 32
CMP_STRIDE = 16
CMP_HIDDEN = 256
SEL_BLOCK = 64
SEL_TOPN = 16
WINDOW = 512
FORCE_SCORE = 1e4
MASK_VALUE = -1e30
LOG2E = 1.4426950408889634
SSD_HEADS = 8
SSD_HEAD_DIM = 64
SSD_INNER = SSD_HEADS * SSD_HEAD_DIM
SSD_GROUPS = 2
SSD_STATE = 64
SSD_CHUNK = 128
SSD_XBC_W = SSD_INNER + 2 * SSD_GROUPS * SSD_STATE
CONV_WIDTH = 4
LRU_WIDTH = 512
LRU_BLOCKS = 8
LRU_BLOCK_DIM = LRU_WIDTH // LRU_BLOCKS
LRU_C = 8.0
N_EXPERTS = 16
N_EXPERT_GROUPS = 4
EXPERTS_PER_GROUP = N_EXPERTS // N_EXPERT_GROUPS
TOP_K = 2
D_EXPERT = 256
DEPTH = 2
ALPHA = (2 * DEPTH) ** 0.25
LN_EPS = 1e-5
RMS_EPS = 1e-5
IN_SIZES = (NSA_Q_W, NSA_KV_W, NSA_KV_W, NSA_KV_W, NSA_KV_W, NSA_KV_W, NSA_KV_W, NSA_HEADS * 3,
            SSD_INNER, SSD_XBC_W, SSD_HEADS, LRU_WIDTH, LRU_WIDTH, 3 * D_MODEL)

LANES = 128
SEL_LANES = 128
HEAD_SLOT = 128

C_MERGE = 0
C_QEXT = 3072
C_SSDZ = 4096
C_LRUX = 4608
C_LRUY = 5120
C_KSEL = 5632
C_KWIN = 5760
C_SMALL = 5888
C_XBC = 6144
PROJ_W = 6912
PROJ_TN = 2304
GATE_W = NSA_HEADS * 3

VMEM_LIMIT = 56 * 1024 * 1024


def _cparams(sem):
    return pltpu.CompilerParams(dimension_semantics=sem, vmem_limit_bytes=VMEM_LIMIT)


def _sigmoid(x):
    return 1.0 / (1.0 + jnp.exp(-x))


def _softplus(x):
    return jnp.maximum(x, 0.0) + jnp.log(1.0 + jnp.exp(-jnp.abs(x)))


def _gelu_tanh(x):
    c = math.sqrt(2.0 / math.pi)
    return 0.5 * x * (1.0 + jnp.tanh(c * (x + 0.044715 * (x * x * x))))


def _dot(a, b):
    return jnp.dot(a, b, preferred_element_type=F32)


def _dot_nt(a, b):
    return lax.dot_general(a, b, (((1,), (1,)), ((), ())), preferred_element_type=F32)


def _dot_tn(a, b):
    return lax.dot_general(a, b, (((0,), (0,)), ((), ())), preferred_element_type=F32)


def _dot_f32(a, b):
    return jnp.dot(a, b, preferred_element_type=F32, precision=lax.Precision.HIGHEST)


def _layer_norm(v, g, b):
    mu = jnp.mean(v, axis=-1, keepdims=True)
    vc = v - mu
    var = jnp.mean(vc * vc, axis=-1, keepdims=True)
    return vc * lax.rsqrt(var + LN_EPS) * g + b


def _matmul_kernel(x_ref, w_ref, o_ref):
    o_ref[...] = _dot(x_ref[...].astype(BF16), w_ref[...]).astype(o_ref.dtype)


def _in_proj(xb, w, layer):
    T, K = xb.shape
    N = w.shape[2]
    tm = min(1024, T)
    tn = PROJ_TN
    return pl.pallas_call(
        _matmul_kernel,
        grid=(N // tn, T // tm),
        in_specs=[pl.BlockSpec((tm, K), lambda j, i: (i, 0)),
                  pl.BlockSpec((None, K, tn), lambda j, i: (layer, 0, j))],
        out_specs=pl.BlockSpec((tm, tn), lambda j, i: (i, j)),
        out_shape=jax.ShapeDtypeStruct((T, N), BF16),
        compiler_params=_cparams(("parallel", "parallel")),
        name="in_proj",
    )(xb, w)


def _split_w_in(w):
    offs = [0]
    for s in IN_SIZES:
        offs.append(offs[-1] + s)
    return [w[..., offs[k]:offs[k + 1]] for k in range(len(IN_SIZES))]


def _prep_w_in(w):
    pc = _split_w_in(w)
    lead = w.shape[:-1]
    q = pc[0].reshape(*lead, NSA_KV_GROUPS, NSA_HPG, NSA_HEAD_DIM) * (NSA_HEAD_DIM ** -0.5 * LOG2E)
    zeros = jnp.zeros_like(q)
    q_ext = jnp.stack([jnp.concatenate([q[..., 0, :, :], zeros[..., 0, :, :]], axis=-1),
                       jnp.concatenate([zeros[..., 1, :, :], q[..., 1, :, :]], axis=-1)], axis=-3)
    q_ext = q_ext.reshape(*lead, NSA_HEADS * HEAD_SLOT)
    small = jnp.pad(pc[7], [(0, 0)] * len(lead) + [(0, C_XBC - C_SMALL - GATE_W)])
    out = jnp.concatenate([pc[13], q_ext, pc[8], pc[11], pc[12], pc[3], pc[5], small, pc[9]], axis=-1)
    assert out.shape[-1] == PROJ_W
    return out.astype(BF16)


VT_ROWS = NSA_HEAD_DIM + 16


def _aux_proj_kernel(x_ref, wc_ref, wvt_ref, wdt_ref, wdtt_ref, cmp_ref, vst_ref, vwt_ref, dtc_ref, dtr_ref):
    x = x_ref[...].astype(BF16)
    tm = x.shape[0]
    cmp_ref[...] = _dot(x, wc_ref[...]).astype(cmp_ref.dtype)
    vt = _dot_nt(wvt_ref[...], x)
    dv = NSA_HEAD_DIM
    ones = jnp.ones((VT_ROWS - dv, LANES), vst_ref.dtype)
    for u in range(tm // LANES):
        for k, o_ref in enumerate((vst_ref, vwt_ref)):
            for g in range(NSA_KV_GROUPS):
                r0 = (k * NSA_KV_GROUPS + g) * dv
                o_ref[g, u, 0:dv, :] = vt[r0:r0 + dv, u * LANES:(u + 1) * LANES].astype(o_ref.dtype)
                o_ref[g, u, dv:VT_ROWS, :] = ones
    xf = x.astype(F32)
    dtc_ref[...] = _dot(xf, wdt_ref[...])
    dtr_ref[...] = _dot_nt(wdtt_ref[...], xf)


def _aux_weights(w):
    pc = _split_w_in(w)
    wc = jnp.concatenate([pc[1], pc[2]], axis=-1).astype(BF16)
    wvt = jnp.swapaxes(jnp.concatenate([pc[4], pc[6]], axis=-1), -1, -2).astype(BF16)
    return wc, wvt, pc[10], jnp.swapaxes(pc[10], -1, -2)


def _aux_proj(xb, aux_w, layer):
    T, K = xb.shape
    wc, wvt, wdt, wdtt = aux_w
    tm = min(1024, T)
    full = lambda a: pl.BlockSpec((None,) + a.shape[1:], lambda i: (layer, 0, 0))
    vt_shape = jax.ShapeDtypeStruct((NSA_KV_GROUPS, T // LANES, VT_ROWS, LANES), BF16)
    vt_spec = pl.BlockSpec((NSA_KV_GROUPS, tm // LANES, VT_ROWS, LANES), lambda i: (0, i, 0, 0))
    return pl.pallas_call(
        _aux_proj_kernel,
        grid=(T // tm,),
        in_specs=[pl.BlockSpec((tm, K), lambda i: (i, 0)), full(wc), full(wvt), full(wdt), full(wdtt)],
        out_specs=[pl.BlockSpec((tm, 2 * NSA_KV_W), lambda i: (i, 0)), vt_spec, vt_spec,
                   pl.BlockSpec((tm, SSD_HEADS), lambda i: (i, 0)),
                   pl.BlockSpec((SSD_HEADS, tm), lambda i: (0, i))],
        out_shape=[jax.ShapeDtypeStruct((T, 2 * NSA_KV_W), BF16), vt_shape, vt_shape,
                   jax.ShapeDtypeStruct((T, SSD_HEADS), F32), jax.ShapeDtypeStruct((SSD_HEADS, T), F32)],
        compiler_params=_cparams(("parallel",)),
        name="aux_proj",
    )(xb, wc, wvt, wdt, wdtt)


CMP_PIECES = 2 * NSA_KV_GROUPS


def _compress_kernel(r_ref, pe_ref, w1_ref, w2k_ref, w2vt_ref, kc_ref, vct_ref):
    r = r_ref[0]
    nr = r.shape[0]
    u = _dot(r, w1_ref[0])
    v = _dot(r, w1_ref[1])
    c = _dot(pe_ref[0], w1_ref[0]) + _dot(pe_ref[1], w1_ref[1])
    hid = u + pltpu.roll(v, nr - 1, axis=0)
    hid = (hid.reshape(nr // 8, 8, hid.shape[1]) + c[None]).reshape(nr, hid.shape[1])
    act = _gelu_tanh(hid).astype(BF16)
    for g in range(NSA_KV_GROUPS):
        kp, vp = g, NSA_KV_GROUPS + g
        kc_ref[0, g] = _dot(act[:, kp * CMP_HIDDEN:(kp + 1) * CMP_HIDDEN], w2k_ref[g]).astype(kc_ref.dtype)
        vct = _dot_nt(w2vt_ref[...], act[:, vp * CMP_HIDDEN:(vp + 1) * CMP_HIDDEN])
        vct_ref[0, g] = vct.astype(vct_ref.dtype)


def _cmp_first_layer(w1k, w1v, pek, pev):
    half = CMP_BLOCK // 2
    blocks, pes = [], []
    for p, (w1, pe1) in enumerate(((w1k, pek), (w1k, pek), (w1v, pev), (w1v, pev))):
        wp = w1.reshape(2, half, NSA_HEAD_DIM, CMP_HIDDEN).astype(BF16)
        blocks.append(jnp.pad(wp, ((0, 0), (0, 0), (0, 0), (p * CMP_HIDDEN, (CMP_PIECES - 1 - p) * CMP_HIDDEN))))
        pes.append(pe1.reshape(2, half, NSA_HEAD_DIM).astype(BF16))
    rows = half * CMP_PIECES * NSA_HEAD_DIM
    big = jnp.concatenate(blocks, axis=2).reshape(2, rows, CMP_PIECES * CMP_HIDDEN)
    pe = jnp.concatenate(pes, axis=2).reshape(2, 1, rows)
    return big, jnp.broadcast_to(pe, (2, 8, rows))


def _nsa_compress(cmp, w1big, pebig, w2k, w2vt, B, S):
    NR = S // CMP_STRIDE
    W = CMP_STRIDE * 2 * NSA_KV_W
    r = cmp.reshape(B, NR, W)
    once = pl.Buffered(1)
    full = lambda a: pl.BlockSpec(a.shape, lambda b: (0,) * a.ndim, pipeline_mode=once)
    return pl.pallas_call(
        _compress_kernel,
        grid=(B,),
        in_specs=[pl.BlockSpec((1, NR, W), lambda b: (b, 0, 0)), full(pebig), full(w1big), full(w2k), full(w2vt)],
        out_specs=[pl.BlockSpec((1, NSA_KV_GROUPS, NR, HEAD_SLOT), lambda b: (b, 0, 0, 0)),
                   pl.BlockSpec((1, NSA_KV_GROUPS, NSA_HEAD_DIM, NR), lambda b: (b, 0, 0, 0))],
        out_shape=[jax.ShapeDtypeStruct((B, NSA_KV_GROUPS, NR, HEAD_SLOT), BF16),
                   jax.ShapeDtypeStruct((B, NSA_KV_GROUPS, NSA_HEAD_DIM, NR), BF16)],
        compiler_params=_cparams(("parallel",)),
        name="nsa_compress",
    )(r, pebig, w1big, w2k, w2vt)


def _stack_heads(q):
    return jnp.concatenate([q[:, j * HEAD_SLOT:(j + 1) * HEAD_SLOT] for j in range(NSA_HPG)], axis=0)


CMP_VARIANTS = 4


def _cmp_attn_kernel(q_ref, kc_ref, vct_ref, ovt_ref, gate_ref, ocmpt_ref, sel_ref, gt_ref, imp_ref, *, tq):
    g = pl.program_id(1)
    i = pl.program_id(2)
    q2 = _stack_heads(q_ref[...])
    nc_all = kc_ref.shape[2]
    gt_ref[...] = _sigmoid(gate_ref[...].astype(F32)).T

    def attend(nc):
        kc = kc_ref[0, 0, 0:nc, :]
        vct = vct_ref[0, 0, :, 0:nc]
        sts = [_dot_nt(kc, q2[j * tq:(j + 1) * tq]) for j in range(NSA_HPG)]
        n = lax.broadcasted_iota(jnp.int32, (nc, tq), 0)
        t = i * tq + lax.broadcasted_iota(jnp.int32, (nc, tq), 1)
        mask = n * CMP_STRIDE + (CMP_BLOCK - 1) <= t
        ps = None
        for j in range(NSA_HPG):
            s = jnp.where(mask, sts[j], MASK_VALUE)
            m = jnp.max(s, axis=0, keepdims=True)
            e = jnp.where(mask, jnp.exp2(s - m), 0.0)
            den = jnp.maximum(jnp.sum(e, axis=0, keepdims=True), 1e-30)
            p = e * (1.0 / den)
            ps = p if ps is None else ps + p
            gate = gt_ref[pl.ds(g * (NSA_HPG * 3) + j * 3, 1), :]
            ocmpt_ref[0, 0, 0, :, j * tq:(j + 1) * tq] = (gate * _dot(vct, p.astype(BF16))).astype(ocmpt_ref.dtype)
        ovt = ovt_ref[:, 0:nc]
        hi = ps.astype(BF16)
        r1 = ps - hi.astype(F32)
        mid = r1.astype(BF16)
        lo = (r1 - mid.astype(F32)).astype(BF16)
        imp_ref[...] = _dot(ovt, hi) + _dot(ovt, mid) + _dot(ovt, lo)

    step = nc_all // CMP_VARIANTS
    variant = jnp.minimum(((i + 1) * (tq // CMP_STRIDE) - 1) // step, CMP_VARIANTS - 1)
    for v in range(CMP_VARIANTS):
        pl.when(variant == v)(functools.partial(attend, (v + 1) * step))

    imp = imp_ref[...]
    blk = lax.broadcasted_iota(jnp.int32, imp.shape, 0)
    tt = i * tq + lax.broadcasted_iota(jnp.int32, imp.shape, 1)
    cur = tt // SEL_BLOCK
    forced = (blk == 0) | (blk == cur) | (blk == cur - 1)
    causal = blk * SEL_BLOCK <= tt
    v0 = jnp.where(forced, FORCE_SCORE, jnp.where(causal, imp, -1.0))

    blk1 = blk[:, 0:LANES]

    def pick(_, tiles):
        out = []
        for v in tiles:
            mx = jnp.max(v, axis=0, keepdims=True)
            idx = jnp.min(jnp.where(v == mx, blk1, SEL_LANES), axis=0, keepdims=True)
            out.append(jnp.where(blk1 == idx, -jnp.inf, v))
        return tuple(out)

    tiles = lax.fori_loop(0, SEL_TOPN, pick, tuple(v0[:, u * LANES:(u + 1) * LANES] for u in range(tq // LANES)))
    for u, v in enumerate(tiles):
        sel_ref[0, 0, u * LANES:(u + 1) * LANES, :] = jnp.where(v == -jnp.inf, 1.0, 0.0).T.astype(sel_ref.dtype)


def _nsa_cmp_attn(proj, kc, vct, ovt, B, S):
    tq = NSA_TQ
    nq = S // tq
    G = NSA_KV_GROUPS
    NC = kc.shape[2]
    qw = NSA_HPG * HEAD_SLOT
    qblk = C_QEXT // qw
    kern = functools.partial(_cmp_attn_kernel, tq=tq)
    return pl.pallas_call(
        kern,
        grid=(B, G, nq),
        in_specs=[pl.BlockSpec((tq, qw), lambda b, g, i: (b * nq + i, qblk + g)),
                  pl.BlockSpec((1, 1, NC, HEAD_SLOT), lambda b, g, i: (b, g, 0, 0)),
                  pl.BlockSpec((1, 1, NSA_HEAD_DIM, NC), lambda b, g, i: (b, g, 0, 0)),
                  pl.BlockSpec(ovt.shape, lambda b, g, i: (0, 0)),
                  pl.BlockSpec((tq, LANES), lambda b, g, i: (b * nq + i, C_SMALL // LANES))],
        out_specs=[pl.BlockSpec((1, 1, 1, NSA_HEAD_DIM, NSA_HPG * tq), lambda b, g, i: (b, g, i, 0, 0)),
                   pl.BlockSpec((1, 1, tq, SEL_LANES), lambda b, g, i: (b, g, i, 0))],
        out_shape=[jax.ShapeDtypeStruct((B, G, nq, NSA_HEAD_DIM, NSA_HPG * tq), BF16),
                   jax.ShapeDtypeStruct((B, G, S, SEL_LANES), BF16)],
        scratch_shapes=[pltpu.VMEM((LANES, tq), F32), pltpu.VMEM((SEL_LANES, tq), F32)],
        compiler_params=_cparams(("parallel", "parallel", "parallel")),
        name="nsa_cmp_attn",
    )(proj, kc, vct, ovt, proj)


SEL_TK = 512
NSA_TQ = 256
WIN_KEYS = WINDOW + NSA_TQ


HALF = 256
NCH = NSA_HPG * NSA_TQ // HALF


def _sel_win_kernel(q_ref, ksel_ref, vselt_ref, kwin_ref, vwint_ref, sel_ref, et_ref, ocmpt_ref,
                    gate_ref, dbias_ref, wbias_ref, o_ref, s_ref, m_ref, acc_ref, gt_ref):
    g = pl.program_id(1)
    i = pl.program_id(2)
    tq = NSA_TQ
    t0 = i * tq
    q2 = _stack_heads(q_ref[...])
    selm1 = sel_ref[0, 0] - 1.0
    qx = jnp.concatenate([q2, jnp.concatenate([selm1] * NSA_HPG, axis=0)], axis=1)
    qxh = [qx[c * HALF:(c + 1) * HALF] for c in range(NCH)]
    q2h = [q2[c * HALF:(c + 1) * HALF] for c in range(NCH)]
    kpb = SEL_TK // LANES
    dv = NSA_HEAD_DIM

    wblk = jnp.maximum((t0 - WINDOW) // LANES, 0)
    start = pl.multiple_of(wblk * LANES, LANES)
    kw = kwin_ref[pl.ds(start, WIN_KEYS), :]
    vwt = jnp.concatenate([vwint_ref[wblk + u] for u in range(WIN_KEYS // LANES)], axis=1)
    wbias = wbias_ref[jnp.minimum(i, WINDOW // tq)].astype(F32)
    sw = _dot_nt(kw, q2)

    def scores_into(kv):
        off = pl.multiple_of(kv * SEL_TK, SEL_TK)
        kx = jnp.concatenate([ksel_ref[pl.ds(off, SEL_TK), :], et_ref[pl.ds(off, SEL_TK), :]], axis=1)
        for c in range(NCH):
            s_ref[:, c * HALF:(c + 1) * HALF] = _dot_nt(kx, qxh[c])

    def update(vt, s_chunks):
        for c in range(NCH):
            cols = slice(c * HALF, (c + 1) * HALF)
            s = s_chunks[c]
            m_old = m_ref[:, cols]
            m_new = jnp.maximum(m_old, jnp.max(s, axis=0, keepdims=True))
            p = jnp.exp2(s - m_new).astype(BF16)
            acc_ref[:, cols] = jnp.exp2(m_old - m_new) * acc_ref[:, cols] + _dot(vt, p)
            m_ref[:, cols] = m_new

    def sel_values(kv):
        return jnp.concatenate([vselt_ref[kv * kpb + u] for u in range(kpb)], axis=1)

    def load_scores():
        return [s_ref[:, c * HALF:(c + 1) * HALF] for c in range(NCH)]

    m_ref[...] = jnp.full(m_ref.shape, MASK_VALUE, F32)
    acc_ref[...] = jnp.zeros(acc_ref.shape, F32)
    kd = t0 // SEL_TK
    scores_into(0)
    sw = sw + jnp.concatenate([wbias] * NSA_HPG, axis=1)
    mw = jnp.max(sw, axis=0, keepdims=True)
    accw = _dot(vwt, jnp.exp2(sw - mw).astype(BF16))
    ot_win = accw[0:dv] * (1.0 / jnp.maximum(accw[dv:dv + 1], 1e-30))

    def body(kv, carry):
        s_chunks = load_scores()
        scores_into(kv + 1)
        update(sel_values(kv), s_chunks)
        return carry

    lax.fori_loop(0, kd, body, 0)
    dbias = dbias_ref[i % (SEL_TK // tq)].astype(F32)
    update(sel_values(kd), [sc + dbias for sc in load_scores()])
    acc = acc_ref[...]
    ot_slc = acc[0:dv] * (1.0 / jnp.maximum(acc[dv:dv + 1], 1e-30))

    gt_ref[...] = _sigmoid(gate_ref[...].astype(F32)).T
    ots = []
    for j in range(NSA_HPG):
        base = g * (NSA_HPG * 3) + j * 3
        ots.append(ocmpt_ref[0, 0, 0, :, j * tq:(j + 1) * tq].astype(F32)
                   + gt_ref[pl.ds(base + 1, 1), :] * ot_slc[:, j * tq:(j + 1) * tq]
                   + gt_ref[pl.ds(base + 2, 1), :] * ot_win[:, j * tq:(j + 1) * tq])
    for jp in range(NSA_HPG // 2):
        pair = jnp.concatenate([ots[2 * jp], ots[2 * jp + 1]], axis=0)
        o_ref[:, jp * LANES:(jp + 1) * LANES] = pair.T.astype(o_ref.dtype)


def _diag_bias():
    r = jnp.arange(SEL_TK)[None, :, None]
    c = jnp.arange(NSA_TQ)[None, None, :]
    off = (jnp.arange(SEL_TK // NSA_TQ) * NSA_TQ)[:, None, None]
    return jnp.where(r <= off + c, 0.0, MASK_VALUE).astype(BF16)


def _window_bias():
    r = jnp.arange(WIN_KEYS)[None, :, None]
    c = jnp.arange(NSA_TQ)[None, None, :]
    off = jnp.minimum(jnp.arange(WINDOW // NSA_TQ + 1) * NSA_TQ, WINDOW)[:, None, None]
    diff = off + c - r
    return jnp.where((diff >= 0) & (diff < WINDOW), 0.0, MASK_VALUE).astype(BF16)


def _nsa_sel_win(proj, vsel_t, vwin_t, sel, et, ocmpt, B, S):
    T = B * S
    tq = NSA_TQ
    assert HALF == tq
    dbias = _diag_bias()
    wbias = _window_bias()
    nq = S // tq
    G = NSA_KV_GROUPS
    qw = NSA_HPG * HEAD_SLOT
    qblk = C_QEXT // qw
    vsel_t = vsel_t.reshape(G, B, S // LANES, VT_ROWS, LANES)
    vwin_t = vwin_t.reshape(G, B, S // LANES, VT_ROWS, LANES)
    ow = NSA_HPG * NSA_HEAD_DIM
    kv_spec = lambda c: pl.BlockSpec((S, LANES), lambda b, g, i: (b, c // LANES))
    vt_spec = pl.BlockSpec((None, None, S // LANES, VT_ROWS, LANES), lambda b, g, i: (g, b, 0, 0, 0))
    return pl.pallas_call(
        _sel_win_kernel,
        grid=(B, G, nq),
        in_specs=[pl.BlockSpec((tq, qw), lambda b, g, i: (b * nq + i, qblk + g)),
                  kv_spec(C_KSEL), vt_spec, kv_spec(C_KWIN), vt_spec,
                  pl.BlockSpec((1, 1, tq, SEL_LANES), lambda b, g, i: (b, g, i, 0)),
                  pl.BlockSpec(et.shape, lambda b, g, i: (0, 0)),
                  pl.BlockSpec((1, 1, 1, NSA_HEAD_DIM, NSA_HPG * tq), lambda b, g, i: (b, g, i, 0, 0)),
                  pl.BlockSpec((tq, LANES), lambda b, g, i: (b * nq + i, C_SMALL // LANES)),
                  pl.BlockSpec(dbias.shape, lambda b, g, i: (0, 0, 0)),
                  pl.BlockSpec(wbias.shape, lambda b, g, i: (0, 0, 0))],
        out_specs=pl.BlockSpec((tq, ow), lambda b, g, i: (b * nq + i, g)),
        out_shape=jax.ShapeDtypeStruct((T, NSA_Q_W), BF16),
        scratch_shapes=[pltpu.VMEM((SEL_TK, NSA_HPG * tq), F32),
                        pltpu.VMEM((1, NSA_HPG * tq), F32),
                        pltpu.VMEM((VT_ROWS, NSA_HPG * tq), F32),
                        pltpu.VMEM((LANES, tq), F32)],
        compiler_params=_cparams(("parallel", "parallel", "parallel")),
        name="nsa_sel_win",
    )(proj, proj, vsel_t, proj, vwin_t, sel, et, ocmpt, proj, dbias, wbias)


TAIL_ROWS = 8


def _causal_conv(x, tail_ref, w, b):
    L, C = x.shape
    nv = L // TAIL_ROWS
    xx = jnp.concatenate([tail_ref[...], x], axis=0).reshape(nv + 1, TAIL_ROWS, C)
    sub = lax.broadcasted_iota(jnp.int32, (nv, TAIL_ROWS, C), 1)
    y = b + w[CONV_WIDTH - 1:CONV_WIDTH] * x
    for k in range(1, CONV_WIDTH):
        rot = pltpu.roll(xx, k, axis=1)
        shifted = jnp.where(sub >= k, rot[1:], rot[:-1]).reshape(L, C)
        y = y + w[CONV_WIDTH - 1 - k:CONV_WIDTH - k] * shifted
    tail_ref[...] = x[L - TAIL_ROWS:L]
    return y


SSD_CPS = 2


def _ssd_kernel(z_ref, xbc_ref, dtc_ref, dtr_ref, cw_ref, cb_ref, dtbc_ref, dtbr_ref, alc_ref, alr_ref,
                dsk_ref, nw_ref, o_ref, state_ref, tail_ref, y_ref):
    c = pl.program_id(1)
    L = SSD_CHUNK
    P = SSD_HEAD_DIM
    N = SSD_STATE
    hpg = SSD_HEADS // SSD_GROUPS

    @pl.when(c == 0)
    def _():
        state_ref[...] = jnp.zeros_like(state_ref)
        tail_ref[...] = jnp.zeros_like(tail_ref)

    conv = _causal_conv(xbc_ref[...].astype(F32), tail_ref, cw_ref[...], cb_ref[...])
    xbc = conv * _sigmoid(conv)
    dt_c = _softplus(dtc_ref[...] + dtbc_ref[...])
    dt_r = _softplus(dtr_ref[...] + dtbr_ref[...])
    a_c = dt_c * (-jnp.exp(alc_ref[...]))
    a_r = dt_r * (-jnp.exp(alr_ref[...]))
    ii = lax.broadcasted_iota(jnp.int32, (L, L), 0)
    jj = lax.broadcasted_iota(jnp.int32, (L, L), 1)
    tri = ii >= jj
    tril = tri.astype(F32)
    triu = (ii <= jj).astype(F32)

    for cc in range(SSD_CPS):
        rs = slice(cc * L, (cc + 1) * L)
        xs = xbc[rs, 0:SSD_INNER]
        bm = xbc[rs, SSD_INNER:SSD_INNER + SSD_GROUPS * N]
        cm = xbc[rs, SSD_INNER + SSD_GROUPS * N:SSD_INNER + 2 * SSD_GROUPS * N]
        acum_c = _dot_f32(tril, a_c[rs])
        acum_r = _dot_f32(a_r[:, rs], triu)
        for g in range(SSD_GROUPS):
            bg = bm[:, g * N:(g + 1) * N]
            cgb = cm[:, g * N:(g + 1) * N].astype(BF16)
            cb = _dot_nt(cgb, bg.astype(BF16))
            for j in range(hpg):
                h = g * hpg + j
                acb = jnp.broadcast_to(acum_c[:, h:h + 1], (L, L))
                a_last = acum_c[L - 1:L, h:h + 1]
                xh_raw = xs[:, h * P:(h + 1) * P]
                xh = (xh_raw * jnp.broadcast_to(dt_c[rs, h:h + 1], (L, P))).astype(BF16)
                lmat = jnp.where(tri, jnp.exp(acb - acum_r[h:h + 1, :]), 0.0)
                y = _dot((cb * lmat).astype(BF16), xh)
                prev = state_ref[h]
                y = y + _dot(cgb, prev.astype(BF16)) * jnp.exp(acb[:, 0:P])
                bd = (bg * jnp.exp(a_last - acb[:, 0:N])).astype(BF16)
                state_ref[h] = jnp.exp(a_last) * prev + _dot_tn(bd, xh)
                y_ref[rs, h * P:(h + 1) * P] = y + dsk_ref[:, h * P:(h + 1) * P] * xh_raw

    zf = z_ref[...].astype(F32)
    yg = y_ref[...] * (zf * _sigmoid(zf))
    ms = jnp.mean(yg * yg, axis=-1, keepdims=True)
    o_ref[...] = (yg * lax.rsqrt(ms + RMS_EPS) * nw_ref[...]).astype(o_ref.dtype)


def _ssd_mixer(proj, dt_col, dt_row, conv_w, conv_b, dt_bias, a_log, d_skip, norm_w, B, S):
    T = B * S
    L = SSD_CPS * SSD_CHUNK
    nc = S // L
    H = SSD_HEADS
    full = lambda a: pl.BlockSpec(a.shape, lambda b, c: (0, 0))
    cb2 = conv_b.reshape(1, -1)
    dtb_c = dt_bias.reshape(1, H)
    dtb_r = dt_bias.reshape(H, 1)
    al_c = a_log.reshape(1, H)
    al_r = a_log.reshape(H, 1)
    dsk = jnp.repeat(d_skip, SSD_HEAD_DIM).reshape(1, SSD_INNER)
    nw = norm_w.reshape(1, SSD_INNER)
    return pl.pallas_call(
        _ssd_kernel,
        grid=(B, nc),
        in_specs=[pl.BlockSpec((L, SSD_INNER), lambda b, c: (b * nc + c, C_SSDZ // SSD_INNER)),
                  pl.BlockSpec((L, SSD_XBC_W), lambda b, c: (b * nc + c, C_XBC // SSD_XBC_W)),
                  pl.BlockSpec((L, H), lambda b, c: (b * nc + c, 0)),
                  pl.BlockSpec((H, L), lambda b, c: (0, b * nc + c)),
                  full(conv_w), full(cb2), full(dtb_c), full(dtb_r), full(al_c), full(al_r),
                  full(dsk), full(nw)],
        out_specs=pl.BlockSpec((L, SSD_INNER), lambda b, c: (b * nc + c, 0)),
        out_shape=jax.ShapeDtypeStruct((T, SSD_INNER), BF16),
        scratch_shapes=[pltpu.VMEM((H, SSD_STATE, SSD_HEAD_DIM), F32),
                        pltpu.VMEM((TAIL_ROWS, SSD_XBC_W), F32),
                        pltpu.VMEM((L, SSD_INNER), F32)],
        compiler_params=_cparams(("parallel", "arbitrary")),
        name="ssd_mixer",
    )(proj, proj, dt_col, dt_row, conv_w, cb2, dtb_c, dtb_r, al_c, al_r, dsk, nw)


SCAN_GROUP = 8


def _lru_kernel(x_ref, y_ref, cw_ref, cb_ref, wa_ref, ba_ref, wx_ref, bx_ref, lam_ref, o_ref,
                h_ref, tail_ref, *, tc):
    c = pl.program_id(1)

    @pl.when(c == 0)
    def _():
        h_ref[...] = jnp.zeros_like(h_ref)
        tail_ref[...] = jnp.zeros_like(tail_ref)

    xr = _causal_conv(x_ref[...].astype(F32), tail_ref, cw_ref[...], cb_ref[...])
    xrb = xr.astype(BF16)
    r = _sigmoid(_dot(xrb, wa_ref[...]) + ba_ref[...])
    ig = _sigmoid(_dot(xrb, wx_ref[...]) + bx_ref[...])
    log_a = -LRU_C * r * _softplus(-lam_ref[...])
    a = jnp.exp(log_a)
    b = jnp.sqrt(1.0 - jnp.exp(2.0 * log_a)) * (ig * xr)
    ng = tc // SCAN_GROUP
    a = a.reshape(ng, SCAN_GROUP, a.shape[1])
    b = b.reshape(ng, SCAN_GROUP, b.shape[1])
    sub = lax.broadcasted_iota(jnp.int32, a.shape, 1)
    k = 1
    while k < SCAN_GROUP:
        keep = sub >= k
        a_s = jnp.where(keep, pltpu.roll(a, k, axis=1), 1.0)
        b_s = jnp.where(keep, pltpu.roll(b, k, axis=1), 0.0)
        b = a * b_s + b
        a = a * a_s
        k *= 2
    carry = h_ref[0:1, :]
    groups = []
    for rg in range(ng):
        hg = a[rg] * carry + b[rg]
        groups.append(hg)
        carry = hg[SCAN_GROUP - 1:SCAN_GROUP]
    h = jnp.concatenate(groups, axis=0)
    h_ref[...] = jnp.broadcast_to(carry, h_ref.shape)
    o_ref[...] = (h * _gelu_tanh(y_ref[...].astype(F32))).astype(o_ref.dtype)


def _block_diag(w):
    nb, c, d = w.shape
    eye = jnp.eye(nb, dtype=w.dtype)
    return (eye[:, None, :, None] * w[:, :, None, :]).reshape(nb * c, nb * d)


def _lru_mixer(proj, conv_w, conv_b, wa, ba, wx, bx, lam, B, S):
    T = B * S
    tc = min(256, S)
    nt = S // tc
    W = LRU_WIDTH
    wa_bd = _block_diag(wa).astype(BF16)
    wx_bd = _block_diag(wx).astype(BF16)
    row = lambda v: v.reshape(1, W)
    full = lambda a: pl.BlockSpec(a.shape, lambda b, c: (0, 0))
    args = (conv_w, row(conv_b), wa_bd, row(ba), wx_bd, row(bx), row(lam))
    return pl.pallas_call(
        functools.partial(_lru_kernel, tc=tc),
        grid=(B, nt),
        in_specs=[pl.BlockSpec((tc, W), lambda b, c: (b * nt + c, C_LRUX // W)),
                  pl.BlockSpec((tc, W), lambda b, c: (b * nt + c, C_LRUY // W))]
                 + [full(a) for a in args],
        out_specs=pl.BlockSpec((tc, W), lambda b, c: (b * nt + c, 0)),
        out_shape=jax.ShapeDtypeStruct((T, W), BF16),
        scratch_shapes=[pltpu.VMEM((TAIL_ROWS, W), F32), pltpu.VMEM((TAIL_ROWS, W), F32)],
        compiler_params=_cparams(("parallel", "arbitrary")),
        name="lru_mixer",
    )(proj, proj, *args)


def _merge_kernel(x_ref, mg_ref, on_ref, os_ref, ol_ref, pn_ref, ps_ref, pl_ref, wo_ref, g_ref, b_ref,
                  of_ref, ob_ref):
    d = D_MODEL
    gate = _sigmoid(mg_ref[...].astype(F32))
    mixed = (gate[:, 0:d] * _dot(on_ref[...], pn_ref[...])
             + gate[:, d:2 * d] * _dot(os_ref[...], ps_ref[...])
             + gate[:, 2 * d:3 * d] * _dot(ol_ref[...], pl_ref[...]))
    v = ALPHA * x_ref[...] + _dot(mixed.astype(BF16), wo_ref[...])
    out = _layer_norm(v, g_ref[...], b_ref[...])
    of_ref[...] = out
    ob_ref[...] = out.astype(BF16)


def _merge(x, proj, o_nsa, o_ssd, o_lru, pn, ps, plru, wo, g, b, layer):
    T = x.shape[0]
    tm = min(512, T)
    d = D_MODEL
    rowblk = lambda w: pl.BlockSpec((tm, w), lambda i: (i, 0))
    full = lambda a: pl.BlockSpec((None,) + a.shape[1:], lambda i: (layer, 0, 0))
    g2, b2 = g.reshape(-1, 1, d), b.reshape(-1, 1, d)
    return pl.pallas_call(
        _merge_kernel,
        grid=(T // tm,),
        in_specs=[rowblk(d), pl.BlockSpec((tm, 3 * d), lambda i: (i, C_MERGE // (3 * d))),
                  rowblk(o_nsa.shape[1]), rowblk(o_ssd.shape[1]), rowblk(o_lru.shape[1]),
                  full(pn), full(ps), full(plru), full(wo), full(g2), full(b2)],
        out_specs=[rowblk(d), rowblk(d)],
        out_shape=[jax.ShapeDtypeStruct((T, d), F32), jax.ShapeDtypeStruct((T, d), BF16)],
        compiler_params=_cparams(("parallel",)),
        name="merge_ln",
    )(x, proj, o_nsa, o_ssd, o_lru, pn, ps, plru, wo, g2, b2)


def _route(sel, aff):
    epg = EXPERTS_PER_GROUP
    scores = []
    for gi in range(N_EXPERT_GROUPS):
        v = sel[gi * epg:(gi + 1) * epg]
        pair = None
        for a in range(epg):
            for b in range(a + 1, epg):
                sab = v[a] + v[b]
                pair = sab if pair is None else jnp.maximum(pair, sab)
        scores.append(pair)
    best = jnp.zeros_like(scores[0], dtype=jnp.int32)
    best_s = scores[0]
    for gi in range(1, N_EXPERT_GROUPS):
        better = scores[gi] > best_s
        best = jnp.where(better, gi, best)
        best_s = jnp.where(better, scores[gi], best_s)
    chosen = []
    for k in range(N_EXPERTS):
        gi = k // epg
        rank = jnp.zeros_like(best)
        for o in range(gi * epg, (gi + 1) * epg):
            if o == k:
                continue
            ahead = (sel[o] > sel[k]) | ((sel[o] == sel[k]) & (o < k))
            rank = rank + ahead.astype(jnp.int32)
        chosen.append((best == gi) & (rank < TOP_K))
    wsum = None
    for k in range(N_EXPERTS):
        wk = jnp.where(chosen[k], aff[k], 0.0)
        wsum = wk if wsum is None else wsum + wk
    inv = 1.0 / wsum
    return [jnp.where(chosen[k], aff[k], 0.0) * inv for k in range(N_EXPERTS)]


EXPERT_PAIRS = N_EXPERTS // 2
PAIRS_PER_ITER = 2


def _moe_kernel(xb_ref, xf_ref, p_ref, rw_ref, rb_ref, pg_ref, pp_ref, wg_ref, wu_ref, wd_ref,
                g_ref, b_ref, of_ref, ob_ref, acc_ref, gates_ref):
    xb = xb_ref[...]
    tm = xb.shape[0]
    logits = _dot_nt(rw_ref[...], xb)
    aff = _sigmoid(logits)
    sel = aff + rb_ref[...]
    gate_rows = _route([sel[k:k + 1, :] for k in range(N_EXPERTS)],
                       [aff[k:k + 1, :] for k in range(N_EXPERTS)])
    gt = jnp.concatenate(gate_rows + [jnp.zeros((LANES - N_EXPERTS, tm), F32)], axis=0)
    gates_ref[...] = gt.T
    acc_ref[...] = _sigmoid(_dot(xb, pg_ref[...])) * _dot(p_ref[...].astype(BF16), pp_ref[...])
    lane = lax.broadcasted_iota(jnp.int32, (tm, LANES), 1)

    def gate_cols(k):
        gates = gates_ref[...]
        cols = [jnp.broadcast_to(jnp.sum(jnp.where(lane == 2 * k + u, gates, 0.0), axis=-1, keepdims=True),
                                 (tm, D_EXPERT)) for u in range(2)]
        return jnp.concatenate(cols, axis=1)

    def step(it, carry):
        ks = [it * PAIRS_PER_ITER + u for u in range(PAIRS_PER_ITER)]
        hgs = [[_dot(xb, wg_ref[2 * k + u]) for u in range(2)] for k in ks]
        hus = [[_dot(xb, wu_ref[2 * k + u]) for u in range(2)] for k in ks]
        for k, hg2, hu2 in zip(ks, hgs, hus):
            hg = jnp.concatenate(hg2, axis=1)
            h = (hg * _sigmoid(hg)) * jnp.concatenate(hu2, axis=1) * gate_cols(k)
            acc_ref[...] += _dot(h.astype(BF16), wd_ref[k])
        return carry

    lax.fori_loop(0, EXPERT_PAIRS // PAIRS_PER_ITER, step, 0)
    out = _layer_norm(ALPHA * xf_ref[...] + acc_ref[...], g_ref[...], b_ref[...])
    of_ref[...] = out
    ob_ref[...] = out.astype(BF16)


def _moe_ple(xb, xf, p, layer, rw_t, rb, pg, pp, wg, wu, wd, g, b):
    T = xb.shape[0]
    tm = min(512, T)
    d = D_MODEL
    rowblk = lambda w: pl.BlockSpec((tm, w), lambda i: (i, 0))
    once = pl.Buffered(1)
    full = lambda a: pl.BlockSpec(a.shape, lambda i: (0,) * a.ndim, pipeline_mode=once)
    lyr = lambda a: pl.BlockSpec((None,) + a.shape[1:], lambda i: (layer,) + (0,) * (a.ndim - 1), pipeline_mode=once)
    g2, b2 = g.reshape(-1, 1, d), b.reshape(-1, 1, d)
    return pl.pallas_call(
        _moe_kernel,
        grid=(T // tm,),
        in_specs=[rowblk(d), rowblk(d), pl.BlockSpec((None, tm, PLE_DIM), lambda i: (layer, i, 0)),
                  full(rw_t), full(rb), lyr(pg), lyr(pp),
                  lyr(wg), lyr(wu), lyr(wd), lyr(g2), lyr(b2)],
        out_specs=[rowblk(d), rowblk(d)],
        out_shape=[jax.ShapeDtypeStruct((T, d), F32), jax.ShapeDtypeStruct((T, d), BF16)],
        scratch_shapes=[pltpu.VMEM((tm, d), F32), pltpu.VMEM((tm, LANES), F32)],
        compiler_params=_cparams(("parallel",)),
        name="moe_ple_ln",
    )(xb, xf, p, rw_t, rb, pg, pp, wg, wu, wd, g2, b2)


def _overlap_matrix(nc):
    n = jnp.arange(nc)[None, :]
    m = jnp.arange(SEL_LANES)[:, None]
    ratio = SEL_BLOCK // CMP_STRIDE
    ov = jnp.zeros((SEL_LANES, nc), F32)
    for k in range(CMP_BLOCK // CMP_STRIDE):
        ov = ov + ((n + k) // ratio == m).astype(F32)
    return ov.astype(BF16)


def _expand_matrix(S):
    c = jnp.arange(S)[:, None]
    m = jnp.arange(SEL_LANES)[None, :]
    return jnp.where(c // SEL_BLOCK == m, -MASK_VALUE, 0.0).astype(BF16)


def _pad_w2(w2):
    out = jnp.zeros((NSA_KV_GROUPS, CMP_HIDDEN, HEAD_SLOT), F32)
    for g in range(NSA_KV_GROUPS):
        out = out.at[g, :, g * NSA_HEAD_DIM:(g + 1) * NSA_HEAD_DIM].set(w2)
    return out.astype(BF16)


def kernel(x, p, w_in, nsa_pe_k, nsa_w1_k, nsa_w2_k, nsa_pe_v, nsa_w1_v, nsa_w2_v, ssd_conv_w, ssd_conv_b, ssd_dt_bias, ssd_a_log, ssd_d, ssd_norm_w, lru_conv_w, lru_conv_b, lru_wa, lru_ba, lru_wx, lru_bx, lru_lambda, proj_nsa, proj_ssd, proj_lru, w_out, ln1_g, ln1_b, router_w, router_b, exp_w_gate, exp_w_up, exp_w_down, ple_w_gate, ple_w_proj, ln2_g, ln2_b):
    B, S, d = x.shape
    T = B * S
    depth = w_in.shape[0]
    assert d == D_MODEL and S % SEL_TK == 0 and S >= WIN_KEYS and S // SEL_BLOCK <= SEL_LANES
    nr = S // CMP_STRIDE
    ov = _overlap_matrix(nr)
    emat = _expand_matrix(S)
    rw_t = router_w.T.astype(BF16)
    rb = router_b.reshape(N_EXPERTS, 1).astype(F32)

    w_main = _prep_w_in(w_in)
    aux_w = _aux_weights(w_in)
    pn, ps, plru, wo = (a.astype(BF16) for a in (proj_nsa, proj_ssd, proj_lru, w_out))
    pg, pp, wg, wu = (a.astype(BF16) for a in (ple_w_gate, ple_w_proj, exp_w_gate, exp_w_up))
    wd = exp_w_down.reshape(depth, EXPERT_PAIRS, 2 * D_EXPERT, d).astype(BF16)
    p3 = p.reshape(depth, T, PLE_DIM)

    xf = x.reshape(T, d)
    xb = xf
    for i in range(depth):
        proj = _in_proj(xb, w_main, i)
        cmp, vsel_t, vwin_t, dt_col, dt_row = _aux_proj(xb, aux_w, i)

        w1big, pebig = _cmp_first_layer(nsa_w1_k[i], nsa_w1_v[i], nsa_pe_k[i], nsa_pe_v[i])
        kc, vct = _nsa_compress(cmp, w1big, pebig, _pad_w2(nsa_w2_k[i]),
                                nsa_w2_v[i].T.astype(BF16), B, S)
        ocmpt, sel = _nsa_cmp_attn(proj, kc, vct, ov, B, S)
        o_nsa = _nsa_sel_win(proj, vsel_t, vwin_t, sel, emat, ocmpt, B, S)

        o_ssd = _ssd_mixer(proj, dt_col, dt_row, ssd_conv_w[i], ssd_conv_b[i], ssd_dt_bias[i],
                           ssd_a_log[i], ssd_d[i], ssd_norm_w[i], B, S)
        o_lru = _lru_mixer(proj, lru_conv_w[i], lru_conv_b[i], lru_wa[i], lru_ba[i], lru_wx[i],
                           lru_bx[i], lru_lambda[i], B, S)

        xf, xb = _merge(xf, proj, o_nsa, o_ssd, o_lru, pn, ps, plru, wo, ln1_g, ln1_b, i)
        xf, xb = _moe_ple(xb, xf, p3, i, rw_t, rb, pg, pp, wg, wu, wd, ln2_g, ln2_b)
    return xf.reshape(B, S, d)
```

```python
import functools
import math

import jax
import jax.numpy as jnp
from jax import lax
from jax.experimental import pallas as pl
from jax.experimental.pallas import tpu as pltpu

F32 = jnp.float32
BF16 = jnp.bfloat16

D_MODEL = 1024
PLE_DIM = 256
NSA_HEADS = 8
NSA_KV_GROUPS = 2
NSA_HEAD_DIM = 64
NSA_HPG = NSA_HEADS // NSA_KV_GROUPS
NSA_Q_W = NSA_HEADS * NSA_HEAD_DIM
NSA_KV_W = NSA_KV_GROUPS * NSA_HEAD_DIM
CMP_BLOCK = 32
CMP_STRIDE = 16
CMP_HIDDEN = 256
SEL_BLOCK = 64
SEL_TOPN = 16
WINDOW = 512
FORCE_SCORE = 1e4
MASK_VALUE = -1e30
LOG2E = 1.4426950408889634
SSD_HEADS = 8
SSD_HEAD_DIM = 64
SSD_INNER = SSD_HEADS * SSD_HEAD_DIM
SSD_GROUPS = 2
SSD_STATE = 64
SSD_CHUNK = 128
SSD_XBC_W = SSD_INNER + 2 * SSD_GROUPS * SSD_STATE
CONV_WIDTH = 4
LRU_WIDTH = 512
LRU_BLOCKS = 8
LRU_BLOCK_DIM = LRU_WIDTH // LRU_BLOCKS
LRU_C = 8.0
N_EXPERTS = 16
N_EXPERT_GROUPS = 4
EXPERTS_PER_GROUP = N_EXPERTS // N_EXPERT_GROUPS
TOP_K = 2
D_EXPERT = 256
DEPTH = 2
ALPHA = (2 * DEPTH) ** 0.25
LN_EPS = 1e-5
RMS_EPS = 1e-5
IN_SIZES = (NSA_Q_W, NSA_KV_W, NSA_KV_W, NSA_KV_W, NSA_KV_W, NSA_KV_W, NSA_KV_W, NSA_HEADS * 3,
            SSD_INNER, SSD_XBC_W, SSD_HEADS, LRU_WIDTH, LRU_WIDTH, 3 * D_MODEL)

LANES = 128
SEL_LANES = 128
HEAD_SLOT = 128

C_MERGE = 0
C_QEXT = 3072
C_SSDZ = 4096
C_LRUX = 4608
C_LRUY = 5120
C_KSEL = 5632
C_KWIN = 5760
C_SMALL = 5888
C_XBC = 6144
PROJ_W = 6912
PROJ_TN = 2304
GATE_W = NSA_HEADS * 3

VMEM_LIMIT = 56 * 1024 * 1024


def _cparams(sem):
    return pltpu.CompilerParams(dimension_semantics=sem, vmem_limit_bytes=VMEM_LIMIT)


def _sigmoid(x):
    return 1.0 / (1.0 + jnp.exp(-x))


def _softplus(x):
    return jnp.maximum(x, 0.0) + jnp.log(1.0 + jnp.exp(-jnp.abs(x)))


def _gelu_tanh(x):
    c = math.sqrt(2.0 / math.pi)
    return 0.5 * x * (1.0 + jnp.tanh(c * (x + 0.044715 * (x * x * x))))


def _dot(a, b):
    return jnp.dot(a, b, preferred_element_type=F32)


def _dot_nt(a, b):
    return lax.dot_general(a, b, (((1,), (1,)), ((), ())), preferred_element_type=F32)


def _dot_tn(a, b):
    return lax.dot_general(a, b, (((0,), (0,)), ((), ())), preferred_element_type=F32)


def _dot_f32(a, b):
    return jnp.dot(a, b, preferred_element_type=F32, precision=lax.Precision.HIGHEST)


def _layer_norm(v, g, b):
    mu = jnp.mean(v, axis=-1, keepdims=True)
    vc = v - mu
    var = jnp.mean(vc * vc, axis=-1, keepdims=True)
    return vc * lax.rsqrt(var + LN_EPS) * g + b


def _matmul_kernel(x_ref, w_ref, o_ref):
    o_ref[...] = _dot(x_ref[...].astype(BF16), w_ref[...]).astype(o_ref.dtype)


def _in_proj(xb, w, layer):
    T, K = xb.shape
    N = w.shape[2]
    tm = min(1024, T)
    tn = PROJ_TN
    return pl.pallas_call(
        _matmul_kernel,
        grid=(N // tn, T // tm),
        in_specs=[pl.BlockSpec((tm, K), lambda j, i: (i, 0)),
                  pl.BlockSpec((None, K, tn), lambda j, i: (layer, 0, j))],
        out_specs=pl.BlockSpec((tm, tn), lambda j, i: (i, j)),
        out_shape=jax.ShapeDtypeStruct((T, N), BF16),
        compiler_params=_cparams(("parallel", "parallel")),
        name="in_proj",
    )(xb, w)


def _split_w_in(w):
    offs = [0]
    for s in IN_SIZES:
        offs.append(offs[-1] + s)
    return [w[..., offs[k]:offs[k + 1]] for k in range(len(IN_SIZES))]


def _prep_w_in(w):
    pc = _split_w_in(w)
    lead = w.shape[:-1]
    q = pc[0].reshape(*lead, NSA_KV_GROUPS, NSA_HPG, NSA_HEAD_DIM) * (NSA_HEAD_DIM ** -0.5 * LOG2E)
    zeros = jnp.zeros_like(q)
    q_ext = jnp.stack([jnp.concatenate([q[..., 0, :, :], zeros[..., 0, :, :]], axis=-1),
                       jnp.concatenate([zeros[..., 1, :, :], q[..., 1, :, :]], axis=-1)], axis=-3)
    q_ext = q_ext.reshape(*lead, NSA_HEADS * HEAD_SLOT)
    small = jnp.pad(pc[7], [(0, 0)] * len(lead) + [(0, C_XBC - C_SMALL - GATE_W)])
    out = jnp.concatenate([pc[13], q_ext, pc[8], pc[11], pc[12], pc[3], pc[5], small, pc[9]], axis=-1)
    assert out.shape[-1] == PROJ_W
    return out.astype(BF16)


VT_ROWS = NSA_HEAD_DIM + 16


def _aux_proj_kernel(x_ref, wc_ref, wvt_ref, wdt_ref, wdtt_ref, cmp_ref, vst_ref, vwt_ref, dtc_ref, dtr_ref):
    x = x_ref[...].astype(BF16)
    tm = x.shape[0]
    cmp_ref[...] = _dot(x, wc_ref[...]).astype(cmp_ref.dtype)
    vt = _dot_nt(wvt_ref[...], x)
    dv = NSA_HEAD_DIM
    ones = jnp.ones((VT_ROWS - dv, LANES), vst_ref.dtype)
    for u in range(tm // LANES):
        for k, o_ref in enumerate((vst_ref, vwt_ref)):
            for g in range(NSA_KV_GROUPS):
                r0 = (k * NSA_KV_GROUPS + g) * dv
                o_ref[g, u, 0:dv, :] = vt[r0:r0 + dv, u * LANES:(u + 1) * LANES].astype(o_ref.dtype)
                o_ref[g, u, dv:VT_ROWS, :] = ones
    xf = x.astype(F32)
    dtc_ref[...] = _dot(xf, wdt_ref[...])
    dtr_ref[...] = _dot_nt(wdtt_ref[...], xf)


def _aux_weights(w):
    pc = _split_w_in(w)
    wc = jnp.concatenate([pc[1], pc[2]], axis=-1).astype(BF16)
    wvt = jnp.swapaxes(jnp.concatenate([pc[4], pc[6]], axis=-1), -1, -2).astype(BF16)
    return wc, wvt, pc[10], jnp.swapaxes(pc[10], -1, -2)


def _aux_proj(xb, aux_w, layer):
    T, K = xb.shape
    wc, wvt, wdt, wdtt = aux_w
    tm = min(1024, T)
    full = lambda a: pl.BlockSpec((None,) + a.shape[1:], lambda i: (layer, 0, 0))
    vt_shape = jax.ShapeDtypeStruct((NSA_KV_GROUPS, T // LANES, VT_ROWS, LANES), BF16)
    vt_spec = pl.BlockSpec((NSA_KV_GROUPS, tm // LANES, VT_ROWS, LANES), lambda i: (0, i, 0, 0))
    return pl.pallas_call(
        _aux_proj_kernel,
        grid=(T // tm,),
        in_specs=[pl.BlockSpec((tm, K), lambda i: (i, 0)), full(wc), full(wvt), full(wdt), full(wdtt)],
        out_specs=[pl.BlockSpec((tm, 2 * NSA_KV_W), lambda i: (i, 0)), vt_spec, vt_spec,
                   pl.BlockSpec((tm, SSD_HEADS), lambda i: (i, 0)),
                   pl.BlockSpec((SSD_HEADS, tm), lambda i: (0, i))],
        out_shape=[jax.ShapeDtypeStruct((T, 2 * NSA_KV_W), BF16), vt_shape, vt_shape,
                   jax.ShapeDtypeStruct((T, SSD_HEADS), F32), jax.ShapeDtypeStruct((SSD_HEADS, T), F32)],
        compiler_params=_cparams(("parallel",)),
        name="aux_proj",
    )(xb, wc, wvt, wdt, wdtt)


CMP_PIECES = 2 * NSA_KV_GROUPS


def _compress_kernel(r_ref, pe_ref, w1_ref, w2k_ref, w2vt_ref, kc_ref, vct_ref):
    r = r_ref[0]
    nr = r.shape[0]
    u = _dot(r, w1_ref[0])
    v = _dot(r, w1_ref[1])
    c = _dot(pe_ref[0], w1_ref[0]) + _dot(pe_ref[1], w1_ref[1])
    hid = u + pltpu.roll(v, nr - 1, axis=0)
    hid = (hid.reshape(nr // 8, 8, hid.shape[1]) + c[None]).reshape(nr, hid.shape[1])
    act = _gelu_tanh(hid).astype(BF16)
    for g in range(NSA_KV_GROUPS):
        kp, vp = g, NSA_KV_GROUPS + g
        kc_ref[0, g] = _dot(act[:, kp * CMP_HIDDEN:(kp + 1) * CMP_HIDDEN], w2k_ref[g]).astype(kc_ref.dtype)
        vct = _dot_nt(w2vt_ref[...], act[:, vp * CMP_HIDDEN:(vp + 1) * CMP_HIDDEN])
        vct_ref[0, g] = vct.astype(vct_ref.dtype)


def _cmp_first_layer(w1k, w1v, pek, pev):
    half = CMP_BLOCK // 2
    blocks, pes = [], []
    for p, (w1, pe1) in enumerate(((w1k, pek), (w1k, pek), (w1v, pev), (w1v, pev))):
        wp = w1.reshape(2, half, NSA_HEAD_DIM, CMP_HIDDEN).astype(BF16)
        blocks.append(jnp.pad(wp, ((0, 0), (0, 0), (0, 0), (p * CMP_HIDDEN, (CMP_PIECES - 1 - p) * CMP_HIDDEN))))
        pes.append(pe1.reshape(2, half, NSA_HEAD_DIM).astype(BF16))
    rows = half * CMP_PIECES * NSA_HEAD_DIM
    big = jnp.concatenate(blocks, axis=2).reshape(2, rows, CMP_PIECES * CMP_HIDDEN)
    pe = jnp.concatenate(pes, axis=2).reshape(2, 1, rows)
    return big, jnp.broadcast_to(pe, (2, 8, rows))


def _nsa_compress(cmp, w1big, pebig, w2k, w2vt, B, S):
    NR = S // CMP_STRIDE
    W = CMP_STRIDE * 2 * NSA_KV_W
    r = cmp.reshape(B, NR, W)
    once = pl.Buffered(1)
    full = lambda a: pl.BlockSpec(a.shape, lambda b: (0,) * a.ndim, pipeline_mode=once)
    return pl.pallas_call(
        _compress_kernel,
        grid=(B,),
        in_specs=[pl.BlockSpec((1, NR, W), lambda b: (b, 0, 0)), full(pebig), full(w1big), full(w2k), full(w2vt)],
        out_specs=[pl.BlockSpec((1, NSA_KV_GROUPS, NR, HEAD_SLOT), lambda b: (b, 0, 0, 0)),
                   pl.BlockSpec((1, NSA_KV_GROUPS, NSA_HEAD_DIM, NR), lambda b: (b, 0, 0, 0))],
        out_shape=[jax.ShapeDtypeStruct((B, NSA_KV_GROUPS, NR, HEAD_SLOT), BF16),
                   jax.ShapeDtypeStruct((B, NSA_KV_GROUPS, NSA_HEAD_DIM, NR), BF16)],
        compiler_params=_cparams(("parallel",)),
        name="nsa_compress",
    )(r, pebig, w1big, w2k, w2vt)


def _stack_heads(q):
    return jnp.concatenate([q[:, j * HEAD_SLOT:(j + 1) * HEAD_SLOT] for j in range(NSA_HPG)], axis=0)


CMP_VARIANTS = 4


def _cmp_attn_kernel(q_ref, kc_ref, vct_ref, ovt_ref, gate_ref, ocmpt_ref, sel_ref, gt_ref, imp_ref, *, tq):
    g = pl.program_id(1)
    i = pl.program_id(2)
    q2 = _stack_heads(q_ref[...])
    nc_all = kc_ref.shape[2]
    gt_ref[...] = _sigmoid(gate_ref[...].astype(F32)).T

    def attend(nc):
        kc = kc_ref[0, 0, 0:nc, :]
        vct = vct_ref[0, 0, :, 0:nc]
        sts = [_dot_nt(kc, q2[j * tq:(j + 1) * tq]) for j in range(NSA_HPG)]
        n = lax.broadcasted_iota(jnp.int32, (nc, tq), 0)
        t = i * tq + lax.broadcasted_iota(jnp.int32, (nc, tq), 1)
        mask = n * CMP_STRIDE + (CMP_BLOCK - 1) <= t
        ps = None
        for j in range(NSA_HPG):
            s = jnp.where(mask, sts[j], MASK_VALUE)
            m = jnp.max(s, axis=0, keepdims=True)
            e = jnp.where(mask, jnp.exp2(s - m), 0.0)
            den = jnp.maximum(jnp.sum(e, axis=0, keepdims=True), 1e-30)
            p = e * (1.0 / den)
            ps = p if ps is None else ps + p
            gate = gt_ref[pl.ds(g * (NSA_HPG * 3) + j * 3, 1), :]
            ocmpt_ref[0, 0, 0, :, j * tq:(j + 1) * tq] = (gate * _dot(vct, p.astype(BF16))).astype(ocmpt_ref.dtype)
        ovt = ovt_ref[:, 0:nc]
        hi = ps.astype(BF16)
        r1 = ps - hi.astype(F32)
        mid = r1.astype(BF16)
        lo = (r1 - mid.astype(F32)).astype(BF16)
        imp_ref[...] = _dot(ovt, hi) + _dot(ovt, mid) + _dot(ovt, lo)

    step = nc_all // CMP_VARIANTS
    variant = jnp.minimum(((i + 1) * (tq // CMP_STRIDE) - 1) // step, CMP_VARIANTS - 1)
    for v in range(CMP_VARIANTS):
        pl.when(variant == v)(functools.partial(attend, (v + 1) * step))

    imp = imp_ref[...]
    blk = lax.broadcasted_iota(jnp.int32, imp.shape, 0)
    tt = i * tq + lax.broadcasted_iota(jnp.int32, imp.shape, 1)
    cur = tt // SEL_BLOCK
    forced = (blk == 0) | (blk == cur) | (blk == cur - 1)
    causal = blk * SEL_BLOCK <= tt
    v0 = jnp.where(forced, FORCE_SCORE, jnp.where(causal, imp, -1.0))

    blk1 = blk[:, 0:LANES]

    def pick(_, tiles):
        out = []
        for v in tiles:
            mx = jnp.max(v, axis=0, keepdims=True)
            idx = jnp.min(jnp.where(v == mx, blk1, SEL_LANES), axis=0, keepdims=True)
            out.append(jnp.where(blk1 == idx, -jnp.inf, v))
        return tuple(out)

    tiles = lax.fori_loop(0, SEL_TOPN, pick, tuple(v0[:, u * LANES:(u + 1) * LANES] for u in range(tq // LANES)))
    for u, v in enumerate(tiles):
        sel_ref[0, 0, u * LANES:(u + 1) * LANES, :] = jnp.where(v == -jnp.inf, 1.0, 0.0).T.astype(sel_ref.dtype)


def _nsa_cmp_attn(proj, kc, vct, ovt, B, S):
    tq = NSA_TQ
    nq = S // tq
    G = NSA_KV_GROUPS
    NC = kc.shape[2]
    qw = NSA_HPG * HEAD_SLOT
    qblk = C_QEXT // qw
    kern = functools.partial(_cmp_attn_kernel, tq=tq)
    return pl.pallas_call(
        kern,
        grid=(B, G, nq),
        in_specs=[pl.BlockSpec((tq, qw), lambda b, g, i: (b * nq + i, qblk + g)),
                  pl.BlockSpec((1, 1, NC, HEAD_SLOT), lambda b, g, i: (b, g, 0, 0)),
                  pl.BlockSpec((1, 1, NSA_HEAD_DIM, NC), lambda b, g, i: (b, g, 0, 0)),
                  pl.BlockSpec(ovt.shape, lambda b, g, i: (0, 0)),
                  pl.BlockSpec((tq, LANES), lambda b, g, i: (b * nq + i, C_SMALL // LANES))],
        out_specs=[pl.BlockSpec((1, 1, 1, NSA_HEAD_DIM, NSA_HPG * tq), lambda b, g, i: (b, g, i, 0, 0)),
                   pl.BlockSpec((1, 1, tq, SEL_LANES), lambda b, g, i: (b, g, i, 0))],
        out_shape=[jax.ShapeDtypeStruct((B, G, nq, NSA_HEAD_DIM, NSA_HPG * tq), BF16),
                   jax.ShapeDtypeStruct((B, G, S, SEL_LANES), BF16)],
        scratch_shapes=[pltpu.VMEM((LANES, tq), F32), pltpu.VMEM((SEL_LANES, tq), F32)],
        compiler_params=_cparams(("parallel", "parallel", "parallel")),
        name="nsa_cmp_attn",
    )(proj, kc, vct, ovt, proj)


SEL_TK = 512
NSA_TQ = 512
WIN_KEYS = WINDOW + NSA_TQ


HALF = NSA_TQ
NCH = NSA_HPG * NSA_TQ // HALF


def _sel_win_kernel(q_ref, ksel_ref, vselt_ref, kwin_ref, vwint_ref, sel_ref, et_ref, ocmpt_ref,
                    gate_ref, dbias_ref, wbias_ref, o_ref, s_ref, m_ref, acc_ref, gt_ref):
    g = pl.program_id(1)
    i = pl.program_id(2)
    tq = NSA_TQ
    t0 = i * tq
    q2 = _stack_heads(q_ref[...])
    selm1 = sel_ref[0, 0] - 1.0
    qx = jnp.concatenate([q2, jnp.concatenate([selm1] * NSA_HPG, axis=0)], axis=1)
    qxh = [qx[c * HALF:(c + 1) * HALF] for c in range(NCH)]
    q2h = [q2[c * HALF:(c + 1) * HALF] for c in range(NCH)]
    kpb = SEL_TK // LANES
    dv = NSA_HEAD_DIM

    wblk = jnp.maximum((t0 - WINDOW) // LANES, 0)
    start = pl.multiple_of(wblk * LANES, LANES)
    kw = kwin_ref[pl.ds(start, WIN_KEYS), :]
    vwt = jnp.concatenate([vwint_ref[wblk + u] for u in range(WIN_KEYS // LANES)], axis=1)
    wbias = wbias_ref[jnp.minimum(i, WINDOW // tq)].astype(F32)
    sw = _dot_nt(kw, q2)

    def scores_into(kv):
        off = pl.multiple_of(kv * SEL_TK, SEL_TK)
        kx = jnp.concatenate([ksel_ref[pl.ds(off, SEL_TK), :], et_ref[pl.ds(off, SEL_TK), :]], axis=1)
        for c in range(NCH):
            s_ref[:, c * HALF:(c + 1) * HALF] = _dot_nt(kx, qxh[c])

    def update(vt, s_chunks):
        for c in range(NCH):
            cols = slice(c * HALF, (c + 1) * HALF)
            s = s_chunks[c]
            m_old = m_ref[:, cols]
            m_new = jnp.maximum(m_old, jnp.max(s, axis=0, keepdims=True))
            p = jnp.exp2(s - m_new).astype(BF16)
            acc_ref[:, cols] = jnp.exp2(m_old - m_new) * acc_ref[:, cols] + _dot(vt, p)
            m_ref[:, cols] = m_new

    def sel_values(kv):
        return jnp.concatenate([vselt_ref[kv * kpb + u] for u in range(kpb)], axis=1)

    def load_scores():
        return [s_ref[:, c * HALF:(c + 1) * HALF] for c in range(NCH)]

    m_ref[...] = jnp.full(m_ref.shape, MASK_VALUE, F32)
    acc_ref[...] = jnp.zeros(acc_ref.shape, F32)
    kd = t0 // SEL_TK
    scores_into(0)
    sw = sw + jnp.concatenate([wbias] * NSA_HPG, axis=1)
    mw = jnp.max(sw, axis=0, keepdims=True)
    accw = _dot(vwt, jnp.exp2(sw - mw).astype(BF16))
    ot_win = accw[0:dv] * (1.0 / jnp.maximum(accw[dv:dv + 1], 1e-30))

    def body(kv, carry):
        s_chunks = load_scores()
        scores_into(kv + 1)
        update(sel_values(kv), s_chunks)
        return carry

    lax.fori_loop(0, kd, body, 0)
    dbias = dbias_ref[i % (SEL_TK // tq)].astype(F32)
    update(sel_values(kd), [sc + dbias for sc in load_scores()])
    acc = acc_ref[...]
    ot_slc = acc[0:dv] * (1.0 / jnp.maximum(acc[dv:dv + 1], 1e-30))

    gt_ref[...] = _sigmoid(gate_ref[...].astype(F32)).T
    ots = []
    for j in range(NSA_HPG):
        base = g * (NSA_HPG * 3) + j * 3
        ots.append(ocmpt_ref[0, 0, 0, :, j * tq:(j + 1) * tq].astype(F32)
                   + gt_ref[pl.ds(base + 1, 1), :] * ot_slc[:, j * tq:(j + 1) * tq]
                   + gt_ref[pl.ds(base + 2, 1), :] * ot_win[:, j * tq:(j + 1) * tq])
    for jp in range(NSA_HPG // 2):
        pair = jnp.concatenate([ots[2 * jp], ots[2 * jp + 1]], axis=0)
        o_ref[:, jp * LANES:(jp + 1) * LANES] = pair.T.astype(o_ref.dtype)


def _diag_bias():
    r = jnp.arange(SEL_TK)[None, :, None]
    c = jnp.arange(NSA_TQ)[None, None, :]
    off = (jnp.arange(SEL_TK // NSA_TQ) * NSA_TQ)[:, None, None]
    return jnp.where(r <= off + c, 0.0, MASK_VALUE).astype(BF16)


def _window_bias():
    r = jnp.arange(WIN_KEYS)[None, :, None]
    c = jnp.arange(NSA_TQ)[None, None, :]
    off = jnp.minimum(jnp.arange(WINDOW // NSA_TQ + 1) * NSA_TQ, WINDOW)[:, None, None]
    diff = off + c - r
    return jnp.where((diff >= 0) & (diff < WINDOW), 0.0, MASK_VALUE).astype(BF16)


def _nsa_sel_win(proj, vsel_t, vwin_t, sel, et, ocmpt, B, S):
    T = B * S
    tq = NSA_TQ
    assert HALF == tq
    dbias = _diag_bias()
    wbias = _window_bias()
    nq = S // tq
    G = NSA_KV_GROUPS
    qw = NSA_HPG * HEAD_SLOT
    qblk = C_QEXT // qw
    vsel_t = vsel_t.reshape(G, B, S // LANES, VT_ROWS, LANES)
    vwin_t = vwin_t.reshape(G, B, S // LANES, VT_ROWS, LANES)
    ow = NSA_HPG * NSA_HEAD_DIM
    kv_spec = lambda c: pl.BlockSpec((S, LANES), lambda b, g, i: (b, c // LANES))
    vt_spec = pl.BlockSpec((None, None, S // LANES, VT_ROWS, LANES), lambda b, g, i: (g, b, 0, 0, 0))
    return pl.pallas_call(
        _sel_win_kernel,
        grid=(B, G, nq),
        in_specs=[pl.BlockSpec((tq, qw), lambda b, g, i: (b * nq + i, qblk + g)),
                  kv_spec(C_KSEL), vt_spec, kv_spec(C_KWIN), vt_spec,
                  pl.BlockSpec((1, 1, tq, SEL_LANES), lambda b, g, i: (b, g, i, 0)),
                  pl.BlockSpec(et.shape, lambda b, g, i: (0, 0)),
                  pl.BlockSpec((1, 1, 1, NSA_HEAD_DIM, NSA_HPG * tq), lambda b, g, i: (b, g, i, 0, 0)),
                  pl.BlockSpec((tq, LANES), lambda b, g, i: (b * nq + i, C_SMALL // LANES)),
                  pl.BlockSpec(dbias.shape, lambda b, g, i: (0, 0, 0)),
                  pl.BlockSpec(wbias.shape, lambda b, g, i: (0, 0, 0))],
        out_specs=pl.BlockSpec((tq, ow), lambda b, g, i: (b * nq + i, g)),
        out_shape=jax.ShapeDtypeStruct((T, NSA_Q_W), BF16),
        scratch_shapes=[pltpu.VMEM((SEL_TK, NSA_HPG * tq), F32),
                        pltpu.VMEM((1, NSA_HPG * tq), F32),
                        pltpu.VMEM((VT_ROWS, NSA_HPG * tq), F32),
                        pltpu.VMEM((LANES, tq), F32)],
        compiler_params=_cparams(("parallel", "parallel", "parallel")),
        name="nsa_sel_win",
    )(proj, proj, vsel_t, proj, vwin_t, sel, et, ocmpt, proj, dbias, wbias)


TAIL_ROWS = 8


def _causal_conv(x, tail_ref, w, b):
    L, C = x.shape
    nv = L // TAIL_ROWS
    xx = jnp.concatenate([tail_ref[...], x], axis=0).reshape(nv + 1, TAIL_ROWS, C)
    sub = lax.broadcasted_iota(jnp.int32, (nv, TAIL_ROWS, C), 1)
    y = b + w[CONV_WIDTH - 1:CONV_WIDTH] * x
    for k in range(1, CONV_WIDTH):
        rot = pltpu.roll(xx, k, axis=1)
        shifted = jnp.where(sub >= k, rot[1:], rot[:-1]).reshape(L, C)
        y = y + w[CONV_WIDTH - 1 - k:CONV_WIDTH - k] * shifted
    tail_ref[...] = x[L - TAIL_ROWS:L]
    return y


SSD_CPS = 2


def _ssd_kernel(z_ref, xbc_ref, dtc_ref, dtr_ref, cw_ref, cb_ref, dtbc_ref, dtbr_ref, alc_ref, alr_ref,
                dsk_ref, nw_ref, o_ref, state_ref, tail_ref, y_ref):
    c = pl.program_id(1)
    L = SSD_CHUNK
    P = SSD_HEAD_DIM
    N = SSD_STATE
    hpg = SSD_HEADS // SSD_GROUPS

    @pl.when(c == 0)
    def _():
        state_ref[...] = jnp.zeros_like(state_ref)
        tail_ref[...] = jnp.zeros_like(tail_ref)

    conv = _causal_conv(xbc_ref[...].astype(F32), tail_ref, cw_ref[...], cb_ref[...])
    xbc = conv * _sigmoid(conv)
    dt_c = _softplus(dtc_ref[...] + dtbc_ref[...])
    dt_r = _softplus(dtr_ref[...] + dtbr_ref[...])
    a_c = dt_c * (-jnp.exp(alc_ref[...]))
    a_r = dt_r * (-jnp.exp(alr_ref[...]))
    ii = lax.broadcasted_iota(jnp.int32, (L, L), 0)
    jj = lax.broadcasted_iota(jnp.int32, (L, L), 1)
    tri = ii >= jj
    tril = tri.astype(F32)
    triu = (ii <= jj).astype(F32)

    for cc in range(SSD_CPS):
        rs = slice(cc * L, (cc + 1) * L)
        xs = xbc[rs, 0:SSD_INNER]
        bm = xbc[rs, SSD_INNER:SSD_INNER + SSD_GROUPS * N]
        cm = xbc[rs, SSD_INNER + SSD_GROUPS * N:SSD_INNER + 2 * SSD_GROUPS * N]
        acum_c = _dot_f32(tril, a_c[rs])
        acum_r = _dot_f32(a_r[:, rs], triu)
        for g in range(SSD_GROUPS):
            bg = bm[:, g * N:(g + 1) * N]
            cgb = cm[:, g * N:(g + 1) * N].astype(BF16)
            cb = _dot_nt(cgb, bg.astype(BF16))
            for j in range(hpg):
                h = g * hpg + j
                acb = jnp.broadcast_to(acum_c[:, h:h + 1], (L, L))
                a_last = acum_c[L - 1:L, h:h + 1]
                xh_raw = xs[:, h * P:(h + 1) * P]
                xh = (xh_raw * jnp.broadcast_to(dt_c[rs, h:h + 1], (L, P))).astype(BF16)
                lmat = jnp.where(tri, jnp.exp(acb - acum_r[h:h + 1, :]), 0.0)
                y = _dot((cb * lmat).astype(BF16), xh)
                prev = state_ref[h]
                y = y + _dot(cgb, prev.astype(BF16)) * jnp.exp(acb[:, 0:P])
                bd = (bg * jnp.exp(a_last - acb[:, 0:N])).astype(BF16)
                state_ref[h] = jnp.exp(a_last) * prev + _dot_tn(bd, xh)
                y_ref[rs, h * P:(h + 1) * P] = y + dsk_ref[:, h * P:(h + 1) * P] * xh_raw

    zf = z_ref[...].astype(F32)
    yg = y_ref[...] * (zf * _sigmoid(zf))
    ms = jnp.mean(yg * yg, axis=-1, keepdims=True)
    o_ref[...] = (yg * lax.rsqrt(ms + RMS_EPS) * nw_ref[...]).astype(o_ref.dtype)


def _ssd_mixer(proj, dt_col, dt_row, conv_w, conv_b, dt_bias, a_log, d_skip, norm_w, B, S):
    T = B * S
    L = SSD_CPS * SSD_CHUNK
    nc = S // L
    H = SSD_HEADS
    full = lambda a: pl.BlockSpec(a.shape, lambda b, c: (0, 0))
    cb2 = conv_b.reshape(1, -1)
    dtb_c = dt_bias.reshape(1, H)
    dtb_r = dt_bias.reshape(H, 1)
    al_c = a_log.reshape(1, H)
    al_r = a_log.reshape(H, 1)
    dsk = jnp.repeat(d_skip, SSD_HEAD_DIM).reshape(1, SSD_INNER)
    nw = norm_w.reshape(1, SSD_INNER)
    return pl.pallas_call(
        _ssd_kernel,
        grid=(B, nc),
        in_specs=[pl.BlockSpec((L, SSD_INNER), lambda b, c: (b * nc + c, C_SSDZ // SSD_INNER)),
                  pl.BlockSpec((L, SSD_XBC_W), lambda b, c: (b * nc + c, C_XBC // SSD_XBC_W)),
                  pl.BlockSpec((L, H), lambda b, c: (b * nc + c, 0)),
                  pl.BlockSpec((H, L), lambda b, c: (0, b * nc + c)),
                  full(conv_w), full(cb2), full(dtb_c), full(dtb_r), full(al_c), full(al_r),
                  full(dsk), full(nw)],
        out_specs=pl.BlockSpec((L, SSD_INNER), lambda b, c: (b * nc + c, 0)),
        out_shape=jax.ShapeDtypeStruct((T, SSD_INNER), BF16),
        scratch_shapes=[pltpu.VMEM((H, SSD_STATE, SSD_HEAD_DIM), F32),
                        pltpu.VMEM((TAIL_ROWS, SSD_XBC_W), F32),
                        pltpu.VMEM((L, SSD_INNER), F32)],
        compiler_params=_cparams(("parallel", "arbitrary")),
        name="ssd_mixer",
    )(proj, proj, dt_col, dt_row, conv_w, cb2, dtb_c, dtb_r, al_c, al_r, dsk, nw)


SCAN_GROUP = 8


def _lru_kernel(x_ref, y_ref, cw_ref, cb_ref, wa_ref, ba_ref, wx_ref, bx_ref, lam_ref, o_ref,
                h_ref, tail_ref, *, tc):
    c = pl.program_id(1)

    @pl.when(c == 0)
    def _():
        h_ref[...] = jnp.zeros_like(h_ref)
        tail_ref[...] = jnp.zeros_like(tail_ref)

    xr = _causal_conv(x_ref[...].astype(F32), tail_ref, cw_ref[...], cb_ref[...])
    xrb = xr.astype(BF16)
    r = _sigmoid(_dot(xrb, wa_ref[...]) + ba_ref[...])
    ig = _sigmoid(_dot(xrb, wx_ref[...]) + bx_ref[...])
    log_a = -LRU_C * r * _softplus(-lam_ref[...])
    a = jnp.exp(log_a)
    b = jnp.sqrt(1.0 - jnp.exp(2.0 * log_a)) * (ig * xr)
    ng = tc // SCAN_GROUP
    a = a.reshape(ng, SCAN_GROUP, a.shape[1])
    b = b.reshape(ng, SCAN_GROUP, b.shape[1])
    sub = lax.broadcasted_iota(jnp.int32, a.shape, 1)
    k = 1
    while k < SCAN_GROUP:
        keep = sub >= k
        a_s = jnp.where(keep, pltpu.roll(a, k, axis=1), 1.0)
        b_s = jnp.where(keep, pltpu.roll(b, k, axis=1), 0.0)
        b = a * b_s + b
        a = a * a_s
        k *= 2
    carry = h_ref[0:1, :]
    groups = []
    for rg in range(ng):
        hg = a[rg] * carry + b[rg]
        groups.append(hg)
        carry = hg[SCAN_GROUP - 1:SCAN_GROUP]
    h = jnp.concatenate(groups, axis=0)
    h_ref[...] = jnp.broadcast_to(carry, h_ref.shape)
    o_ref[...] = (h * _gelu_tanh(y_ref[...].astype(F32))).astype(o_ref.dtype)


def _block_diag(w):
    nb, c, d = w.shape
    eye = jnp.eye(nb, dtype=w.dtype)
    return (eye[:, None, :, None] * w[:, :, None, :]).reshape(nb * c, nb * d)


def _lru_mixer(proj, conv_w, conv_b, wa, ba, wx, bx, lam, B, S):
    T = B * S
    tc = min(256, S)
    nt = S // tc
    W = LRU_WIDTH
    wa_bd = _block_diag(wa).astype(BF16)
    wx_bd = _block_diag(wx).astype(BF16)
    row = lambda v: v.reshape(1, W)
    full = lambda a: pl.BlockSpec(a.shape, lambda b, c: (0, 0))
    args = (conv_w, row(conv_b), wa_bd, row(ba), wx_bd, row(bx), row(lam))
    return pl.pallas_call(
        functools.partial(_lru_kernel, tc=tc),
        grid=(B, nt),
        in_specs=[pl.BlockSpec((tc, W), lambda b, c: (b * nt + c, C_LRUX // W)),
                  pl.BlockSpec((tc, W), lambda b, c: (b * nt + c, C_LRUY // W))]
                 + [full(a) for a in args],
        out_specs=pl.BlockSpec((tc, W), lambda b, c: (b * nt + c, 0)),
        out_shape=jax.ShapeDtypeStruct((T, W), BF16),
        scratch_shapes=[pltpu.VMEM((TAIL_ROWS, W), F32), pltpu.VMEM((TAIL_ROWS, W), F32)],
        compiler_params=_cparams(("parallel", "arbitrary")),
        name="lru_mixer",
    )(proj, proj, *args)


def _merge_kernel(x_ref, mg_ref, on_ref, os_ref, ol_ref, pn_ref, ps_ref, pl_ref, wo_ref, g_ref, b_ref,
                  of_ref, ob_ref):
    d = D_MODEL
    gate = _sigmoid(mg_ref[...].astype(F32))
    mixed = (gate[:, 0:d] * _dot(on_ref[...], pn_ref[...])
             + gate[:, d:2 * d] * _dot(os_ref[...], ps_ref[...])
             + gate[:, 2 * d:3 * d] * _dot(ol_ref[...], pl_ref[...]))
    v = ALPHA * x_ref[...] + _dot(mixed.astype(BF16), wo_ref[...])
    out = _layer_norm(v, g_ref[...], b_ref[...])
    of_ref[...] = out
    ob_ref[...] = out.astype(BF16)


def _merge(x, proj, o_nsa, o_ssd, o_lru, pn, ps, plru, wo, g, b, layer):
    T = x.shape[0]
    tm = min(512, T)
    d = D_MODEL
    rowblk = lambda w: pl.BlockSpec((tm, w), lambda i: (i, 0))
    full = lambda a: pl.BlockSpec((None,) + a.shape[1:], lambda i: (layer, 0, 0))
    g2, b2 = g.reshape(-1, 1, d), b.reshape(-1, 1, d)
    return pl.pallas_call(
        _merge_kernel,
        grid=(T // tm,),
        in_specs=[rowblk(d), pl.BlockSpec((tm, 3 * d), lambda i: (i, C_MERGE // (3 * d))),
                  rowblk(o_nsa.shape[1]), rowblk(o_ssd.shape[1]), rowblk(o_lru.shape[1]),
                  full(pn), full(ps), full(plru), full(wo), full(g2), full(b2)],
        out_specs=[rowblk(d), rowblk(d)],
        out_shape=[jax.ShapeDtypeStruct((T, d), F32), jax.ShapeDtypeStruct((T, d), BF16)],
        compiler_params=_cparams(("parallel",)),
        name="merge_ln",
    )(x, proj, o_nsa, o_ssd, o_lru, pn, ps, plru, wo, g2, b2)


def _route(sel, aff):
    epg = EXPERTS_PER_GROUP
    scores = []
    for gi in range(N_EXPERT_GROUPS):
        v = sel[gi * epg:(gi + 1) * epg]
        pair = None
        for a in range(epg):
            for b in range(a + 1, epg):
                sab = v[a] + v[b]
                pair = sab if pair is None else jnp.maximum(pair, sab)
        scores.append(pair)
    best = jnp.zeros_like(scores[0], dtype=jnp.int32)
    best_s = scores[0]
    for gi in range(1, N_EXPERT_GROUPS):
        better = scores[gi] > best_s
        best = jnp.where(better, gi, best)
        best_s = jnp.where(better, scores[gi], best_s)
    chosen = []
    for k in range(N_EXPERTS):
        gi = k // epg
        rank = jnp.zeros_like(best)
        for o in range(gi * epg, (gi + 1) * epg):
            if o == k:
                continue
            ahead = (sel[o] > sel[k]) | ((sel[o] == sel[k]) & (o < k))
            rank = rank + ahead.astype(jnp.int32)
        chosen.append((best == gi) & (rank < TOP_K))
    wsum = None
    for k in range(N_EXPERTS):
        wk = jnp.where(chosen[k], aff[k], 0.0)
        wsum = wk if wsum is None else wsum + wk
    inv = 1.0 / wsum
    return [jnp.where(chosen[k], aff[k], 0.0) * inv for k in range(N_EXPERTS)]


EXPERT_PAIRS = N_EXPERTS // 2
PAIRS_PER_ITER = 2


def _moe_kernel(xb_ref, xf_ref, p_ref, rw_ref, rb_ref, pg_ref, pp_ref, wg_ref, wu_ref, wd_ref,
                g_ref, b_ref, of_ref, ob_ref, acc_ref, gates_ref):
    xb = xb_ref[...]
    tm = xb.shape[0]
    logits = _dot_nt(rw_ref[...], xb)
    aff = _sigmoid(logits)
    sel = aff + rb_ref[...]
    gate_rows = _route([sel[k:k + 1, :] for k in range(N_EXPERTS)],
                       [aff[k:k + 1, :] for k in range(N_EXPERTS)])
    gt = jnp.concatenate(gate_rows + [jnp.zeros((LANES - N_EXPERTS, tm), F32)], axis=0)
    gates_ref[...] = gt.T
    acc_ref[...] = _sigmoid(_dot(xb, pg_ref[...])) * _dot(p_ref[...].astype(BF16), pp_ref[...])
    lane = lax.broadcasted_iota(jnp.int32, (tm, LANES), 1)

    def gate_cols(k):
        gates = gates_ref[...]
        cols = [jnp.broadcast_to(jnp.sum(jnp.where(lane == 2 * k + u, gates, 0.0), axis=-1, keepdims=True),
                                 (tm, D_EXPERT)) for u in range(2)]
        return jnp.concatenate(cols, axis=1)

    def step(it, carry):
        ks = [it * PAIRS_PER_ITER + u for u in range(PAIRS_PER_ITER)]
        hgs = [[_dot(xb, wg_ref[2 * k + u]) for u in range(2)] for k in ks]
        hus = [[_dot(xb, wu_ref[2 * k + u]) for u in range(2)] for k in ks]
        for k, hg2, hu2 in zip(ks, hgs, hus):
            hg = jnp.concatenate(hg2, axis=1)
            h = (hg * _sigmoid(hg)) * jnp.concatenate(hu2, axis=1) * gate_cols(k)
            acc_ref[...] += _dot(h.astype(BF16), wd_ref[k])
        return carry

    lax.fori_loop(0, EXPERT_PAIRS // PAIRS_PER_ITER, step, 0)
    out = _layer_norm(ALPHA * xf_ref[...] + acc_ref[...], g_ref[...], b_ref[...])
    of_ref[...] = out
    ob_ref[...] = out.astype(BF16)


def _moe_ple(xb, xf, p, layer, rw_t, rb, pg, pp, wg, wu, wd, g, b):
    T = xb.shape[0]
    tm = min(512, T)
    d = D_MODEL
    rowblk = lambda w: pl.BlockSpec((tm, w), lambda i: (i, 0))
    once = pl.Buffered(1)
    full = lambda a: pl.BlockSpec(a.shape, lambda i: (0,) * a.ndim, pipeline_mode=once)
    lyr = lambda a: pl.BlockSpec((None,) + a.shape[1:], lambda i: (layer,) + (0,) * (a.ndim - 1), pipeline_mode=once)
    g2, b2 = g.reshape(-1, 1, d), b.reshape(-1, 1, d)
    return pl.pallas_call(
        _moe_kernel,
        grid=(T // tm,),
        in_specs=[rowblk(d), rowblk(d), pl.BlockSpec((None, tm, PLE_DIM), lambda i: (layer, i, 0)),
                  full(rw_t), full(rb), lyr(pg), lyr(pp),
                  lyr(wg), lyr(wu), lyr(wd), lyr(g2), lyr(b2)],
        out_specs=[rowblk(d), rowblk(d)],
        out_shape=[jax.ShapeDtypeStruct((T, d), F32), jax.ShapeDtypeStruct((T, d), BF16)],
        scratch_shapes=[pltpu.VMEM((tm, d), F32), pltpu.VMEM((tm, LANES), F32)],
        compiler_params=_cparams(("parallel",)),
        name="moe_ple_ln",
    )(xb, xf, p, rw_t, rb, pg, pp, wg, wu, wd, g2, b2)


def _overlap_matrix(nc):
    n = jnp.arange(nc)[None, :]
    m = jnp.arange(SEL_LANES)[:, None]
    ratio = SEL_BLOCK // CMP_STRIDE
    ov = jnp.zeros((SEL_LANES, nc), F32)
    for k in range(CMP_BLOCK // CMP_STRIDE):
        ov = ov + ((n + k) // ratio == m).astype(F32)
    return ov.astype(BF16)


def _expand_matrix(S):
    c = jnp.arange(S)[:, None]
    m = jnp.arange(SEL_LANES)[None, :]
    return jnp.where(c // SEL_BLOCK == m, -MASK_VALUE, 0.0).astype(BF16)


def _pad_w2(w2):
    out = jnp.zeros((NSA_KV_GROUPS, CMP_HIDDEN, HEAD_SLOT), F32)
    for g in range(NSA_KV_GROUPS):
        out = out.at[g, :, g * NSA_HEAD_DIM:(g + 1) * NSA_HEAD_DIM].set(w2)
    return out.astype(BF16)


def kernel(x, p, w_in, nsa_pe_k, nsa_w1_k, nsa_w2_k, nsa_pe_v, nsa_w1_v, nsa_w2_v, ssd_conv_w, ssd_conv_b, ssd_dt_bias, ssd_a_log, ssd_d, ssd_norm_w, lru_conv_w, lru_conv_b, lru_wa, lru_ba, lru_wx, lru_bx, lru_lambda, proj_nsa, proj_ssd, proj_lru, w_out, ln1_g, ln1_b, router_w, router_b, exp_w_gate, exp_w_up, exp_w_down, ple_w_gate, ple_w_proj, ln2_g, ln2_b):
    B, S, d = x.shape
    T = B * S
    depth = w_in.shape[0]
    assert d == D_MODEL and S % SEL_TK == 0 and S >= WIN_KEYS and S // SEL_BLOCK <= SEL_LANES
    nr = S // CMP_STRIDE
    ov = _overlap_matrix(nr)
    emat = _expand_matrix(S)
    rw_t = router_w.T.astype(BF16)
    rb = router_b.reshape(N_EXPERTS, 1).astype(F32)

    w_main = _prep_w_in(w_in)
    aux_w = _aux_weights(w_in)
    pn, ps, plru, wo = (a.astype(BF16) for a in (proj_nsa, proj_ssd, proj_lru, w_out))
    pg, pp, wg, wu = (a.astype(BF16) for a in (ple_w_gate, ple_w_proj, exp_w_gate, exp_w_up))
    wd = exp_w_down.reshape(depth, EXPERT_PAIRS, 2 * D_EXPERT, d).astype(BF16)
    p3 = p.reshape(depth, T, PLE_DIM)

    xf = x.reshape(T, d)
    xb = xf
    for i in range(depth):
        proj = _in_proj(xb, w_main, i)
        cmp, vsel_t, vwin_t, dt_col, dt_row = _aux_proj(xb, aux_w, i)

        w1big, pebig = _cmp_first_layer(nsa_w1_k[i], nsa_w1_v[i], nsa_pe_k[i], nsa_pe_v[i])
        kc, vct = _nsa_compress(cmp, w1big, pebig, _pad_w2(nsa_w2_k[i]),
                                nsa_w2_v[i].T.astype(BF16), B, S)
        ocmpt, sel = _nsa_cmp_attn(proj, kc, vct, ov, B, S)
        o_nsa = _nsa_sel_win(proj, vsel_t, vwin_t, sel, emat, ocmpt, B, S)

        o_ssd = _ssd_mixer(proj, dt_col, dt_row, ssd_conv_w[i], ssd_conv_b[i], ssd_dt_bias[i],
                           ssd_a_log[i], ssd_d[i], ssd_norm_w[i], B, S)
        o_lru = _lru_mixer(proj, lru_conv_w[i], lru_conv_b[i], lru_wa[i], lru_ba[i], lru_wx[i],
                           lru_bx[i], lru_lambda[i], B, S)

        xf, xb = _merge(xf, proj, o_nsa, o_ssd, o_lru, pn, ps, plru, wo, ln1_g, ln1_b, i)
        xf, xb = _moe_ple(xb, xf, p3, i, rw_t, rb, pg, pp, wg, wu, wd, ln2_g, ln2_b)
    return xf.reshape(B, S, d)
```

```python
import functools
import math

import jax
import jax.numpy as jnp
from jax import lax
from jax.experimental import pallas as pl
from jax.experimental.pallas import tpu as pltpu

F32 = jnp.float32
BF16 = jnp.bfloat16

D_MODEL = 1024
PLE_DIM = 256
NSA_HEADS = 8
NSA_KV_GROUPS = 2
NSA_HEAD_DIM = 64
NSA_HPG = NSA_HEADS // NSA_KV_GROUPS
NSA_Q_W = NSA_HEADS * NSA_HEAD_DIM
NSA_KV_W = NSA_KV_GROUPS * NSA_HEAD_DIM
CMP_BLOCK = 32
CMP_STRIDE = 16
CMP_HIDDEN = 256
SEL_BLOCK = 64
SEL_TOPN = 16
WINDOW = 512
FORCE_SCORE = 1e4
MASK_VALUE = -1e30
LOG2E = 1.4426950408889634
SSD_HEADS = 8
SSD_HEAD_DIM = 64
SSD_INNER = SSD_HEADS * SSD_HEAD_DIM
SSD_GROUPS = 2
SSD_STATE = 64
SSD_CHUNK = 128
SSD_XBC_W = SSD_INNER + 2 * SSD_GROUPS * SSD_STATE
CONV_WIDTH = 4
LRU_WIDTH = 512
LRU_BLOCKS = 8
LRU_BLOCK_DIM = LRU_WIDTH // LRU_BLOCKS
LRU_C = 8.0
N_EXPERTS = 16
N_EXPERT_GROUPS = 4
EXPERTS_PER_GROUP = N_EXPERTS // N_EXPERT_GROUPS
TOP_K = 2
D_EXPERT = 256
DEPTH = 2
ALPHA = (2 * DEPTH) ** 0.25
LN_EPS = 1e-5
RMS_EPS = 1e-5
IN_SIZES = (NSA_Q_W, NSA_KV_W, NSA_KV_W, NSA_KV_W, NSA_KV_W, NSA_KV_W, NSA_KV_W, NSA_HEADS * 3,
            SSD_INNER, SSD_XBC_W, SSD_HEADS, LRU_WIDTH, LRU_WIDTH, 3 * D_MODEL)

LANES = 128
SEL_LANES = 128
HEAD_SLOT = 128

C_MERGE = 0
C_QEXT = 3072
C_SSDZ = 4096
C_LRUX = 4608
C_LRUY = 5120
C_KSEL = 5632
C_KWIN = 5760
C_SMALL = 5888
C_XBC = 6144
PROJ_W = 6912
PROJ_TN = 2304
GATE_W = NSA_HEADS * 3

VMEM_LIMIT = 56 * 1024 * 1024


def _cparams(sem):
    return pltpu.CompilerParams(dimension_semantics=sem, vmem_limit_bytes=VMEM_LIMIT)


def _sigmoid(x):
    return 1.0 / (1.0 + jnp.exp(-x))


def _softplus(x):
    return jnp.maximum(x, 0.0) + jnp.log(1.0 + jnp.exp(-jnp.abs(x)))


def _gelu_tanh(x):
    c = math.sqrt(2.0 / math.pi)
    return 0.5 * x * (1.0 + jnp.tanh(c * (x + 0.044715 * (x * x * x))))


def _dot(a, b):
    return jnp.dot(a, b, preferred_element_type=F32)


def _dot_nt(a, b):
    return lax.dot_general(a, b, (((1,), (1,)), ((), ())), preferred_element_type=F32)


def _dot_tn(a, b):
    return lax.dot_general(a, b, (((0,), (0,)), ((), ())), preferred_element_type=F32)


def _dot_f32(a, b):
    return jnp.dot(a, b, preferred_element_type=F32, precision=lax.Precision.HIGHEST)


def _layer_norm(v, g, b):
    mu = jnp.mean(v, axis=-1, keepdims=True)
    vc = v - mu
    var = jnp.mean(vc * vc, axis=-1, keepdims=True)
    return vc * lax.rsqrt(var + LN_EPS) * g + b


def _matmul_kernel(x_ref, w_ref, o_ref):
    o_ref[...] = _dot(x_ref[...].astype(BF16), w_ref[...]).astype(o_ref.dtype)


def _in_proj(xb, w, layer):
    T, K = xb.shape
    N = w.shape[2]
    tm = min(1024, T)
    tn = PROJ_TN
    return pl.pallas_call(
        _matmul_kernel,
        grid=(N // tn, T // tm),
        in_specs=[pl.BlockSpec((tm, K), lambda j, i: (i, 0)),
                  pl.BlockSpec((None, K, tn), lambda j, i: (layer, 0, j))],
        out_specs=pl.BlockSpec((tm, tn), lambda j, i: (i, j)),
        out_shape=jax.ShapeDtypeStruct((T, N), BF16),
        compiler_params=_cparams(("parallel", "parallel")),
        name="in_proj",
    )(xb, w)


def _split_w_in(w):
    offs = [0]
    for s in IN_SIZES:
        offs.append(offs[-1] + s)
    return [w[..., offs[k]:offs[k + 1]] for k in range(len(IN_SIZES))]


def _prep_w_in(w):
    pc = _split_w_in(w)
    lead = w.shape[:-1]
    q = pc[0].reshape(*lead, NSA_KV_GROUPS, NSA_HPG, NSA_HEAD_DIM) * (NSA_HEAD_DIM ** -0.5 * LOG2E)
    zeros = jnp.zeros_like(q)
    q_ext = jnp.stack([jnp.concatenate([q[..., 0, :, :], zeros[..., 0, :, :]], axis=-1),
                       jnp.concatenate([zeros[..., 1, :, :], q[..., 1, :, :]], axis=-1)], axis=-3)
    q_ext = q_ext.reshape(*lead, NSA_HEADS * HEAD_SLOT)
    small = jnp.pad(pc[7], [(0, 0)] * len(lead) + [(0, C_XBC - C_SMALL - GATE_W)])
    out = jnp.concatenate([pc[13], q_ext, pc[8], pc[11], pc[12], pc[3], pc[5], small, pc[9]], axis=-1)
    assert out.shape[-1] == PROJ_W
    return out.astype(BF16)


VT_ROWS = NSA_HEAD_DIM + 16


def _aux_proj_kernel(x_ref, wc_ref, wvt_ref, wdt_ref, wdtt_ref, cmp_ref, vst_ref, vwt_ref, dtc_ref, dtr_ref):
    x = x_ref[...].astype(BF16)
    tm = x.shape[0]
    cmp_ref[...] = _dot(x, wc_ref[...]).astype(cmp_ref.dtype)
    vt = _dot_nt(wvt_ref[...], x)
    dv = NSA_HEAD_DIM
    ones = jnp.ones((VT_ROWS - dv, LANES), vst_ref.dtype)
    for u in range(tm // LANES):
        for k, o_ref in enumerate((vst_ref, vwt_ref)):
            for g in range(NSA_KV_GROUPS):
                r0 = (k * NSA_KV_GROUPS + g) * dv
                o_ref[g, u, 0:dv, :] = vt[r0:r0 + dv, u * LANES:(u + 1) * LANES].astype(o_ref.dtype)
                o_ref[g, u, dv:VT_ROWS, :] = ones
    xf = x.astype(F32)
    dtc_ref[...] = _dot(xf, wdt_ref[...])
    dtr_ref[...] = _dot_nt(wdtt_ref[...], xf)


def _aux_weights(w):
    pc = _split_w_in(w)
    wc = jnp.concatenate([pc[1], pc[2]], axis=-1).astype(BF16)
    wvt = jnp.swapaxes(jnp.concatenate([pc[4], pc[6]], axis=-1), -1, -2).astype(BF16)
    return wc, wvt, pc[10], jnp.swapaxes(pc[10], -1, -2)


def _aux_proj(xb, aux_w, layer):
    T, K = xb.shape
    wc, wvt, wdt, wdtt = aux_w
    tm = min(1024, T)
    full = lambda a: pl.BlockSpec((None,) + a.shape[1:], lambda i: (layer, 0, 0))
    vt_shape = jax.ShapeDtypeStruct((NSA_KV_GROUPS, T // LANES, VT_ROWS, LANES), BF16)
    vt_spec = pl.BlockSpec((NSA_KV_GROUPS, tm // LANES, VT_ROWS, LANES), lambda i: (0, i, 0, 0))
    return pl.pallas_call(
        _aux_proj_kernel,
        grid=(T // tm,),
        in_specs=[pl.BlockSpec((tm, K), lambda i: (i, 0)), full(wc), full(wvt), full(wdt), full(wdtt)],
        out_specs=[pl.BlockSpec((tm, 2 * NSA_KV_W), lambda i: (i, 0)), vt_spec, vt_spec,
                   pl.BlockSpec((tm, SSD_HEADS), lambda i: (i, 0)),
                   pl.BlockSpec((SSD_HEADS, tm), lambda i: (0, i))],
        out_shape=[jax.ShapeDtypeStruct((T, 2 * NSA_KV_W), BF16), vt_shape, vt_shape,
                   jax.ShapeDtypeStruct((T, SSD_HEADS), F32), jax.ShapeDtypeStruct((SSD_HEADS, T), F32)],
        compiler_params=_cparams(("parallel",)),
        name="aux_proj",
    )(xb, wc, wvt, wdt, wdtt)


CMP_PIECES = 2 * NSA_KV_GROUPS


def _compress_kernel(r_ref, pe_ref, w1_ref, w2k_ref, w2vt_ref, kc_ref, vct_ref):
    r = r_ref[0]
    nr = r.shape[0]
    u = _dot(r, w1_ref[0])
    v = _dot(r, w1_ref[1])
    c = _dot(pe_ref[0], w1_ref[0]) + _dot(pe_ref[1], w1_ref[1])
    hid = u + pltpu.roll(v, nr - 1, axis=0)
    hid = (hid.reshape(nr // 8, 8, hid.shape[1]) + c[None]).reshape(nr, hid.shape[1])
    act = _gelu_tanh(hid).astype(BF16)
    for g in range(NSA_KV_GROUPS):
        kp, vp = g, NSA_KV_GROUPS + g
        kc_ref[0, g] = _dot(act[:, kp * CMP_HIDDEN:(kp + 1) * CMP_HIDDEN], w2k_ref[g]).astype(kc_ref.dtype)
        vct = _dot_nt(w2vt_ref[...], act[:, vp * CMP_HIDDEN:(vp + 1) * CMP_HIDDEN])
        vct_ref[0, g] = vct.astype(vct_ref.dtype)


def _cmp_first_layer(w1k, w1v, pek, pev):
    half = CMP_BLOCK // 2
    blocks, pes = [], []
    for p, (w1, pe1) in enumerate(((w1k, pek), (w1k, pek), (w1v, pev), (w1v, pev))):
        wp = w1.reshape(2, half, NSA_HEAD_DIM, CMP_HIDDEN).astype(BF16)
        blocks.append(jnp.pad(wp, ((0, 0), (0, 0), (0, 0), (p * CMP_HIDDEN, (CMP_PIECES - 1 - p) * CMP_HIDDEN))))
        pes.append(pe1.reshape(2, half, NSA_HEAD_DIM).astype(BF16))
    rows = half * CMP_PIECES * NSA_HEAD_DIM
    big = jnp.concatenate(blocks, axis=2).reshape(2, rows, CMP_PIECES * CMP_HIDDEN)
    pe = jnp.concatenate(pes, axis=2).reshape(2, 1, rows)
    return big, jnp.broadcast_to(pe, (2, 8, rows))


def _nsa_compress(cmp, w1big, pebig, w2k, w2vt, B, S):
    NR = S // CMP_STRIDE
    W = CMP_STRIDE * 2 * NSA_KV_W
    r = cmp.reshape(B, NR, W)
    once = pl.Buffered(1)
    full = lambda a: pl.BlockSpec(a.shape, lambda b: (0,) * a.ndim, pipeline_mode=once)
    return pl.pallas_call(
        _compress_kernel,
        grid=(B,),
        in_specs=[pl.BlockSpec((1, NR, W), lambda b: (b, 0, 0)), full(pebig), full(w1big), full(w2k), full(w2vt)],
        out_specs=[pl.BlockSpec((1, NSA_KV_GROUPS, NR, HEAD_SLOT), lambda b: (b, 0, 0, 0)),
                   pl.BlockSpec((1, NSA_KV_GROUPS, NSA_HEAD_DIM, NR), lambda b: (b, 0, 0, 0))],
        out_shape=[jax.ShapeDtypeStruct((B, NSA_KV_GROUPS, NR, HEAD_SLOT), BF16),
                   jax.ShapeDtypeStruct((B, NSA_KV_GROUPS, NSA_HEAD_DIM, NR), BF16)],
        compiler_params=_cparams(("parallel",)),
        name="nsa_compress",
    )(r, pebig, w1big, w2k, w2vt)


def _stack_heads(q):
    return jnp.concatenate([q[:, j * HEAD_SLOT:(j + 1) * HEAD_SLOT] for j in range(NSA_HPG)], axis=0)


CMP_VARIANTS = 4


def _cmp_attn_kernel(q_ref, kc_ref, vct_ref, ovt_ref, gate_ref, ocmpt_ref, sel_ref, gt_ref, imp_ref, *, tq):
    g = pl.program_id(1)
    i = pl.program_id(2)
    q2 = _stack_heads(q_ref[...])
    nc_all = kc_ref.shape[2]
    gt_ref[...] = _sigmoid(gate_ref[...].astype(F32)).T

    def attend(nc):
        kc = kc_ref[0, 0, 0:nc, :]
        vct = vct_ref[0, 0, :, 0:nc]
        sts = [_dot_nt(kc, q2[j * tq:(j + 1) * tq]) for j in range(NSA_HPG)]
        n = lax.broadcasted_iota(jnp.int32, (nc, tq), 0)
        t = i * tq + lax.broadcasted_iota(jnp.int32, (nc, tq), 1)
        mask = n * CMP_STRIDE + (CMP_BLOCK - 1) <= t
        ps = None
        for j in range(NSA_HPG):
            s = jnp.where(mask, sts[j], MASK_VALUE)
            m = jnp.max(s, axis=0, keepdims=True)
            e = jnp.where(mask, jnp.exp2(s - m), 0.0)
            den = jnp.maximum(jnp.sum(e, axis=0, keepdims=True), 1e-30)
            p = e * (1.0 / den)
            ps = p if ps is None else ps + p
            gate = gt_ref[pl.ds(g * (NSA_HPG * 3) + j * 3, 1), :]
            ocmpt_ref[0, 0, 0, :, j * tq:(j + 1) * tq] = (gate * _dot(vct, p.astype(BF16))).astype(ocmpt_ref.dtype)
        ovt = ovt_ref[:, 0:nc]
        hi = ps.astype(BF16)
        r1 = ps - hi.astype(F32)
        mid = r1.astype(BF16)
        lo = (r1 - mid.astype(F32)).astype(BF16)
        imp_ref[...] = _dot(ovt, hi) + _dot(ovt, mid) + _dot(ovt, lo)

    step = nc_all // CMP_VARIANTS
    variant = jnp.minimum(((i + 1) * (tq // CMP_STRIDE) - 1) // step, CMP_VARIANTS - 1)
    for v in range(CMP_VARIANTS):
        pl.when(variant == v)(functools.partial(attend, (v + 1) * step))

    imp = imp_ref[...]
    blk = lax.broadcasted_iota(jnp.int32, imp.shape, 0)
    tt = i * tq + lax.broadcasted_iota(jnp.int32, imp.shape, 1)
    cur = tt // SEL_BLOCK
    forced = (blk == 0) | (blk == cur) | (blk == cur - 1)
    causal = blk * SEL_BLOCK <= tt
    v0 = jnp.where(forced, FORCE_SCORE, jnp.where(causal, imp, -1.0))

    blk1 = blk[:, 0:LANES]

    def pick(_, tiles):
        out = []
        for v in tiles:
            mx = jnp.max(v, axis=0, keepdims=True)
            idx = jnp.min(jnp.where(v == mx, blk1, SEL_LANES), axis=0, keepdims=True)
            out.append(jnp.where(blk1 == idx, -jnp.inf, v))
        return tuple(out)

    tiles = lax.fori_loop(0, SEL_TOPN, pick, tuple(v0[:, u * LANES:(u + 1) * LANES] for u in range(tq // LANES)))
    for u, v in enumerate(tiles):
        sel_ref[0, 0, u * LANES:(u + 1) * LANES, :] = jnp.where(v == -jnp.inf, 1.0, 0.0).T.astype(sel_ref.dtype)


def _nsa_cmp_attn(proj, kc, vct, ovt, B, S):
    tq = NSA_TQ
    nq = S // tq
    G = NSA_KV_GROUPS
    NC = kc.shape[2]
    qw = NSA_HPG * HEAD_SLOT
    qblk = C_QEXT // qw
    kern = functools.partial(_cmp_attn_kernel, tq=tq)
    return pl.pallas_call(
        kern,
        grid=(B, G, nq),
        in_specs=[pl.BlockSpec((tq, qw), lambda b, g, i: (b * nq + i, qblk + g)),
                  pl.BlockSpec((1, 1, NC, HEAD_SLOT), lambda b, g, i: (b, g, 0, 0)),
                  pl.BlockSpec((1, 1, NSA_HEAD_DIM, NC), lambda b, g, i: (b, g, 0, 0)),
                  pl.BlockSpec(ovt.shape, lambda b, g, i: (0, 0)),
                  pl.BlockSpec((tq, LANES), lambda b, g, i: (b * nq + i, C_SMALL // LANES))],
        out_specs=[pl.BlockSpec((1, 1, 1, NSA_HEAD_DIM, NSA_HPG * tq), lambda b, g, i: (b, g, i, 0, 0)),
                   pl.BlockSpec((1, 1, tq, SEL_LANES), lambda b, g, i: (b, g, i, 0))],
        out_shape=[jax.ShapeDtypeStruct((B, G, nq, NSA_HEAD_DIM, NSA_HPG * tq), BF16),
                   jax.ShapeDtypeStruct((B, G, S, SEL_LANES), BF16)],
        scratch_shapes=[pltpu.VMEM((LANES, tq), F32), pltpu.VMEM((SEL_LANES, tq), F32)],
        compiler_params=_cparams(("parallel", "parallel", "parallel")),
        name="nsa_cmp_attn",
    )(proj, kc, vct, ovt, proj)


SEL_TK = 512
NSA_TQ = 512
WIN_KEYS = WINDOW + NSA_TQ


HALF = NSA_TQ
NCH = NSA_HPG * NSA_TQ // HALF


def _sel_win_kernel(q_ref, ksel_ref, vselt_ref, kwin_ref, vwint_ref, sel_ref, et_ref, ocmpt_ref,
                    gate_ref, dbias_ref, wbias_ref, o_ref, s_ref, m_ref, acc_ref, gt_ref):
    g = pl.program_id(1)
    i = pl.program_id(2)
    tq = NSA_TQ
    t0 = i * tq
    q2 = _stack_heads(q_ref[...])
    selm1 = sel_ref[0, 0] - 1.0
    qx = jnp.concatenate([q2, jnp.concatenate([selm1] * NSA_HPG, axis=0)], axis=1)
    qxh = [qx[c * HALF:(c + 1) * HALF] for c in range(NCH)]
    q2h = [q2[c * HALF:(c + 1) * HALF] for c in range(NCH)]
    kpb = SEL_TK // LANES
    dv = NSA_HEAD_DIM

    wblk = jnp.maximum((t0 - WINDOW) // LANES, 0)
    start = pl.multiple_of(wblk * LANES, LANES)
    kw = kwin_ref[pl.ds(start, WIN_KEYS), :]
    vwt = jnp.concatenate([vwint_ref[wblk + u] for u in range(WIN_KEYS // LANES)], axis=1)
    wbias = wbias_ref[jnp.minimum(i, WINDOW // tq)].astype(F32)
    sw = _dot_nt(kw, q2)

    def key_tile(kv):
        off = pl.multiple_of(kv * SEL_TK, SEL_TK)
        return jnp.concatenate([ksel_ref[pl.ds(off, SEL_TK), :], et_ref[pl.ds(off, SEL_TK), :]], axis=1)

    def scores_into(kv):
        kx = key_tile(kv)
        for c in range(NCH):
            s_ref[:, c * HALF:(c + 1) * HALF] = _dot_nt(kx, qxh[c])

    def update_chunk(c, vt, s):
        cols = slice(c * HALF, (c + 1) * HALF)
        m_old = m_ref[:, cols]
        m_new = jnp.maximum(m_old, jnp.max(s, axis=0, keepdims=True))
        p = jnp.exp2(s - m_new).astype(BF16)
        acc_ref[:, cols] = jnp.exp2(m_old - m_new) * acc_ref[:, cols] + _dot(vt, p)
        m_ref[:, cols] = m_new

    def update(vt, s_chunks):
        for c in range(NCH):
            update_chunk(c, vt, s_chunks[c])

    def sel_values(kv):
        return jnp.concatenate([vselt_ref[kv * kpb + u] for u in range(kpb)], axis=1)

    def load_scores():
        return [s_ref[:, c * HALF:(c + 1) * HALF] for c in range(NCH)]

    m_ref[...] = jnp.full(m_ref.shape, MASK_VALUE, F32)
    acc_ref[...] = jnp.zeros(acc_ref.shape, F32)
    kd = t0 // SEL_TK
    scores_into(0)
    sw = sw + jnp.concatenate([wbias] * NSA_HPG, axis=1)
    mw = jnp.max(sw, axis=0, keepdims=True)
    accw = _dot(vwt, jnp.exp2(sw - mw).astype(BF16))
    ot_win = accw[0:dv] * (1.0 / jnp.maximum(accw[dv:dv + 1], 1e-30))

    def body(kv, carry):
        kx = key_tile(kv + 1)
        vt = sel_values(kv)
        pending = None
        for c in range(NCH):
            cols = slice(c * HALF, (c + 1) * HALF)
            s = s_ref[:, cols]
            s_ref[:, cols] = _dot_nt(kx, qxh[c])
            if pending is not None:
                update_chunk(c - 1, vt, pending)
            pending = s
        update_chunk(NCH - 1, vt, pending)
        return carry

    lax.fori_loop(0, kd, body, 0)
    dbias = dbias_ref[i % (SEL_TK // tq)].astype(F32)
    update(sel_values(kd), [sc + dbias for sc in load_scores()])
    acc = acc_ref[...]
    ot_slc = acc[0:dv] * (1.0 / jnp.maximum(acc[dv:dv + 1], 1e-30))

    gt_ref[...] = _sigmoid(gate_ref[...].astype(F32)).T
    ots = []
    for j in range(NSA_HPG):
        base = g * (NSA_HPG * 3) + j * 3
        ots.append(ocmpt_ref[0, 0, 0, :, j * tq:(j + 1) * tq].astype(F32)
                   + gt_ref[pl.ds(base + 1, 1), :] * ot_slc[:, j * tq:(j + 1) * tq]
                   + gt_ref[pl.ds(base + 2, 1), :] * ot_win[:, j * tq:(j + 1) * tq])
    for jp in range(NSA_HPG // 2):
        pair = jnp.concatenate([ots[2 * jp], ots[2 * jp + 1]], axis=0)
        o_ref[:, jp * LANES:(jp + 1) * LANES] = pair.T.astype(o_ref.dtype)


def _diag_bias():
    r = jnp.arange(SEL_TK)[None, :, None]
    c = jnp.arange(NSA_TQ)[None, None, :]
    off = (jnp.arange(SEL_TK // NSA_TQ) * NSA_TQ)[:, None, None]
    return jnp.where(r <= off + c, 0.0, MASK_VALUE).astype(BF16)


def _window_bias():
    r = jnp.arange(WIN_KEYS)[None, :, None]
    c = jnp.arange(NSA_TQ)[None, None, :]
    off = jnp.minimum(jnp.arange(WINDOW // NSA_TQ + 1) * NSA_TQ, WINDOW)[:, None, None]
    diff = off + c - r
    return jnp.where((diff >= 0) & (diff < WINDOW), 0.0, MASK_VALUE).astype(BF16)


def _nsa_sel_win(proj, vsel_t, vwin_t, sel, et, ocmpt, B, S):
    T = B * S
    tq = NSA_TQ
    assert HALF == tq
    dbias = _diag_bias()
    wbias = _window_bias()
    nq = S // tq
    G = NSA_KV_GROUPS
    qw = NSA_HPG * HEAD_SLOT
    qblk = C_QEXT // qw
    vsel_t = vsel_t.reshape(G, B, S // LANES, VT_ROWS, LANES)
    vwin_t = vwin_t.reshape(G, B, S // LANES, VT_ROWS, LANES)
    ow = NSA_HPG * NSA_HEAD_DIM
    kv_spec = lambda c: pl.BlockSpec((S, LANES), lambda b, g, i: (b, c // LANES))
    vt_spec = pl.BlockSpec((None, None, S // LANES, VT_ROWS, LANES), lambda b, g, i: (g, b, 0, 0, 0))
    return pl.pallas_call(
        _sel_win_kernel,
        grid=(B, G, nq),
        in_specs=[pl.BlockSpec((tq, qw), lambda b, g, i: (b * nq + i, qblk + g)),
                  kv_spec(C_KSEL), vt_spec, kv_spec(C_KWIN), vt_spec,
                  pl.BlockSpec((1, 1, tq, SEL_LANES), lambda b, g, i: (b, g, i, 0)),
                  pl.BlockSpec(et.shape, lambda b, g, i: (0, 0)),
                  pl.BlockSpec((1, 1, 1, NSA_HEAD_DIM, NSA_HPG * tq), lambda b, g, i: (b, g, i, 0, 0)),
                  pl.BlockSpec((tq, LANES), lambda b, g, i: (b * nq + i, C_SMALL // LANES)),
                  pl.BlockSpec(dbias.shape, lambda b, g, i: (0, 0, 0)),
                  pl.BlockSpec(wbias.shape, lambda b, g, i: (0, 0, 0))],
        out_specs=pl.BlockSpec((tq, ow), lambda b, g, i: (b * nq + i, g)),
        out_shape=jax.ShapeDtypeStruct((T, NSA_Q_W), BF16),
        scratch_shapes=[pltpu.VMEM((SEL_TK, NSA_HPG * tq), F32),
                        pltpu.VMEM((1, NSA_HPG * tq), F32),
                        pltpu.VMEM((VT_ROWS, NSA_HPG * tq), F32),
                        pltpu.VMEM((LANES, tq), F32)],
        compiler_params=_cparams(("parallel", "parallel", "parallel")),
        name="nsa_sel_win",
    )(proj, proj, vsel_t, proj, vwin_t, sel, et, ocmpt, proj, dbias, wbias)


TAIL_ROWS = 8


def _causal_conv(x, tail_ref, w, b):
    L, C = x.shape
    nv = L // TAIL_ROWS
    xx = jnp.concatenate([tail_ref[...], x], axis=0).reshape(nv + 1, TAIL_ROWS, C)
    sub = lax.broadcasted_iota(jnp.int32, (nv, TAIL_ROWS, C), 1)
    y = b + w[CONV_WIDTH - 1:CONV_WIDTH] * x
    for k in range(1, CONV_WIDTH):
        rot = pltpu.roll(xx, k, axis=1)
        shifted = jnp.where(sub >= k, rot[1:], rot[:-1]).reshape(L, C)
        y = y + w[CONV_WIDTH - 1 - k:CONV_WIDTH - k] * shifted
    tail_ref[...] = x[L - TAIL_ROWS:L]
    return y


SSD_CPS = 2


def _ssd_kernel(z_ref, xbc_ref, dtc_ref, dtr_ref, cw_ref, cb_ref, dtbc_ref, dtbr_ref, alc_ref, alr_ref,
                dsk_ref, nw_ref, o_ref, state_ref, tail_ref, y_ref):
    c = pl.program_id(1)
    L = SSD_CHUNK
    P = SSD_HEAD_DIM
    N = SSD_STATE
    hpg = SSD_HEADS // SSD_GROUPS

    @pl.when(c == 0)
    def _():
        state_ref[...] = jnp.zeros_like(state_ref)
        tail_ref[...] = jnp.zeros_like(tail_ref)

    conv = _causal_conv(xbc_ref[...].astype(F32), tail_ref, cw_ref[...], cb_ref[...])
    xbc = conv * _sigmoid(conv)
    dt_c = _softplus(dtc_ref[...] + dtbc_ref[...])
    dt_r = _softplus(dtr_ref[...] + dtbr_ref[...])
    a_c = dt_c * (-jnp.exp(alc_ref[...]))
    a_r = dt_r * (-jnp.exp(alr_ref[...]))
    ii = lax.broadcasted_iota(jnp.int32, (L, L), 0)
    jj = lax.broadcasted_iota(jnp.int32, (L, L), 1)
    tri = ii >= jj
    tril = tri.astype(F32)
    triu = (ii <= jj).astype(F32)

    for cc in range(SSD_CPS):
        rs = slice(cc * L, (cc + 1) * L)
        xs = xbc[rs, 0:SSD_INNER]
        bm = xbc[rs, SSD_INNER:SSD_INNER + SSD_GROUPS * N]
        cm = xbc[rs, SSD_INNER + SSD_GROUPS * N:SSD_INNER + 2 * SSD_GROUPS * N]
        acum_c = _dot_f32(tril, a_c[rs])
        acum_r = _dot_f32(a_r[:, rs], triu)
        for g in range(SSD_GROUPS):
            bg = bm[:, g * N:(g + 1) * N]
            cgb = cm[:, g * N:(g + 1) * N].astype(BF16)
            cb = _dot_nt(cgb, bg.astype(BF16))
            for j in range(hpg):
                h = g * hpg + j
                acb = jnp.broadcast_to(acum_c[:, h:h + 1], (L, L))
                a_last = acum_c[L - 1:L, h:h + 1]
                xh_raw = xs[:, h * P:(h + 1) * P]
                xh = (xh_raw * jnp.broadcast_to(dt_c[rs, h:h + 1], (L, P))).astype(BF16)
                lmat = jnp.where(tri, jnp.exp(acb - acum_r[h:h + 1, :]), 0.0)
                y = _dot((cb * lmat).astype(BF16), xh)
                prev = state_ref[h]
                y = y + _dot(cgb, prev.astype(BF16)) * jnp.exp(acb[:, 0:P])
                bd = (bg * jnp.exp(a_last - acb[:, 0:N])).astype(BF16)
                state_ref[h] = jnp.exp(a_last) * prev + _dot_tn(bd, xh)
                y_ref[rs, h * P:(h + 1) * P] = y + dsk_ref[:, h * P:(h + 1) * P] * xh_raw

    zf = z_ref[...].astype(F32)
    yg = y_ref[...] * (zf * _sigmoid(zf))
    ms = jnp.mean(yg * yg, axis=-1, keepdims=True)
    o_ref[...] = (yg * lax.rsqrt(ms + RMS_EPS) * nw_ref[...]).astype(o_ref.dtype)


def _ssd_mixer(proj, dt_col, dt_row, conv_w, conv_b, dt_bias, a_log, d_skip, norm_w, B, S):
    T = B * S
    L = SSD_CPS * SSD_CHUNK
    nc = S // L
    H = SSD_HEADS
    full = lambda a: pl.BlockSpec(a.shape, lambda b, c: (0, 0))
    cb2 = conv_b.reshape(1, -1)
    dtb_c = dt_bias.reshape(1, H)
    dtb_r = dt_bias.reshape(H, 1)
    al_c = a_log.reshape(1, H)
    al_r = a_log.reshape(H, 1)
    dsk = jnp.repeat(d_skip, SSD_HEAD_DIM).reshape(1, SSD_INNER)
    nw = norm_w.reshape(1, SSD_INNER)
    return pl.pallas_call(
        _ssd_kernel,
        grid=(B, nc),
        in_specs=[pl.BlockSpec((L, SSD_INNER), lambda b, c: (b * nc + c, C_SSDZ // SSD_INNER)),
                  pl.BlockSpec((L, SSD_XBC_W), lambda b, c: (b * nc + c, C_XBC // SSD_XBC_W)),
                  pl.BlockSpec((L, H), lambda b, c: (b * nc + c, 0)),
                  pl.BlockSpec((H, L), lambda b, c: (0, b * nc + c)),
                  full(conv_w), full(cb2), full(dtb_c), full(dtb_r), full(al_c), full(al_r),
                  full(dsk), full(nw)],
        out_specs=pl.BlockSpec((L, SSD_INNER), lambda b, c: (b * nc + c, 0)),
        out_shape=jax.ShapeDtypeStruct((T, SSD_INNER), BF16),
        scratch_shapes=[pltpu.VMEM((H, SSD_STATE, SSD_HEAD_DIM), F32),
                        pltpu.VMEM((TAIL_ROWS, SSD_XBC_W), F32),
                        pltpu.VMEM((L, SSD_INNER), F32)],
        compiler_params=_cparams(("parallel", "arbitrary")),
        name="ssd_mixer",
    )(proj, proj, dt_col, dt_row, conv_w, cb2, dtb_c, dtb_r, al_c, al_r, dsk, nw)


SCAN_GROUP = 8


def _lru_kernel(x_ref, y_ref, cw_ref, cb_ref, wa_ref, ba_ref, wx_ref, bx_ref, lam_ref, o_ref,
                h_ref, tail_ref, *, tc):
    c = pl.program_id(1)

    @pl.when(c == 0)
    def _():
        h_ref[...] = jnp.zeros_like(h_ref)
        tail_ref[...] = jnp.zeros_like(tail_ref)

    xr = _causal_conv(x_ref[...].astype(F32), tail_ref, cw_ref[...], cb_ref[...])
    xrb = xr.astype(BF16)
    r = _sigmoid(_dot(xrb, wa_ref[...]) + ba_ref[...])
    ig = _sigmoid(_dot(xrb, wx_ref[...]) + bx_ref[...])
    log_a = -LRU_C * r * _softplus(-lam_ref[...])
    a = jnp.exp(log_a)
    b = jnp.sqrt(1.0 - jnp.exp(2.0 * log_a)) * (ig * xr)
    ng = tc // SCAN_GROUP
    a = a.reshape(ng, SCAN_GROUP, a.shape[1])
    b = b.reshape(ng, SCAN_GROUP, b.shape[1])
    sub = lax.broadcasted_iota(jnp.int32, a.shape, 1)
    k = 1
    while k < SCAN_GROUP:
        keep = sub >= k
        a_s = jnp.where(keep, pltpu.roll(a, k, axis=1), 1.0)
        b_s = jnp.where(keep, pltpu.roll(b, k, axis=1), 0.0)
        b = a * b_s + b
        a = a * a_s
        k *= 2
    carry = h_ref[0:1, :]
    groups = []
    for rg in range(ng):
        hg = a[rg] * carry + b[rg]
        groups.append(hg)
        carry = hg[SCAN_GROUP - 1:SCAN_GROUP]
    h = jnp.concatenate(groups, axis=0)
    h_ref[...] = jnp.broadcast_to(carry, h_ref.shape)
    o_ref[...] = (h * _gelu_tanh(y_ref[...].astype(F32))).astype(o_ref.dtype)


def _block_diag(w):
    nb, c, d = w.shape
    eye = jnp.eye(nb, dtype=w.dtype)
    return (eye[:, None, :, None] * w[:, :, None, :]).reshape(nb * c, nb * d)


def _lru_mixer(proj, conv_w, conv_b, wa, ba, wx, bx, lam, B, S):
    T = B * S
    tc = min(256, S)
    nt = S // tc
    W = LRU_WIDTH
    wa_bd = _block_diag(wa).astype(BF16)
    wx_bd = _block_diag(wx).astype(BF16)
    row = lambda v: v.reshape(1, W)
    full = lambda a: pl.BlockSpec(a.shape, lambda b, c: (0, 0))
    args = (conv_w, row(conv_b), wa_bd, row(ba), wx_bd, row(bx), row(lam))
    return pl.pallas_call(
        functools.partial(_lru_kernel, tc=tc),
        grid=(B, nt),
        in_specs=[pl.BlockSpec((tc, W), lambda b, c: (b * nt + c, C_LRUX // W)),
                  pl.BlockSpec((tc, W), lambda b, c: (b * nt + c, C_LRUY // W))]
                 + [full(a) for a in args],
        out_specs=pl.BlockSpec((tc, W), lambda b, c: (b * nt + c, 0)),
        out_shape=jax.ShapeDtypeStruct((T, W), BF16),
        scratch_shapes=[pltpu.VMEM((TAIL_ROWS, W), F32), pltpu.VMEM((TAIL_ROWS, W), F32)],
        compiler_params=_cparams(("parallel", "arbitrary")),
        name="lru_mixer",
    )(proj, proj, *args)


def _merge_kernel(x_ref, mg_ref, on_ref, os_ref, ol_ref, pn_ref, ps_ref, pl_ref, wo_ref, g_ref, b_ref,
                  of_ref, ob_ref):
    d = D_MODEL
    gate = _sigmoid(mg_ref[...].astype(F32))
    mixed = (gate[:, 0:d] * _dot(on_ref[...], pn_ref[...])
             + gate[:, d:2 * d] * _dot(os_ref[...], ps_ref[...])
             + gate[:, 2 * d:3 * d] * _dot(ol_ref[...], pl_ref[...]))
    v = ALPHA * x_ref[...] + _dot(mixed.astype(BF16), wo_ref[...])
    out = _layer_norm(v, g_ref[...], b_ref[...])
    of_ref[...] = out
    ob_ref[...] = out.astype(BF16)


def _merge(x, proj, o_nsa, o_ssd, o_lru, pn, ps, plru, wo, g, b, layer):
    T = x.shape[0]
    tm = min(512, T)
    d = D_MODEL
    rowblk = lambda w: pl.BlockSpec((tm, w), lambda i: (i, 0))
    full = lambda a: pl.BlockSpec((None,) + a.shape[1:], lambda i: (layer, 0, 0))
    g2, b2 = g.reshape(-1, 1, d), b.reshape(-1, 1, d)
    return pl.pallas_call(
        _merge_kernel,
        grid=(T // tm,),
        in_specs=[rowblk(d), pl.BlockSpec((tm, 3 * d), lambda i: (i, C_MERGE // (3 * d))),
                  rowblk(o_nsa.shape[1]), rowblk(o_ssd.shape[1]), rowblk(o_lru.shape[1]),
                  full(pn), full(ps), full(plru), full(wo), full(g2), full(b2)],
        out_specs=[rowblk(d), rowblk(d)],
        out_shape=[jax.ShapeDtypeStruct((T, d), F32), jax.ShapeDtypeStruct((T, d), BF16)],
        compiler_params=_cparams(("parallel",)),
        name="merge_ln",
    )(x, proj, o_nsa, o_ssd, o_lru, pn, ps, plru, wo, g2, b2)


def _route(sel, aff):
    epg = EXPERTS_PER_GROUP
    scores = []
    for gi in range(N_EXPERT_GROUPS):
        v = sel[gi * epg:(gi + 1) * epg]
        pair = None
        for a in range(epg):
            for b in range(a + 1, epg):
                sab = v[a] + v[b]
                pair = sab if pair is None else jnp.maximum(pair, sab)
        scores.append(pair)
    best = jnp.zeros_like(scores[0], dtype=jnp.int32)
    best_s = scores[0]
    for gi in range(1, N_EXPERT_GROUPS):
        better = scores[gi] > best_s
        best = jnp.where(better, gi, best)
        best_s = jnp.where(better, scores[gi], best_s)
    chosen = []
    for k in range(N_EXPERTS):
        gi = k // epg
        rank = jnp.zeros_like(best)
        for o in range(gi * epg, (gi + 1) * epg):
            if o == k:
                continue
            ahead = (sel[o] > sel[k]) | ((sel[o] == sel[k]) & (o < k))
            rank = rank + ahead.astype(jnp.int32)
        chosen.append((best == gi) & (rank < TOP_K))
    wsum = None
    for k in range(N_EXPERTS):
        wk = jnp.where(chosen[k], aff[k], 0.0)
        wsum = wk if wsum is None else wsum + wk
    inv = 1.0 / wsum
    return [jnp.where(chosen[k], aff[k], 0.0) * inv for k in range(N_EXPERTS)]


EXPERT_PAIRS = N_EXPERTS // 2
PAIRS_PER_ITER = 2


def _moe_kernel(xb_ref, xf_ref, p_ref, rw_ref, rb_ref, pg_ref, pp_ref, wg_ref, wu_ref, wd_ref,
                g_ref, b_ref, of_ref, ob_ref, acc_ref, gates_ref):
    xb = xb_ref[...]
    tm = xb.shape[0]
    logits = _dot_nt(rw_ref[...], xb)
    aff = _sigmoid(logits)
    sel = aff + rb_ref[...]
    gate_rows = _route([sel[k:k + 1, :] for k in range(N_EXPERTS)],
                       [aff[k:k + 1, :] for k in range(N_EXPERTS)])
    gt = jnp.concatenate(gate_rows + [jnp.zeros((LANES - N_EXPERTS, tm), F32)], axis=0)
    gates_ref[...] = gt.T
    acc_ref[...] = _sigmoid(_dot(xb, pg_ref[...])) * _dot(p_ref[...].astype(BF16), pp_ref[...])
    lane = lax.broadcasted_iota(jnp.int32, (tm, LANES), 1)

    def gate_cols(k):
        gates = gates_ref[...]
        cols = [jnp.broadcast_to(jnp.sum(jnp.where(lane == 2 * k + u, gates, 0.0), axis=-1, keepdims=True),
                                 (tm, D_EXPERT)) for u in range(2)]
        return jnp.concatenate(cols, axis=1)

    def step(it, carry):
        ks = [it * PAIRS_PER_ITER + u for u in range(PAIRS_PER_ITER)]
        hgs = [[_dot(xb, wg_ref[2 * k + u]) for u in range(2)] for k in ks]
        hus = [[_dot(xb, wu_ref[2 * k + u]) for u in range(2)] for k in ks]
        for k, hg2, hu2 in zip(ks, hgs, hus):
            hg = jnp.concatenate(hg2, axis=1)
            h = (hg * _sigmoid(hg)) * jnp.concatenate(hu2, axis=1) * gate_cols(k)
            acc_ref[...] += _dot(h.astype(BF16), wd_ref[k])
        return carry

    lax.fori_loop(0, EXPERT_PAIRS // PAIRS_PER_ITER, step, 0)
    out = _layer_norm(ALPHA * xf_ref[...] + acc_ref[...], g_ref[...], b_ref[...])
    of_ref[...] = out
    ob_ref[...] = out.astype(BF16)


def _moe_ple(xb, xf, p, layer, rw_t, rb, pg, pp, wg, wu, wd, g, b):
    T = xb.shape[0]
    tm = min(512, T)
    d = D_MODEL
    rowblk = lambda w: pl.BlockSpec((tm, w), lambda i: (i, 0))
    once = pl.Buffered(1)
    full = lambda a: pl.BlockSpec(a.shape, lambda i: (0,) * a.ndim, pipeline_mode=once)
    lyr = lambda a: pl.BlockSpec((None,) + a.shape[1:], lambda i: (layer,) + (0,) * (a.ndim - 1), pipeline_mode=once)
    g2, b2 = g.reshape(-1, 1, d), b.reshape(-1, 1, d)
    return pl.pallas_call(
        _moe_kernel,
        grid=(T // tm,),
        in_specs=[rowblk(d), rowblk(d), pl.BlockSpec((None, tm, PLE_DIM), lambda i: (layer, i, 0)),
                  full(rw_t), full(rb), lyr(pg), lyr(pp),
                  lyr(wg), lyr(wu), lyr(wd), lyr(g2), lyr(b2)],
        out_specs=[rowblk(d), rowblk(d)],
        out_shape=[jax.ShapeDtypeStruct((T, d), F32), jax.ShapeDtypeStruct((T, d), BF16)],
        scratch_shapes=[pltpu.VMEM((tm, d), F32), pltpu.VMEM((tm, LANES), F32)],
        compiler_params=_cparams(("parallel",)),
        name="moe_ple_ln",
    )(xb, xf, p, rw_t, rb, pg, pp, wg, wu, wd, g2, b2)


def _overlap_matrix(nc):
    n = jnp.arange(nc)[None, :]
    m = jnp.arange(SEL_LANES)[:, None]
    ratio = SEL_BLOCK // CMP_STRIDE
    ov = jnp.zeros((SEL_LANES, nc), F32)
    for k in range(CMP_BLOCK // CMP_STRIDE):
        ov = ov + ((n + k) // ratio == m).astype(F32)
    return ov.astype(BF16)


def _expand_matrix(S):
    c = jnp.arange(S)[:, None]
    m = jnp.arange(SEL_LANES)[None, :]
    return jnp.where(c // SEL_BLOCK == m, -MASK_VALUE, 0.0).astype(BF16)


def _pad_w2(w2):
    out = jnp.zeros((NSA_KV_GROUPS, CMP_HIDDEN, HEAD_SLOT), F32)
    for g in range(NSA_KV_GROUPS):
        out = out.at[g, :, g * NSA_HEAD_DIM:(g + 1) * NSA_HEAD_DIM].set(w2)
    return out.astype(BF16)


def kernel(x, p, w_in, nsa_pe_k, nsa_w1_k, nsa_w2_k, nsa_pe_v, nsa_w1_v, nsa_w2_v, ssd_conv_w, ssd_conv_b, ssd_dt_bias, ssd_a_log, ssd_d, ssd_norm_w, lru_conv_w, lru_conv_b, lru_wa, lru_ba, lru_wx, lru_bx, lru_lambda, proj_nsa, proj_ssd, proj_lru, w_out, ln1_g, ln1_b, router_w, router_b, exp_w_gate, exp_w_up, exp_w_down, ple_w_gate, ple_w_proj, ln2_g, ln2_b):
    B, S, d = x.shape
    T = B * S
    depth = w_in.shape[0]
    assert d == D_MODEL and S % SEL_TK == 0 and S >= WIN_KEYS and S // SEL_BLOCK <= SEL_LANES
    nr = S // CMP_STRIDE
    ov = _overlap_matrix(nr)
    emat = _expand_matrix(S)
    rw_t = router_w.T.astype(BF16)
    rb = router_b.reshape(N_EXPERTS, 1).astype(F32)

    w_main = _prep_w_in(w_in)
    aux_w = _aux_weights(w_in)
    pn, ps, plru, wo = (a.astype(BF16) for a in (proj_nsa, proj_ssd, proj_lru, w_out))
    pg, pp, wg, wu = (a.astype(BF16) for a in (ple_w_gate, ple_w_proj, exp_w_gate, exp_w_up))
    wd = exp_w_down.reshape(depth, EXPERT_PAIRS, 2 * D_EXPERT, d).astype(BF16)
    p3 = p.reshape(depth, T, PLE_DIM)

    xf = x.reshape(T, d)
    xb = xf
    for i in range(depth):
        proj = _in_proj(xb, w_main, i)
        cmp, vsel_t, vwin_t, dt_col, dt_row = _aux_proj(xb, aux_w, i)

        w1big, pebig = _cmp_first_layer(nsa_w1_k[i], nsa_w1_v[i], nsa_pe_k[i], nsa_pe_v[i])
        kc, vct = _nsa_compress(cmp, w1big, pebig, _pad_w2(nsa_w2_k[i]),
                                nsa_w2_v[i].T.astype(BF16), B, S)
        ocmpt, sel = _nsa_cmp_attn(proj, kc, vct, ov, B, S)
        o_nsa = _nsa_sel_win(proj, vsel_t, vwin_t, sel, emat, ocmpt, B, S)

        o_ssd = _ssd_mixer(proj, dt_col, dt_row, ssd_conv_w[i], ssd_conv_b[i], ssd_dt_bias[i],
                           ssd_a_log[i], ssd_d[i], ssd_norm_w[i], B, S)
        o_lru = _lru_mixer(proj, lru_conv_w[i], lru_conv_b[i], lru_wa[i], lru_ba[i], lru_wx[i],
                           lru_bx[i], lru_lambda[i], B, S)

        xf, xb = _merge(xf, proj, o_nsa, o_ssd, o_lru, pn, ps, plru, wo, ln1_g, ln1_b, i)
        xf, xb = _moe_ple(xb, xf, p3, i, rw_t, rb, pg, pp, wg, wu, wd, ln2_g, ln2_b)
    return xf.reshape(B, S, d)
```

```python
import functools
import math

import jax
import jax.numpy as jnp
from jax import lax
from jax.experimental import pallas as pl
from jax.experimental.pallas import tpu as pltpu

F32 = jnp.float32
BF16 = jnp.bfloat16

D_MODEL = 1024
PLE_DIM = 256
NSA_HEADS = 8
NSA_KV_GROUPS = 2
NSA_HEAD_DIM = 64
NSA_HPG = NSA_HEADS // NSA_KV_GROUPS
NSA_Q_W = NSA_HEADS * NSA_HEAD_DIM
NSA_KV_W = NSA_KV_GROUPS * NSA_HEAD_DIM
CMP_BLOCK = 32
CMP_STRIDE = 16
CMP_HIDDEN = 256
SEL_BLOCK = 64
SEL_TOPN = 16
WINDOW = 512
FORCE_SCORE = 1e4
MASK_VALUE = -1e30
LOG2E = 1.4426950408889634
SSD_HEADS = 8
SSD_HEAD_DIM = 64
SSD_INNER = SSD_HEADS * SSD_HEAD_DIM
SSD_GROUPS = 2
SSD_STATE = 64
SSD_CHUNK = 128
SSD_XBC_W = SSD_INNER + 2 * SSD_GROUPS * SSD_STATE
CONV_WIDTH = 4
LRU_WIDTH = 512
LRU_BLOCKS = 8
LRU_BLOCK_DIM = LRU_WIDTH // LRU_BLOCKS
LRU_C = 8.0
N_EXPERTS = 16
N_EXPERT_GROUPS = 4
EXPERTS_PER_GROUP = N_EXPERTS // N_EXPERT_GROUPS
TOP_K = 2
D_EXPERT = 256
DEPTH = 2
ALPHA = (2 * DEPTH) ** 0.25
LN_EPS = 1e-5
RMS_EPS = 1e-5
IN_SIZES = (NSA_Q_W, NSA_KV_W, NSA_KV_W, NSA_KV_W, NSA_KV_W, NSA_KV_W, NSA_KV_W, NSA_HEADS * 3,
            SSD_INNER, SSD_XBC_W, SSD_HEADS, LRU_WIDTH, LRU_WIDTH, 3 * D_MODEL)

LANES = 128
SEL_LANES = 128
HEAD_SLOT = 128

C_MERGE = 0
C_QEXT = 3072
C_SSDZ = 4096
C_LRUX = 4608
C_LRUY = 5120
C_KSEL = 5632
C_KWIN = 5760
C_SMALL = 5888
C_XBC = 6144
PROJ_W = 6912
PROJ_TN = 2304
GATE_W = NSA_HEADS * 3

VMEM_LIMIT = 56 * 1024 * 1024


def _cparams(sem):
    return pltpu.CompilerParams(dimension_semantics=sem, vmem_limit_bytes=VMEM_LIMIT)


def _sigmoid(x):
    return 1.0 / (1.0 + jnp.exp(-x))


def _softplus(x):
    return jnp.maximum(x, 0.0) + jnp.log(1.0 + jnp.exp(-jnp.abs(x)))


def _gelu_tanh(x):
    c = math.sqrt(2.0 / math.pi)
    return 0.5 * x * (1.0 + jnp.tanh(c * (x + 0.044715 * (x * x * x))))


def _dot(a, b):
    return jnp.dot(a, b, preferred_element_type=F32)


def _dot_nt(a, b):
    return lax.dot_general(a, b, (((1,), (1,)), ((), ())), preferred_element_type=F32)


def _dot_tn(a, b):
    return lax.dot_general(a, b, (((0,), (0,)), ((), ())), preferred_element_type=F32)


def _dot_f32(a, b):
    return jnp.dot(a, b, preferred_element_type=F32, precision=lax.Precision.HIGHEST)


def _layer_norm(v, g, b):
    mu = jnp.mean(v, axis=-1, keepdims=True)
    vc = v - mu
    var = jnp.mean(vc * vc, axis=-1, keepdims=True)
    return vc * lax.rsqrt(var + LN_EPS) * g + b


def _matmul_kernel(x_ref, w_ref, o_ref):
    o_ref[...] = _dot(x_ref[...].astype(BF16), w_ref[...]).astype(o_ref.dtype)


def _in_proj(xb, w, layer):
    T, K = xb.shape
    N = w.shape[2]
    tm = min(1024, T)
    tn = PROJ_TN
    return pl.pallas_call(
        _matmul_kernel,
        grid=(N // tn, T // tm),
        in_specs=[pl.BlockSpec((tm, K), lambda j, i: (i, 0)),
                  pl.BlockSpec((None, K, tn), lambda j, i: (layer, 0, j))],
        out_specs=pl.BlockSpec((tm, tn), lambda j, i: (i, j)),
        out_shape=jax.ShapeDtypeStruct((T, N), BF16),
        compiler_params=_cparams(("parallel", "parallel")),
        name="in_proj",
    )(xb, w)


def _split_w_in(w):
    offs = [0]
    for s in IN_SIZES:
        offs.append(offs[-1] + s)
    return [w[..., offs[k]:offs[k + 1]] for k in range(len(IN_SIZES))]


def _prep_w_in(w):
    pc = _split_w_in(w)
    lead = w.shape[:-1]
    q = pc[0].reshape(*lead, NSA_KV_GROUPS, NSA_HPG, NSA_HEAD_DIM) * (NSA_HEAD_DIM ** -0.5 * LOG2E)
    zeros = jnp.zeros_like(q)
    q_ext = jnp.stack([jnp.concatenate([q[..., 0, :, :], zeros[..., 0, :, :]], axis=-1),
                       jnp.concatenate([zeros[..., 1, :, :], q[..., 1, :, :]], axis=-1)], axis=-3)
    q_ext = q_ext.reshape(*lead, NSA_HEADS * HEAD_SLOT)
    small = jnp.pad(pc[7], [(0, 0)] * len(lead) + [(0, C_XBC - C_SMALL - GATE_W)])
    out = jnp.concatenate([pc[13], q_ext, pc[8], pc[11], pc[12], pc[3], pc[5], small, pc[9]], axis=-1)
    assert out.shape[-1] == PROJ_W
    return out.astype(BF16)


VT_ROWS = NSA_HEAD_DIM + 16


def _aux_proj_kernel(x_ref, wc_ref, wvt_ref, wdt_ref, wdtt_ref, cmp_ref, vst_ref, vwt_ref, dtc_ref, dtr_ref):
    x = x_ref[...].astype(BF16)
    tm = x.shape[0]
    cmp_ref[...] = _dot(x, wc_ref[...]).astype(cmp_ref.dtype)
    vt = _dot_nt(wvt_ref[...], x)
    dv = NSA_HEAD_DIM
    ones = jnp.ones((VT_ROWS - dv, LANES), vst_ref.dtype)
    for u in range(tm // LANES):
        for k, o_ref in enumerate((vst_ref, vwt_ref)):
            for g in range(NSA_KV_GROUPS):
                r0 = (k * NSA_KV_GROUPS + g) * dv
                o_ref[g, u, 0:dv, :] = vt[r0:r0 + dv, u * LANES:(u + 1) * LANES].astype(o_ref.dtype)
                o_ref[g, u, dv:VT_ROWS, :] = ones
    xf = x.astype(F32)
    dtc_ref[...] = _dot(xf, wdt_ref[...])
    dtr_ref[...] = _dot_nt(wdtt_ref[...], xf)


def _aux_weights(w):
    pc = _split_w_in(w)
    wc = jnp.concatenate([pc[1], pc[2]], axis=-1).astype(BF16)
    wvt = jnp.swapaxes(jnp.concatenate([pc[4], pc[6]], axis=-1), -1, -2).astype(BF16)
    return wc, wvt, pc[10], jnp.swapaxes(pc[10], -1, -2)


def _aux_proj(xb, aux_w, layer):
    T, K = xb.shape
    wc, wvt, wdt, wdtt = aux_w
    tm = min(1024, T)
    full = lambda a: pl.BlockSpec((None,) + a.shape[1:], lambda i: (layer, 0, 0))
    vt_shape = jax.ShapeDtypeStruct((NSA_KV_GROUPS, T // LANES, VT_ROWS, LANES), BF16)
    vt_spec = pl.BlockSpec((NSA_KV_GROUPS, tm // LANES, VT_ROWS, LANES), lambda i: (0, i, 0, 0))
    return pl.pallas_call(
        _aux_proj_kernel,
        grid=(T // tm,),
        in_specs=[pl.BlockSpec((tm, K), lambda i: (i, 0)), full(wc), full(wvt), full(wdt), full(wdtt)],
        out_specs=[pl.BlockSpec((tm, 2 * NSA_KV_W), lambda i: (i, 0)), vt_spec, vt_spec,
                   pl.BlockSpec((tm, SSD_HEADS), lambda i: (i, 0)),
                   pl.BlockSpec((SSD_HEADS, tm), lambda i: (0, i))],
        out_shape=[jax.ShapeDtypeStruct((T, 2 * NSA_KV_W), BF16), vt_shape, vt_shape,
                   jax.ShapeDtypeStruct((T, SSD_HEADS), F32), jax.ShapeDtypeStruct((SSD_HEADS, T), F32)],
        compiler_params=_cparams(("parallel",)),
        name="aux_proj",
    )(xb, wc, wvt, wdt, wdtt)


CMP_PIECES = 2 * NSA_KV_GROUPS


def _compress_kernel(r_ref, pe_ref, w1_ref, w2k_ref, w2vt_ref, kc_ref, vct_ref):
    r = r_ref[0]
    nr = r.shape[0]
    u = _dot(r, w1_ref[0])
    v = _dot(r, w1_ref[1])
    c = _dot(pe_ref[0], w1_ref[0]) + _dot(pe_ref[1], w1_ref[1])
    hid = u + pltpu.roll(v, nr - 1, axis=0)
    hid = (hid.reshape(nr // 8, 8, hid.shape[1]) + c[None]).reshape(nr, hid.shape[1])
    act = _gelu_tanh(hid).astype(BF16)
    for g in range(NSA_KV_GROUPS):
        kp, vp = g, NSA_KV_GROUPS + g
        kc_ref[0, g] = _dot(act[:, kp * CMP_HIDDEN:(kp + 1) * CMP_HIDDEN], w2k_ref[g]).astype(kc_ref.dtype)
        vct = _dot_nt(w2vt_ref[...], act[:, vp * CMP_HIDDEN:(vp + 1) * CMP_HIDDEN])
        vct_ref[0, g] = vct.astype(vct_ref.dtype)


def _cmp_first_layer(w1k, w1v, pek, pev):
    half = CMP_BLOCK // 2
    blocks, pes = [], []
    for p, (w1, pe1) in enumerate(((w1k, pek), (w1k, pek), (w1v, pev), (w1v, pev))):
        wp = w1.reshape(2, half, NSA_HEAD_DIM, CMP_HIDDEN).astype(BF16)
        blocks.append(jnp.pad(wp, ((0, 0), (0, 0), (0, 0), (p * CMP_HIDDEN, (CMP_PIECES - 1 - p) * CMP_HIDDEN))))
        pes.append(pe1.reshape(2, half, NSA_HEAD_DIM).astype(BF16))
    rows = half * CMP_PIECES * NSA_HEAD_DIM
    big = jnp.concatenate(blocks, axis=2).reshape(2, rows, CMP_PIECES * CMP_HIDDEN)
    pe = jnp.concatenate(pes, axis=2).reshape(2, 1, rows)
    return big, jnp.broadcast_to(pe, (2, 8, rows))


def _nsa_compress(cmp, w1big, pebig, w2k, w2vt, B, S):
    NR = S // CMP_STRIDE
    W = CMP_STRIDE * 2 * NSA_KV_W
    r = cmp.reshape(B, NR, W)
    once = pl.Buffered(1)
    full = lambda a: pl.BlockSpec(a.shape, lambda b: (0,) * a.ndim, pipeline_mode=once)
    return pl.pallas_call(
        _compress_kernel,
        grid=(B,),
        in_specs=[pl.BlockSpec((1, NR, W), lambda b: (b, 0, 0)), full(pebig), full(w1big), full(w2k), full(w2vt)],
        out_specs=[pl.BlockSpec((1, NSA_KV_GROUPS, NR, HEAD_SLOT), lambda b: (b, 0, 0, 0)),
                   pl.BlockSpec((1, NSA_KV_GROUPS, NSA_HEAD_DIM, NR), lambda b: (b, 0, 0, 0))],
        out_shape=[jax.ShapeDtypeStruct((B, NSA_KV_GROUPS, NR, HEAD_SLOT), BF16),
                   jax.ShapeDtypeStruct((B, NSA_KV_GROUPS, NSA_HEAD_DIM, NR), BF16)],
        compiler_params=_cparams(("parallel",)),
        name="nsa_compress",
    )(r, pebig, w1big, w2k, w2vt)


def _stack_heads(q):
    return jnp.concatenate([q[:, j * HEAD_SLOT:(j + 1) * HEAD_SLOT] for j in range(NSA_HPG)], axis=0)


CMP_VARIANTS = 4


def _cmp_attn_kernel(q_ref, kc_ref, vct_ref, ovt_ref, gate_ref, ocmpt_ref, sel_ref, gt_ref, imp_ref, *, tq):
    g = pl.program_id(1)
    i = pl.program_id(2)
    q2 = _stack_heads(q_ref[...])
    nc_all = kc_ref.shape[2]
    gt_ref[...] = _sigmoid(gate_ref[...].astype(F32)).T

    def attend(nc):
        kc = kc_ref[0, 0, 0:nc, :]
        vct = vct_ref[0, 0, :, 0:nc]
        sts = [_dot_nt(kc, q2[j * tq:(j + 1) * tq]) for j in range(NSA_HPG)]
        n = lax.broadcasted_iota(jnp.int32, (nc, tq), 0)
        t = i * tq + lax.broadcasted_iota(jnp.int32, (nc, tq), 1)
        mask = n * CMP_STRIDE + (CMP_BLOCK - 1) <= t
        ps = None
        for j in range(NSA_HPG):
            s = jnp.where(mask, sts[j], MASK_VALUE)
            m = jnp.max(s, axis=0, keepdims=True)
            e = jnp.where(mask, jnp.exp2(s - m), 0.0)
            den = jnp.maximum(jnp.sum(e, axis=0, keepdims=True), 1e-30)
            p = e * (1.0 / den)
            ps = p if ps is None else ps + p
            gate = gt_ref[pl.ds(g * (NSA_HPG * 3) + j * 3, 1), :]
            ocmpt_ref[0, 0, 0, :, j * tq:(j + 1) * tq] = (gate * _dot(vct, p.astype(BF16))).astype(ocmpt_ref.dtype)
        ovt = ovt_ref[:, 0:nc]
        hi = ps.astype(BF16)
        r1 = ps - hi.astype(F32)
        mid = r1.astype(BF16)
        lo = (r1 - mid.astype(F32)).astype(BF16)
        imp_ref[...] = _dot(ovt, hi) + _dot(ovt, mid) + _dot(ovt, lo)

    step = nc_all // CMP_VARIANTS
    variant = jnp.minimum(((i + 1) * (tq // CMP_STRIDE) - 1) // step, CMP_VARIANTS - 1)
    for v in range(CMP_VARIANTS):
        pl.when(variant == v)(functools.partial(attend, (v + 1) * step))

    imp = imp_ref[...]
    blk = lax.broadcasted_iota(jnp.int32, imp.shape, 0)
    tt = i * tq + lax.broadcasted_iota(jnp.int32, imp.shape, 1)
    cur = tt // SEL_BLOCK
    forced = (blk == 0) | (blk == cur) | (blk == cur - 1)
    causal = blk * SEL_BLOCK <= tt
    v0 = jnp.where(forced, FORCE_SCORE, jnp.where(causal, imp, -1.0))

    blk1 = blk[:, 0:LANES]

    def pick(_, tiles):
        out = []
        for v in tiles:
            mx = jnp.max(v, axis=0, keepdims=True)
            idx = jnp.min(jnp.where(v == mx, blk1, SEL_LANES), axis=0, keepdims=True)
            out.append(jnp.where(blk1 == idx, -jnp.inf, v))
        return tuple(out)

    tiles = lax.fori_loop(0, SEL_TOPN, pick, tuple(v0[:, u * LANES:(u + 1) * LANES] for u in range(tq // LANES)))
    for u, v in enumerate(tiles):
        sel_ref[0, 0, u * LANES:(u + 1) * LANES, :] = jnp.where(v == -jnp.inf, 1.0, 0.0).T.astype(sel_ref.dtype)


def _nsa_cmp_attn(proj, kc, vct, ovt, B, S):
    tq = NSA_TQ
    nq = S // tq
    G = NSA_KV_GROUPS
    NC = kc.shape[2]
    qw = NSA_HPG * HEAD_SLOT
    qblk = C_QEXT // qw
    kern = functools.partial(_cmp_attn_kernel, tq=tq)
    return pl.pallas_call(
        kern,
        grid=(B, G, nq),
        in_specs=[pl.BlockSpec((tq, qw), lambda b, g, i: (b * nq + i, qblk + g)),
                  pl.BlockSpec((1, 1, NC, HEAD_SLOT), lambda b, g, i: (b, g, 0, 0)),
                  pl.BlockSpec((1, 1, NSA_HEAD_DIM, NC), lambda b, g, i: (b, g, 0, 0)),
                  pl.BlockSpec(ovt.shape, lambda b, g, i: (0, 0)),
                  pl.BlockSpec((tq, LANES), lambda b, g, i: (b * nq + i, C_SMALL // LANES))],
        out_specs=[pl.BlockSpec((1, 1, 1, NSA_HEAD_DIM, NSA_HPG * tq), lambda b, g, i: (b, g, i, 0, 0)),
                   pl.BlockSpec((1, 1, tq, SEL_LANES), lambda b, g, i: (b, g, i, 0))],
        out_shape=[jax.ShapeDtypeStruct((B, G, nq, NSA_HEAD_DIM, NSA_HPG * tq), BF16),
                   jax.ShapeDtypeStruct((B, G, S, SEL_LANES), BF16)],
        scratch_shapes=[pltpu.VMEM((LANES, tq), F32), pltpu.VMEM((SEL_LANES, tq), F32)],
        compiler_params=_cparams(("parallel", "parallel", "parallel")),
        name="nsa_cmp_attn",
    )(proj, kc, vct, ovt, proj)


SEL_TK = 512
NSA_TQ = 512
WIN_TQ = 256
WIN_KEYS = WINDOW + WIN_TQ


HALF = NSA_TQ
NCH = NSA_HPG * NSA_TQ // HALF


def _sel_win_kernel(q_ref, ksel_ref, vselt_ref, kwin_ref, vwint_ref, sel_ref, et_ref, ocmpt_ref,
                    gate_ref, dbias_ref, wbias_ref, o_ref, s_ref, m_ref, acc_ref, gt_ref):
    g = pl.program_id(1)
    i = pl.program_id(2)
    tq = NSA_TQ
    t0 = i * tq
    q2 = _stack_heads(q_ref[...])
    selm1 = sel_ref[0, 0] - 1.0
    qx = jnp.concatenate([q2, jnp.concatenate([selm1] * NSA_HPG, axis=0)], axis=1)
    qxh = [qx[c * HALF:(c + 1) * HALF] for c in range(NCH)]
    q2h = [q2[c * HALF:(c + 1) * HALF] for c in range(NCH)]
    kpb = SEL_TK // LANES
    dv = NSA_HEAD_DIM

    wq = WIN_TQ
    nsub = tq // wq
    win_q, win_k, win_v, win_b = [], [], [], []
    for h in range(nsub):
        u0 = t0 + h * wq
        wblk = jnp.maximum((u0 - WINDOW) // LANES, 0)
        start = pl.multiple_of(wblk * LANES, LANES)
        win_k.append(kwin_ref[pl.ds(start, WIN_KEYS), :])
        win_v.append(jnp.concatenate([vwint_ref[wblk + u] for u in range(WIN_KEYS // LANES)], axis=1))
        win_b.append(wbias_ref[jnp.minimum(i * nsub + h, WINDOW // wq)].astype(F32))
        win_q.append(jnp.concatenate([q2[j * tq + h * wq:j * tq + (h + 1) * wq] for j in range(NSA_HPG)], axis=0))
    sws = [_dot_nt(win_k[h], win_q[h]) for h in range(nsub)]

    def key_tile(kv):
        off = pl.multiple_of(kv * SEL_TK, SEL_TK)
        return jnp.concatenate([ksel_ref[pl.ds(off, SEL_TK), :], et_ref[pl.ds(off, SEL_TK), :]], axis=1)

    def scores_into(kv):
        kx = key_tile(kv)
        for c in range(NCH):
            s_ref[:, c * HALF:(c + 1) * HALF] = _dot_nt(kx, qxh[c])

    def update_chunk(c, vt, s):
        cols = slice(c * HALF, (c + 1) * HALF)
        m_old = m_ref[:, cols]
        m_new = jnp.maximum(m_old, jnp.max(s, axis=0, keepdims=True))
        p = jnp.exp2(s - m_new).astype(BF16)
        acc_ref[:, cols] = jnp.exp2(m_old - m_new) * acc_ref[:, cols] + _dot(vt, p)
        m_ref[:, cols] = m_new

    def update(vt, s_chunks):
        for c in range(NCH):
            update_chunk(c, vt, s_chunks[c])

    def sel_values(kv):
        return jnp.concatenate([vselt_ref[kv * kpb + u] for u in range(kpb)], axis=1)

    def load_scores():
        return [s_ref[:, c * HALF:(c + 1) * HALF] for c in range(NCH)]

    m_ref[...] = jnp.full(m_ref.shape, MASK_VALUE, F32)
    acc_ref[...] = jnp.zeros(acc_ref.shape, F32)
    kd = t0 // SEL_TK
    scores_into(0)
    ot_win = []
    for h in range(nsub):
        sw = sws[h] + jnp.concatenate([win_b[h]] * NSA_HPG, axis=1)
        mw = jnp.max(sw, axis=0, keepdims=True)
        accw = _dot(win_v[h], jnp.exp2(sw - mw).astype(BF16))
        ot_win.append(accw[0:dv] * (1.0 / jnp.maximum(accw[dv:dv + 1], 1e-30)))

    def body(kv, carry):
        kx = key_tile(kv + 1)
        vt = sel_values(kv)
        pending = None
        for c in range(NCH):
            cols = slice(c * HALF, (c + 1) * HALF)
            s = s_ref[:, cols]
            s_ref[:, cols] = _dot_nt(kx, qxh[c])
            if pending is not None:
                update_chunk(c - 1, vt, pending)
            pending = s
        update_chunk(NCH - 1, vt, pending)
        return carry

    lax.fori_loop(0, kd, body, 0)
    dbias = dbias_ref[i % (SEL_TK // tq)].astype(F32)
    update(sel_values(kd), [sc + dbias for sc in load_scores()])
    acc = acc_ref[...]
    ot_slc = acc[0:dv] * (1.0 / jnp.maximum(acc[dv:dv + 1], 1e-30))

    gt_ref[...] = _sigmoid(gate_ref[...].astype(F32)).T
    ots = []
    for j in range(NSA_HPG):
        base = g * (NSA_HPG * 3) + j * 3
        ots.append(ocmpt_ref[0, 0, 0, :, j * tq:(j + 1) * tq].astype(F32)
                   + gt_ref[pl.ds(base + 1, 1), :] * ot_slc[:, j * tq:(j + 1) * tq]
                   + gt_ref[pl.ds(base + 2, 1), :]
                   * jnp.concatenate([ot_win[h][:, j * wq:(j + 1) * wq] for h in range(nsub)], axis=1))
    for jp in range(NSA_HPG // 2):
        pair = jnp.concatenate([ots[2 * jp], ots[2 * jp + 1]], axis=0)
        o_ref[:, jp * LANES:(jp + 1) * LANES] = pair.T.astype(o_ref.dtype)


def _diag_bias():
    r = jnp.arange(SEL_TK)[None, :, None]
    c = jnp.arange(NSA_TQ)[None, None, :]
    off = (jnp.arange(SEL_TK // NSA_TQ) * NSA_TQ)[:, None, None]
    return jnp.where(r <= off + c, 0.0, MASK_VALUE).astype(BF16)


def _window_bias():
    r = jnp.arange(WIN_KEYS)[None, :, None]
    c = jnp.arange(WIN_TQ)[None, None, :]
    off = jnp.minimum(jnp.arange(WINDOW // WIN_TQ + 1) * WIN_TQ, WINDOW)[:, None, None]
    diff = off + c - r
    return jnp.where((diff >= 0) & (diff < WINDOW), 0.0, MASK_VALUE).astype(BF16)


def _nsa_sel_win(proj, vsel_t, vwin_t, sel, et, ocmpt, B, S):
    T = B * S
    tq = NSA_TQ
    assert HALF == tq
    dbias = _diag_bias()
    wbias = _window_bias()
    nq = S // tq
    G = NSA_KV_GROUPS
    qw = NSA_HPG * HEAD_SLOT
    qblk = C_QEXT // qw
    vsel_t = vsel_t.reshape(G, B, S // LANES, VT_ROWS, LANES)
    vwin_t = vwin_t.reshape(G, B, S // LANES, VT_ROWS, LANES)
    ow = NSA_HPG * NSA_HEAD_DIM
    kv_spec = lambda c: pl.BlockSpec((S, LANES), lambda b, g, i: (b, c // LANES))
    vt_spec = pl.BlockSpec((None, None, S // LANES, VT_ROWS, LANES), lambda b, g, i: (g, b, 0, 0, 0))
    return pl.pallas_call(
        _sel_win_kernel,
        grid=(B, G, nq),
        in_specs=[pl.BlockSpec((tq, qw), lambda b, g, i: (b * nq + i, qblk + g)),
                  kv_spec(C_KSEL), vt_spec, kv_spec(C_KWIN), vt_spec,
                  pl.BlockSpec((1, 1, tq, SEL_LANES), lambda b, g, i: (b, g, i, 0)),
                  pl.BlockSpec(et.shape, lambda b, g, i: (0, 0)),
                  pl.BlockSpec((1, 1, 1, NSA_HEAD_DIM, NSA_HPG * tq), lambda b, g, i: (b, g, i, 0, 0)),
                  pl.BlockSpec((tq, LANES), lambda b, g, i: (b * nq + i, C_SMALL // LANES)),
                  pl.BlockSpec(dbias.shape, lambda b, g, i: (0, 0, 0)),
                  pl.BlockSpec(wbias.shape, lambda b, g, i: (0, 0, 0))],
        out_specs=pl.BlockSpec((tq, ow), lambda b, g, i: (b * nq + i, g)),
        out_shape=jax.ShapeDtypeStruct((T, NSA_Q_W), BF16),
        scratch_shapes=[pltpu.VMEM((SEL_TK, NSA_HPG * tq), F32),
                        pltpu.VMEM((1, NSA_HPG * tq), F32),
                        pltpu.VMEM((VT_ROWS, NSA_HPG * tq), F32),
                        pltpu.VMEM((LANES, tq), F32)],
        compiler_params=_cparams(("parallel", "parallel", "parallel")),
        name="nsa_sel_win",
    )(proj, proj, vsel_t, proj, vwin_t, sel, et, ocmpt, proj, dbias, wbias)


TAIL_ROWS = 8


def _causal_conv(x, tail_ref, w, b):
    L, C = x.shape
    nv = L // TAIL_ROWS
    xx = jnp.concatenate([tail_ref[...], x], axis=0).reshape(nv + 1, TAIL_ROWS, C)
    sub = lax.broadcasted_iota(jnp.int32, (nv, TAIL_ROWS, C), 1)
    y = b + w[CONV_WIDTH - 1:CONV_WIDTH] * x
    for k in range(1, CONV_WIDTH):
        rot = pltpu.roll(xx, k, axis=1)
        shifted = jnp.where(sub >= k, rot[1:], rot[:-1]).reshape(L, C)
        y = y + w[CONV_WIDTH - 1 - k:CONV_WIDTH - k] * shifted
    tail_ref[...] = x[L - TAIL_ROWS:L]
    return y


SSD_CPS = 2


def _ssd_kernel(z_ref, xbc_ref, dtc_ref, dtr_ref, cw_ref, cb_ref, dtbc_ref, dtbr_ref, alc_ref, alr_ref,
                dsk_ref, nw_ref, o_ref, state_ref, tail_ref, y_ref):
    c = pl.program_id(1)
    L = SSD_CHUNK
    P = SSD_HEAD_DIM
    N = SSD_STATE
    hpg = SSD_HEADS // SSD_GROUPS

    @pl.when(c == 0)
    def _():
        state_ref[...] = jnp.zeros_like(state_ref)
        tail_ref[...] = jnp.zeros_like(tail_ref)

    conv = _causal_conv(xbc_ref[...].astype(F32), tail_ref, cw_ref[...], cb_ref[...])
    xbc = conv * _sigmoid(conv)
    dt_c = _softplus(dtc_ref[...] + dtbc_ref[...])
    dt_r = _softplus(dtr_ref[...] + dtbr_ref[...])
    a_c = dt_c * (-jnp.exp(alc_ref[...]))
    a_r = dt_r * (-jnp.exp(alr_ref[...]))
    ii = lax.broadcasted_iota(jnp.int32, (L, L), 0)
    jj = lax.broadcasted_iota(jnp.int32, (L, L), 1)
    tri = ii >= jj
    tril = tri.astype(F32)
    triu = (ii <= jj).astype(F32)

    for cc in range(SSD_CPS):
        rs = slice(cc * L, (cc + 1) * L)
        xs = xbc[rs, 0:SSD_INNER]
        bm = xbc[rs, SSD_INNER:SSD_INNER + SSD_GROUPS * N]
        cm = xbc[rs, SSD_INNER + SSD_GROUPS * N:SSD_INNER + 2 * SSD_GROUPS * N]
        acum_c = _dot_f32(tril, a_c[rs])
        acum_r = _dot_f32(a_r[:, rs], triu)
        for g in range(SSD_GROUPS):
            bg = bm[:, g * N:(g + 1) * N]
            cgb = cm[:, g * N:(g + 1) * N].astype(BF16)
            cb = _dot_nt(cgb, bg.astype(BF16))
            for j in range(hpg):
                h = g * hpg + j
                acb = jnp.broadcast_to(acum_c[:, h:h + 1], (L, L))
                a_last = acum_c[L - 1:L, h:h + 1]
                xh_raw = xs[:, h * P:(h + 1) * P]
                xh = (xh_raw * jnp.broadcast_to(dt_c[rs, h:h + 1], (L, P))).astype(BF16)
                lmat = jnp.where(tri, jnp.exp(acb - acum_r[h:h + 1, :]), 0.0)
                y = _dot((cb * lmat).astype(BF16), xh)
                prev = state_ref[h]
                y = y + _dot(cgb, prev.astype(BF16)) * jnp.exp(acb[:, 0:P])
                bd = (bg * jnp.exp(a_last - acb[:, 0:N])).astype(BF16)
                state_ref[h] = jnp.exp(a_last) * prev + _dot_tn(bd, xh)
                y_ref[rs, h * P:(h + 1) * P] = y + dsk_ref[:, h * P:(h + 1) * P] * xh_raw

    zf = z_ref[...].astype(F32)
    yg = y_ref[...] * (zf * _sigmoid(zf))
    ms = jnp.mean(yg * yg, axis=-1, keepdims=True)
    o_ref[...] = (yg * lax.rsqrt(ms + RMS_EPS) * nw_ref[...]).astype(o_ref.dtype)


def _ssd_mixer(proj, dt_col, dt_row, conv_w, conv_b, dt_bias, a_log, d_skip, norm_w, B, S):
    T = B * S
    L = SSD_CPS * SSD_CHUNK
    nc = S // L
    H = SSD_HEADS
    full = lambda a: pl.BlockSpec(a.shape, lambda b, c: (0, 0))
    cb2 = conv_b.reshape(1, -1)
    dtb_c = dt_bias.reshape(1, H)
    dtb_r = dt_bias.reshape(H, 1)
    al_c = a_log.reshape(1, H)
    al_r = a_log.reshape(H, 1)
    dsk = jnp.repeat(d_skip, SSD_HEAD_DIM).reshape(1, SSD_INNER)
    nw = norm_w.reshape(1, SSD_INNER)
    return pl.pallas_call(
        _ssd_kernel,
        grid=(B, nc),
        in_specs=[pl.BlockSpec((L, SSD_INNER), lambda b, c: (b * nc + c, C_SSDZ // SSD_INNER)),
                  pl.BlockSpec((L, SSD_XBC_W), lambda b, c: (b * nc + c, C_XBC // SSD_XBC_W)),
                  pl.BlockSpec((L, H), lambda b, c: (b * nc + c, 0)),
                  pl.BlockSpec((H, L), lambda b, c: (0, b * nc + c)),
                  full(conv_w), full(cb2), full(dtb_c), full(dtb_r), full(al_c), full(al_r),
                  full(dsk), full(nw)],
        out_specs=pl.BlockSpec((L, SSD_INNER), lambda b, c: (b * nc + c, 0)),
        out_shape=jax.ShapeDtypeStruct((T, SSD_INNER), BF16),
        scratch_shapes=[pltpu.VMEM((H, SSD_STATE, SSD_HEAD_DIM), F32),
                        pltpu.VMEM((TAIL_ROWS, SSD_XBC_W), F32),
                        pltpu.VMEM((L, SSD_INNER), F32)],
        compiler_params=_cparams(("parallel", "arbitrary")),
        name="ssd_mixer",
    )(proj, proj, dt_col, dt_row, conv_w, cb2, dtb_c, dtb_r, al_c, al_r, dsk, nw)


SCAN_GROUP = 8


def _lru_kernel(x_ref, y_ref, cw_ref, cb_ref, wa_ref, ba_ref, wx_ref, bx_ref, lam_ref, o_ref,
                h_ref, tail_ref, *, tc):
    c = pl.program_id(1)

    @pl.when(c == 0)
    def _():
        h_ref[...] = jnp.zeros_like(h_ref)
        tail_ref[...] = jnp.zeros_like(tail_ref)

    xr = _causal_conv(x_ref[...].astype(F32), tail_ref, cw_ref[...], cb_ref[...])
    xrb = xr.astype(BF16)
    r = _sigmoid(_dot(xrb, wa_ref[...]) + ba_ref[...])
    ig = _sigmoid(_dot(xrb, wx_ref[...]) + bx_ref[...])
    log_a = -LRU_C * r * _softplus(-lam_ref[...])
    a = jnp.exp(log_a)
    b = jnp.sqrt(1.0 - jnp.exp(2.0 * log_a)) * (ig * xr)
    ng = tc // SCAN_GROUP
    a = a.reshape(ng, SCAN_GROUP, a.shape[1])
    b = b.reshape(ng, SCAN_GROUP, b.shape[1])
    sub = lax.broadcasted_iota(jnp.int32, a.shape, 1)
    k = 1
    while k < SCAN_GROUP:
        keep = sub >= k
        a_s = jnp.where(keep, pltpu.roll(a, k, axis=1), 1.0)
        b_s = jnp.where(keep, pltpu.roll(b, k, axis=1), 0.0)
        b = a * b_s + b
        a = a * a_s
        k *= 2
    carry = h_ref[0:1, :]
    groups = []
    for rg in range(ng):
        hg = a[rg] * carry + b[rg]
        groups.append(hg)
        carry = hg[SCAN_GROUP - 1:SCAN_GROUP]
    h = jnp.concatenate(groups, axis=0)
    h_ref[...] = jnp.broadcast_to(carry, h_ref.shape)
    o_ref[...] = (h * _gelu_tanh(y_ref[...].astype(F32))).astype(o_ref.dtype)


def _block_diag(w):
    nb, c, d = w.shape
    eye = jnp.eye(nb, dtype=w.dtype)
    return (eye[:, None, :, None] * w[:, :, None, :]).reshape(nb * c, nb * d)


def _lru_mixer(proj, conv_w, conv_b, wa, ba, wx, bx, lam, B, S):
    T = B * S
    tc = min(256, S)
    nt = S // tc
    W = LRU_WIDTH
    wa_bd = _block_diag(wa).astype(BF16)
    wx_bd = _block_diag(wx).astype(BF16)
    row = lambda v: v.reshape(1, W)
    full = lambda a: pl.BlockSpec(a.shape, lambda b, c: (0, 0))
    args = (conv_w, row(conv_b), wa_bd, row(ba), wx_bd, row(bx), row(lam))
    return pl.pallas_call(
        functools.partial(_lru_kernel, tc=tc),
        grid=(B, nt),
        in_specs=[pl.BlockSpec((tc, W), lambda b, c: (b * nt + c, C_LRUX // W)),
                  pl.BlockSpec((tc, W), lambda b, c: (b * nt + c, C_LRUY // W))]
                 + [full(a) for a in args],
        out_specs=pl.BlockSpec((tc, W), lambda b, c: (b * nt + c, 0)),
        out_shape=jax.ShapeDtypeStruct((T, W), BF16),
        scratch_shapes=[pltpu.VMEM((TAIL_ROWS, W), F32), pltpu.VMEM((TAIL_ROWS, W), F32)],
        compiler_params=_cparams(("parallel", "arbitrary")),
        name="lru_mixer",
    )(proj, proj, *args)


def _merge_kernel(x_ref, mg_ref, on_ref, os_ref, ol_ref, pn_ref, ps_ref, pl_ref, wo_ref, g_ref, b_ref,
                  of_ref, ob_ref):
    d = D_MODEL
    gate = _sigmoid(mg_ref[...].astype(F32))
    mixed = (gate[:, 0:d] * _dot(on_ref[...], pn_ref[...])
             + gate[:, d:2 * d] * _dot(os_ref[...], ps_ref[...])
             + gate[:, 2 * d:3 * d] * _dot(ol_ref[...], pl_ref[...]))
    v = ALPHA * x_ref[...] + _dot(mixed.astype(BF16), wo_ref[...])
    out = _layer_norm(v, g_ref[...], b_ref[...])
    of_ref[...] = out
    ob_ref[...] = out.astype(BF16)


def _merge(x, proj, o_nsa, o_ssd, o_lru, pn, ps, plru, wo, g, b, layer):
    T = x.shape[0]
    tm = min(512, T)
    d = D_MODEL
    rowblk = lambda w: pl.BlockSpec((tm, w), lambda i: (i, 0))
    full = lambda a: pl.BlockSpec((None,) + a.shape[1:], lambda i: (layer, 0, 0))
    g2, b2 = g.reshape(-1, 1, d), b.reshape(-1, 1, d)
    return pl.pallas_call(
        _merge_kernel,
        grid=(T // tm,),
        in_specs=[rowblk(d), pl.BlockSpec((tm, 3 * d), lambda i: (i, C_MERGE // (3 * d))),
                  rowblk(o_nsa.shape[1]), rowblk(o_ssd.shape[1]), rowblk(o_lru.shape[1]),
                  full(pn), full(ps), full(plru), full(wo), full(g2), full(b2)],
        out_specs=[rowblk(d), rowblk(d)],
        out_shape=[jax.ShapeDtypeStruct((T, d), F32), jax.ShapeDtypeStruct((T, d), BF16)],
        compiler_params=_cparams(("parallel",)),
        name="merge_ln",
    )(x, proj, o_nsa, o_ssd, o_lru, pn, ps, plru, wo, g2, b2)


def _route(sel, aff):
    epg = EXPERTS_PER_GROUP
    scores = []
    for gi in range(N_EXPERT_GROUPS):
        v = sel[gi * epg:(gi + 1) * epg]
        pair = None
        for a in range(epg):
            for b in range(a + 1, epg):
                sab = v[a] + v[b]
                pair = sab if pair is None else jnp.maximum(pair, sab)
        scores.append(pair)
    best = jnp.zeros_like(scores[0], dtype=jnp.int32)
    best_s = scores[0]
    for gi in range(1, N_EXPERT_GROUPS):
        better = scores[gi] > best_s
        best = jnp.where(better, gi, best)
        best_s = jnp.where(better, scores[gi], best_s)
    chosen = []
    for k in range(N_EXPERTS):
        gi = k // epg
        rank = jnp.zeros_like(best)
        for o in range(gi * epg, (gi + 1) * epg):
            if o == k:
                continue
            ahead = (sel[o] > sel[k]) | ((sel[o] == sel[k]) & (o < k))
            rank = rank + ahead.astype(jnp.int32)
        chosen.append((best == gi) & (rank < TOP_K))
    wsum = None
    for k in range(N_EXPERTS):
        wk = jnp.where(chosen[k], aff[k], 0.0)
        wsum = wk if wsum is None else wsum + wk
    inv = 1.0 / wsum
    return [jnp.where(chosen[k], aff[k], 0.0) * inv for k in range(N_EXPERTS)]


EXPERT_PAIRS = N_EXPERTS // 2
PAIRS_PER_ITER = 2


def _moe_kernel(xb_ref, xf_ref, p_ref, rw_ref, rb_ref, pg_ref, pp_ref, wg_ref, wu_ref, wd_ref,
                g_ref, b_ref, of_ref, ob_ref, acc_ref, gates_ref):
    xb = xb_ref[...]
    tm = xb.shape[0]
    logits = _dot_nt(rw_ref[...], xb)
    aff = _sigmoid(logits)
    sel = aff + rb_ref[...]
    gate_rows = _route([sel[k:k + 1, :] for k in range(N_EXPERTS)],
                       [aff[k:k + 1, :] for k in range(N_EXPERTS)])
    gt = jnp.concatenate(gate_rows + [jnp.zeros((LANES - N_EXPERTS, tm), F32)], axis=0)
    gates_ref[...] = gt.T
    acc_ref[...] = _sigmoid(_dot(xb, pg_ref[...])) * _dot(p_ref[...].astype(BF16), pp_ref[...])
    lane = lax.broadcasted_iota(jnp.int32, (tm, LANES), 1)

    def gate_cols(k):
        gates = gates_ref[...]
        cols = [jnp.broadcast_to(jnp.sum(jnp.where(lane == 2 * k + u, gates, 0.0), axis=-1, keepdims=True),
                                 (tm, D_EXPERT)) for u in range(2)]
        return jnp.concatenate(cols, axis=1)

    def step(it, carry):
        ks = [it * PAIRS_PER_ITER + u for u in range(PAIRS_PER_ITER)]
        hgs = [[_dot(xb, wg_ref[2 * k + u]) for u in range(2)] for k in ks]
        hus = [[_dot(xb, wu_ref[2 * k + u]) for u in range(2)] for k in ks]
        for k, hg2, hu2 in zip(ks, hgs, hus):
            hg = jnp.concatenate(hg2, axis=1)
            h = (hg * _sigmoid(hg)) * jnp.concatenate(hu2, axis=1) * gate_cols(k)
            acc_ref[...] += _dot(h.astype(BF16), wd_ref[k])
        return carry

    lax.fori_loop(0, EXPERT_PAIRS // PAIRS_PER_ITER, step, 0)
    out = _layer_norm(ALPHA * xf_ref[...] + acc_ref[...], g_ref[...], b_ref[...])
    of_ref[...] = out
    ob_ref[...] = out.astype(BF16)


def _moe_ple(xb, xf, p, layer, rw_t, rb, pg, pp, wg, wu, wd, g, b):
    T = xb.shape[0]
    tm = min(512, T)
    d = D_MODEL
    rowblk = lambda w: pl.BlockSpec((tm, w), lambda i: (i, 0))
    once = pl.Buffered(1)
    full = lambda a: pl.BlockSpec(a.shape, lambda i: (0,) * a.ndim, pipeline_mode=once)
    lyr = lambda a: pl.BlockSpec((None,) + a.shape[1:], lambda i: (layer,) + (0,) * (a.ndim - 1), pipeline_mode=once)
    g2, b2 = g.reshape(-1, 1, d), b.reshape(-1, 1, d)
    return pl.pallas_call(
        _moe_kernel,
        grid=(T // tm,),
        in_specs=[rowblk(d), rowblk(d), pl.BlockSpec((None, tm, PLE_DIM), lambda i: (layer, i, 0)),
                  full(rw_t), full(rb), lyr(pg), lyr(pp),
                  lyr(wg), lyr(wu), lyr(wd), lyr(g2), lyr(b2)],
        out_specs=[rowblk(d), rowblk(d)],
        out_shape=[jax.ShapeDtypeStruct((T, d), F32), jax.ShapeDtypeStruct((T, d), BF16)],
        scratch_shapes=[pltpu.VMEM((tm, d), F32), pltpu.VMEM((tm, LANES), F32)],
        compiler_params=_cparams(("parallel",)),
        name="moe_ple_ln",
    )(xb, xf, p, rw_t, rb, pg, pp, wg, wu, wd, g2, b2)


def _overlap_matrix(nc):
    n = jnp.arange(nc)[None, :]
    m = jnp.arange(SEL_LANES)[:, None]
    ratio = SEL_BLOCK // CMP_STRIDE
    ov = jnp.zeros((SEL_LANES, nc), F32)
    for k in range(CMP_BLOCK // CMP_STRIDE):
        ov = ov + ((n + k) // ratio == m).astype(F32)
    return ov.astype(BF16)


def _expand_matrix(S):
    c = jnp.arange(S)[:, None]
    m = jnp.arange(SEL_LANES)[None, :]
    return jnp.where(c // SEL_BLOCK == m, -MASK_VALUE, 0.0).astype(BF16)


def _pad_w2(w2):
    out = jnp.zeros((NSA_KV_GROUPS, CMP_HIDDEN, HEAD_SLOT), F32)
    for g in range(NSA_KV_GROUPS):
        out = out.at[g, :, g * NSA_HEAD_DIM:(g + 1) * NSA_HEAD_DIM].set(w2)
    return out.astype(BF16)


def kernel(x, p, w_in, nsa_pe_k, nsa_w1_k, nsa_w2_k, nsa_pe_v, nsa_w1_v, nsa_w2_v, ssd_conv_w, ssd_conv_b, ssd_dt_bias, ssd_a_log, ssd_d, ssd_norm_w, lru_conv_w, lru_conv_b, lru_wa, lru_ba, lru_wx, lru_bx, lru_lambda, proj_nsa, proj_ssd, proj_lru, w_out, ln1_g, ln1_b, router_w, router_b, exp_w_gate, exp_w_up, exp_w_down, ple_w_gate, ple_w_proj, ln2_g, ln2_b):
    B, S, d = x.shape
    T = B * S
    depth = w_in.shape[0]
    assert d == D_MODEL and S % SEL_TK == 0 and S >= WIN_KEYS and S // SEL_BLOCK <= SEL_LANES
    nr = S // CMP_STRIDE
    ov = _overlap_matrix(nr)
    emat = _expand_matrix(S)
    rw_t = router_w.T.astype(BF16)
    rb = router_b.reshape(N_EXPERTS, 1).astype(F32)

    w_main = _prep_w_in(w_in)
    aux_w = _aux_weights(w_in)
    pn, ps, plru, wo = (a.astype(BF16) for a in (proj_nsa, proj_ssd, proj_lru, w_out))
    pg, pp, wg, wu = (a.astype(BF16) for a in (ple_w_gate, ple_w_proj, exp_w_gate, exp_w_up))
    wd = exp_w_down.reshape(depth, EXPERT_PAIRS, 2 * D_EXPERT, d).astype(BF16)
    p3 = p.reshape(depth, T, PLE_DIM)

    xf = x.reshape(T, d)
    xb = xf
    for i in range(depth):
        proj = _in_proj(xb, w_main, i)
        cmp, vsel_t, vwin_t, dt_col, dt_row = _aux_proj(xb, aux_w, i)

        w1big, pebig = _cmp_first_layer(nsa_w1_k[i], nsa_w1_v[i], nsa_pe_k[i], nsa_pe_v[i])
        kc, vct = _nsa_compress(cmp, w1big, pebig, _pad_w2(nsa_w2_k[i]),
                                nsa_w2_v[i].T.astype(BF16), B, S)
        ocmpt, sel = _nsa_cmp_attn(proj, kc, vct, ov, B, S)
        o_nsa = _nsa_sel_win(proj, vsel_t, vwin_t, sel, emat, ocmpt, B, S)

        o_ssd = _ssd_mixer(proj, dt_col, dt_row, ssd_conv_w[i], ssd_conv_b[i], ssd_dt_bias[i],
                           ssd_a_log[i], ssd_d[i], ssd_norm_w[i], B, S)
        o_lru = _lru_mixer(proj, lru_conv_w[i], lru_conv_b[i], lru_wa[i], lru_ba[i], lru_wx[i],
                           lru_bx[i], lru_lambda[i], B, S)

        xf, xb = _merge(xf, proj, o_nsa, o_ssd, o_lru, pn, ps, plru, wo, ln1_g, ln1_b, i)
        xf, xb = _moe_ple(xb, xf, p3, i, rw_t, rb, pg, pp, wg, wu, wd, ln2_g, ln2_b)
    return xf.reshape(B, S, d)
```

```python
import functools
import math

import jax
import jax.numpy as jnp
from jax import lax
from jax.experimental import pallas as pl
from jax.experimental.pallas import tpu as pltpu

F32 = jnp.float32
BF16 = jnp.bfloat16

D_MODEL = 1024
PLE_DIM = 256
NSA_HEADS = 8
NSA_KV_GROUPS = 2
NSA_HEAD_DIM = 64
NSA_HPG = NSA_HEADS // NSA_KV_GROUPS
NSA_Q_W = NSA_HEADS * NSA_HEAD_DIM
NSA_KV_W = NSA_KV_GROUPS * NSA_HEAD_DIM
CMP_BLOCK = 32
CMP_STRIDE = 16
CMP_HIDDEN = 256
SEL_BLOCK = 64
SEL_TOPN = 16
WINDOW = 512
FORCE_SCORE = 1e4
MASK_VALUE = -1e30
LOG2E = 1.4426950408889634
SSD_HEADS = 8
SSD_HEAD_DIM = 64
SSD_INNER = SSD_HEADS * SSD_HEAD_DIM
SSD_GROUPS = 2
SSD_STATE = 64
SSD_CHUNK = 128
SSD_XBC_W = SSD_INNER + 2 * SSD_GROUPS * SSD_STATE
CONV_WIDTH = 4
LRU_WIDTH = 512
LRU_BLOCKS = 8
LRU_BLOCK_DIM = LRU_WIDTH // LRU_BLOCKS
LRU_C = 8.0
N_EXPERTS = 16
N_EXPERT_GROUPS = 4
EXPERTS_PER_GROUP = N_EXPERTS // N_EXPERT_GROUPS
TOP_K = 2
D_EXPERT = 256
DEPTH = 2
ALPHA = (2 * DEPTH) ** 0.25
LN_EPS = 1e-5
RMS_EPS = 1e-5
IN_SIZES = (NSA_Q_W, NSA_KV_W, NSA_KV_W, NSA_KV_W, NSA_KV_W, NSA_KV_W, NSA_KV_W, NSA_HEADS * 3,
            SSD_INNER, SSD_XBC_W, SSD_HEADS, LRU_WIDTH, LRU_WIDTH, 3 * D_MODEL)

LANES = 128
SEL_LANES = 128
HEAD_SLOT = 128

C_MERGE = 0
C_QEXT = 3072
C_SSDZ = 4096
C_LRUX = 4608
C_LRUY = 5120
C_KSEL = 5632
C_KWIN = 5760
C_SMALL = 5888
C_XBC = 6144
PROJ_W = 6912
PROJ_TN = 2304
GATE_W = NSA_HEADS * 3

VMEM_LIMIT = 56 * 1024 * 1024


def _cparams(sem):
    return pltpu.CompilerParams(dimension_semantics=sem, vmem_limit_bytes=VMEM_LIMIT)


def _sigmoid(x):
    return 1.0 / (1.0 + jnp.exp(-x))


def _softplus(x):
    return jnp.maximum(x, 0.0) + jnp.log(1.0 + jnp.exp(-jnp.abs(x)))


def _gelu_tanh(x):
    c = math.sqrt(2.0 / math.pi)
    return 0.5 * x * (1.0 + jnp.tanh(c * (x + 0.044715 * (x * x * x))))


def _dot(a, b):
    return jnp.dot(a, b, preferred_element_type=F32)


def _dot_nt(a, b):
    return lax.dot_general(a, b, (((1,), (1,)), ((), ())), preferred_element_type=F32)


def _dot_tn(a, b):
    return lax.dot_general(a, b, (((0,), (0,)), ((), ())), preferred_element_type=F32)


def _dot_f32(a, b):
    return jnp.dot(a, b, preferred_element_type=F32, precision=lax.Precision.HIGHEST)


def _layer_norm(v, g, b):
    mu = jnp.mean(v, axis=-1, keepdims=True)
    vc = v - mu
    var = jnp.mean(vc * vc, axis=-1, keepdims=True)
    return vc * lax.rsqrt(var + LN_EPS) * g + b


def _matmul_kernel(x_ref, w_ref, o_ref):
    o_ref[...] = _dot(x_ref[...].astype(BF16), w_ref[...]).astype(o_ref.dtype)


def _in_proj(xb, w, layer):
    T, K = xb.shape
    N = w.shape[2]
    tm = min(1024, T)
    tn = PROJ_TN
    return pl.pallas_call(
        _matmul_kernel,
        grid=(N // tn, T // tm),
        in_specs=[pl.BlockSpec((tm, K), lambda j, i: (i, 0)),
                  pl.BlockSpec((None, K, tn), lambda j, i: (layer, 0, j))],
        out_specs=pl.BlockSpec((tm, tn), lambda j, i: (i, j)),
        out_shape=jax.ShapeDtypeStruct((T, N), BF16),
        compiler_params=_cparams(("parallel", "parallel")),
        name="in_proj",
    )(xb, w)


def _split_w_in(w):
    offs = [0]
    for s in IN_SIZES:
        offs.append(offs[-1] + s)
    return [w[..., offs[k]:offs[k + 1]] for k in range(len(IN_SIZES))]


def _prep_w_in(w):
    pc = _split_w_in(w)
    lead = w.shape[:-1]
    q = pc[0].reshape(*lead, NSA_KV_GROUPS, NSA_HPG, NSA_HEAD_DIM) * (NSA_HEAD_DIM ** -0.5 * LOG2E)
    zeros = jnp.zeros_like(q)
    q_ext = jnp.stack([jnp.concatenate([q[..., 0, :, :], zeros[..., 0, :, :]], axis=-1),
                       jnp.concatenate([zeros[..., 1, :, :], q[..., 1, :, :]], axis=-1)], axis=-3)
    q_ext = q_ext.reshape(*lead, NSA_HEADS * HEAD_SLOT)
    small = jnp.pad(pc[7], [(0, 0)] * len(lead) + [(0, C_XBC - C_SMALL - GATE_W)])
    out = jnp.concatenate([pc[13], q_ext, pc[8], pc[11], pc[12], pc[3], pc[5], small, pc[9]], axis=-1)
    assert out.shape[-1] == PROJ_W
    return out.astype(BF16)


VT_ROWS = NSA_HEAD_DIM + 16


def _aux_proj_kernel(x_ref, wc_ref, wvt_ref, wdt_ref, wdtt_ref, cmp_ref, vst_ref, vwt_ref, dtc_ref, dtr_ref):
    x = x_ref[...].astype(BF16)
    tm = x.shape[0]
    cmp_ref[...] = _dot(x, wc_ref[...]).astype(cmp_ref.dtype)
    vt = _dot_nt(wvt_ref[...], x)
    dv = NSA_HEAD_DIM
    ones = jnp.ones((VT_ROWS - dv, LANES), vst_ref.dtype)
    for u in range(tm // LANES):
        for k, o_ref in enumerate((vst_ref, vwt_ref)):
            for g in range(NSA_KV_GROUPS):
                r0 = (k * NSA_KV_GROUPS + g) * dv
                o_ref[g, u, 0:dv, :] = vt[r0:r0 + dv, u * LANES:(u + 1) * LANES].astype(o_ref.dtype)
                o_ref[g, u, dv:VT_ROWS, :] = ones
    xf = x.astype(F32)
    dtc_ref[...] = _dot(xf, wdt_ref[...])
    dtr_ref[...] = _dot_nt(wdtt_ref[...], xf)


def _aux_weights(w):
    pc = _split_w_in(w)
    wc = jnp.concatenate([pc[1], pc[2]], axis=-1).astype(BF16)
    wvt = jnp.swapaxes(jnp.concatenate([pc[4], pc[6]], axis=-1), -1, -2).astype(BF16)
    return wc, wvt, pc[10], jnp.swapaxes(pc[10], -1, -2)


def _aux_proj(xb, aux_w, layer):
    T, K = xb.shape
    wc, wvt, wdt, wdtt = aux_w
    tm = min(1024, T)
    full = lambda a: pl.BlockSpec((None,) + a.shape[1:], lambda i: (layer, 0, 0))
    vt_shape = jax.ShapeDtypeStruct((NSA_KV_GROUPS, T // LANES, VT_ROWS, LANES), BF16)
    vt_spec = pl.BlockSpec((NSA_KV_GROUPS, tm // LANES, VT_ROWS, LANES), lambda i: (0, i, 0, 0))
    return pl.pallas_call(
        _aux_proj_kernel,
        grid=(T // tm,),
        in_specs=[pl.BlockSpec((tm, K), lambda i: (i, 0)), full(wc), full(wvt), full(wdt), full(wdtt)],
        out_specs=[pl.BlockSpec((tm, 2 * NSA_KV_W), lambda i: (i, 0)), vt_spec, vt_spec,
                   pl.BlockSpec((tm, SSD_HEADS), lambda i: (i, 0)),
                   pl.BlockSpec((SSD_HEADS, tm), lambda i: (0, i))],
        out_shape=[jax.ShapeDtypeStruct((T, 2 * NSA_KV_W), BF16), vt_shape, vt_shape,
                   jax.ShapeDtypeStruct((T, SSD_HEADS), F32), jax.ShapeDtypeStruct((SSD_HEADS, T), F32)],
        compiler_params=_cparams(("parallel",)),
        name="aux_proj",
    )(xb, wc, wvt, wdt, wdtt)


CMP_PIECES = 2 * NSA_KV_GROUPS


def _compress_kernel(r_ref, pe_ref, w1_ref, w2k_ref, w2vt_ref, kc_ref, vct_ref):
    r = r_ref[0]
    nr = r.shape[0]
    u = _dot(r, w1_ref[0])
    v = _dot(r, w1_ref[1])
    c = _dot(pe_ref[0], w1_ref[0]) + _dot(pe_ref[1], w1_ref[1])
    hid = u + pltpu.roll(v, nr - 1, axis=0)
    hid = (hid.reshape(nr // 8, 8, hid.shape[1]) + c[None]).reshape(nr, hid.shape[1])
    act = _gelu_tanh(hid).astype(BF16)
    for g in range(NSA_KV_GROUPS):
        kp, vp = g, NSA_KV_GROUPS + g
        kc_ref[0, g] = _dot(act[:, kp * CMP_HIDDEN:(kp + 1) * CMP_HIDDEN], w2k_ref[g]).astype(kc_ref.dtype)
        vct = _dot_nt(w2vt_ref[...], act[:, vp * CMP_HIDDEN:(vp + 1) * CMP_HIDDEN])
        vct_ref[0, g] = vct.astype(vct_ref.dtype)


def _cmp_first_layer(w1k, w1v, pek, pev):
    half = CMP_BLOCK // 2
    blocks, pes = [], []
    for p, (w1, pe1) in enumerate(((w1k, pek), (w1k, pek), (w1v, pev), (w1v, pev))):
        wp = w1.reshape(2, half, NSA_HEAD_DIM, CMP_HIDDEN).astype(BF16)
        blocks.append(jnp.pad(wp, ((0, 0), (0, 0), (0, 0), (p * CMP_HIDDEN, (CMP_PIECES - 1 - p) * CMP_HIDDEN))))
        pes.append(pe1.reshape(2, half, NSA_HEAD_DIM).astype(BF16))
    rows = half * CMP_PIECES * NSA_HEAD_DIM
    big = jnp.concatenate(blocks, axis=2).reshape(2, rows, CMP_PIECES * CMP_HIDDEN)
    pe = jnp.concatenate(pes, axis=2).reshape(2, 1, rows)
    return big, jnp.broadcast_to(pe, (2, 8, rows))


def _nsa_compress(cmp, w1big, pebig, w2k, w2vt, B, S):
    NR = S // CMP_STRIDE
    W = CMP_STRIDE * 2 * NSA_KV_W
    r = cmp.reshape(B, NR, W)
    once = pl.Buffered(1)
    full = lambda a: pl.BlockSpec(a.shape, lambda b: (0,) * a.ndim, pipeline_mode=once)
    return pl.pallas_call(
        _compress_kernel,
        grid=(B,),
        in_specs=[pl.BlockSpec((1, NR, W), lambda b: (b, 0, 0)), full(pebig), full(w1big), full(w2k), full(w2vt)],
        out_specs=[pl.BlockSpec((1, NSA_KV_GROUPS, NR, HEAD_SLOT), lambda b: (b, 0, 0, 0)),
                   pl.BlockSpec((1, NSA_KV_GROUPS, NSA_HEAD_DIM, NR), lambda b: (b, 0, 0, 0))],
        out_shape=[jax.ShapeDtypeStruct((B, NSA_KV_GROUPS, NR, HEAD_SLOT), BF16),
                   jax.ShapeDtypeStruct((B, NSA_KV_GROUPS, NSA_HEAD_DIM, NR), BF16)],
        compiler_params=_cparams(("parallel",)),
        name="nsa_compress",
    )(r, pebig, w1big, w2k, w2vt)


def _stack_heads(q):
    return jnp.concatenate([q[:, j * HEAD_SLOT:(j + 1) * HEAD_SLOT] for j in range(NSA_HPG)], axis=0)


CMP_VARIANTS = 4


def _cmp_attn_kernel(q_ref, kc_ref, vct_ref, ovt_ref, gate_ref, ocmpt_ref, sel_ref, gt_ref, imp_ref, *, tq):
    g = pl.program_id(1)
    i = pl.program_id(2)
    q2 = _stack_heads(q_ref[...])
    nc_all = kc_ref.shape[2]
    gt_ref[...] = _sigmoid(gate_ref[...].astype(F32)).T

    def attend(nc):
        kc = kc_ref[0, 0, 0:nc, :]
        vct = vct_ref[0, 0, :, 0:nc]
        sts = [_dot_nt(kc, q2[j * tq:(j + 1) * tq]) for j in range(NSA_HPG)]
        n = lax.broadcasted_iota(jnp.int32, (nc, tq), 0)
        t = i * tq + lax.broadcasted_iota(jnp.int32, (nc, tq), 1)
        mask = n * CMP_STRIDE + (CMP_BLOCK - 1) <= t
        ps = None
        for j in range(NSA_HPG):
            s = jnp.where(mask, sts[j], MASK_VALUE)
            m = jnp.max(s, axis=0, keepdims=True)
            e = jnp.where(mask, jnp.exp2(s - m), 0.0)
            den = jnp.maximum(jnp.sum(e, axis=0, keepdims=True), 1e-30)
            p = e * (1.0 / den)
            ps = p if ps is None else ps + p
            gate = gt_ref[pl.ds(g * (NSA_HPG * 3) + j * 3, 1), :]
            ocmpt_ref[0, 0, 0, :, j * tq:(j + 1) * tq] = (gate * _dot(vct, p.astype(BF16))).astype(ocmpt_ref.dtype)
        ovt = ovt_ref[:, 0:nc]
        hi = ps.astype(BF16)
        r1 = ps - hi.astype(F32)
        mid = r1.astype(BF16)
        lo = (r1 - mid.astype(F32)).astype(BF16)
        imp_ref[...] = _dot(ovt, hi) + _dot(ovt, mid) + _dot(ovt, lo)

    step = nc_all // CMP_VARIANTS
    variant = jnp.minimum(((i + 1) * (tq // CMP_STRIDE) - 1) // step, CMP_VARIANTS - 1)
    for v in range(CMP_VARIANTS):
        pl.when(variant == v)(functools.partial(attend, (v + 1) * step))

    imp = imp_ref[...]
    blk = lax.broadcasted_iota(jnp.int32, imp.shape, 0)
    tt = i * tq + lax.broadcasted_iota(jnp.int32, imp.shape, 1)
    cur = tt // SEL_BLOCK
    forced = (blk == 0) | (blk == cur) | (blk == cur - 1)
    causal = blk * SEL_BLOCK <= tt
    v0 = jnp.where(forced, FORCE_SCORE, jnp.where(causal, imp, -1.0))

    blk1 = blk[:, 0:LANES]

    def pick(_, tiles):
        out = []
        for v in tiles:
            mx = jnp.max(v, axis=0, keepdims=True)
            idx = jnp.min(jnp.where(v == mx, blk1, SEL_LANES), axis=0, keepdims=True)
            out.append(jnp.where(blk1 == idx, -jnp.inf, v))
        return tuple(out)

    tiles = lax.fori_loop(0, SEL_TOPN, pick, tuple(v0[:, u * LANES:(u + 1) * LANES] for u in range(tq // LANES)))
    for u, v in enumerate(tiles):
        sel_ref[0, 0, u * LANES:(u + 1) * LANES, :] = jnp.where(v == -jnp.inf, 1.0, 0.0).T.astype(sel_ref.dtype)


def _nsa_cmp_attn(proj, kc, vct, ovt, B, S):
    tq = NSA_TQ
    nq = S // tq
    G = NSA_KV_GROUPS
    NC = kc.shape[2]
    qw = NSA_HPG * HEAD_SLOT
    qblk = C_QEXT // qw
    kern = functools.partial(_cmp_attn_kernel, tq=tq)
    return pl.pallas_call(
        kern,
        grid=(B, G, nq),
        in_specs=[pl.BlockSpec((tq, qw), lambda b, g, i: (b * nq + i, qblk + g)),
                  pl.BlockSpec((1, 1, NC, HEAD_SLOT), lambda b, g, i: (b, g, 0, 0)),
                  pl.BlockSpec((1, 1, NSA_HEAD_DIM, NC), lambda b, g, i: (b, g, 0, 0)),
                  pl.BlockSpec(ovt.shape, lambda b, g, i: (0, 0)),
                  pl.BlockSpec((tq, LANES), lambda b, g, i: (b * nq + i, C_SMALL // LANES))],
        out_specs=[pl.BlockSpec((1, 1, 1, NSA_HEAD_DIM, NSA_HPG * tq), lambda b, g, i: (b, g, i, 0, 0)),
                   pl.BlockSpec((1, 1, tq, SEL_LANES), lambda b, g, i: (b, g, i, 0))],
        out_shape=[jax.ShapeDtypeStruct((B, G, nq, NSA_HEAD_DIM, NSA_HPG * tq), BF16),
                   jax.ShapeDtypeStruct((B, G, S, SEL_LANES), BF16)],
        scratch_shapes=[pltpu.VMEM((LANES, tq), F32), pltpu.VMEM((SEL_LANES, tq), F32)],
        compiler_params=_cparams(("parallel", "parallel", "parallel")),
        name="nsa_cmp_attn",
    )(proj, kc, vct, ovt, proj)


SEL_TK = 512
NSA_TQ = 512
WIN_TQ = 256
WIN_KEYS = WINDOW + WIN_TQ


HALF = NSA_TQ
NCH = NSA_HPG * NSA_TQ // HALF


def _sel_win_kernel(q_ref, ksel_ref, vselt_ref, kwin_ref, vwint_ref, sel_ref, et_ref, ocmpt_ref,
                    gate_ref, dbias_ref, wbias_ref, o_ref, s_ref, m_ref, acc_ref, gt_ref):
    g = pl.program_id(1)
    i = pl.program_id(2)
    tq = NSA_TQ
    t0 = i * tq
    q2 = _stack_heads(q_ref[...])
    selm1 = sel_ref[0, 0] - 1.0
    qx = jnp.concatenate([q2, jnp.concatenate([selm1] * NSA_HPG, axis=0)], axis=1)
    qxh = [qx[c * HALF:(c + 1) * HALF] for c in range(NCH)]
    q2h = [q2[c * HALF:(c + 1) * HALF] for c in range(NCH)]
    kpb = SEL_TK // LANES
    dv = NSA_HEAD_DIM

    wq = WIN_TQ
    nsub = tq // wq
    win_q, win_k, win_v, win_b = [], [], [], []
    for h in range(nsub):
        u0 = t0 + h * wq
        wblk = jnp.maximum((u0 - WINDOW) // LANES, 0)
        start = pl.multiple_of(wblk * LANES, LANES)
        win_k.append(kwin_ref[pl.ds(start, WIN_KEYS), :])
        win_v.append(jnp.concatenate([vwint_ref[wblk + u] for u in range(WIN_KEYS // LANES)], axis=1))
        win_b.append(wbias_ref[jnp.minimum(i * nsub + h, WINDOW // wq)].astype(F32))
        win_q.append(jnp.concatenate([q2[j * tq + h * wq:j * tq + (h + 1) * wq] for j in range(NSA_HPG)], axis=0))
    sws = [_dot_nt(win_k[h], win_q[h]) for h in range(nsub)]

    def key_tile(kv):
        off = pl.multiple_of(kv * SEL_TK, SEL_TK)
        return jnp.concatenate([ksel_ref[pl.ds(off, SEL_TK), :], et_ref[pl.ds(off, SEL_TK), :]], axis=1)

    def scores_into(kv):
        kx = key_tile(kv)
        for c in range(NCH):
            s_ref[:, c * HALF:(c + 1) * HALF] = _dot_nt(kx, qxh[c])

    def update_chunk(c, vt, s):
        cols = slice(c * HALF, (c + 1) * HALF)
        m_old = m_ref[:, cols]
        m_new = jnp.maximum(m_old, jnp.max(s, axis=0, keepdims=True))
        p = jnp.exp2(s - m_new).astype(BF16)
        acc_ref[:, cols] = jnp.exp2(m_old - m_new) * acc_ref[:, cols] + _dot(vt, p)
        m_ref[:, cols] = m_new

    def update(vt, s_chunks):
        for c in range(NCH):
            update_chunk(c, vt, s_chunks[c])

    def sel_values(kv):
        return jnp.concatenate([vselt_ref[kv * kpb + u] for u in range(kpb)], axis=1)

    def load_scores():
        return [s_ref[:, c * HALF:(c + 1) * HALF] for c in range(NCH)]

    m_ref[...] = jnp.full(m_ref.shape, MASK_VALUE, F32)
    acc_ref[...] = jnp.zeros(acc_ref.shape, F32)
    kd = t0 // SEL_TK
    scores_into(0)
    ot_win = []
    for h in range(nsub):
        sw = sws[h] + jnp.concatenate([win_b[h]] * NSA_HPG, axis=1)
        mw = jnp.max(sw, axis=0, keepdims=True)
        accw = _dot(win_v[h], jnp.exp2(sw - mw).astype(BF16))
        ot_win.append(accw[0:dv] * (1.0 / jnp.maximum(accw[dv:dv + 1], 1e-30)))

    def body(kv, carry):
        kx = key_tile(kv + 1)
        vt = sel_values(kv)
        pending = None
        for c in range(NCH):
            cols = slice(c * HALF, (c + 1) * HALF)
            s = s_ref[:, cols]
            s_ref[:, cols] = _dot_nt(kx, qxh[c])
            if pending is not None:
                update_chunk(c - 1, vt, pending)
            pending = s
        update_chunk(NCH - 1, vt, pending)
        return carry

    lax.fori_loop(0, kd, body, 0)
    dbias = dbias_ref[i % (SEL_TK // tq)].astype(F32)
    update(sel_values(kd), [sc + dbias for sc in load_scores()])
    acc = acc_ref[...]
    ot_slc = acc[0:dv] * (1.0 / jnp.maximum(acc[dv:dv + 1], 1e-30))

    gt_ref[...] = _sigmoid(gate_ref[...].astype(F32)).T
    ots = []
    for j in range(NSA_HPG):
        base = g * (NSA_HPG * 3) + j * 3
        ots.append(ocmpt_ref[0, 0, 0, :, j * tq:(j + 1) * tq].astype(F32)
                   + gt_ref[pl.ds(base + 1, 1), :] * ot_slc[:, j * tq:(j + 1) * tq]
                   + gt_ref[pl.ds(base + 2, 1), :]
                   * jnp.concatenate([ot_win[h][:, j * wq:(j + 1) * wq] for h in range(nsub)], axis=1))
    for jp in range(NSA_HPG // 2):
        pair = jnp.concatenate([ots[2 * jp], ots[2 * jp + 1]], axis=0)
        o_ref[:, jp * LANES:(jp + 1) * LANES] = pair.T.astype(o_ref.dtype)


def _diag_bias():
    r = jnp.arange(SEL_TK)[None, :, None]
    c = jnp.arange(NSA_TQ)[None, None, :]
    off = (jnp.arange(SEL_TK // NSA_TQ) * NSA_TQ)[:, None, None]
    return jnp.where(r <= off + c, 0.0, MASK_VALUE).astype(BF16)


def _window_bias():
    r = jnp.arange(WIN_KEYS)[None, :, None]
    c = jnp.arange(WIN_TQ)[None, None, :]
    off = jnp.minimum(jnp.arange(WINDOW // WIN_TQ + 1) * WIN_TQ, WINDOW)[:, None, None]
    diff = off + c - r
    return jnp.where((diff >= 0) & (diff < WINDOW), 0.0, MASK_VALUE).astype(BF16)


def _nsa_sel_win(proj, vsel_t, vwin_t, sel, et, ocmpt, B, S):
    T = B * S
    tq = NSA_TQ
    assert HALF == tq
    dbias = _diag_bias()
    wbias = _window_bias()
    nq = S // tq
    G = NSA_KV_GROUPS
    qw = NSA_HPG * HEAD_SLOT
    qblk = C_QEXT // qw
    vsel_t = vsel_t.reshape(G, B, S // LANES, VT_ROWS, LANES)
    vwin_t = vwin_t.reshape(G, B, S // LANES, VT_ROWS, LANES)
    ow = NSA_HPG * NSA_HEAD_DIM
    kv_spec = lambda c: pl.BlockSpec((S, LANES), lambda b, g, i: (b, c // LANES))
    vt_spec = pl.BlockSpec((None, None, S // LANES, VT_ROWS, LANES), lambda b, g, i: (g, b, 0, 0, 0))
    return pl.pallas_call(
        _sel_win_kernel,
        grid=(B, G, nq),
        in_specs=[pl.BlockSpec((tq, qw), lambda b, g, i: (b * nq + i, qblk + g)),
                  kv_spec(C_KSEL), vt_spec, kv_spec(C_KWIN), vt_spec,
                  pl.BlockSpec((1, 1, tq, SEL_LANES), lambda b, g, i: (b, g, i, 0)),
                  pl.BlockSpec(et.shape, lambda b, g, i: (0, 0)),
                  pl.BlockSpec((1, 1, 1, NSA_HEAD_DIM, NSA_HPG * tq), lambda b, g, i: (b, g, i, 0, 0)),
                  pl.BlockSpec((tq, LANES), lambda b, g, i: (b * nq + i, C_SMALL // LANES)),
                  pl.BlockSpec(dbias.shape, lambda b, g, i: (0, 0, 0)),
                  pl.BlockSpec(wbias.shape, lambda b, g, i: (0, 0, 0))],
        out_specs=pl.BlockSpec((tq, ow), lambda b, g, i: (b * nq + i, g)),
        out_shape=jax.ShapeDtypeStruct((T, NSA_Q_W), BF16),
        scratch_shapes=[pltpu.VMEM((SEL_TK, NSA_HPG * tq), F32),
                        pltpu.VMEM((1, NSA_HPG * tq), F32),
                        pltpu.VMEM((VT_ROWS, NSA_HPG * tq), F32),
                        pltpu.VMEM((LANES, tq), F32)],
        compiler_params=_cparams(("parallel", "parallel", "parallel")),
        name="nsa_sel_win",
    )(proj, proj, vsel_t, proj, vwin_t, sel, et, ocmpt, proj, dbias, wbias)


TAIL_ROWS = 8


def _causal_conv(x, tail_ref, w, b):
    L, C = x.shape
    nv = L // TAIL_ROWS
    xx = jnp.concatenate([tail_ref[...], x], axis=0).reshape(nv + 1, TAIL_ROWS, C)
    sub = lax.broadcasted_iota(jnp.int32, (nv, TAIL_ROWS, C), 1)
    y = b + w[CONV_WIDTH - 1:CONV_WIDTH] * x
    for k in range(1, CONV_WIDTH):
        rot = pltpu.roll(xx, k, axis=1)
        shifted = jnp.where(sub >= k, rot[1:], rot[:-1]).reshape(L, C)
        y = y + w[CONV_WIDTH - 1 - k:CONV_WIDTH - k] * shifted
    tail_ref[...] = x[L - TAIL_ROWS:L]
    return y


SSD_CPS = 2


def _ssd_kernel(z_ref, xbc_ref, dtc_ref, dtr_ref, cw_ref, cb_ref, dtbc_ref, dtbr_ref, alc_ref, alr_ref,
                dsk_ref, nw_ref, o_ref, state_ref, tail_ref, y_ref):
    c = pl.program_id(1)
    L = SSD_CHUNK
    P = SSD_HEAD_DIM
    N = SSD_STATE
    hpg = SSD_HEADS // SSD_GROUPS

    @pl.when(c == 0)
    def _():
        state_ref[...] = jnp.zeros_like(state_ref)
        tail_ref[...] = jnp.zeros_like(tail_ref)

    conv = _causal_conv(xbc_ref[...].astype(F32), tail_ref, cw_ref[...], cb_ref[...])
    xbc = conv * _sigmoid(conv)
    dt_c = _softplus(dtc_ref[...] + dtbc_ref[...])
    dt_r = _softplus(dtr_ref[...] + dtbr_ref[...])
    a_c = dt_c * (-jnp.exp(alc_ref[...]))
    a_r = dt_r * (-jnp.exp(alr_ref[...]))
    ii = lax.broadcasted_iota(jnp.int32, (L, L), 0)
    jj = lax.broadcasted_iota(jnp.int32, (L, L), 1)
    tri = ii >= jj
    tril = tri.astype(F32)
    triu = (ii <= jj).astype(F32)

    for cc in range(SSD_CPS):
        rs = slice(cc * L, (cc + 1) * L)
        xs = xbc[rs, 0:SSD_INNER]
        bm = xbc[rs, SSD_INNER:SSD_INNER + SSD_GROUPS * N]
        cm = xbc[rs, SSD_INNER + SSD_GROUPS * N:SSD_INNER + 2 * SSD_GROUPS * N]
        acum_c = _dot_f32(tril, a_c[rs])
        acum_r = _dot_f32(a_r[:, rs], triu)
        for g in range(SSD_GROUPS):
            bg = bm[:, g * N:(g + 1) * N]
            cgb = cm[:, g * N:(g + 1) * N].astype(BF16)
            cb = _dot_nt(cgb, bg.astype(BF16))
            for j in range(hpg):
                h = g * hpg + j
                acb = jnp.broadcast_to(acum_c[:, h:h + 1], (L, L))
                a_last = acum_c[L - 1:L, h:h + 1]
                xh_raw = xs[:, h * P:(h + 1) * P]
                xh = (xh_raw * jnp.broadcast_to(dt_c[rs, h:h + 1], (L, P))).astype(BF16)
                lmat = jnp.where(tri, jnp.exp(acb - acum_r[h:h + 1, :]), 0.0)
                y = _dot((cb * lmat).astype(BF16), xh)
                prev = state_ref[h]
                y = y + _dot(cgb, prev.astype(BF16)) * jnp.exp(acb[:, 0:P])
                bd = (bg * jnp.exp(a_last - acb[:, 0:N])).astype(BF16)
                state_ref[h] = jnp.exp(a_last) * prev + _dot_tn(bd, xh)
                y_ref[rs, h * P:(h + 1) * P] = y + dsk_ref[:, h * P:(h + 1) * P] * xh_raw

    zf = z_ref[...].astype(F32)
    yg = y_ref[...] * (zf * _sigmoid(zf))
    ms = jnp.mean(yg * yg, axis=-1, keepdims=True)
    o_ref[...] = (yg * lax.rsqrt(ms + RMS_EPS) * nw_ref[...]).astype(o_ref.dtype)


def _ssd_mixer(proj, dt_col, dt_row, conv_w, conv_b, dt_bias, a_log, d_skip, norm_w, B, S):
    T = B * S
    L = SSD_CPS * SSD_CHUNK
    nc = S // L
    H = SSD_HEADS
    full = lambda a: pl.BlockSpec(a.shape, lambda b, c: (0, 0))
    cb2 = conv_b.reshape(1, -1)
    dtb_c = dt_bias.reshape(1, H)
    dtb_r = dt_bias.reshape(H, 1)
    al_c = a_log.reshape(1, H)
    al_r = a_log.reshape(H, 1)
    dsk = jnp.repeat(d_skip, SSD_HEAD_DIM).reshape(1, SSD_INNER)
    nw = norm_w.reshape(1, SSD_INNER)
    return pl.pallas_call(
        _ssd_kernel,
        grid=(B, nc),
        in_specs=[pl.BlockSpec((L, SSD_INNER), lambda b, c: (b * nc + c, C_SSDZ // SSD_INNER)),
                  pl.BlockSpec((L, SSD_XBC_W), lambda b, c: (b * nc + c, C_XBC // SSD_XBC_W)),
                  pl.BlockSpec((L, H), lambda b, c: (b * nc + c, 0)),
                  pl.BlockSpec((H, L), lambda b, c: (0, b * nc + c)),
                  full(conv_w), full(cb2), full(dtb_c), full(dtb_r), full(al_c), full(al_r),
                  full(dsk), full(nw)],
        out_specs=pl.BlockSpec((L, SSD_INNER), lambda b, c: (b * nc + c, 0)),
        out_shape=jax.ShapeDtypeStruct((T, SSD_INNER), BF16),
        scratch_shapes=[pltpu.VMEM((H, SSD_STATE, SSD_HEAD_DIM), F32),
                        pltpu.VMEM((TAIL_ROWS, SSD_XBC_W), F32),
                        pltpu.VMEM((L, SSD_INNER), F32)],
        compiler_params=_cparams(("parallel", "arbitrary")),
        name="ssd_mixer",
    )(proj, proj, dt_col, dt_row, conv_w, cb2, dtb_c, dtb_r, al_c, al_r, dsk, nw)


SCAN_GROUP = 8


def _lru_kernel(x_ref, y_ref, cw_ref, cb_ref, wa_ref, ba_ref, wx_ref, bx_ref, lam_ref, o_ref,
                h_ref, tail_ref, *, tc):
    c = pl.program_id(1)

    @pl.when(c == 0)
    def _():
        h_ref[...] = jnp.zeros_like(h_ref)
        tail_ref[...] = jnp.zeros_like(tail_ref)

    xr = _causal_conv(x_ref[...].astype(F32), tail_ref, cw_ref[...], cb_ref[...])
    xrb = xr.astype(BF16)
    r = _sigmoid(_dot(xrb, wa_ref[...]) + ba_ref[...])
    ig = _sigmoid(_dot(xrb, wx_ref[...]) + bx_ref[...])
    log_a = -LRU_C * r * _softplus(-lam_ref[...])
    a = jnp.exp(log_a)
    b = jnp.sqrt(1.0 - jnp.exp(2.0 * log_a)) * (ig * xr)
    ng = tc // SCAN_GROUP
    a = a.reshape(ng, SCAN_GROUP, a.shape[1])
    b = b.reshape(ng, SCAN_GROUP, b.shape[1])
    sub = lax.broadcasted_iota(jnp.int32, a.shape, 1)
    k = 1
    while k < SCAN_GROUP:
        keep = sub >= k
        a_s = jnp.where(keep, pltpu.roll(a, k, axis=1), 1.0)
        b_s = jnp.where(keep, pltpu.roll(b, k, axis=1), 0.0)
        b = a * b_s + b
        a = a * a_s
        k *= 2
    carry = h_ref[0:1, :]
    groups = []
    for rg in range(ng):
        hg = a[rg] * carry + b[rg]
        groups.append(hg)
        carry = hg[SCAN_GROUP - 1:SCAN_GROUP]
    h = jnp.concatenate(groups, axis=0)
    h_ref[...] = jnp.broadcast_to(carry, h_ref.shape)
    o_ref[...] = (h * _gelu_tanh(y_ref[...].astype(F32))).astype(o_ref.dtype)


def _block_diag(w):
    nb, c, d = w.shape
    eye = jnp.eye(nb, dtype=w.dtype)
    return (eye[:, None, :, None] * w[:, :, None, :]).reshape(nb * c, nb * d)


def _lru_mixer(proj, conv_w, conv_b, wa, ba, wx, bx, lam, B, S):
    T = B * S
    tc = min(512, S)
    nt = S // tc
    W = LRU_WIDTH
    wa_bd = _block_diag(wa).astype(BF16)
    wx_bd = _block_diag(wx).astype(BF16)
    row = lambda v: v.reshape(1, W)
    full = lambda a: pl.BlockSpec(a.shape, lambda b, c: (0, 0))
    args = (conv_w, row(conv_b), wa_bd, row(ba), wx_bd, row(bx), row(lam))
    return pl.pallas_call(
        functools.partial(_lru_kernel, tc=tc),
        grid=(B, nt),
        in_specs=[pl.BlockSpec((tc, W), lambda b, c: (b * nt + c, C_LRUX // W)),
                  pl.BlockSpec((tc, W), lambda b, c: (b * nt + c, C_LRUY // W))]
                 + [full(a) for a in args],
        out_specs=pl.BlockSpec((tc, W), lambda b, c: (b * nt + c, 0)),
        out_shape=jax.ShapeDtypeStruct((T, W), BF16),
        scratch_shapes=[pltpu.VMEM((TAIL_ROWS, W), F32), pltpu.VMEM((TAIL_ROWS, W), F32)],
        compiler_params=_cparams(("parallel", "arbitrary")),
        name="lru_mixer",
    )(proj, proj, *args)


def _merge_kernel(x_ref, mg_ref, on_ref, os_ref, ol_ref, pn_ref, ps_ref, pl_ref, wo_ref, g_ref, b_ref,
                  of_ref, ob_ref):
    d = D_MODEL
    gate = _sigmoid(mg_ref[...].astype(F32))
    mixed = (gate[:, 0:d] * _dot(on_ref[...], pn_ref[...])
             + gate[:, d:2 * d] * _dot(os_ref[...], ps_ref[...])
             + gate[:, 2 * d:3 * d] * _dot(ol_ref[...], pl_ref[...]))
    v = ALPHA * x_ref[...] + _dot(mixed.astype(BF16), wo_ref[...])
    out = _layer_norm(v, g_ref[...], b_ref[...])
    of_ref[...] = out
    ob_ref[...] = out.astype(BF16)


def _merge(x, proj, o_nsa, o_ssd, o_lru, pn, ps, plru, wo, g, b, layer):
    T = x.shape[0]
    tm = min(512, T)
    d = D_MODEL
    rowblk = lambda w: pl.BlockSpec((tm, w), lambda i: (i, 0))
    full = lambda a: pl.BlockSpec((None,) + a.shape[1:], lambda i: (layer, 0, 0))
    g2, b2 = g.reshape(-1, 1, d), b.reshape(-1, 1, d)
    return pl.pallas_call(
        _merge_kernel,
        grid=(T // tm,),
        in_specs=[rowblk(d), pl.BlockSpec((tm, 3 * d), lambda i: (i, C_MERGE // (3 * d))),
                  rowblk(o_nsa.shape[1]), rowblk(o_ssd.shape[1]), rowblk(o_lru.shape[1]),
                  full(pn), full(ps), full(plru), full(wo), full(g2), full(b2)],
        out_specs=[rowblk(d), rowblk(d)],
        out_shape=[jax.ShapeDtypeStruct((T, d), F32), jax.ShapeDtypeStruct((T, d), BF16)],
        compiler_params=_cparams(("parallel",)),
        name="merge_ln",
    )(x, proj, o_nsa, o_ssd, o_lru, pn, ps, plru, wo, g2, b2)


def _route(sel, aff):
    epg = EXPERTS_PER_GROUP
    scores = []
    for gi in range(N_EXPERT_GROUPS):
        v = sel[gi * epg:(gi + 1) * epg]
        pair = None
        for a in range(epg):
            for b in range(a + 1, epg):
                sab = v[a] + v[b]
                pair = sab if pair is None else jnp.maximum(pair, sab)
        scores.append(pair)
    best = jnp.zeros_like(scores[0], dtype=jnp.int32)
    best_s = scores[0]
    for gi in range(1, N_EXPERT_GROUPS):
        better = scores[gi] > best_s
        best = jnp.where(better, gi, best)
        best_s = jnp.where(better, scores[gi], best_s)
    chosen = []
    for k in range(N_EXPERTS):
        gi = k // epg
        rank = jnp.zeros_like(best)
        for o in range(gi * epg, (gi + 1) * epg):
            if o == k:
                continue
            ahead = (sel[o] > sel[k]) | ((sel[o] == sel[k]) & (o < k))
            rank = rank + ahead.astype(jnp.int32)
        chosen.append((best == gi) & (rank < TOP_K))
    wsum = None
    for k in range(N_EXPERTS):
        wk = jnp.where(chosen[k], aff[k], 0.0)
        wsum = wk if wsum is None else wsum + wk
    inv = 1.0 / wsum
    return [jnp.where(chosen[k], aff[k], 0.0) * inv for k in range(N_EXPERTS)]


EXPERT_PAIRS = N_EXPERTS // 2
PAIRS_PER_ITER = 4


def _moe_kernel(xb_ref, xf_ref, p_ref, rw_ref, rb_ref, pg_ref, pp_ref, wg_ref, wu_ref, wd_ref,
                g_ref, b_ref, of_ref, ob_ref, acc_ref, gates_ref):
    xb = xb_ref[...]
    tm = xb.shape[0]
    logits = _dot_nt(rw_ref[...], xb)
    aff = _sigmoid(logits)
    sel = aff + rb_ref[...]
    gate_rows = _route([sel[k:k + 1, :] for k in range(N_EXPERTS)],
                       [aff[k:k + 1, :] for k in range(N_EXPERTS)])
    gt = jnp.concatenate(gate_rows + [jnp.zeros((LANES - N_EXPERTS, tm), F32)], axis=0)
    gates_ref[...] = gt.T
    acc_ref[...] = _sigmoid(_dot(xb, pg_ref[...])) * _dot(p_ref[...].astype(BF16), pp_ref[...])
    lane = lax.broadcasted_iota(jnp.int32, (tm, LANES), 1)

    def gate_cols(k):
        gates = gates_ref[...]
        cols = [jnp.broadcast_to(jnp.sum(jnp.where(lane == 2 * k + u, gates, 0.0), axis=-1, keepdims=True),
                                 (tm, D_EXPERT)) for u in range(2)]
        return jnp.concatenate(cols, axis=1)

    def step(it, carry):
        ks = [it * PAIRS_PER_ITER + u for u in range(PAIRS_PER_ITER)]
        hgs = [[_dot(xb, wg_ref[2 * k + u]) for u in range(2)] for k in ks]
        hus = [[_dot(xb, wu_ref[2 * k + u]) for u in range(2)] for k in ks]
        for k, hg2, hu2 in zip(ks, hgs, hus):
            hg = jnp.concatenate(hg2, axis=1)
            h = (hg * _sigmoid(hg)) * jnp.concatenate(hu2, axis=1) * gate_cols(k)
            acc_ref[...] += _dot(h.astype(BF16), wd_ref[k])
        return carry

    lax.fori_loop(0, EXPERT_PAIRS // PAIRS_PER_ITER, step, 0)
    out = _layer_norm(ALPHA * xf_ref[...] + acc_ref[...], g_ref[...], b_ref[...])
    of_ref[...] = out
    ob_ref[...] = out.astype(BF16)


def _moe_ple(xb, xf, p, layer, rw_t, rb, pg, pp, wg, wu, wd, g, b):
    T = xb.shape[0]
    tm = min(512, T)
    d = D_MODEL
    rowblk = lambda w: pl.BlockSpec((tm, w), lambda i: (i, 0))
    once = pl.Buffered(1)
    full = lambda a: pl.BlockSpec(a.shape, lambda i: (0,) * a.ndim, pipeline_mode=once)
    lyr = lambda a: pl.BlockSpec((None,) + a.shape[1:], lambda i: (layer,) + (0,) * (a.ndim - 1), pipeline_mode=once)
    g2, b2 = g.reshape(-1, 1, d), b.reshape(-1, 1, d)
    return pl.pallas_call(
        _moe_kernel,
        grid=(T // tm,),
        in_specs=[rowblk(d), rowblk(d), pl.BlockSpec((None, tm, PLE_DIM), lambda i: (layer, i, 0)),
                  full(rw_t), full(rb), lyr(pg), lyr(pp),
                  lyr(wg), lyr(wu), lyr(wd), lyr(g2), lyr(b2)],
        out_specs=[rowblk(d), rowblk(d)],
        out_shape=[jax.ShapeDtypeStruct((T, d), F32), jax.ShapeDtypeStruct((T, d), BF16)],
        scratch_shapes=[pltpu.VMEM((tm, d), F32), pltpu.VMEM((tm, LANES), F32)],
        compiler_params=_cparams(("parallel",)),
        name="moe_ple_ln",
    )(xb, xf, p, rw_t, rb, pg, pp, wg, wu, wd, g2, b2)


def _overlap_matrix(nc):
    n = jnp.arange(nc)[None, :]
    m = jnp.arange(SEL_LANES)[:, None]
    ratio = SEL_BLOCK // CMP_STRIDE
    ov = jnp.zeros((SEL_LANES, nc), F32)
    for k in range(CMP_BLOCK // CMP_STRIDE):
        ov = ov + ((n + k) // ratio == m).astype(F32)
    return ov.astype(BF16)


def _expand_matrix(S):
    c = jnp.arange(S)[:, None]
    m = jnp.arange(SEL_LANES)[None, :]
    return jnp.where(c // SEL_BLOCK == m, -MASK_VALUE, 0.0).astype(BF16)


def _pad_w2(w2):
    out = jnp.zeros((NSA_KV_GROUPS, CMP_HIDDEN, HEAD_SLOT), F32)
    for g in range(NSA_KV_GROUPS):
        out = out.at[g, :, g * NSA_HEAD_DIM:(g + 1) * NSA_HEAD_DIM].set(w2)
    return out.astype(BF16)


def kernel(x, p, w_in, nsa_pe_k, nsa_w1_k, nsa_w2_k, nsa_pe_v, nsa_w1_v, nsa_w2_v, ssd_conv_w, ssd_conv_b, ssd_dt_bias, ssd_a_log, ssd_d, ssd_norm_w, lru_conv_w, lru_conv_b, lru_wa, lru_ba, lru_wx, lru_bx, lru_lambda, proj_nsa, proj_ssd, proj_lru, w_out, ln1_g, ln1_b, router_w, router_b, exp_w_gate, exp_w_up, exp_w_down, ple_w_gate, ple_w_proj, ln2_g, ln2_b):
    B, S, d = x.shape
    T = B * S
    depth = w_in.shape[0]
    assert d == D_MODEL and S % SEL_TK == 0 and S >= WIN_KEYS and S // SEL_BLOCK <= SEL_LANES
    nr = S // CMP_STRIDE
    ov = _overlap_matrix(nr)
    emat = _expand_matrix(S)
    rw_t = router_w.T.astype(BF16)
    rb = router_b.reshape(N_EXPERTS, 1).astype(F32)

    w_main = _prep_w_in(w_in)
    aux_w = _aux_weights(w_in)
    pn, ps, plru, wo = (a.astype(BF16) for a in (proj_nsa, proj_ssd, proj_lru, w_out))
    pg, pp, wg, wu = (a.astype(BF16) for a in (ple_w_gate, ple_w_proj, exp_w_gate, exp_w_up))
    wd = exp_w_down.reshape(depth, EXPERT_PAIRS, 2 * D_EXPERT, d).astype(BF16)
    p3 = p.reshape(depth, T, PLE_DIM)

    xf = x.reshape(T, d)
    xb = xf
    for i in range(depth):
        proj = _in_proj(xb, w_main, i)
        cmp, vsel_t, vwin_t, dt_col, dt_row = _aux_proj(xb, aux_w, i)

        w1big, pebig = _cmp_first_layer(nsa_w1_k[i], nsa_w1_v[i], nsa_pe_k[i], nsa_pe_v[i])
        kc, vct = _nsa_compress(cmp, w1big, pebig, _pad_w2(nsa_w2_k[i]),
                                nsa_w2_v[i].T.astype(BF16), B, S)
        ocmpt, sel = _nsa_cmp_attn(proj, kc, vct, ov, B, S)
        o_nsa = _nsa_sel_win(proj, vsel_t, vwin_t, sel, emat, ocmpt, B, S)

        o_ssd = _ssd_mixer(proj, dt_col, dt_row, ssd_conv_w[i], ssd_conv_b[i], ssd_dt_bias[i],
                           ssd_a_log[i], ssd_d[i], ssd_norm_w[i], B, S)
        o_lru = _lru_mixer(proj, lru_conv_w[i], lru_conv_b[i], lru_wa[i], lru_ba[i], lru_wx[i],
                           lru_bx[i], lru_lambda[i], B, S)

        xf, xb = _merge(xf, proj, o_nsa, o_ssd, o_lru, pn, ps, plru, wo, ln1_g, ln1_b, i)
        xf, xb = _moe_ple(xb, xf, p3, i, rw_t, rb, pg, pp, wg, wu, wd, ln2_g, ln2_b)
    return xf.reshape(B, S, d)
```

```python
import functools
import math

import jax
import jax.numpy as jnp
from jax import lax
from jax.experimental import pallas as pl
from jax.experimental.pallas import tpu as pltpu

F32 = jnp.float32
BF16 = jnp.bfloat16

D_MODEL = 1024
PLE_DIM = 256
NSA_HEADS = 8
NSA_KV_GROUPS = 2
NSA_HEAD_DIM = 64
NSA_HPG = NSA_HEADS // NSA_KV_GROUPS
NSA_Q_W = NSA_HEADS * NSA_HEAD_DIM
NSA_KV_W = NSA_KV_GROUPS * NSA_HEAD_DIM
CMP_BLOCK = 32
CMP_STRIDE = 16
CMP_HIDDEN = 256
SEL_BLOCK = 64
SEL_TOPN = 16
WINDOW = 512
FORCE_SCORE = 1e4
MASK_VALUE = -1e30
LOG2E = 1.4426950408889634
SSD_HEADS = 8
SSD_HEAD_DIM = 64
SSD_INNER = SSD_HEADS * SSD_HEAD_DIM
SSD_GROUPS = 2
SSD_STATE = 64
SSD_CHUNK = 128
SSD_XBC_W = SSD_INNER + 2 * SSD_GROUPS * SSD_STATE
CONV_WIDTH = 4
LRU_WIDTH = 512
LRU_BLOCKS = 8
LRU_BLOCK_DIM = LRU_WIDTH // LRU_BLOCKS
LRU_C = 8.0
N_EXPERTS = 16
N_EXPERT_GROUPS = 4
EXPERTS_PER_GROUP = N_EXPERTS // N_EXPERT_GROUPS
TOP_K = 2
D_EXPERT = 256
DEPTH = 2
ALPHA = (2 * DEPTH) ** 0.25
LN_EPS = 1e-5
RMS_EPS = 1e-5
IN_SIZES = (NSA_Q_W, NSA_KV_W, NSA_KV_W, NSA_KV_W, NSA_KV_W, NSA_KV_W, NSA_KV_W, NSA_HEADS * 3,
            SSD_INNER, SSD_XBC_W, SSD_HEADS, LRU_WIDTH, LRU_WIDTH, 3 * D_MODEL)

LANES = 128
SEL_LANES = 128
HEAD_SLOT = 128

C_MERGE = 0
C_QEXT = 3072
C_SSDZ = 4096
C_LRUX = 4608
C_LRUY = 5120
C_KSEL = 5632
C_KWIN = 5760
C_SMALL = 5888
C_XBC = 6144
PROJ_W = 6912
PROJ_TN = 2304
GATE_W = NSA_HEADS * 3

VMEM_LIMIT = 56 * 1024 * 1024


def _cparams(sem):
    return pltpu.CompilerParams(dimension_semantics=sem, vmem_limit_bytes=VMEM_LIMIT)


def _sigmoid(x):
    return 1.0 / (1.0 + jnp.exp(-x))


def _softplus(x):
    return jnp.maximum(x, 0.0) + jnp.log(1.0 + jnp.exp(-jnp.abs(x)))


def _gelu_tanh(x):
    c = math.sqrt(2.0 / math.pi)
    return 0.5 * x * (1.0 + jnp.tanh(c * (x + 0.044715 * (x * x * x))))


def _dot(a, b):
    return jnp.dot(a, b, preferred_element_type=F32)


def _dot_nt(a, b):
    return lax.dot_general(a, b, (((1,), (1,)), ((), ())), preferred_element_type=F32)


def _dot_tn(a, b):
    return lax.dot_general(a, b, (((0,), (0,)), ((), ())), preferred_element_type=F32)


def _dot_f32(a, b):
    return jnp.dot(a, b, preferred_element_type=F32, precision=lax.Precision.HIGHEST)


def _layer_norm(v, g, b):
    mu = jnp.mean(v, axis=-1, keepdims=True)
    vc = v - mu
    var = jnp.mean(vc * vc, axis=-1, keepdims=True)
    return vc * lax.rsqrt(var + LN_EPS) * g + b


def _matmul_kernel(x_ref, w_ref, o_ref):
    o_ref[...] = _dot(x_ref[...].astype(BF16), w_ref[...]).astype(o_ref.dtype)


def _in_proj(xb, w, layer):
    T, K = xb.shape
    N = w.shape[2]
    tm = min(1024, T)
    tn = PROJ_TN
    return pl.pallas_call(
        _matmul_kernel,
        grid=(N // tn, T // tm),
        in_specs=[pl.BlockSpec((tm, K), lambda j, i: (i, 0)),
                  pl.BlockSpec((None, K, tn), lambda j, i: (layer, 0, j))],
        out_specs=pl.BlockSpec((tm, tn), lambda j, i: (i, j)),
        out_shape=jax.ShapeDtypeStruct((T, N), BF16),
        compiler_params=_cparams(("parallel", "parallel")),
        name="in_proj",
    )(xb, w)


def _split_w_in(w):
    offs = [0]
    for s in IN_SIZES:
        offs.append(offs[-1] + s)
    return [w[..., offs[k]:offs[k + 1]] for k in range(len(IN_SIZES))]


def _prep_w_in(w):
    pc = _split_w_in(w)
    lead = w.shape[:-1]
    q = pc[0].reshape(*lead, NSA_KV_GROUPS, NSA_HPG, NSA_HEAD_DIM) * (NSA_HEAD_DIM ** -0.5 * LOG2E)
    zeros = jnp.zeros_like(q)
    q_ext = jnp.stack([jnp.concatenate([q[..., 0, :, :], zeros[..., 0, :, :]], axis=-1),
                       jnp.concatenate([zeros[..., 1, :, :], q[..., 1, :, :]], axis=-1)], axis=-3)
    q_ext = q_ext.reshape(*lead, NSA_HEADS * HEAD_SLOT)
    small = jnp.pad(pc[7], [(0, 0)] * len(lead) + [(0, C_XBC - C_SMALL - GATE_W)])
    out = jnp.concatenate([pc[13], q_ext, pc[8], pc[11], pc[12], pc[3], pc[5], small, pc[9]], axis=-1)
    assert out.shape[-1] == PROJ_W
    return out.astype(BF16)


VT_ROWS = NSA_HEAD_DIM + 16


def _aux_proj_kernel(x_ref, wc_ref, wvt_ref, wdt_ref, wdtt_ref, cmp_ref, vst_ref, vwt_ref, dtc_ref, dtr_ref):
    x = x_ref[...].astype(BF16)
    tm = x.shape[0]
    cmp_ref[...] = _dot(x, wc_ref[...]).astype(cmp_ref.dtype)
    vt = _dot_nt(wvt_ref[...], x)
    dv = NSA_HEAD_DIM
    ones = jnp.ones((VT_ROWS - dv, LANES), vst_ref.dtype)
    for u in range(tm // LANES):
        for k, o_ref in enumerate((vst_ref, vwt_ref)):
            for g in range(NSA_KV_GROUPS):
                r0 = (k * NSA_KV_GROUPS + g) * dv
                o_ref[g, u, 0:dv, :] = vt[r0:r0 + dv, u * LANES:(u + 1) * LANES].astype(o_ref.dtype)
                o_ref[g, u, dv:VT_ROWS, :] = ones
    xf = x.astype(F32)
    dtc_ref[...] = _dot(xf, wdt_ref[...])
    dtr_ref[...] = _dot_nt(wdtt_ref[...], xf)


def _aux_weights(w):
    pc = _split_w_in(w)
    wc = jnp.concatenate([pc[1], pc[2]], axis=-1).astype(BF16)
    wvt = jnp.swapaxes(jnp.concatenate([pc[4], pc[6]], axis=-1), -1, -2).astype(BF16)
    return wc, wvt, pc[10], jnp.swapaxes(pc[10], -1, -2)


def _aux_proj(xb, aux_w, layer):
    T, K = xb.shape
    wc, wvt, wdt, wdtt = aux_w
    tm = min(1024, T)
    full = lambda a: pl.BlockSpec((None,) + a.shape[1:], lambda i: (layer, 0, 0))
    vt_shape = jax.ShapeDtypeStruct((NSA_KV_GROUPS, T // LANES, VT_ROWS, LANES), BF16)
    vt_spec = pl.BlockSpec((NSA_KV_GROUPS, tm // LANES, VT_ROWS, LANES), lambda i: (0, i, 0, 0))
    return pl.pallas_call(
        _aux_proj_kernel,
        grid=(T // tm,),
        in_specs=[pl.BlockSpec((tm, K), lambda i: (i, 0)), full(wc), full(wvt), full(wdt), full(wdtt)],
        out_specs=[pl.BlockSpec((tm, 2 * NSA_KV_W), lambda i: (i, 0)), vt_spec, vt_spec,
                   pl.BlockSpec((tm, SSD_HEADS), lambda i: (i, 0)),
                   pl.BlockSpec((SSD_HEADS, tm), lambda i: (0, i))],
        out_shape=[jax.ShapeDtypeStruct((T, 2 * NSA_KV_W), BF16), vt_shape, vt_shape,
                   jax.ShapeDtypeStruct((T, SSD_HEADS), F32), jax.ShapeDtypeStruct((SSD_HEADS, T), F32)],
        compiler_params=_cparams(("parallel",)),
        name="aux_proj",
    )(xb, wc, wvt, wdt, wdtt)


CMP_PIECES = 2 * NSA_KV_GROUPS


def _compress_kernel(r_ref, pe_ref, w1_ref, w2k_ref, w2vt_ref, kc_ref, vct_ref):
    r = r_ref[0]
    nr = r.shape[0]
    u = _dot(r, w1_ref[0])
    v = _dot(r, w1_ref[1])
    c = _dot(pe_ref[0], w1_ref[0]) + _dot(pe_ref[1], w1_ref[1])
    hid = u + pltpu.roll(v, nr - 1, axis=0)
    hid = (hid.reshape(nr // 8, 8, hid.shape[1]) + c[None]).reshape(nr, hid.shape[1])
    act = _gelu_tanh(hid).astype(BF16)
    for g in range(NSA_KV_GROUPS):
        kp, vp = g, NSA_KV_GROUPS + g
        kc_ref[0, g] = _dot(act[:, kp * CMP_HIDDEN:(kp + 1) * CMP_HIDDEN], w2k_ref[g]).astype(kc_ref.dtype)
        vct = _dot_nt(w2vt_ref[...], act[:, vp * CMP_HIDDEN:(vp + 1) * CMP_HIDDEN])
        vct_ref[0, g] = vct.astype(vct_ref.dtype)


def _cmp_first_layer(w1k, w1v, pek, pev):
    half = CMP_BLOCK // 2
    blocks, pes = [], []
    for p, (w1, pe1) in enumerate(((w1k, pek), (w1k, pek), (w1v, pev), (w1v, pev))):
        wp = w1.reshape(2, half, NSA_HEAD_DIM, CMP_HIDDEN).astype(BF16)
        blocks.append(jnp.pad(wp, ((0, 0), (0, 0), (0, 0), (p * CMP_HIDDEN, (CMP_PIECES - 1 - p) * CMP_HIDDEN))))
        pes.append(pe1.reshape(2, half, NSA_HEAD_DIM).astype(BF16))
    rows = half * CMP_PIECES * NSA_HEAD_DIM
    big = jnp.concatenate(blocks, axis=2).reshape(2, rows, CMP_PIECES * CMP_HIDDEN)
    pe = jnp.concatenate(pes, axis=2).reshape(2, 1, rows)
    return big, jnp.broadcast_to(pe, (2, 8, rows))


def _nsa_compress(cmp, w1big, pebig, w2k, w2vt, B, S):
    NR = S // CMP_STRIDE
    W = CMP_STRIDE * 2 * NSA_KV_W
    r = cmp.reshape(B, NR, W)
    once = pl.Buffered(1)
    full = lambda a: pl.BlockSpec(a.shape, lambda b: (0,) * a.ndim, pipeline_mode=once)
    return pl.pallas_call(
        _compress_kernel,
        grid=(B,),
        in_specs=[pl.BlockSpec((1, NR, W), lambda b: (b, 0, 0)), full(pebig), full(w1big), full(w2k), full(w2vt)],
        out_specs=[pl.BlockSpec((1, NSA_KV_GROUPS, NR, HEAD_SLOT), lambda b: (b, 0, 0, 0)),
                   pl.BlockSpec((1, NSA_KV_GROUPS, NSA_HEAD_DIM, NR), lambda b: (b, 0, 0, 0))],
        out_shape=[jax.ShapeDtypeStruct((B, NSA_KV_GROUPS, NR, HEAD_SLOT), BF16),
                   jax.ShapeDtypeStruct((B, NSA_KV_GROUPS, NSA_HEAD_DIM, NR), BF16)],
        compiler_params=_cparams(("parallel",)),
        name="nsa_compress",
    )(r, pebig, w1big, w2k, w2vt)


def _stack_heads(q):
    return jnp.concatenate([q[:, j * HEAD_SLOT:(j + 1) * HEAD_SLOT] for j in range(NSA_HPG)], axis=0)


CMP_VARIANTS = 4


def _cmp_attn_kernel(q_ref, kc_ref, vct_ref, ovt_ref, gate_ref, ocmpt_ref, sel_ref, gt_ref, imp_ref, *, tq):
    g = pl.program_id(1)
    i = pl.program_id(2)
    q2 = _stack_heads(q_ref[...])
    nc_all = kc_ref.shape[2]
    gt_ref[...] = _sigmoid(gate_ref[...].astype(F32)).T

    def attend(nc):
        kc = kc_ref[0, 0, 0:nc, :]
        vct = vct_ref[0, 0, :, 0:nc]
        sts = [_dot_nt(kc, q2[j * tq:(j + 1) * tq]) for j in range(NSA_HPG)]
        n = lax.broadcasted_iota(jnp.int32, (nc, tq), 0)
        t = i * tq + lax.broadcasted_iota(jnp.int32, (nc, tq), 1)
        mask = n * CMP_STRIDE + (CMP_BLOCK - 1) <= t
        ps = None
        for j in range(NSA_HPG):
            s = jnp.where(mask, sts[j], MASK_VALUE)
            m = jnp.max(s, axis=0, keepdims=True)
            e = jnp.where(mask, jnp.exp2(s - m), 0.0)
            den = jnp.maximum(jnp.sum(e, axis=0, keepdims=True), 1e-30)
            p = e * (1.0 / den)
            ps = p if ps is None else ps + p
            gate = gt_ref[pl.ds(g * (NSA_HPG * 3) + j * 3, 1), :]
            ocmpt_ref[0, 0, 0, :, j * tq:(j + 1) * tq] = (gate * _dot(vct, p.astype(BF16))).astype(ocmpt_ref.dtype)
        ovt = ovt_ref[:, 0:nc]
        hi = ps.astype(BF16)
        r1 = ps - hi.astype(F32)
        mid = r1.astype(BF16)
        lo = (r1 - mid.astype(F32)).astype(BF16)
        imp_ref[...] = _dot(ovt, hi) + _dot(ovt, mid) + _dot(ovt, lo)

    step = nc_all // CMP_VARIANTS
    variant = jnp.minimum(((i + 1) * (tq // CMP_STRIDE) - 1) // step, CMP_VARIANTS - 1)
    for v in range(CMP_VARIANTS):
        pl.when(variant == v)(functools.partial(attend, (v + 1) * step))

    imp = imp_ref[...]
    blk = lax.broadcasted_iota(jnp.int32, imp.shape, 0)
    tt = i * tq + lax.broadcasted_iota(jnp.int32, imp.shape, 1)
    cur = tt // SEL_BLOCK
    forced = (blk == 0) | (blk == cur) | (blk == cur - 1)
    causal = blk * SEL_BLOCK <= tt
    v0 = jnp.where(forced, FORCE_SCORE, jnp.where(causal, imp, -1.0))

    blk1 = blk[:, 0:LANES]

    def pick(_, tiles):
        out = []
        for v in tiles:
            mx = jnp.max(v, axis=0, keepdims=True)
            idx = jnp.min(jnp.where(v == mx, blk1, SEL_LANES), axis=0, keepdims=True)
            out.append(jnp.where(blk1 == idx, -jnp.inf, v))
        return tuple(out)

    tiles = lax.fori_loop(0, SEL_TOPN, pick, tuple(v0[:, u * LANES:(u + 1) * LANES] for u in range(tq // LANES)))
    for u, v in enumerate(tiles):
        sel_ref[0, 0, u * LANES:(u + 1) * LANES, :] = jnp.where(v == -jnp.inf, 1.0, 0.0).T.astype(sel_ref.dtype)


def _nsa_cmp_attn(proj, kc, vct, ovt, B, S):
    tq = NSA_TQ
    nq = S // tq
    G = NSA_KV_GROUPS
    NC = kc.shape[2]
    qw = NSA_HPG * HEAD_SLOT
    qblk = C_QEXT // qw
    kern = functools.partial(_cmp_attn_kernel, tq=tq)
    return pl.pallas_call(
        kern,
        grid=(B, G, nq),
        in_specs=[pl.BlockSpec((tq, qw), lambda b, g, i: (b * nq + i, qblk + g)),
                  pl.BlockSpec((1, 1, NC, HEAD_SLOT), lambda b, g, i: (b, g, 0, 0)),
                  pl.BlockSpec((1, 1, NSA_HEAD_DIM, NC), lambda b, g, i: (b, g, 0, 0)),
                  pl.BlockSpec(ovt.shape, lambda b, g, i: (0, 0)),
                  pl.BlockSpec((tq, LANES), lambda b, g, i: (b * nq + i, C_SMALL // LANES))],
        out_specs=[pl.BlockSpec((1, 1, 1, NSA_HEAD_DIM, NSA_HPG * tq), lambda b, g, i: (b, g, i, 0, 0)),
                   pl.BlockSpec((1, 1, tq, SEL_LANES), lambda b, g, i: (b, g, i, 0))],
        out_shape=[jax.ShapeDtypeStruct((B, G, nq, NSA_HEAD_DIM, NSA_HPG * tq), BF16),
                   jax.ShapeDtypeStruct((B, G, S, SEL_LANES), BF16)],
        scratch_shapes=[pltpu.VMEM((LANES, tq), F32), pltpu.VMEM((SEL_LANES, tq), F32)],
        compiler_params=_cparams(("parallel", "parallel", "parallel")),
        name="nsa_cmp_attn",
    )(proj, kc, vct, ovt, proj)


SEL_TK = 512
NSA_TQ = 512
WIN_TQ = 256
WIN_KEYS = WINDOW + WIN_TQ


HALF = NSA_TQ
NCH = NSA_HPG * NSA_TQ // HALF


def _sel_win_kernel(q_ref, ksel_ref, vselt_ref, kwin_ref, vwint_ref, sel_ref, et_ref, ocmpt_ref,
                    gate_ref, dbias_ref, wbias_ref, o_ref, s_ref, m_ref, acc_ref, gt_ref):
    g = pl.program_id(1)
    i = pl.program_id(2)
    tq = NSA_TQ
    t0 = i * tq
    q2 = _stack_heads(q_ref[...])
    selm1 = sel_ref[0, 0] - 1.0
    qx = jnp.concatenate([q2, jnp.concatenate([selm1] * NSA_HPG, axis=0)], axis=1)
    qxh = [qx[c * HALF:(c + 1) * HALF] for c in range(NCH)]
    q2h = [q2[c * HALF:(c + 1) * HALF] for c in range(NCH)]
    kpb = SEL_TK // LANES
    dv = NSA_HEAD_DIM

    wq = WIN_TQ
    nsub = tq // wq
    win_q, win_k, win_v, win_b = [], [], [], []
    for h in range(nsub):
        u0 = t0 + h * wq
        wblk = jnp.maximum((u0 - WINDOW) // LANES, 0)
        start = pl.multiple_of(wblk * LANES, LANES)
        win_k.append(kwin_ref[pl.ds(start, WIN_KEYS), :])
        win_v.append(jnp.concatenate([vwint_ref[wblk + u] for u in range(WIN_KEYS // LANES)], axis=1))
        win_b.append(wbias_ref[jnp.minimum(i * nsub + h, WINDOW // wq)].astype(F32))
        win_q.append(jnp.concatenate([q2[j * tq + h * wq:j * tq + (h + 1) * wq] for j in range(NSA_HPG)], axis=0))
    sws = [_dot_nt(win_k[h], win_q[h]) for h in range(nsub)]

    def key_tile(kv):
        off = pl.multiple_of(kv * SEL_TK, SEL_TK)
        return jnp.concatenate([ksel_ref[pl.ds(off, SEL_TK), :], et_ref[pl.ds(off, SEL_TK), :]], axis=1)

    def scores_into(kv):
        kx = key_tile(kv)
        for c in range(NCH):
            s_ref[:, c * HALF:(c + 1) * HALF] = _dot_nt(kx, qxh[c])

    def update_cols(cols, vt, s):
        m_old = m_ref[:, cols]
        m_new = jnp.maximum(m_old, jnp.max(s, axis=0, keepdims=True))
        p = jnp.exp2(s - m_new).astype(BF16)
        acc_ref[:, cols] = jnp.exp2(m_old - m_new) * acc_ref[:, cols] + _dot(vt, p)
        m_ref[:, cols] = m_new

    def update_chunk(c, vt, s):
        update_cols(slice(c * HALF, (c + 1) * HALF), vt, s)

    def update(vt, s_chunks):
        for c in range(NCH):
            update_chunk(c, vt, s_chunks[c])

    def sel_values(kv):
        return jnp.concatenate([vselt_ref[kv * kpb + u] for u in range(kpb)], axis=1)

    def load_scores():
        return [s_ref[:, c * HALF:(c + 1) * HALF] for c in range(NCH)]

    m_ref[...] = jnp.full(m_ref.shape, MASK_VALUE, F32)
    acc_ref[...] = jnp.zeros(acc_ref.shape, F32)
    kd = t0 // SEL_TK
    scores_into(0)
    ot_win = []
    for h in range(nsub):
        sw = sws[h] + jnp.concatenate([win_b[h]] * NSA_HPG, axis=1)
        mw = jnp.max(sw, axis=0, keepdims=True)
        accw = _dot(win_v[h], jnp.exp2(sw - mw).astype(BF16))
        ot_win.append(accw[0:dv] * (1.0 / jnp.maximum(accw[dv:dv + 1], 1e-30)))

    def body(kv, carry):
        kx = key_tile(kv + 1)
        vt = sel_values(kv)
        pending = None
        for c in range(NCH):
            cols = slice(c * HALF, (c + 1) * HALF)
            s = s_ref[:, cols]
            s_ref[:, cols] = _dot_nt(kx, qxh[c])
            if pending is not None:
                update_chunk(c - 1, vt, pending)
            pending = s
        update_chunk(NCH - 1, vt, pending)
        return carry

    lax.fori_loop(0, kd, body, 0)
    dbias = dbias_ref[i % (SEL_TK // tq)].astype(F32)
    vt = sel_values(kd)
    if SEL_TK == tq and tq % (2 * LANES) == 0:
        hq = tq // 2
        for c in range(NCH):
            for j0 in range(c * HALF, (c + 1) * HALF, tq):
                update_cols(slice(j0, j0 + hq), vt[:, 0:hq], s_ref[0:hq, j0:j0 + hq] + dbias[0:hq, 0:hq])
                update_cols(slice(j0 + hq, j0 + tq), vt, s_ref[:, j0 + hq:j0 + tq] + dbias[:, hq:tq])
    else:
        update(vt, [sc + dbias for sc in load_scores()])
    acc = acc_ref[...]
    ot_slc = acc[0:dv] * (1.0 / jnp.maximum(acc[dv:dv + 1], 1e-30))

    gt_ref[...] = _sigmoid(gate_ref[...].astype(F32)).T
    ots = []
    for j in range(NSA_HPG):
        base = g * (NSA_HPG * 3) + j * 3
        ots.append(ocmpt_ref[0, 0, 0, :, j * tq:(j + 1) * tq].astype(F32)
                   + gt_ref[pl.ds(base + 1, 1), :] * ot_slc[:, j * tq:(j + 1) * tq]
                   + gt_ref[pl.ds(base + 2, 1), :]
                   * jnp.concatenate([ot_win[h][:, j * wq:(j + 1) * wq] for h in range(nsub)], axis=1))
    for jp in range(NSA_HPG // 2):
        pair = jnp.concatenate([ots[2 * jp], ots[2 * jp + 1]], axis=0)
        o_ref[:, jp * LANES:(jp + 1) * LANES] = pair.T.astype(o_ref.dtype)


def _diag_bias():
    r = jnp.arange(SEL_TK)[None, :, None]
    c = jnp.arange(NSA_TQ)[None, None, :]
    off = (jnp.arange(SEL_TK // NSA_TQ) * NSA_TQ)[:, None, None]
    return jnp.where(r <= off + c, 0.0, MASK_VALUE).astype(BF16)


def _window_bias():
    r = jnp.arange(WIN_KEYS)[None, :, None]
    c = jnp.arange(WIN_TQ)[None, None, :]
    off = jnp.minimum(jnp.arange(WINDOW // WIN_TQ + 1) * WIN_TQ, WINDOW)[:, None, None]
    diff = off + c - r
    return jnp.where((diff >= 0) & (diff < WINDOW), 0.0, MASK_VALUE).astype(BF16)


def _nsa_sel_win(proj, vsel_t, vwin_t, sel, et, ocmpt, B, S):
    T = B * S
    tq = NSA_TQ
    assert HALF == tq
    dbias = _diag_bias()
    wbias = _window_bias()
    nq = S // tq
    G = NSA_KV_GROUPS
    qw = NSA_HPG * HEAD_SLOT
    qblk = C_QEXT // qw
    vsel_t = vsel_t.reshape(G, B, S // LANES, VT_ROWS, LANES)
    vwin_t = vwin_t.reshape(G, B, S // LANES, VT_ROWS, LANES)
    ow = NSA_HPG * NSA_HEAD_DIM
    kv_spec = lambda c: pl.BlockSpec((S, LANES), lambda b, g, i: (b, c // LANES))
    vt_spec = pl.BlockSpec((None, None, S // LANES, VT_ROWS, LANES), lambda b, g, i: (g, b, 0, 0, 0))
    return pl.pallas_call(
        _sel_win_kernel,
        grid=(B, G, nq),
        in_specs=[pl.BlockSpec((tq, qw), lambda b, g, i: (b * nq + i, qblk + g)),
                  kv_spec(C_KSEL), vt_spec, kv_spec(C_KWIN), vt_spec,
                  pl.BlockSpec((1, 1, tq, SEL_LANES), lambda b, g, i: (b, g, i, 0)),
                  pl.BlockSpec(et.shape, lambda b, g, i: (0, 0)),
                  pl.BlockSpec((1, 1, 1, NSA_HEAD_DIM, NSA_HPG * tq), lambda b, g, i: (b, g, i, 0, 0)),
                  pl.BlockSpec((tq, LANES), lambda b, g, i: (b * nq + i, C_SMALL // LANES)),
                  pl.BlockSpec(dbias.shape, lambda b, g, i: (0, 0, 0)),
                  pl.BlockSpec(wbias.shape, lambda b, g, i: (0, 0, 0))],
        out_specs=pl.BlockSpec((tq, ow), lambda b, g, i: (b * nq + i, g)),
        out_shape=jax.ShapeDtypeStruct((T, NSA_Q_W), BF16),
        scratch_shapes=[pltpu.VMEM((SEL_TK, NSA_HPG * tq), F32),
                        pltpu.VMEM((1, NSA_HPG * tq), F32),
                        pltpu.VMEM((VT_ROWS, NSA_HPG * tq), F32),
                        pltpu.VMEM((LANES, tq), F32)],
        compiler_params=_cparams(("parallel", "parallel", "parallel")),
        name="nsa_sel_win",
    )(proj, proj, vsel_t, proj, vwin_t, sel, et, ocmpt, proj, dbias, wbias)


TAIL_ROWS = 8


def _causal_conv(x, tail_ref, w, b):
    L, C = x.shape
    nv = L // TAIL_ROWS
    xx = jnp.concatenate([tail_ref[...], x], axis=0).reshape(nv + 1, TAIL_ROWS, C)
    sub = lax.broadcasted_iota(jnp.int32, (nv, TAIL_ROWS, C), 1)
    y = b + w[CONV_WIDTH - 1:CONV_WIDTH] * x
    for k in range(1, CONV_WIDTH):
        rot = pltpu.roll(xx, k, axis=1)
        shifted = jnp.where(sub >= k, rot[1:], rot[:-1]).reshape(L, C)
        y = y + w[CONV_WIDTH - 1 - k:CONV_WIDTH - k] * shifted
    tail_ref[...] = x[L - TAIL_ROWS:L]
    return y


SSD_CPS = 2


def _ssd_kernel(z_ref, xbc_ref, dtc_ref, dtr_ref, cw_ref, cb_ref, dtbc_ref, dtbr_ref, alc_ref, alr_ref,
                dsk_ref, nw_ref, o_ref, state_ref, tail_ref, y_ref):
    c = pl.program_id(1)
    L = SSD_CHUNK
    P = SSD_HEAD_DIM
    N = SSD_STATE
    hpg = SSD_HEADS // SSD_GROUPS

    @pl.when(c == 0)
    def _():
        state_ref[...] = jnp.zeros_like(state_ref)
        tail_ref[...] = jnp.zeros_like(tail_ref)

    conv = _causal_conv(xbc_ref[...].astype(F32), tail_ref, cw_ref[...], cb_ref[...])
    xbc = conv * _sigmoid(conv)
    dt_c = _softplus(dtc_ref[...] + dtbc_ref[...])
    dt_r = _softplus(dtr_ref[...] + dtbr_ref[...])
    a_c = dt_c * (-jnp.exp(alc_ref[...]))
    a_r = dt_r * (-jnp.exp(alr_ref[...]))
    ii = lax.broadcasted_iota(jnp.int32, (L, L), 0)
    jj = lax.broadcasted_iota(jnp.int32, (L, L), 1)
    tri = ii >= jj
    tril = tri.astype(F32)
    triu = (ii <= jj).astype(F32)

    for cc in range(SSD_CPS):
        rs = slice(cc * L, (cc + 1) * L)
        xs = xbc[rs, 0:SSD_INNER]
        bm = xbc[rs, SSD_INNER:SSD_INNER + SSD_GROUPS * N]
        cm = xbc[rs, SSD_INNER + SSD_GROUPS * N:SSD_INNER + 2 * SSD_GROUPS * N]
        acum_c = _dot_f32(tril, a_c[rs])
        acum_r = _dot_f32(a_r[:, rs], triu)
        for g in range(SSD_GROUPS):
            bg = bm[:, g * N:(g + 1) * N]
            cgb = cm[:, g * N:(g + 1) * N].astype(BF16)
            cb = _dot_nt(cgb, bg.astype(BF16))
            for j in range(hpg):
                h = g * hpg + j
                acb = jnp.broadcast_to(acum_c[:, h:h + 1], (L, L))
                a_last = acum_c[L - 1:L, h:h + 1]
                xh_raw = xs[:, h * P:(h + 1) * P]
                xh = (xh_raw * jnp.broadcast_to(dt_c[rs, h:h + 1], (L, P))).astype(BF16)
                lmat = jnp.where(tri, jnp.exp(acb - acum_r[h:h + 1, :]), 0.0)
                y = _dot((cb * lmat).astype(BF16), xh)
                prev = state_ref[h]
                y = y + _dot(cgb, prev.astype(BF16)) * jnp.exp(acb[:, 0:P])
                bd = (bg * jnp.exp(a_last - acb[:, 0:N])).astype(BF16)
                state_ref[h] = jnp.exp(a_last) * prev + _dot_tn(bd, xh)
                y_ref[rs, h * P:(h + 1) * P] = y + dsk_ref[:, h * P:(h + 1) * P] * xh_raw

    zf = z_ref[...].astype(F32)
    yg = y_ref[...] * (zf * _sigmoid(zf))
    ms = jnp.mean(yg * yg, axis=-1, keepdims=True)
    o_ref[...] = (yg * lax.rsqrt(ms + RMS_EPS) * nw_ref[...]).astype(o_ref.dtype)


def _ssd_mixer(proj, dt_col, dt_row, conv_w, conv_b, dt_bias, a_log, d_skip, norm_w, B, S):
    T = B * S
    L = SSD_CPS * SSD_CHUNK
    nc = S // L
    H = SSD_HEADS
    full = lambda a: pl.BlockSpec(a.shape, lambda b, c: (0, 0))
    cb2 = conv_b.reshape(1, -1)
    dtb_c = dt_bias.reshape(1, H)
    dtb_r = dt_bias.reshape(H, 1)
    al_c = a_log.reshape(1, H)
    al_r = a_log.reshape(H, 1)
    dsk = jnp.repeat(d_skip, SSD_HEAD_DIM).reshape(1, SSD_INNER)
    nw = norm_w.reshape(1, SSD_INNER)
    return pl.pallas_call(
        _ssd_kernel,
        grid=(B, nc),
        in_specs=[pl.BlockSpec((L, SSD_INNER), lambda b, c: (b * nc + c, C_SSDZ // SSD_INNER)),
                  pl.BlockSpec((L, SSD_XBC_W), lambda b, c: (b * nc + c, C_XBC // SSD_XBC_W)),
                  pl.BlockSpec((L, H), lambda b, c: (b * nc + c, 0)),
                  pl.BlockSpec((H, L), lambda b, c: (0, b * nc + c)),
                  full(conv_w), full(cb2), full(dtb_c), full(dtb_r), full(al_c), full(al_r),
                  full(dsk), full(nw)],
        out_specs=pl.BlockSpec((L, SSD_INNER), lambda b, c: (b * nc + c, 0)),
        out_shape=jax.ShapeDtypeStruct((T, SSD_INNER), BF16),
        scratch_shapes=[pltpu.VMEM((H, SSD_STATE, SSD_HEAD_DIM), F32),
                        pltpu.VMEM((TAIL_ROWS, SSD_XBC_W), F32),
                        pltpu.VMEM((L, SSD_INNER), F32)],
        compiler_params=_cparams(("parallel", "arbitrary")),
        name="ssd_mixer",
    )(proj, proj, dt_col, dt_row, conv_w, cb2, dtb_c, dtb_r, al_c, al_r, dsk, nw)


SCAN_GROUP = 8


def _lru_kernel(x_ref, y_ref, cw_ref, cb_ref, wa_ref, ba_ref, wx_ref, bx_ref, lam_ref, o_ref,
                h_ref, tail_ref, *, tc):
    c = pl.program_id(1)

    @pl.when(c == 0)
    def _():
        h_ref[...] = jnp.zeros_like(h_ref)
        tail_ref[...] = jnp.zeros_like(tail_ref)

    xr = _causal_conv(x_ref[...].astype(F32), tail_ref, cw_ref[...], cb_ref[...])
    xrb = xr.astype(BF16)
    r = _sigmoid(_dot(xrb, wa_ref[...]) + ba_ref[...])
    ig = _sigmoid(_dot(xrb, wx_ref[...]) + bx_ref[...])
    log_a = -LRU_C * r * _softplus(-lam_ref[...])
    a = jnp.exp(log_a)
    b = jnp.sqrt(1.0 - jnp.exp(2.0 * log_a)) * (ig * xr)
    ng = tc // SCAN_GROUP
    a = a.reshape(ng, SCAN_GROUP, a.shape[1])
    b = b.reshape(ng, SCAN_GROUP, b.shape[1])
    sub = lax.broadcasted_iota(jnp.int32, a.shape, 1)
    k = 1
    while k < SCAN_GROUP:
        keep = sub >= k
        a_s = jnp.where(keep, pltpu.roll(a, k, axis=1), 1.0)
        b_s = jnp.where(keep, pltpu.roll(b, k, axis=1), 0.0)
        b = a * b_s + b
        a = a * a_s
        k *= 2
    carry = h_ref[0:1, :]
    groups = []
    for rg in range(ng):
        hg = a[rg] * carry + b[rg]
        groups.append(hg)
        carry = hg[SCAN_GROUP - 1:SCAN_GROUP]
    h = jnp.concatenate(groups, axis=0)
    h_ref[...] = jnp.broadcast_to(carry, h_ref.shape)
    o_ref[...] = (h * _gelu_tanh(y_ref[...].astype(F32))).astype(o_ref.dtype)


def _block_diag(w):
    nb, c, d = w.shape
    eye = jnp.eye(nb, dtype=w.dtype)
    return (eye[:, None, :, None] * w[:, :, None, :]).reshape(nb * c, nb * d)


def _lru_mixer(proj, conv_w, conv_b, wa, ba, wx, bx, lam, B, S):
    T = B * S
    tc = min(512, S)
    nt = S // tc
    W = LRU_WIDTH
    wa_bd = _block_diag(wa).astype(BF16)
    wx_bd = _block_diag(wx).astype(BF16)
    row = lambda v: v.reshape(1, W)
    full = lambda a: pl.BlockSpec(a.shape, lambda b, c: (0, 0))
    args = (conv_w, row(conv_b), wa_bd, row(ba), wx_bd, row(bx), row(lam))
    return pl.pallas_call(
        functools.partial(_lru_kernel, tc=tc),
        grid=(B, nt),
        in_specs=[pl.BlockSpec((tc, W), lambda b, c: (b * nt + c, C_LRUX // W)),
                  pl.BlockSpec((tc, W), lambda b, c: (b * nt + c, C_LRUY // W))]
                 + [full(a) for a in args],
        out_specs=pl.BlockSpec((tc, W), lambda b, c: (b * nt + c, 0)),
        out_shape=jax.ShapeDtypeStruct((T, W), BF16),
        scratch_shapes=[pltpu.VMEM((TAIL_ROWS, W), F32), pltpu.VMEM((TAIL_ROWS, W), F32)],
        compiler_params=_cparams(("parallel", "arbitrary")),
        name="lru_mixer",
    )(proj, proj, *args)


def _merge_kernel(x_ref, mg_ref, on_ref, os_ref, ol_ref, pn_ref, ps_ref, pl_ref, wo_ref, g_ref, b_ref,
                  of_ref, ob_ref):
    d = D_MODEL
    gate = _sigmoid(mg_ref[...].astype(F32))
    mixed = (gate[:, 0:d] * _dot(on_ref[...], pn_ref[...])
             + gate[:, d:2 * d] * _dot(os_ref[...], ps_ref[...])
             + gate[:, 2 * d:3 * d] * _dot(ol_ref[...], pl_ref[...]))
    v = ALPHA * x_ref[...] + _dot(mixed.astype(BF16), wo_ref[...])
    out = _layer_norm(v, g_ref[...], b_ref[...])
    of_ref[...] = out
    ob_ref[...] = out.astype(BF16)


def _merge(x, proj, o_nsa, o_ssd, o_lru, pn, ps, plru, wo, g, b, layer):
    T = x.shape[0]
    tm = min(512, T)
    d = D_MODEL
    rowblk = lambda w: pl.BlockSpec((tm, w), lambda i: (i, 0))
    full = lambda a: pl.BlockSpec((None,) + a.shape[1:], lambda i: (layer, 0, 0))
    g2, b2 = g.reshape(-1, 1, d), b.reshape(-1, 1, d)
    return pl.pallas_call(
        _merge_kernel,
        grid=(T // tm,),
        in_specs=[rowblk(d), pl.BlockSpec((tm, 3 * d), lambda i: (i, C_MERGE // (3 * d))),
                  rowblk(o_nsa.shape[1]), rowblk(o_ssd.shape[1]), rowblk(o_lru.shape[1]),
                  full(pn), full(ps), full(plru), full(wo), full(g2), full(b2)],
        out_specs=[rowblk(d), rowblk(d)],
        out_shape=[jax.ShapeDtypeStruct((T, d), F32), jax.ShapeDtypeStruct((T, d), BF16)],
        compiler_params=_cparams(("parallel",)),
        name="merge_ln",
    )(x, proj, o_nsa, o_ssd, o_lru, pn, ps, plru, wo, g2, b2)


def _route(sel, aff):
    epg = EXPERTS_PER_GROUP
    scores = []
    for gi in range(N_EXPERT_GROUPS):
        v = sel[gi * epg:(gi + 1) * epg]
        pair = None
        for a in range(epg):
            for b in range(a + 1, epg):
                sab = v[a] + v[b]
                pair = sab if pair is None else jnp.maximum(pair, sab)
        scores.append(pair)
    best = jnp.zeros_like(scores[0], dtype=jnp.int32)
    best_s = scores[0]
    for gi in range(1, N_EXPERT_GROUPS):
        better = scores[gi] > best_s
        best = jnp.where(better, gi, best)
        best_s = jnp.where(better, scores[gi], best_s)
    chosen = []
    for k in range(N_EXPERTS):
        gi = k // epg
        rank = jnp.zeros_like(best)
        for o in range(gi * epg, (gi + 1) * epg):
            if o == k:
                continue
            ahead = (sel[o] > sel[k]) | ((sel[o] == sel[k]) & (o < k))
            rank = rank + ahead.astype(jnp.int32)
        chosen.append((best == gi) & (rank < TOP_K))
    wsum = None
    for k in range(N_EXPERTS):
        wk = jnp.where(chosen[k], aff[k], 0.0)
        wsum = wk if wsum is None else wsum + wk
    inv = 1.0 / wsum
    return [jnp.where(chosen[k], aff[k], 0.0) * inv for k in range(N_EXPERTS)]


EXPERT_PAIRS = N_EXPERTS // 2
PAIRS_PER_ITER = 4


def _moe_kernel(xb_ref, xf_ref, p_ref, rw_ref, rb_ref, pg_ref, pp_ref, wg_ref, wu_ref, wd_ref,
                g_ref, b_ref, of_ref, ob_ref, acc_ref, gates_ref):
    xb = xb_ref[...]
    tm = xb.shape[0]
    logits = _dot_nt(rw_ref[...], xb)
    aff = _sigmoid(logits)
    sel = aff + rb_ref[...]
    gate_rows = _route([sel[k:k + 1, :] for k in range(N_EXPERTS)],
                       [aff[k:k + 1, :] for k in range(N_EXPERTS)])
    gt = jnp.concatenate(gate_rows + [jnp.zeros((LANES - N_EXPERTS, tm), F32)], axis=0)
    gates_ref[...] = gt.T
    acc_ref[...] = _sigmoid(_dot(xb, pg_ref[...])) * _dot(p_ref[...].astype(BF16), pp_ref[...])
    lane = lax.broadcasted_iota(jnp.int32, (tm, LANES), 1)

    def gate_cols(k):
        gates = gates_ref[...]
        cols = [jnp.broadcast_to(jnp.sum(jnp.where(lane == 2 * k + u, gates, 0.0), axis=-1, keepdims=True),
                                 (tm, D_EXPERT)) for u in range(2)]
        return jnp.concatenate(cols, axis=1)

    def step(it, carry):
        ks = [it * PAIRS_PER_ITER + u for u in range(PAIRS_PER_ITER)]
        hgs = [[_dot(xb, wg_ref[2 * k + u]) for u in range(2)] for k in ks]
        hus = [[_dot(xb, wu_ref[2 * k + u]) for u in range(2)] for k in ks]
        for k, hg2, hu2 in zip(ks, hgs, hus):
            hg = jnp.concatenate(hg2, axis=1)
            h = (hg * _sigmoid(hg)) * jnp.concatenate(hu2, axis=1) * gate_cols(k)
            acc_ref[...] += _dot(h.astype(BF16), wd_ref[k])
        return carry

    lax.fori_loop(0, EXPERT_PAIRS // PAIRS_PER_ITER, step, 0)
    out = _layer_norm(ALPHA * xf_ref[...] + acc_ref[...], g_ref[...], b_ref[...])
    of_ref[...] = out
    ob_ref[...] = out.astype(BF16)


def _moe_ple(xb, xf, p, layer, rw_t, rb, pg, pp, wg, wu, wd, g, b):
    T = xb.shape[0]
    tm = min(512, T)
    d = D_MODEL
    rowblk = lambda w: pl.BlockSpec((tm, w), lambda i: (i, 0))
    once = pl.Buffered(1)
    full = lambda a: pl.BlockSpec(a.shape, lambda i: (0,) * a.ndim, pipeline_mode=once)
    lyr = lambda a: pl.BlockSpec((None,) + a.shape[1:], lambda i: (layer,) + (0,) * (a.ndim - 1), pipeline_mode=once)
    g2, b2 = g.reshape(-1, 1, d), b.reshape(-1, 1, d)
    return pl.pallas_call(
        _moe_kernel,
        grid=(T // tm,),
        in_specs=[rowblk(d), rowblk(d), pl.BlockSpec((None, tm, PLE_DIM), lambda i: (layer, i, 0)),
                  full(rw_t), full(rb), lyr(pg), lyr(pp),
                  lyr(wg), lyr(wu), lyr(wd), lyr(g2), lyr(b2)],
        out_specs=[rowblk(d), rowblk(d)],
        out_shape=[jax.ShapeDtypeStruct((T, d), F32), jax.ShapeDtypeStruct((T, d), BF16)],
        scratch_shapes=[pltpu.VMEM((tm, d), F32), pltpu.VMEM((tm, LANES), F32)],
        compiler_params=_cparams(("parallel",)),
        name="moe_ple_ln",
    )(xb, xf, p, rw_t, rb, pg, pp, wg, wu, wd, g2, b2)


def _overlap_matrix(nc):
    n = jnp.arange(nc)[None, :]
    m = jnp.arange(SEL_LANES)[:, None]
    ratio = SEL_BLOCK // CMP_STRIDE
    ov = jnp.zeros((SEL_LANES, nc), F32)
    for k in range(CMP_BLOCK // CMP_STRIDE):
        ov = ov + ((n + k) // ratio == m).astype(F32)
    return ov.astype(BF16)


def _expand_matrix(S):
    c = jnp.arange(S)[:, None]
    m = jnp.arange(SEL_LANES)[None, :]
    return jnp.where(c // SEL_BLOCK == m, -MASK_VALUE, 0.0).astype(BF16)


def _pad_w2(w2):
    out = jnp.zeros((NSA_KV_GROUPS, CMP_HIDDEN, HEAD_SLOT), F32)
    for g in range(NSA_KV_GROUPS):
        out = out.at[g, :, g * NSA_HEAD_DIM:(g + 1) * NSA_HEAD_DIM].set(w2)
    return out.astype(BF16)


def kernel(x, p, w_in, nsa_pe_k, nsa_w1_k, nsa_w2_k, nsa_pe_v, nsa_w1_v, nsa_w2_v, ssd_conv_w, ssd_conv_b, ssd_dt_bias, ssd_a_log, ssd_d, ssd_norm_w, lru_conv_w, lru_conv_b, lru_wa, lru_ba, lru_wx, lru_bx, lru_lambda, proj_nsa, proj_ssd, proj_lru, w_out, ln1_g, ln1_b, router_w, router_b, exp_w_gate, exp_w_up, exp_w_down, ple_w_gate, ple_w_proj, ln2_g, ln2_b):
    B, S, d = x.shape
    T = B * S
    depth = w_in.shape[0]
    assert d == D_MODEL and S % SEL_TK == 0 and S >= WIN_KEYS and S // SEL_BLOCK <= SEL_LANES
    nr = S // CMP_STRIDE
    ov = _overlap_matrix(nr)
    emat = _expand_matrix(S)
    rw_t = router_w.T.astype(BF16)
    rb = router_b.reshape(N_EXPERTS, 1).astype(F32)

    w_main = _prep_w_in(w_in)
    aux_w = _aux_weights(w_in)
    pn, ps, plru, wo = (a.astype(BF16) for a in (proj_nsa, proj_ssd, proj_lru, w_out))
    pg, pp, wg, wu = (a.astype(BF16) for a in (ple_w_gate, ple_w_proj, exp_w_gate, exp_w_up))
    wd = exp_w_down.reshape(depth, EXPERT_PAIRS, 2 * D_EXPERT, d).astype(BF16)
    p3 = p.reshape(depth, T, PLE_DIM)

    xf = x.reshape(T, d)
    xb = xf
    for i in range(depth):
        proj = _in_proj(xb, w_main, i)
        cmp, vsel_t, vwin_t, dt_col, dt_row = _aux_proj(xb, aux_w, i)

        w1big, pebig = _cmp_first_layer(nsa_w1_k[i], nsa_w1_v[i], nsa_pe_k[i], nsa_pe_v[i])
        kc, vct = _nsa_compress(cmp, w1big, pebig, _pad_w2(nsa_w2_k[i]),
                                nsa_w2_v[i].T.astype(BF16), B, S)
        ocmpt, sel = _nsa_cmp_attn(proj, kc, vct, ov, B, S)
        o_nsa = _nsa_sel_win(proj, vsel_t, vwin_t, sel, emat, ocmpt, B, S)

        o_ssd = _ssd_mixer(proj, dt_col, dt_row, ssd_conv_w[i], ssd_conv_b[i], ssd_dt_bias[i],
                           ssd_a_log[i], ssd_d[i], ssd_norm_w[i], B, S)
        o_lru = _lru_mixer(proj, lru_conv_w[i], lru_conv_b[i], lru_wa[i], lru_ba[i], lru_wx[i],
                           lru_bx[i], lru_lambda[i], B, S)

        xf, xb = _merge(xf, proj, o_nsa, o_ssd, o_lru, pn, ps, plru, wo, ln1_g, ln1_b, i)
        xf, xb = _moe_ple(xb, xf, p3, i, rw_t, rb, pg, pp, wg, wu, wd, ln2_g, ln2_b)
    return xf.reshape(B, S, d)
```

```python
import functools
import math

import jax
import jax.numpy as jnp
from jax import lax
from jax.experimental import pallas as pl
from jax.experimental.pallas import tpu as pltpu

F32 = jnp.float32
BF16 = jnp.bfloat16

D_MODEL = 1024
PLE_DIM = 256
NSA_HEADS = 8
NSA_KV_GROUPS = 2
NSA_HEAD_DIM = 64
NSA_HPG = NSA_HEADS // NSA_KV_GROUPS
NSA_Q_W = NSA_HEADS * NSA_HEAD_DIM
NSA_KV_W = NSA_KV_GROUPS * NSA_HEAD_DIM
CMP_BLOCK = 32
CMP_STRIDE = 16
CMP_HIDDEN = 256
SEL_BLOCK = 64
SEL_TOPN = 16
WINDOW = 512
FORCE_SCORE = 1e4
MASK_VALUE = -1e30
LOG2E = 1.4426950408889634
SSD_HEADS = 8
SSD_HEAD_DIM = 64
SSD_INNER = SSD_HEADS * SSD_HEAD_DIM
SSD_GROUPS = 2
SSD_STATE = 64
SSD_CHUNK = 128
SSD_XBC_W = SSD_INNER + 2 * SSD_GROUPS * SSD_STATE
CONV_WIDTH = 4
LRU_WIDTH = 512
LRU_BLOCKS = 8
LRU_BLOCK_DIM = LRU_WIDTH // LRU_BLOCKS
LRU_C = 8.0
N_EXPERTS = 16
N_EXPERT_GROUPS = 4
EXPERTS_PER_GROUP = N_EXPERTS // N_EXPERT_GROUPS
TOP_K = 2
D_EXPERT = 256
DEPTH = 2
ALPHA = (2 * DEPTH) ** 0.25
LN_EPS = 1e-5
RMS_EPS = 1e-5
IN_SIZES = (NSA_Q_W, NSA_KV_W, NSA_KV_W, NSA_KV_W, NSA_KV_W, NSA_KV_W, NSA_KV_W, NSA_HEADS * 3,
            SSD_INNER, SSD_XBC_W, SSD_HEADS, LRU_WIDTH, LRU_WIDTH, 3 * D_MODEL)

LANES = 128
SEL_LANES = 128
HEAD_SLOT = 128

C_MERGE = 0
C_QEXT = 3072
C_SSDZ = 4096
C_LRUX = 4608
C_LRUY = 5120
C_KSEL = 5632
C_KWIN = 5760
C_SMALL = 5888
C_XBC = 6144
PROJ_W = 6912
PROJ_TN = 2304
GATE_W = NSA_HEADS * 3

VMEM_LIMIT = 56 * 1024 * 1024


def _cparams(sem):
    return pltpu.CompilerParams(dimension_semantics=sem, vmem_limit_bytes=VMEM_LIMIT)


def _sigmoid(x):
    return 1.0 / (1.0 + jnp.exp(-x))


def _softplus(x):
    return jnp.maximum(x, 0.0) + jnp.log(1.0 + jnp.exp(-jnp.abs(x)))


def _gelu_tanh(x):
    c = math.sqrt(2.0 / math.pi)
    return 0.5 * x * (1.0 + jnp.tanh(c * (x + 0.044715 * (x * x * x))))


def _dot(a, b):
    return jnp.dot(a, b, preferred_element_type=F32)


def _dot_nt(a, b):
    return lax.dot_general(a, b, (((1,), (1,)), ((), ())), preferred_element_type=F32)


def _dot_tn(a, b):
    return lax.dot_general(a, b, (((0,), (0,)), ((), ())), preferred_element_type=F32)


def _dot_f32(a, b):
    return jnp.dot(a, b, preferred_element_type=F32, precision=lax.Precision.HIGHEST)


def _layer_norm(v, g, b):
    mu = jnp.mean(v, axis=-1, keepdims=True)
    vc = v - mu
    var = jnp.mean(vc * vc, axis=-1, keepdims=True)
    return vc * lax.rsqrt(var + LN_EPS) * g + b


def _matmul_kernel(x_ref, w_ref, o_ref):
    o_ref[...] = _dot(x_ref[...].astype(BF16), w_ref[...]).astype(o_ref.dtype)


def _in_proj(xb, w, layer):
    T, K = xb.shape
    N = w.shape[2]
    tm = min(1024, T)
    tn = PROJ_TN
    return pl.pallas_call(
        _matmul_kernel,
        grid=(N // tn, T // tm),
        in_specs=[pl.BlockSpec((tm, K), lambda j, i: (i, 0)),
                  pl.BlockSpec((None, K, tn), lambda j, i: (layer, 0, j))],
        out_specs=pl.BlockSpec((tm, tn), lambda j, i: (i, j)),
        out_shape=jax.ShapeDtypeStruct((T, N), BF16),
        compiler_params=_cparams(("parallel", "parallel")),
        name="in_proj",
    )(xb, w)


def _split_w_in(w):
    offs = [0]
    for s in IN_SIZES:
        offs.append(offs[-1] + s)
    return [w[..., offs[k]:offs[k + 1]] for k in range(len(IN_SIZES))]


def _prep_w_in(w):
    pc = _split_w_in(w)
    lead = w.shape[:-1]
    q = pc[0].reshape(*lead, NSA_KV_GROUPS, NSA_HPG, NSA_HEAD_DIM) * (NSA_HEAD_DIM ** -0.5 * LOG2E)
    zeros = jnp.zeros_like(q)
    q_ext = jnp.stack([jnp.concatenate([q[..., 0, :, :], zeros[..., 0, :, :]], axis=-1),
                       jnp.concatenate([zeros[..., 1, :, :], q[..., 1, :, :]], axis=-1)], axis=-3)
    q_ext = q_ext.reshape(*lead, NSA_HEADS * HEAD_SLOT)
    small = jnp.pad(pc[7], [(0, 0)] * len(lead) + [(0, C_XBC - C_SMALL - GATE_W)])
    out = jnp.concatenate([pc[13], q_ext, pc[8], pc[11], pc[12], pc[3], pc[5], small, pc[9]], axis=-1)
    assert out.shape[-1] == PROJ_W
    return out.astype(BF16)


VT_ROWS = NSA_HEAD_DIM + 16


def _aux_proj_kernel(x_ref, wc_ref, wvt_ref, wdt_ref, wdtt_ref, cmp_ref, vst_ref, vwt_ref, dtc_ref, dtr_ref):
    x = x_ref[...].astype(BF16)
    tm = x.shape[0]
    cmp_ref[...] = _dot(x, wc_ref[...]).astype(cmp_ref.dtype)
    vt = _dot_nt(wvt_ref[...], x)
    dv = NSA_HEAD_DIM
    ones = jnp.ones((VT_ROWS - dv, LANES), vst_ref.dtype)
    for u in range(tm // LANES):
        for k, o_ref in enumerate((vst_ref, vwt_ref)):
            for g in range(NSA_KV_GROUPS):
                r0 = (k * NSA_KV_GROUPS + g) * dv
                o_ref[g, u, 0:dv, :] = vt[r0:r0 + dv, u * LANES:(u + 1) * LANES].astype(o_ref.dtype)
                o_ref[g, u, dv:VT_ROWS, :] = ones
    xf = x.astype(F32)
    dtc_ref[...] = _dot(xf, wdt_ref[...])
    dtr_ref[...] = _dot_nt(wdtt_ref[...], xf)


def _aux_weights(w):
    pc = _split_w_in(w)
    wc = jnp.concatenate([pc[1], pc[2]], axis=-1).astype(BF16)
    wvt = jnp.swapaxes(jnp.concatenate([pc[4], pc[6]], axis=-1), -1, -2).astype(BF16)
    return wc, wvt, pc[10], jnp.swapaxes(pc[10], -1, -2)


def _aux_proj(xb, aux_w, layer):
    T, K = xb.shape
    wc, wvt, wdt, wdtt = aux_w
    tm = min(1024, T)
    full = lambda a: pl.BlockSpec((None,) + a.shape[1:], lambda i: (layer, 0, 0))
    vt_shape = jax.ShapeDtypeStruct((NSA_KV_GROUPS, T // LANES, VT_ROWS, LANES), BF16)
    vt_spec = pl.BlockSpec((NSA_KV_GROUPS, tm // LANES, VT_ROWS, LANES), lambda i: (0, i, 0, 0))
    return pl.pallas_call(
        _aux_proj_kernel,
        grid=(T // tm,),
        in_specs=[pl.BlockSpec((tm, K), lambda i: (i, 0)), full(wc), full(wvt), full(wdt), full(wdtt)],
        out_specs=[pl.BlockSpec((tm, 2 * NSA_KV_W), lambda i: (i, 0)), vt_spec, vt_spec,
                   pl.BlockSpec((tm, SSD_HEADS), lambda i: (i, 0)),
                   pl.BlockSpec((SSD_HEADS, tm), lambda i: (0, i))],
        out_shape=[jax.ShapeDtypeStruct((T, 2 * NSA_KV_W), BF16), vt_shape, vt_shape,
                   jax.ShapeDtypeStruct((T, SSD_HEADS), F32), jax.ShapeDtypeStruct((SSD_HEADS, T), F32)],
        compiler_params=_cparams(("parallel",)),
        name="aux_proj",
    )(xb, wc, wvt, wdt, wdtt)


CMP_PIECES = 2 * NSA_KV_GROUPS


def _compress_kernel(r_ref, pe_ref, w1_ref, w2k_ref, w2vt_ref, kc_ref, vct_ref):
    r = r_ref[0]
    nr = r.shape[0]
    u = _dot(r, w1_ref[0])
    v = _dot(r, w1_ref[1])
    c = _dot(pe_ref[0], w1_ref[0]) + _dot(pe_ref[1], w1_ref[1])
    hid = u + pltpu.roll(v, nr - 1, axis=0)
    hid = (hid.reshape(nr // 8, 8, hid.shape[1]) + c[None]).reshape(nr, hid.shape[1])
    act = _gelu_tanh(hid).astype(BF16)
    for g in range(NSA_KV_GROUPS):
        kp, vp = g, NSA_KV_GROUPS + g
        kc_ref[0, g] = _dot(act[:, kp * CMP_HIDDEN:(kp + 1) * CMP_HIDDEN], w2k_ref[g]).astype(kc_ref.dtype)
        vct = _dot_nt(w2vt_ref[...], act[:, vp * CMP_HIDDEN:(vp + 1) * CMP_HIDDEN])
        vct_ref[0, g] = vct.astype(vct_ref.dtype)


def _cmp_first_layer(w1k, w1v, pek, pev):
    half = CMP_BLOCK // 2
    blocks, pes = [], []
    for p, (w1, pe1) in enumerate(((w1k, pek), (w1k, pek), (w1v, pev), (w1v, pev))):
        wp = w1.reshape(2, half, NSA_HEAD_DIM, CMP_HIDDEN).astype(BF16)
        blocks.append(jnp.pad(wp, ((0, 0), (0, 0), (0, 0), (p * CMP_HIDDEN, (CMP_PIECES - 1 - p) * CMP_HIDDEN))))
        pes.append(pe1.reshape(2, half, NSA_HEAD_DIM).astype(BF16))
    rows = half * CMP_PIECES * NSA_HEAD_DIM
    big = jnp.concatenate(blocks, axis=2).reshape(2, rows, CMP_PIECES * CMP_HIDDEN)
    pe = jnp.concatenate(pes, axis=2).reshape(2, 1, rows)
    return big, jnp.broadcast_to(pe, (2, 8, rows))


def _nsa_compress(cmp, w1big, pebig, w2k, w2vt, B, S):
    NR = S // CMP_STRIDE
    W = CMP_STRIDE * 2 * NSA_KV_W
    r = cmp.reshape(B, NR, W)
    once = pl.Buffered(1)
    full = lambda a: pl.BlockSpec(a.shape, lambda b: (0,) * a.ndim, pipeline_mode=once)
    return pl.pallas_call(
        _compress_kernel,
        grid=(B,),
        in_specs=[pl.BlockSpec((1, NR, W), lambda b: (b, 0, 0)), full(pebig), full(w1big), full(w2k), full(w2vt)],
        out_specs=[pl.BlockSpec((1, NSA_KV_GROUPS, NR, HEAD_SLOT), lambda b: (b, 0, 0, 0)),
                   pl.BlockSpec((1, NSA_KV_GROUPS, NSA_HEAD_DIM, NR), lambda b: (b, 0, 0, 0))],
        out_shape=[jax.ShapeDtypeStruct((B, NSA_KV_GROUPS, NR, HEAD_SLOT), BF16),
                   jax.ShapeDtypeStruct((B, NSA_KV_GROUPS, NSA_HEAD_DIM, NR), BF16)],
        compiler_params=_cparams(("parallel",)),
        name="nsa_compress",
    )(r, pebig, w1big, w2k, w2vt)


def _stack_heads(q):
    return jnp.concatenate([q[:, j * HEAD_SLOT:(j + 1) * HEAD_SLOT] for j in range(NSA_HPG)], axis=0)


CMP_VARIANTS = 4


def _cmp_attn_kernel(q_ref, kc_ref, vct_ref, ovt_ref, gate_ref, ocmpt_ref, sel_ref, gt_ref, imp_ref, *, tq):
    g = pl.program_id(1)
    i = pl.program_id(2)
    q2 = _stack_heads(q_ref[...])
    nc_all = kc_ref.shape[2]
    gt_ref[...] = _sigmoid(gate_ref[...].astype(F32)).T

    def attend(nc):
        kc = kc_ref[0, 0, 0:nc, :]
        vct = vct_ref[0, 0, :, 0:nc]
        sts = [_dot_nt(kc, q2[j * tq:(j + 1) * tq]) for j in range(NSA_HPG)]
        n = lax.broadcasted_iota(jnp.int32, (nc, tq), 0)
        t = i * tq + lax.broadcasted_iota(jnp.int32, (nc, tq), 1)
        mask = n * CMP_STRIDE + (CMP_BLOCK - 1) <= t
        ps = None
        for j in range(NSA_HPG):
            s = jnp.where(mask, sts[j], MASK_VALUE)
            m = jnp.max(s, axis=0, keepdims=True)
            e = jnp.where(mask, jnp.exp2(s - m), 0.0)
            den = jnp.maximum(jnp.sum(e, axis=0, keepdims=True), 1e-30)
            p = e * (1.0 / den)
            ps = p if ps is None else ps + p
            gate = gt_ref[pl.ds(g * (NSA_HPG * 3) + j * 3, 1), :]
            ocmpt_ref[0, 0, 0, :, j * tq:(j + 1) * tq] = (gate * _dot(vct, p.astype(BF16))).astype(ocmpt_ref.dtype)
        ovt = ovt_ref[:, 0:nc]
        hi = ps.astype(BF16)
        r1 = ps - hi.astype(F32)
        mid = r1.astype(BF16)
        lo = (r1 - mid.astype(F32)).astype(BF16)
        imp_ref[...] = _dot(ovt, hi) + _dot(ovt, mid) + _dot(ovt, lo)

    step = nc_all // CMP_VARIANTS
    variant = jnp.minimum(((i + 1) * (tq // CMP_STRIDE) - 1) // step, CMP_VARIANTS - 1)
    for v in range(CMP_VARIANTS):
        pl.when(variant == v)(functools.partial(attend, (v + 1) * step))

    imp = imp_ref[...]
    blk = lax.broadcasted_iota(jnp.int32, imp.shape, 0)
    tt = i * tq + lax.broadcasted_iota(jnp.int32, imp.shape, 1)
    cur = tt // SEL_BLOCK
    forced = (blk == 0) | (blk == cur) | (blk == cur - 1)
    causal = blk * SEL_BLOCK <= tt
    v0 = jnp.where(forced, FORCE_SCORE, jnp.where(causal, imp, -1.0))

    blk1 = blk[:, 0:LANES]

    def pick(_, tiles):
        out = []
        for v in tiles:
            mx = jnp.max(v, axis=0, keepdims=True)
            idx = jnp.min(jnp.where(v == mx, blk1, SEL_LANES), axis=0, keepdims=True)
            out.append(jnp.where(blk1 == idx, -jnp.inf, v))
        return tuple(out)

    tiles = lax.fori_loop(0, SEL_TOPN, pick, tuple(v0[:, u * LANES:(u + 1) * LANES] for u in range(tq // LANES)))
    for u, v in enumerate(tiles):
        sel_ref[0, 0, u * LANES:(u + 1) * LANES, :] = jnp.where(v == -jnp.inf, 1.0, 0.0).T.astype(sel_ref.dtype)


def _nsa_cmp_attn(proj, kc, vct, ovt, B, S):
    tq = NSA_TQ
    nq = S // tq
    G = NSA_KV_GROUPS
    NC = kc.shape[2]
    qw = NSA_HPG * HEAD_SLOT
    qblk = C_QEXT // qw
    kern = functools.partial(_cmp_attn_kernel, tq=tq)
    return pl.pallas_call(
        kern,
        grid=(B, G, nq),
        in_specs=[pl.BlockSpec((tq, qw), lambda b, g, i: (b * nq + i, qblk + g)),
                  pl.BlockSpec((1, 1, NC, HEAD_SLOT), lambda b, g, i: (b, g, 0, 0)),
                  pl.BlockSpec((1, 1, NSA_HEAD_DIM, NC), lambda b, g, i: (b, g, 0, 0)),
                  pl.BlockSpec(ovt.shape, lambda b, g, i: (0, 0)),
                  pl.BlockSpec((tq, LANES), lambda b, g, i: (b * nq + i, C_SMALL // LANES))],
        out_specs=[pl.BlockSpec((1, 1, 1, NSA_HEAD_DIM, NSA_HPG * tq), lambda b, g, i: (b, g, i, 0, 0)),
                   pl.BlockSpec((1, 1, tq, SEL_LANES), lambda b, g, i: (b, g, i, 0))],
        out_shape=[jax.ShapeDtypeStruct((B, G, nq, NSA_HEAD_DIM, NSA_HPG * tq), BF16),
                   jax.ShapeDtypeStruct((B, G, S, SEL_LANES), BF16)],
        scratch_shapes=[pltpu.VMEM((LANES, tq), F32), pltpu.VMEM((SEL_LANES, tq), F32)],
        compiler_params=_cparams(("parallel", "parallel", "parallel")),
        name="nsa_cmp_attn",
    )(proj, kc, vct, ovt, proj)


SEL_TK = 512
NSA_TQ = 512
WIN_TQ = 256
WIN_KEYS = WINDOW + WIN_TQ


HALF = NSA_TQ
NCH = NSA_HPG * NSA_TQ // HALF


def _sel_win_kernel(q_ref, ksel_ref, vselt_ref, kwin_ref, vwint_ref, sel_ref, et_ref, ocmpt_ref,
                    gate_ref, dbias_ref, wbias_ref, o_ref, s_ref, m_ref, acc_ref, gt_ref):
    g = pl.program_id(1)
    i = pl.program_id(2)
    tq = NSA_TQ
    t0 = i * tq
    q2 = _stack_heads(q_ref[...])
    selm1 = sel_ref[0, 0] - 1.0
    qx = jnp.concatenate([q2, jnp.concatenate([selm1] * NSA_HPG, axis=0)], axis=1)
    qxh = [qx[c * HALF:(c + 1) * HALF] for c in range(NCH)]
    q2h = [q2[c * HALF:(c + 1) * HALF] for c in range(NCH)]
    kpb = SEL_TK // LANES
    dv = NSA_HEAD_DIM

    wq = WIN_TQ
    nsub = tq // wq
    win_q, win_k, win_v, win_b = [], [], [], []
    for h in range(nsub):
        u0 = t0 + h * wq
        wblk = jnp.maximum((u0 - WINDOW) // LANES, 0)
        start = pl.multiple_of(wblk * LANES, LANES)
        win_k.append(kwin_ref[pl.ds(start, WIN_KEYS), :])
        win_v.append(jnp.concatenate([vwint_ref[wblk + u] for u in range(WIN_KEYS // LANES)], axis=1))
        win_b.append(wbias_ref[jnp.minimum(i * nsub + h, WINDOW // wq)].astype(F32))
        win_q.append(jnp.concatenate([q2[j * tq + h * wq:j * tq + (h + 1) * wq] for j in range(NSA_HPG)], axis=0))
    sws = [_dot_nt(win_k[h], win_q[h]) for h in range(nsub)]

    def key_tile(kv):
        off = pl.multiple_of(kv * SEL_TK, SEL_TK)
        return jnp.concatenate([ksel_ref[pl.ds(off, SEL_TK), :], et_ref[pl.ds(off, SEL_TK), :]], axis=1)

    def scores_into(kv):
        kx = key_tile(kv)
        for c in range(NCH):
            s_ref[:, c * HALF:(c + 1) * HALF] = _dot_nt(kx, qxh[c])

    def update_cols(cols, vt, s):
        m_old = m_ref[:, cols]
        m_new = jnp.maximum(m_old, jnp.max(s, axis=0, keepdims=True))
        p = jnp.exp2(s - m_new).astype(BF16)
        acc_ref[:, cols] = jnp.exp2(m_old - m_new) * acc_ref[:, cols] + _dot(vt, p)
        m_ref[:, cols] = m_new

    def update_chunk(c, vt, s):
        update_cols(slice(c * HALF, (c + 1) * HALF), vt, s)

    def update(vt, s_chunks):
        for c in range(NCH):
            update_chunk(c, vt, s_chunks[c])

    def sel_values(kv):
        return jnp.concatenate([vselt_ref[kv * kpb + u] for u in range(kpb)], axis=1)

    def load_scores():
        return [s_ref[:, c * HALF:(c + 1) * HALF] for c in range(NCH)]

    m_ref[...] = jnp.full(m_ref.shape, MASK_VALUE, F32)
    acc_ref[...] = jnp.zeros(acc_ref.shape, F32)
    kd = t0 // SEL_TK
    scores_into(0)
    ot_win = []
    for h in range(nsub):
        sw = sws[h] + jnp.concatenate([win_b[h]] * NSA_HPG, axis=1)
        mw = jnp.max(sw, axis=0, keepdims=True)
        accw = _dot(win_v[h], jnp.exp2(sw - mw).astype(BF16))
        ot_win.append(accw[0:dv] * (1.0 / jnp.maximum(accw[dv:dv + 1], 1e-30)))

    def body(kv, carry):
        kx = key_tile(kv + 1)
        vt = sel_values(kv)
        pending = None
        for c in range(NCH):
            cols = slice(c * HALF, (c + 1) * HALF)
            s = s_ref[:, cols]
            s_ref[:, cols] = _dot_nt(kx, qxh[c])
            if pending is not None:
                update_chunk(c - 1, vt, pending)
            pending = s
        update_chunk(NCH - 1, vt, pending)
        return carry

    lax.fori_loop(0, kd, body, 0)
    dbias = dbias_ref[i % (SEL_TK // tq)].astype(F32)
    vt = sel_values(kd)
    if SEL_TK == tq and tq % (2 * LANES) == 0:
        hq = tq // 2
        for c in range(NCH):
            for j0 in range(c * HALF, (c + 1) * HALF, tq):
                update_cols(slice(j0, j0 + hq), vt[:, 0:hq], s_ref[0:hq, j0:j0 + hq] + dbias[0:hq, 0:hq])
                update_cols(slice(j0 + hq, j0 + tq), vt, s_ref[:, j0 + hq:j0 + tq] + dbias[:, hq:tq])
    else:
        update(vt, [sc + dbias for sc in load_scores()])
    acc = acc_ref[...]
    ot_slc = acc[0:dv] * (1.0 / jnp.maximum(acc[dv:dv + 1], 1e-30))

    gt_ref[...] = _sigmoid(gate_ref[...].astype(F32)).T
    ots = []
    for j in range(NSA_HPG):
        base = g * (NSA_HPG * 3) + j * 3
        ots.append(ocmpt_ref[0, 0, 0, :, j * tq:(j + 1) * tq].astype(F32)
                   + gt_ref[pl.ds(base + 1, 1), :] * ot_slc[:, j * tq:(j + 1) * tq]
                   + gt_ref[pl.ds(base + 2, 1), :]
                   * jnp.concatenate([ot_win[h][:, j * wq:(j + 1) * wq] for h in range(nsub)], axis=1))
    for jp in range(NSA_HPG // 2):
        pair = jnp.concatenate([ots[2 * jp], ots[2 * jp + 1]], axis=0)
        o_ref[:, jp * LANES:(jp + 1) * LANES] = pair.T.astype(o_ref.dtype)


def _diag_bias():
    r = jnp.arange(SEL_TK)[None, :, None]
    c = jnp.arange(NSA_TQ)[None, None, :]
    off = (jnp.arange(SEL_TK // NSA_TQ) * NSA_TQ)[:, None, None]
    return jnp.where(r <= off + c, 0.0, MASK_VALUE).astype(BF16)


def _window_bias():
    r = jnp.arange(WIN_KEYS)[None, :, None]
    c = jnp.arange(WIN_TQ)[None, None, :]
    off = jnp.minimum(jnp.arange(WINDOW // WIN_TQ + 1) * WIN_TQ, WINDOW)[:, None, None]
    diff = off + c - r
    return jnp.where((diff >= 0) & (diff < WINDOW), 0.0, MASK_VALUE).astype(BF16)


def _nsa_sel_win(proj, vsel_t, vwin_t, sel, et, ocmpt, B, S):
    T = B * S
    tq = NSA_TQ
    assert HALF == tq
    dbias = _diag_bias()
    wbias = _window_bias()
    nq = S // tq
    G = NSA_KV_GROUPS
    qw = NSA_HPG * HEAD_SLOT
    qblk = C_QEXT // qw
    vsel_t = vsel_t.reshape(G, B, S // LANES, VT_ROWS, LANES)
    vwin_t = vwin_t.reshape(G, B, S // LANES, VT_ROWS, LANES)
    ow = NSA_HPG * NSA_HEAD_DIM
    kv_spec = lambda c: pl.BlockSpec((S, LANES), lambda b, g, i: (b, c // LANES))
    vt_spec = pl.BlockSpec((None, None, S // LANES, VT_ROWS, LANES), lambda b, g, i: (g, b, 0, 0, 0))
    return pl.pallas_call(
        _sel_win_kernel,
        grid=(B, G, nq),
        in_specs=[pl.BlockSpec((tq, qw), lambda b, g, i: (b * nq + i, qblk + g)),
                  kv_spec(C_KSEL), vt_spec, kv_spec(C_KWIN), vt_spec,
                  pl.BlockSpec((1, 1, tq, SEL_LANES), lambda b, g, i: (b, g, i, 0)),
                  pl.BlockSpec(et.shape, lambda b, g, i: (0, 0)),
                  pl.BlockSpec((1, 1, 1, NSA_HEAD_DIM, NSA_HPG * tq), lambda b, g, i: (b, g, i, 0, 0)),
                  pl.BlockSpec((tq, LANES), lambda b, g, i: (b * nq + i, C_SMALL // LANES)),
                  pl.BlockSpec(dbias.shape, lambda b, g, i: (0, 0, 0)),
                  pl.BlockSpec(wbias.shape, lambda b, g, i: (0, 0, 0))],
        out_specs=pl.BlockSpec((tq, ow), lambda b, g, i: (b * nq + i, g)),
        out_shape=jax.ShapeDtypeStruct((T, NSA_Q_W), BF16),
        scratch_shapes=[pltpu.VMEM((SEL_TK, NSA_HPG * tq), F32),
                        pltpu.VMEM((1, NSA_HPG * tq), F32),
                        pltpu.VMEM((VT_ROWS, NSA_HPG * tq), F32),
                        pltpu.VMEM((LANES, tq), F32)],
        compiler_params=_cparams(("parallel", "parallel", "parallel")),
        name="nsa_sel_win",
    )(proj, proj, vsel_t, proj, vwin_t, sel, et, ocmpt, proj, dbias, wbias)


TAIL_ROWS = 8


def _causal_conv(x, tail_ref, w, b):
    L, C = x.shape
    nv = L // TAIL_ROWS
    xx = jnp.concatenate([tail_ref[...], x], axis=0).reshape(nv + 1, TAIL_ROWS, C)
    sub = lax.broadcasted_iota(jnp.int32, (nv, TAIL_ROWS, C), 1)
    y = b + w[CONV_WIDTH - 1:CONV_WIDTH] * x
    for k in range(1, CONV_WIDTH):
        rot = pltpu.roll(xx, k, axis=1)
        shifted = jnp.where(sub >= k, rot[1:], rot[:-1]).reshape(L, C)
        y = y + w[CONV_WIDTH - 1 - k:CONV_WIDTH - k] * shifted
    tail_ref[...] = x[L - TAIL_ROWS:L]
    return y


SSD_CPS = 2


def _ssd_kernel(z_ref, xbc_ref, dtc_ref, dtr_ref, cw_ref, cb_ref, dtbc_ref, dtbr_ref, alc_ref, alr_ref,
                dsk_ref, nw_ref, o_ref, state_ref, tail_ref, y_ref):
    c = pl.program_id(1)
    L = SSD_CHUNK
    P = SSD_HEAD_DIM
    N = SSD_STATE
    hpg = SSD_HEADS // SSD_GROUPS

    @pl.when(c == 0)
    def _():
        state_ref[...] = jnp.zeros_like(state_ref)
        tail_ref[...] = jnp.zeros_like(tail_ref)

    conv = _causal_conv(xbc_ref[...].astype(F32), tail_ref, cw_ref[...], cb_ref[...])
    xbc = conv * _sigmoid(conv)
    dt_c = _softplus(dtc_ref[...] + dtbc_ref[...])
    dt_r = _softplus(dtr_ref[...] + dtbr_ref[...])
    a_c = dt_c * (-jnp.exp(alc_ref[...]))
    a_r = dt_r * (-jnp.exp(alr_ref[...]))
    ii = lax.broadcasted_iota(jnp.int32, (L, L), 0)
    jj = lax.broadcasted_iota(jnp.int32, (L, L), 1)
    tri = ii >= jj
    tril = tri.astype(F32)
    triu = (ii <= jj).astype(F32)

    for cc in range(SSD_CPS):
        rs = slice(cc * L, (cc + 1) * L)
        xs = xbc[rs, 0:SSD_INNER]
        bm = xbc[rs, SSD_INNER:SSD_INNER + SSD_GROUPS * N]
        cm = xbc[rs, SSD_INNER + SSD_GROUPS * N:SSD_INNER + 2 * SSD_GROUPS * N]
        acum_c = _dot_f32(tril, a_c[rs])
        acum_r = _dot_f32(a_r[:, rs], triu)
        for g in range(SSD_GROUPS):
            bg = bm[:, g * N:(g + 1) * N]
            cgb = cm[:, g * N:(g + 1) * N].astype(BF16)
            cb = _dot_nt(cgb, bg.astype(BF16))
            for j in range(hpg):
                h = g * hpg + j
                acb = jnp.broadcast_to(acum_c[:, h:h + 1], (L, L))
                a_last = acum_c[L - 1:L, h:h + 1]
                xh_raw = xs[:, h * P:(h + 1) * P]
                xh = (xh_raw * jnp.broadcast_to(dt_c[rs, h:h + 1], (L, P))).astype(BF16)
                lmat = jnp.where(tri, jnp.exp(acb - acum_r[h:h + 1, :]), 0.0)
                y = _dot((cb * lmat).astype(BF16), xh)
                prev = state_ref[h]
                y = y + _dot(cgb, prev.astype(BF16)) * jnp.exp(acb[:, 0:P])
                bd = (bg * jnp.exp(a_last - acb[:, 0:N])).astype(BF16)
                state_ref[h] = jnp.exp(a_last) * prev + _dot_tn(bd, xh)
                y_ref[rs, h * P:(h + 1) * P] = y + dsk_ref[:, h * P:(h + 1) * P] * xh_raw

    zf = z_ref[...].astype(F32)
    yg = y_ref[...] * (zf * _sigmoid(zf))
    ms = jnp.mean(yg * yg, axis=-1, keepdims=True)
    o_ref[...] = (yg * lax.rsqrt(ms + RMS_EPS) * nw_ref[...]).astype(o_ref.dtype)


def _ssd_mixer(proj, dt_col, dt_row, conv_w, conv_b, dt_bias, a_log, d_skip, norm_w, B, S):
    T = B * S
    L = SSD_CPS * SSD_CHUNK
    nc = S // L
    H = SSD_HEADS
    full = lambda a: pl.BlockSpec(a.shape, lambda b, c: (0, 0))
    cb2 = conv_b.reshape(1, -1)
    dtb_c = dt_bias.reshape(1, H)
    dtb_r = dt_bias.reshape(H, 1)
    al_c = a_log.reshape(1, H)
    al_r = a_log.reshape(H, 1)
    dsk = jnp.repeat(d_skip, SSD_HEAD_DIM).reshape(1, SSD_INNER)
    nw = norm_w.reshape(1, SSD_INNER)
    return pl.pallas_call(
        _ssd_kernel,
        grid=(B, nc),
        in_specs=[pl.BlockSpec((L, SSD_INNER), lambda b, c: (b * nc + c, C_SSDZ // SSD_INNER)),
                  pl.BlockSpec((L, SSD_XBC_W), lambda b, c: (b * nc + c, C_XBC // SSD_XBC_W)),
                  pl.BlockSpec((L, H), lambda b, c: (b * nc + c, 0)),
                  pl.BlockSpec((H, L), lambda b, c: (0, b * nc + c)),
                  full(conv_w), full(cb2), full(dtb_c), full(dtb_r), full(al_c), full(al_r),
                  full(dsk), full(nw)],
        out_specs=pl.BlockSpec((L, SSD_INNER), lambda b, c: (b * nc + c, 0)),
        out_shape=jax.ShapeDtypeStruct((T, SSD_INNER), BF16),
        scratch_shapes=[pltpu.VMEM((H, SSD_STATE, SSD_HEAD_DIM), F32),
                        pltpu.VMEM((TAIL_ROWS, SSD_XBC_W), F32),
                        pltpu.VMEM((L, SSD_INNER), F32)],
        compiler_params=_cparams(("parallel", "arbitrary")),
        name="ssd_mixer",
    )(proj, proj, dt_col, dt_row, conv_w, cb2, dtb_c, dtb_r, al_c, al_r, dsk, nw)


SCAN_GROUP = 8


def _lru_kernel(x_ref, y_ref, cw_ref, cb_ref, wa_ref, ba_ref, wx_ref, bx_ref, lam_ref, o_ref,
                h_ref, tail_ref, *, tc):
    c = pl.program_id(1)

    @pl.when(c == 0)
    def _():
        h_ref[...] = jnp.zeros_like(h_ref)
        tail_ref[...] = jnp.zeros_like(tail_ref)

    xr = _causal_conv(x_ref[...].astype(F32), tail_ref, cw_ref[...], cb_ref[...])
    xrb = xr.astype(BF16)
    r = _sigmoid(_dot(xrb, wa_ref[...]) + ba_ref[...])
    ig = _sigmoid(_dot(xrb, wx_ref[...]) + bx_ref[...])
    log_a = -LRU_C * r * _softplus(-lam_ref[...])
    a = jnp.exp(log_a)
    b = jnp.sqrt(1.0 - jnp.exp(2.0 * log_a)) * (ig * xr)
    ng = tc // SCAN_GROUP
    a = a.reshape(ng, SCAN_GROUP, a.shape[1])
    b = b.reshape(ng, SCAN_GROUP, b.shape[1])
    sub = lax.broadcasted_iota(jnp.int32, a.shape, 1)
    k = 1
    while k < SCAN_GROUP:
        keep = sub >= k
        a_s = jnp.where(keep, pltpu.roll(a, k, axis=1), 1.0)
        b_s = jnp.where(keep, pltpu.roll(b, k, axis=1), 0.0)
        b = a * b_s + b
        a = a * a_s
        k *= 2
    carry = h_ref[0:1, :]
    groups = []
    for rg in range(ng):
        hg = a[rg] * carry + b[rg]
        groups.append(hg)
        carry = hg[SCAN_GROUP - 1:SCAN_GROUP]
    h = jnp.concatenate(groups, axis=0)
    h_ref[...] = jnp.broadcast_to(carry, h_ref.shape)
    o_ref[...] = (h * _gelu_tanh(y_ref[...].astype(F32))).astype(o_ref.dtype)


def _block_diag(w):
    nb, c, d = w.shape
    eye = jnp.eye(nb, dtype=w.dtype)
    return (eye[:, None, :, None] * w[:, :, None, :]).reshape(nb * c, nb * d)


def _lru_mixer(proj, conv_w, conv_b, wa, ba, wx, bx, lam, B, S):
    T = B * S
    tc = min(512, S)
    nt = S // tc
    W = LRU_WIDTH
    wa_bd = _block_diag(wa).astype(BF16)
    wx_bd = _block_diag(wx).astype(BF16)
    row = lambda v: v.reshape(1, W)
    full = lambda a: pl.BlockSpec(a.shape, lambda b, c: (0, 0))
    args = (conv_w, row(conv_b), wa_bd, row(ba), wx_bd, row(bx), row(lam))
    return pl.pallas_call(
        functools.partial(_lru_kernel, tc=tc),
        grid=(B, nt),
        in_specs=[pl.BlockSpec((tc, W), lambda b, c: (b * nt + c, C_LRUX // W)),
                  pl.BlockSpec((tc, W), lambda b, c: (b * nt + c, C_LRUY // W))]
                 + [full(a) for a in args],
        out_specs=pl.BlockSpec((tc, W), lambda b, c: (b * nt + c, 0)),
        out_shape=jax.ShapeDtypeStruct((T, W), BF16),
        scratch_shapes=[pltpu.VMEM((TAIL_ROWS, W), F32), pltpu.VMEM((TAIL_ROWS, W), F32)],
        compiler_params=_cparams(("parallel", "arbitrary")),
        name="lru_mixer",
    )(proj, proj, *args)


def _merge_kernel(x_ref, mg_ref, on_ref, os_ref, ol_ref, pn_ref, ps_ref, pl_ref, wo_ref, g_ref, b_ref,
                  of_ref, ob_ref):
    d = D_MODEL
    gate = _sigmoid(mg_ref[...].astype(F32))
    mixed = (gate[:, 0:d] * _dot(on_ref[...], pn_ref[...])
             + gate[:, d:2 * d] * _dot(os_ref[...], ps_ref[...])
             + gate[:, 2 * d:3 * d] * _dot(ol_ref[...], pl_ref[...]))
    v = ALPHA * x_ref[...] + _dot(mixed.astype(BF16), wo_ref[...])
    out = _layer_norm(v, g_ref[...], b_ref[...])
    of_ref[...] = out
    ob_ref[...] = out.astype(BF16)


def _merge(x, proj, o_nsa, o_ssd, o_lru, pn, ps, plru, wo, g, b, layer):
    T = x.shape[0]
    tm = min(512, T)
    d = D_MODEL
    rowblk = lambda w: pl.BlockSpec((tm, w), lambda i: (i, 0))
    full = lambda a: pl.BlockSpec((None,) + a.shape[1:], lambda i: (layer, 0, 0))
    g2, b2 = g.reshape(-1, 1, d), b.reshape(-1, 1, d)
    return pl.pallas_call(
        _merge_kernel,
        grid=(T // tm,),
        in_specs=[rowblk(d), pl.BlockSpec((tm, 3 * d), lambda i: (i, C_MERGE // (3 * d))),
                  rowblk(o_nsa.shape[1]), rowblk(o_ssd.shape[1]), rowblk(o_lru.shape[1]),
                  full(pn), full(ps), full(plru), full(wo), full(g2), full(b2)],
        out_specs=[rowblk(d), rowblk(d)],
        out_shape=[jax.ShapeDtypeStruct((T, d), F32), jax.ShapeDtypeStruct((T, d), BF16)],
        compiler_params=_cparams(("parallel",)),
        name="merge_ln",
    )(x, proj, o_nsa, o_ssd, o_lru, pn, ps, plru, wo, g2, b2)


def _route(sel, aff):
    epg = EXPERTS_PER_GROUP
    scores = []
    for gi in range(N_EXPERT_GROUPS):
        v = sel[gi * epg:(gi + 1) * epg]
        pair = None
        for a in range(epg):
            for b in range(a + 1, epg):
                sab = v[a] + v[b]
                pair = sab if pair is None else jnp.maximum(pair, sab)
        scores.append(pair)
    best = jnp.zeros_like(scores[0], dtype=jnp.int32)
    best_s = scores[0]
    for gi in range(1, N_EXPERT_GROUPS):
        better = scores[gi] > best_s
        best = jnp.where(better, gi, best)
        best_s = jnp.where(better, scores[gi], best_s)
    chosen = []
    for k in range(N_EXPERTS):
        gi = k // epg
        rank = jnp.zeros_like(best)
        for o in range(gi * epg, (gi + 1) * epg):
            if o == k:
                continue
            ahead = (sel[o] > sel[k]) | ((sel[o] == sel[k]) & (o < k))
            rank = rank + ahead.astype(jnp.int32)
        chosen.append((best == gi) & (rank < TOP_K))
    wsum = None
    for k in range(N_EXPERTS):
        wk = jnp.where(chosen[k], aff[k], 0.0)
        wsum = wk if wsum is None else wsum + wk
    inv = 1.0 / wsum
    return [jnp.where(chosen[k], aff[k], 0.0) * inv for k in range(N_EXPERTS)]


EXPERT_PAIRS = N_EXPERTS // 2
PAIRS_PER_ITER = 8


def _moe_kernel(xb_ref, xf_ref, p_ref, rw_ref, rb_ref, pg_ref, pp_ref, wg_ref, wu_ref, wd_ref,
                g_ref, b_ref, of_ref, ob_ref, acc_ref, gates_ref):
    xb = xb_ref[...]
    tm = xb.shape[0]
    logits = _dot_nt(rw_ref[...], xb)
    aff = _sigmoid(logits)
    sel = aff + rb_ref[...]
    gate_rows = _route([sel[k:k + 1, :] for k in range(N_EXPERTS)],
                       [aff[k:k + 1, :] for k in range(N_EXPERTS)])
    gt = jnp.concatenate(gate_rows + [jnp.zeros((LANES - N_EXPERTS, tm), F32)], axis=0)
    gates_ref[...] = gt.T
    acc_ref[...] = _sigmoid(_dot(xb, pg_ref[...])) * _dot(p_ref[...].astype(BF16), pp_ref[...])
    lane = lax.broadcasted_iota(jnp.int32, (tm, LANES), 1)

    def gate_cols(k):
        gates = gates_ref[...]
        cols = [jnp.broadcast_to(jnp.sum(jnp.where(lane == 2 * k + u, gates, 0.0), axis=-1, keepdims=True),
                                 (tm, D_EXPERT)) for u in range(2)]
        return jnp.concatenate(cols, axis=1)

    def step(it, carry):
        ks = [it * PAIRS_PER_ITER + u for u in range(PAIRS_PER_ITER)]
        hgs = [[_dot(xb, wg_ref[2 * k + u]) for u in range(2)] for k in ks]
        hus = [[_dot(xb, wu_ref[2 * k + u]) for u in range(2)] for k in ks]
        for k, hg2, hu2 in zip(ks, hgs, hus):
            hg = jnp.concatenate(hg2, axis=1)
            h = (hg * _sigmoid(hg)) * jnp.concatenate(hu2, axis=1) * gate_cols(k)
            acc_ref[...] += _dot(h.astype(BF16), wd_ref[k])
        return carry

    lax.fori_loop(0, EXPERT_PAIRS // PAIRS_PER_ITER, step, 0)
    out = _layer_norm(ALPHA * xf_ref[...] + acc_ref[...], g_ref[...], b_ref[...])
    of_ref[...] = out
    ob_ref[...] = out.astype(BF16)


def _moe_ple(xb, xf, p, layer, rw_t, rb, pg, pp, wg, wu, wd, g, b):
    T = xb.shape[0]
    tm = min(512, T)
    d = D_MODEL
    rowblk = lambda w: pl.BlockSpec((tm, w), lambda i: (i, 0))
    once = pl.Buffered(1)
    full = lambda a: pl.BlockSpec(a.shape, lambda i: (0,) * a.ndim, pipeline_mode=once)
    lyr = lambda a: pl.BlockSpec((None,) + a.shape[1:], lambda i: (layer,) + (0,) * (a.ndim - 1), pipeline_mode=once)
    g2, b2 = g.reshape(-1, 1, d), b.reshape(-1, 1, d)
    return pl.pallas_call(
        _moe_kernel,
        grid=(T // tm,),
        in_specs=[rowblk(d), rowblk(d), pl.BlockSpec((None, tm, PLE_DIM), lambda i: (layer, i, 0)),
                  full(rw_t), full(rb), lyr(pg), lyr(pp),
                  lyr(wg), lyr(wu), lyr(wd), lyr(g2), lyr(b2)],
        out_specs=[rowblk(d), rowblk(d)],
        out_shape=[jax.ShapeDtypeStruct((T, d), F32), jax.ShapeDtypeStruct((T, d), BF16)],
        scratch_shapes=[pltpu.VMEM((tm, d), F32), pltpu.VMEM((tm, LANES), F32)],
        compiler_params=_cparams(("parallel",)),
        name="moe_ple_ln",
    )(xb, xf, p, rw_t, rb, pg, pp, wg, wu, wd, g2, b2)


def _overlap_matrix(nc):
    n = jnp.arange(nc)[None, :]
    m = jnp.arange(SEL_LANES)[:, None]
    ratio = SEL_BLOCK // CMP_STRIDE
    ov = jnp.zeros((SEL_LANES, nc), F32)
    for k in range(CMP_BLOCK // CMP_STRIDE):
        ov = ov + ((n + k) // ratio == m).astype(F32)
    return ov.astype(BF16)


def _expand_matrix(S):
    c = jnp.arange(S)[:, None]
    m = jnp.arange(SEL_LANES)[None, :]
    return jnp.where(c // SEL_BLOCK == m, -MASK_VALUE, 0.0).astype(BF16)


def _pad_w2(w2):
    out = jnp.zeros((NSA_KV_GROUPS, CMP_HIDDEN, HEAD_SLOT), F32)
    for g in range(NSA_KV_GROUPS):
        out = out.at[g, :, g * NSA_HEAD_DIM:(g + 1) * NSA_HEAD_DIM].set(w2)
    return out.astype(BF16)


def kernel(x, p, w_in, nsa_pe_k, nsa_w1_k, nsa_w2_k, nsa_pe_v, nsa_w1_v, nsa_w2_v, ssd_conv_w, ssd_conv_b, ssd_dt_bias, ssd_a_log, ssd_d, ssd_norm_w, lru_conv_w, lru_conv_b, lru_wa, lru_ba, lru_wx, lru_bx, lru_lambda, proj_nsa, proj_ssd, proj_lru, w_out, ln1_g, ln1_b, router_w, router_b, exp_w_gate, exp_w_up, exp_w_down, ple_w_gate, ple_w_proj, ln2_g, ln2_b):
    B, S, d = x.shape
    T = B * S
    depth = w_in.shape[0]
    assert d == D_MODEL and S % SEL_TK == 0 and S >= WIN_KEYS and S // SEL_BLOCK <= SEL_LANES
    nr = S // CMP_STRIDE
    ov = _overlap_matrix(nr)
    emat = _expand_matrix(S)
    rw_t = router_w.T.astype(BF16)
    rb = router_b.reshape(N_EXPERTS, 1).astype(F32)

    w_main = _prep_w_in(w_in)
    aux_w = _aux_weights(w_in)
    pn, ps, plru, wo = (a.astype(BF16) for a in (proj_nsa, proj_ssd, proj_lru, w_out))
    pg, pp, wg, wu = (a.astype(BF16) for a in (ple_w_gate, ple_w_proj, exp_w_gate, exp_w_up))
    wd = exp_w_down.reshape(depth, EXPERT_PAIRS, 2 * D_EXPERT, d).astype(BF16)
    p3 = p.reshape(depth, T, PLE_DIM)

    xf = x.reshape(T, d)
    xb = xf
    for i in range(depth):
        proj = _in_proj(xb, w_main, i)
        cmp, vsel_t, vwin_t, dt_col, dt_row = _aux_proj(xb, aux_w, i)

        w1big, pebig = _cmp_first_layer(nsa_w1_k[i], nsa_w1_v[i], nsa_pe_k[i], nsa_pe_v[i])
        kc, vct = _nsa_compress(cmp, w1big, pebig, _pad_w2(nsa_w2_k[i]),
                                nsa_w2_v[i].T.astype(BF16), B, S)
        ocmpt, sel = _nsa_cmp_attn(proj, kc, vct, ov, B, S)
        o_nsa = _nsa_sel_win(proj, vsel_t, vwin_t, sel, emat, ocmpt, B, S)

        o_ssd = _ssd_mixer(proj, dt_col, dt_row, ssd_conv_w[i], ssd_conv_b[i], ssd_dt_bias[i],
                           ssd_a_log[i], ssd_d[i], ssd_norm_w[i], B, S)
        o_lru = _lru_mixer(proj, lru_conv_w[i], lru_conv_b[i], lru_wa[i], lru_ba[i], lru_wx[i],
                           lru_bx[i], lru_lambda[i], B, S)

        xf, xb = _merge(xf, proj, o_nsa, o_ssd, o_lru, pn, ps, plru, wo, ln1_g, ln1_b, i)
        xf, xb = _moe_ple(xb, xf, p3, i, rw_t, rb, pg, pp, wg, wu, wd, ln2_g, ln2_b)
    return xf.reshape(B, S, d)
```

```python
import functools
import math

import jax
import jax.numpy as jnp
from jax import lax
from jax.experimental import pallas as pl
from jax.experimental.pallas import tpu as pltpu

F32 = jnp.float32
BF16 = jnp.bfloat16

D_MODEL = 1024
PLE_DIM = 256
NSA_HEADS = 8
NSA_KV_GROUPS = 2
NSA_HEAD_DIM = 64
NSA_HPG = NSA_HEADS // NSA_KV_GROUPS
NSA_Q_W = NSA_HEADS * NSA_HEAD_DIM
NSA_KV_W = NSA_KV_GROUPS * NSA_HEAD_DIM
CMP_BLOCK = 32
CMP_STRIDE = 16
CMP_HIDDEN = 256
SEL_BLOCK = 64
SEL_TOPN = 16
WINDOW = 512
FORCE_SCORE = 1e4
MASK_VALUE = -1e30
LOG2E = 1.4426950408889634
SSD_HEADS = 8
SSD_HEAD_DIM = 64
SSD_INNER = SSD_HEADS * SSD_HEAD_DIM
SSD_GROUPS = 2
SSD_STATE = 64
SSD_CHUNK = 128
SSD_XBC_W = SSD_INNER + 2 * SSD_GROUPS * SSD_STATE
CONV_WIDTH = 4
LRU_WIDTH = 512
LRU_BLOCKS = 8
LRU_BLOCK_DIM = LRU_WIDTH // LRU_BLOCKS
LRU_C = 8.0
N_EXPERTS = 16
N_EXPERT_GROUPS = 4
EXPERTS_PER_GROUP = N_EXPERTS // N_EXPERT_GROUPS
TOP_K = 2
D_EXPERT = 256
DEPTH = 2
ALPHA = (2 * DEPTH) ** 0.25
LN_EPS = 1e-5
RMS_EPS = 1e-5
IN_SIZES = (NSA_Q_W, NSA_KV_W, NSA_KV_W, NSA_KV_W, NSA_KV_W, NSA_KV_W, NSA_KV_W, NSA_HEADS * 3,
            SSD_INNER, SSD_XBC_W, SSD_HEADS, LRU_WIDTH, LRU_WIDTH, 3 * D_MODEL)

LANES = 128
SEL_LANES = 128
HEAD_SLOT = 128

C_MERGE = 0
C_QEXT = 3072
C_SSDZ = 4096
C_LRUX = 4608
C_LRUY = 5120
C_KSEL = 5632
C_KWIN = 5760
C_SMALL = 5888
C_XBC = 6144
PROJ_W = 6912
PROJ_TN = 2304
GATE_W = NSA_HEADS * 3

VMEM_LIMIT = 56 * 1024 * 1024


def _cparams(sem):
    return pltpu.CompilerParams(dimension_semantics=sem, vmem_limit_bytes=VMEM_LIMIT)


def _sigmoid(x):
    return 1.0 / (1.0 + jnp.exp(-x))


def _softplus(x):
    return jnp.maximum(x, 0.0) + jnp.log(1.0 + jnp.exp(-jnp.abs(x)))


def _gelu_tanh(x):
    c = math.sqrt(2.0 / math.pi)
    return 0.5 * x * (1.0 + jnp.tanh(c * (x + 0.044715 * (x * x * x))))


def _dot(a, b):
    return jnp.dot(a, b, preferred_element_type=F32)


def _dot_nt(a, b):
    return lax.dot_general(a, b, (((1,), (1,)), ((), ())), preferred_element_type=F32)


def _dot_tn(a, b):
    return lax.dot_general(a, b, (((0,), (0,)), ((), ())), preferred_element_type=F32)


def _dot_f32(a, b):
    return jnp.dot(a, b, preferred_element_type=F32, precision=lax.Precision.HIGHEST)


def _layer_norm(v, g, b):
    mu = jnp.mean(v, axis=-1, keepdims=True)
    vc = v - mu
    var = jnp.mean(vc * vc, axis=-1, keepdims=True)
    return vc * lax.rsqrt(var + LN_EPS) * g + b


def _matmul_kernel(x_ref, w_ref, o_ref):
    o_ref[...] = _dot(x_ref[...].astype(BF16), w_ref[...]).astype(o_ref.dtype)


def _in_proj(xb, w, layer):
    T, K = xb.shape
    N = w.shape[2]
    tm = min(1024, T)
    tn = PROJ_TN
    return pl.pallas_call(
        _matmul_kernel,
        grid=(N // tn, T // tm),
        in_specs=[pl.BlockSpec((tm, K), lambda j, i: (i, 0)),
                  pl.BlockSpec((None, K, tn), lambda j, i: (layer, 0, j))],
        out_specs=pl.BlockSpec((tm, tn), lambda j, i: (i, j)),
        out_shape=jax.ShapeDtypeStruct((T, N), BF16),
        compiler_params=_cparams(("parallel", "parallel")),
        name="in_proj",
    )(xb, w)


def _split_w_in(w):
    offs = [0]
    for s in IN_SIZES:
        offs.append(offs[-1] + s)
    return [w[..., offs[k]:offs[k + 1]] for k in range(len(IN_SIZES))]


def _prep_w_in(w):
    pc = _split_w_in(w)
    lead = w.shape[:-1]
    q = pc[0].reshape(*lead, NSA_KV_GROUPS, NSA_HPG, NSA_HEAD_DIM) * (NSA_HEAD_DIM ** -0.5 * LOG2E)
    zeros = jnp.zeros_like(q)
    q_ext = jnp.stack([jnp.concatenate([q[..., 0, :, :], zeros[..., 0, :, :]], axis=-1),
                       jnp.concatenate([zeros[..., 1, :, :], q[..., 1, :, :]], axis=-1)], axis=-3)
    q_ext = q_ext.reshape(*lead, NSA_HEADS * HEAD_SLOT)
    small = jnp.pad(pc[7], [(0, 0)] * len(lead) + [(0, C_XBC - C_SMALL - GATE_W)])
    out = jnp.concatenate([pc[13], q_ext, pc[8], pc[11], pc[12], pc[3], pc[5], small, pc[9]], axis=-1)
    assert out.shape[-1] == PROJ_W
    return out.astype(BF16)


VT_ROWS = NSA_HEAD_DIM + 16


def _aux_proj_kernel(x_ref, wc_ref, wvt_ref, wdt_ref, wdtt_ref, cmp_ref, vst_ref, vwt_ref, dtc_ref, dtr_ref):
    x = x_ref[...].astype(BF16)
    tm = x.shape[0]
    cmp_ref[...] = _dot(x, wc_ref[...]).astype(cmp_ref.dtype)
    vt = _dot_nt(wvt_ref[...], x)
    dv = NSA_HEAD_DIM
    ones = jnp.ones((VT_ROWS - dv, LANES), vst_ref.dtype)
    for u in range(tm // LANES):
        for k, o_ref in enumerate((vst_ref, vwt_ref)):
            for g in range(NSA_KV_GROUPS):
                r0 = (k * NSA_KV_GROUPS + g) * dv
                o_ref[g, u, 0:dv, :] = vt[r0:r0 + dv, u * LANES:(u + 1) * LANES].astype(o_ref.dtype)
                o_ref[g, u, dv:VT_ROWS, :] = ones
    xf = x.astype(F32)
    dtc_ref[...] = _dot(xf, wdt_ref[...])
    dtr_ref[...] = _dot_nt(wdtt_ref[...], xf)


def _aux_weights(w):
    pc = _split_w_in(w)
    wc = jnp.concatenate([pc[1], pc[2]], axis=-1).astype(BF16)
    wvt = jnp.swapaxes(jnp.concatenate([pc[4], pc[6]], axis=-1), -1, -2).astype(BF16)
    return wc, wvt, pc[10], jnp.swapaxes(pc[10], -1, -2)


def _aux_proj(xb, aux_w, layer):
    T, K = xb.shape
    wc, wvt, wdt, wdtt = aux_w
    tm = min(1024, T)
    full = lambda a: pl.BlockSpec((None,) + a.shape[1:], lambda i: (layer, 0, 0))
    vt_shape = jax.ShapeDtypeStruct((NSA_KV_GROUPS, T // LANES, VT_ROWS, LANES), BF16)
    vt_spec = pl.BlockSpec((NSA_KV_GROUPS, tm // LANES, VT_ROWS, LANES), lambda i: (0, i, 0, 0))
    return pl.pallas_call(
        _aux_proj_kernel,
        grid=(T // tm,),
        in_specs=[pl.BlockSpec((tm, K), lambda i: (i, 0)), full(wc), full(wvt), full(wdt), full(wdtt)],
        out_specs=[pl.BlockSpec((tm, 2 * NSA_KV_W), lambda i: (i, 0)), vt_spec, vt_spec,
                   pl.BlockSpec((tm, SSD_HEADS), lambda i: (i, 0)),
                   pl.BlockSpec((SSD_HEADS, tm), lambda i: (0, i))],
        out_shape=[jax.ShapeDtypeStruct((T, 2 * NSA_KV_W), BF16), vt_shape, vt_shape,
                   jax.ShapeDtypeStruct((T, SSD_HEADS), F32), jax.ShapeDtypeStruct((SSD_HEADS, T), F32)],
        compiler_params=_cparams(("parallel",)),
        name="aux_proj",
    )(xb, wc, wvt, wdt, wdtt)


CMP_PIECES = 2 * NSA_KV_GROUPS


def _compress_kernel(r_ref, pe_ref, w1_ref, w2k_ref, w2vt_ref, kc_ref, vct_ref):
    r = r_ref[0]
    nr = r.shape[0]
    u = _dot(r, w1_ref[0])
    v = _dot(r, w1_ref[1])
    c = _dot(pe_ref[0], w1_ref[0]) + _dot(pe_ref[1], w1_ref[1])
    hid = u + pltpu.roll(v, nr - 1, axis=0)
    hid = (hid.reshape(nr // 8, 8, hid.shape[1]) + c[None]).reshape(nr, hid.shape[1])
    act = _gelu_tanh(hid).astype(BF16)
    for g in range(NSA_KV_GROUPS):
        kp, vp = g, NSA_KV_GROUPS + g
        kc_ref[0, g] = _dot(act[:, kp * CMP_HIDDEN:(kp + 1) * CMP_HIDDEN], w2k_ref[g]).astype(kc_ref.dtype)
        vct = _dot_nt(w2vt_ref[...], act[:, vp * CMP_HIDDEN:(vp + 1) * CMP_HIDDEN])
        vct_ref[0, g] = vct.astype(vct_ref.dtype)


def _cmp_first_layer(w1k, w1v, pek, pev):
    half = CMP_BLOCK // 2
    blocks, pes = [], []
    for p, (w1, pe1) in enumerate(((w1k, pek), (w1k, pek), (w1v, pev), (w1v, pev))):
        wp = w1.reshape(2, half, NSA_HEAD_DIM, CMP_HIDDEN).astype(BF16)
        blocks.append(jnp.pad(wp, ((0, 0), (0, 0), (0, 0), (p * CMP_HIDDEN, (CMP_PIECES - 1 - p) * CMP_HIDDEN))))
        pes.append(pe1.reshape(2, half, NSA_HEAD_DIM).astype(BF16))
    rows = half * CMP_PIECES * NSA_HEAD_DIM
    big = jnp.concatenate(blocks, axis=2).reshape(2, rows, CMP_PIECES * CMP_HIDDEN)
    pe = jnp.concatenate(pes, axis=2).reshape(2, 1, rows)
    return big, jnp.broadcast_to(pe, (2, 8, rows))


def _nsa_compress(cmp, w1big, pebig, w2k, w2vt, B, S):
    NR = S // CMP_STRIDE
    W = CMP_STRIDE * 2 * NSA_KV_W
    r = cmp.reshape(B, NR, W)
    once = pl.Buffered(1)
    full = lambda a: pl.BlockSpec(a.shape, lambda b: (0,) * a.ndim, pipeline_mode=once)
    return pl.pallas_call(
        _compress_kernel,
        grid=(B,),
        in_specs=[pl.BlockSpec((1, NR, W), lambda b: (b, 0, 0)), full(pebig), full(w1big), full(w2k), full(w2vt)],
        out_specs=[pl.BlockSpec((1, NSA_KV_GROUPS, NR, HEAD_SLOT), lambda b: (b, 0, 0, 0)),
                   pl.BlockSpec((1, NSA_KV_GROUPS, NSA_HEAD_DIM, NR), lambda b: (b, 0, 0, 0))],
        out_shape=[jax.ShapeDtypeStruct((B, NSA_KV_GROUPS, NR, HEAD_SLOT), BF16),
                   jax.ShapeDtypeStruct((B, NSA_KV_GROUPS, NSA_HEAD_DIM, NR), BF16)],
        compiler_params=_cparams(("parallel",)),
        name="nsa_compress",
    )(r, pebig, w1big, w2k, w2vt)


def _stack_heads(q):
    return jnp.concatenate([q[:, j * HEAD_SLOT:(j + 1) * HEAD_SLOT] for j in range(NSA_HPG)], axis=0)


CMP_VARIANTS = 4


def _cmp_attn_kernel(q_ref, kc_ref, vct_ref, ovt_ref, gate_ref, ocmpt_ref, sel_ref, gt_ref, imp_ref, *, tq):
    g = pl.program_id(1)
    i = pl.program_id(2)
    q2 = _stack_heads(q_ref[...])
    nc_all = kc_ref.shape[2]
    gt_ref[...] = _sigmoid(gate_ref[...].astype(F32)).T

    def attend(nc):
        kc = kc_ref[0, 0, 0:nc, :]
        vct = vct_ref[0, 0, :, 0:nc]
        sts = [_dot_nt(kc, q2[j * tq:(j + 1) * tq]) for j in range(NSA_HPG)]
        n = lax.broadcasted_iota(jnp.int32, (nc, tq), 0)
        t = i * tq + lax.broadcasted_iota(jnp.int32, (nc, tq), 1)
        mask = n * CMP_STRIDE + (CMP_BLOCK - 1) <= t
        ps = None
        for j in range(NSA_HPG):
            s = jnp.where(mask, sts[j], MASK_VALUE)
            m = jnp.max(s, axis=0, keepdims=True)
            e = jnp.where(mask, jnp.exp2(s - m), 0.0)
            den = jnp.maximum(jnp.sum(e, axis=0, keepdims=True), 1e-30)
            p = e * (1.0 / den)
            ps = p if ps is None else ps + p
            gate = gt_ref[pl.ds(g * (NSA_HPG * 3) + j * 3, 1), :]
            ocmpt_ref[0, 0, 0, :, j * tq:(j + 1) * tq] = (gate * _dot(vct, p.astype(BF16))).astype(ocmpt_ref.dtype)
        ovt = ovt_ref[:, 0:nc]
        hi = ps.astype(BF16)
        r1 = ps - hi.astype(F32)
        mid = r1.astype(BF16)
        lo = (r1 - mid.astype(F32)).astype(BF16)
        imp_ref[...] = _dot(ovt, hi) + _dot(ovt, mid) + _dot(ovt, lo)

    step = nc_all // CMP_VARIANTS
    variant = jnp.minimum(((i + 1) * (tq // CMP_STRIDE) - 1) // step, CMP_VARIANTS - 1)
    for v in range(CMP_VARIANTS):
        pl.when(variant == v)(functools.partial(attend, (v + 1) * step))

    imp = imp_ref[...]
    blk = lax.broadcasted_iota(jnp.int32, imp.shape, 0)
    tt = i * tq + lax.broadcasted_iota(jnp.int32, imp.shape, 1)
    cur = tt // SEL_BLOCK
    forced = (blk == 0) | (blk == cur) | (blk == cur - 1)
    causal = blk * SEL_BLOCK <= tt
    v0 = jnp.where(forced, FORCE_SCORE, jnp.where(causal, imp, -1.0))

    blk1 = blk[:, 0:LANES]

    def pick(_, tiles):
        out = []
        for v in tiles:
            mx = jnp.max(v, axis=0, keepdims=True)
            idx = jnp.min(jnp.where(v == mx, blk1, SEL_LANES), axis=0, keepdims=True)
            out.append(jnp.where(blk1 == idx, -jnp.inf, v))
        return tuple(out)

    tiles = lax.fori_loop(0, SEL_TOPN, pick, tuple(v0[:, u * LANES:(u + 1) * LANES] for u in range(tq // LANES)),
                          unroll=True)
    for u, v in enumerate(tiles):
        sel_ref[0, 0, u * LANES:(u + 1) * LANES, :] = jnp.where(v == -jnp.inf, 1.0, 0.0).T.astype(sel_ref.dtype)


def _nsa_cmp_attn(proj, kc, vct, ovt, B, S):
    tq = NSA_TQ
    nq = S // tq
    G = NSA_KV_GROUPS
    NC = kc.shape[2]
    qw = NSA_HPG * HEAD_SLOT
    qblk = C_QEXT // qw
    kern = functools.partial(_cmp_attn_kernel, tq=tq)
    return pl.pallas_call(
        kern,
        grid=(B, G, nq),
        in_specs=[pl.BlockSpec((tq, qw), lambda b, g, i: (b * nq + i, qblk + g)),
                  pl.BlockSpec((1, 1, NC, HEAD_SLOT), lambda b, g, i: (b, g, 0, 0)),
                  pl.BlockSpec((1, 1, NSA_HEAD_DIM, NC), lambda b, g, i: (b, g, 0, 0)),
                  pl.BlockSpec(ovt.shape, lambda b, g, i: (0, 0)),
                  pl.BlockSpec((tq, LANES), lambda b, g, i: (b * nq + i, C_SMALL // LANES))],
        out_specs=[pl.BlockSpec((1, 1, 1, NSA_HEAD_DIM, NSA_HPG * tq), lambda b, g, i: (b, g, i, 0, 0)),
                   pl.BlockSpec((1, 1, tq, SEL_LANES), lambda b, g, i: (b, g, i, 0))],
        out_shape=[jax.ShapeDtypeStruct((B, G, nq, NSA_HEAD_DIM, NSA_HPG * tq), BF16),
                   jax.ShapeDtypeStruct((B, G, S, SEL_LANES), BF16)],
        scratch_shapes=[pltpu.VMEM((LANES, tq), F32), pltpu.VMEM((SEL_LANES, tq), F32)],
        compiler_params=_cparams(("parallel", "parallel", "parallel")),
        name="nsa_cmp_attn",
    )(proj, kc, vct, ovt, proj)


SEL_TK = 512
NSA_TQ = 512
WIN_TQ = 256
WIN_KEYS = WINDOW + WIN_TQ


HALF = NSA_TQ
NCH = NSA_HPG * NSA_TQ // HALF


def _sel_win_kernel(q_ref, ksel_ref, vselt_ref, kwin_ref, vwint_ref, sel_ref, et_ref, ocmpt_ref,
                    gate_ref, dbias_ref, wbias_ref, o_ref, s_ref, m_ref, acc_ref, gt_ref):
    g = pl.program_id(1)
    i = pl.program_id(2)
    tq = NSA_TQ
    t0 = i * tq
    q2 = _stack_heads(q_ref[...])
    selm1 = sel_ref[0, 0] - 1.0
    qx = jnp.concatenate([q2, jnp.concatenate([selm1] * NSA_HPG, axis=0)], axis=1)
    qxh = [qx[c * HALF:(c + 1) * HALF] for c in range(NCH)]
    q2h = [q2[c * HALF:(c + 1) * HALF] for c in range(NCH)]
    kpb = SEL_TK // LANES
    dv = NSA_HEAD_DIM

    wq = WIN_TQ
    nsub = tq // wq
    win_q, win_k, win_v, win_b = [], [], [], []
    for h in range(nsub):
        u0 = t0 + h * wq
        wblk = jnp.maximum((u0 - WINDOW) // LANES, 0)
        start = pl.multiple_of(wblk * LANES, LANES)
        win_k.append(kwin_ref[pl.ds(start, WIN_KEYS), :])
        win_v.append(jnp.concatenate([vwint_ref[wblk + u] for u in range(WIN_KEYS // LANES)], axis=1))
        win_b.append(wbias_ref[jnp.minimum(i * nsub + h, WINDOW // wq)].astype(F32))
        win_q.append(jnp.concatenate([q2[j * tq + h * wq:j * tq + (h + 1) * wq] for j in range(NSA_HPG)], axis=0))
    sws = [_dot_nt(win_k[h], win_q[h]) for h in range(nsub)]

    def key_tile(kv):
        off = pl.multiple_of(kv * SEL_TK, SEL_TK)
        return jnp.concatenate([ksel_ref[pl.ds(off, SEL_TK), :], et_ref[pl.ds(off, SEL_TK), :]], axis=1)

    def scores_into(kv):
        kx = key_tile(kv)
        for c in range(NCH):
            s_ref[:, c * HALF:(c + 1) * HALF] = _dot_nt(kx, qxh[c])

    def update_cols(cols, vt, s):
        m_old = m_ref[:, cols]
        m_new = jnp.maximum(m_old, jnp.max(s, axis=0, keepdims=True))
        p = jnp.exp2(s - m_new).astype(BF16)
        acc_ref[:, cols] = jnp.exp2(m_old - m_new) * acc_ref[:, cols] + _dot(vt, p)
        m_ref[:, cols] = m_new

    def update_chunk(c, vt, s):
        update_cols(slice(c * HALF, (c + 1) * HALF), vt, s)

    def update(vt, s_chunks):
        for c in range(NCH):
            update_chunk(c, vt, s_chunks[c])

    def sel_values(kv):
        return jnp.concatenate([vselt_ref[kv * kpb + u] for u in range(kpb)], axis=1)

    def load_scores():
        return [s_ref[:, c * HALF:(c + 1) * HALF] for c in range(NCH)]

    m_ref[...] = jnp.full(m_ref.shape, MASK_VALUE, F32)
    acc_ref[...] = jnp.zeros(acc_ref.shape, F32)
    kd = t0 // SEL_TK
    scores_into(0)
    ot_win = []
    for h in range(nsub):
        sw = sws[h] + jnp.concatenate([win_b[h]] * NSA_HPG, axis=1)
        mw = jnp.max(sw, axis=0, keepdims=True)
        accw = _dot(win_v[h], jnp.exp2(sw - mw).astype(BF16))
        ot_win.append(accw[0:dv] * (1.0 / jnp.maximum(accw[dv:dv + 1], 1e-30)))

    def body(kv, carry):
        kx = key_tile(kv + 1)
        vt = sel_values(kv)
        pending = None
        for c in range(NCH):
            cols = slice(c * HALF, (c + 1) * HALF)
            s = s_ref[:, cols]
            s_ref[:, cols] = _dot_nt(kx, qxh[c])
            if pending is not None:
                update_chunk(c - 1, vt, pending)
            pending = s
        update_chunk(NCH - 1, vt, pending)
        return carry

    lax.fori_loop(0, kd, body, 0)
    dbias = dbias_ref[i % (SEL_TK // tq)].astype(F32)
    vt = sel_values(kd)
    if SEL_TK == tq and tq % (2 * LANES) == 0:
        hq = tq // 2
        for c in range(NCH):
            for j0 in range(c * HALF, (c + 1) * HALF, tq):
                update_cols(slice(j0, j0 + hq), vt[:, 0:hq], s_ref[0:hq, j0:j0 + hq] + dbias[0:hq, 0:hq])
                update_cols(slice(j0 + hq, j0 + tq), vt, s_ref[:, j0 + hq:j0 + tq] + dbias[:, hq:tq])
    else:
        update(vt, [sc + dbias for sc in load_scores()])
    acc = acc_ref[...]
    ot_slc = acc[0:dv] * (1.0 / jnp.maximum(acc[dv:dv + 1], 1e-30))

    gt_ref[...] = _sigmoid(gate_ref[...].astype(F32)).T
    ots = []
    for j in range(NSA_HPG):
        base = g * (NSA_HPG * 3) + j * 3
        ots.append(ocmpt_ref[0, 0, 0, :, j * tq:(j + 1) * tq].astype(F32)
                   + gt_ref[pl.ds(base + 1, 1), :] * ot_slc[:, j * tq:(j + 1) * tq]
                   + gt_ref[pl.ds(base + 2, 1), :]
                   * jnp.concatenate([ot_win[h][:, j * wq:(j + 1) * wq] for h in range(nsub)], axis=1))
    for jp in range(NSA_HPG // 2):
        pair = jnp.concatenate([ots[2 * jp], ots[2 * jp + 1]], axis=0)
        o_ref[:, jp * LANES:(jp + 1) * LANES] = pair.T.astype(o_ref.dtype)


def _diag_bias():
    r = jnp.arange(SEL_TK)[None, :, None]
    c = jnp.arange(NSA_TQ)[None, None, :]
    off = (jnp.arange(SEL_TK // NSA_TQ) * NSA_TQ)[:, None, None]
    return jnp.where(r <= off + c, 0.0, MASK_VALUE).astype(BF16)


def _window_bias():
    r = jnp.arange(WIN_KEYS)[None, :, None]
    c = jnp.arange(WIN_TQ)[None, None, :]
    off = jnp.minimum(jnp.arange(WINDOW // WIN_TQ + 1) * WIN_TQ, WINDOW)[:, None, None]
    diff = off + c - r
    return jnp.where((diff >= 0) & (diff < WINDOW), 0.0, MASK_VALUE).astype(BF16)


def _nsa_sel_win(proj, vsel_t, vwin_t, sel, et, ocmpt, B, S):
    T = B * S
    tq = NSA_TQ
    assert HALF == tq
    dbias = _diag_bias()
    wbias = _window_bias()
    nq = S // tq
    G = NSA_KV_GROUPS
    qw = NSA_HPG * HEAD_SLOT
    qblk = C_QEXT // qw
    vsel_t = vsel_t.reshape(G, B, S // LANES, VT_ROWS, LANES)
    vwin_t = vwin_t.reshape(G, B, S // LANES, VT_ROWS, LANES)
    ow = NSA_HPG * NSA_HEAD_DIM
    kv_spec = lambda c: pl.BlockSpec((S, LANES), lambda b, g, i: (b, c // LANES))
    vt_spec = pl.BlockSpec((None, None, S // LANES, VT_ROWS, LANES), lambda b, g, i: (g, b, 0, 0, 0))
    return pl.pallas_call(
        _sel_win_kernel,
        grid=(B, G, nq),
        in_specs=[pl.BlockSpec((tq, qw), lambda b, g, i: (b * nq + i, qblk + g)),
                  kv_spec(C_KSEL), vt_spec, kv_spec(C_KWIN), vt_spec,
                  pl.BlockSpec((1, 1, tq, SEL_LANES), lambda b, g, i: (b, g, i, 0)),
                  pl.BlockSpec(et.shape, lambda b, g, i: (0, 0)),
                  pl.BlockSpec((1, 1, 1, NSA_HEAD_DIM, NSA_HPG * tq), lambda b, g, i: (b, g, i, 0, 0)),
                  pl.BlockSpec((tq, LANES), lambda b, g, i: (b * nq + i, C_SMALL // LANES)),
                  pl.BlockSpec(dbias.shape, lambda b, g, i: (0, 0, 0)),
                  pl.BlockSpec(wbias.shape, lambda b, g, i: (0, 0, 0))],
        out_specs=pl.BlockSpec((tq, ow), lambda b, g, i: (b * nq + i, g)),
        out_shape=jax.ShapeDtypeStruct((T, NSA_Q_W), BF16),
        scratch_shapes=[pltpu.VMEM((SEL_TK, NSA_HPG * tq), F32),
                        pltpu.VMEM((1, NSA_HPG * tq), F32),
                        pltpu.VMEM((VT_ROWS, NSA_HPG * tq), F32),
                        pltpu.VMEM((LANES, tq), F32)],
        compiler_params=_cparams(("parallel", "parallel", "parallel")),
        name="nsa_sel_win",
    )(proj, proj, vsel_t, proj, vwin_t, sel, et, ocmpt, proj, dbias, wbias)


TAIL_ROWS = 8


def _causal_conv(x, tail_ref, w, b):
    L, C = x.shape
    nv = L // TAIL_ROWS
    xx = jnp.concatenate([tail_ref[...], x], axis=0).reshape(nv + 1, TAIL_ROWS, C)
    sub = lax.broadcasted_iota(jnp.int32, (nv, TAIL_ROWS, C), 1)
    y = b + w[CONV_WIDTH - 1:CONV_WIDTH] * x
    for k in range(1, CONV_WIDTH):
        rot = pltpu.roll(xx, k, axis=1)
        shifted = jnp.where(sub >= k, rot[1:], rot[:-1]).reshape(L, C)
        y = y + w[CONV_WIDTH - 1 - k:CONV_WIDTH - k] * shifted
    tail_ref[...] = x[L - TAIL_ROWS:L]
    return y


SSD_CPS = 2


def _ssd_kernel(z_ref, xbc_ref, dtc_ref, dtr_ref, cw_ref, cb_ref, dtbc_ref, dtbr_ref, alc_ref, alr_ref,
                dsk_ref, nw_ref, o_ref, state_ref, tail_ref, y_ref):
    c = pl.program_id(1)
    L = SSD_CHUNK
    P = SSD_HEAD_DIM
    N = SSD_STATE
    hpg = SSD_HEADS // SSD_GROUPS

    @pl.when(c == 0)
    def _():
        state_ref[...] = jnp.zeros_like(state_ref)
        tail_ref[...] = jnp.zeros_like(tail_ref)

    conv = _causal_conv(xbc_ref[...].astype(F32), tail_ref, cw_ref[...], cb_ref[...])
    xbc = conv * _sigmoid(conv)
    dt_c = _softplus(dtc_ref[...] + dtbc_ref[...])
    dt_r = _softplus(dtr_ref[...] + dtbr_ref[...])
    a_c = dt_c * (-jnp.exp(alc_ref[...]))
    a_r = dt_r * (-jnp.exp(alr_ref[...]))
    ii = lax.broadcasted_iota(jnp.int32, (L, L), 0)
    jj = lax.broadcasted_iota(jnp.int32, (L, L), 1)
    tri = ii >= jj
    tril = tri.astype(F32)
    triu = (ii <= jj).astype(F32)

    for cc in range(SSD_CPS):
        rs = slice(cc * L, (cc + 1) * L)
        xs = xbc[rs, 0:SSD_INNER]
        bm = xbc[rs, SSD_INNER:SSD_INNER + SSD_GROUPS * N]
        cm = xbc[rs, SSD_INNER + SSD_GROUPS * N:SSD_INNER + 2 * SSD_GROUPS * N]
        acum_c = _dot_f32(tril, a_c[rs])
        acum_r = _dot_f32(a_r[:, rs], triu)
        for g in range(SSD_GROUPS):
            bg = bm[:, g * N:(g + 1) * N]
            cgb = cm[:, g * N:(g + 1) * N].astype(BF16)
            cb = _dot_nt(cgb, bg.astype(BF16))
            for j in range(hpg):
                h = g * hpg + j
                acb = jnp.broadcast_to(acum_c[:, h:h + 1], (L, L))
                a_last = acum_c[L - 1:L, h:h + 1]
                xh_raw = xs[:, h * P:(h + 1) * P]
                xh = (xh_raw * jnp.broadcast_to(dt_c[rs, h:h + 1], (L, P))).astype(BF16)
                lmat = jnp.where(tri, jnp.exp(acb - acum_r[h:h + 1, :]), 0.0)
                y = _dot((cb * lmat).astype(BF16), xh)
                prev = state_ref[h]
                y = y + _dot(cgb, prev.astype(BF16)) * jnp.exp(acb[:, 0:P])
                bd = (bg * jnp.exp(a_last - acb[:, 0:N])).astype(BF16)
                state_ref[h] = jnp.exp(a_last) * prev + _dot_tn(bd, xh)
                y_ref[rs, h * P:(h + 1) * P] = y + dsk_ref[:, h * P:(h + 1) * P] * xh_raw

    zf = z_ref[...].astype(F32)
    yg = y_ref[...] * (zf * _sigmoid(zf))
    ms = jnp.mean(yg * yg, axis=-1, keepdims=True)
    o_ref[...] = (yg * lax.rsqrt(ms + RMS_EPS) * nw_ref[...]).astype(o_ref.dtype)


def _ssd_mixer(proj, dt_col, dt_row, conv_w, conv_b, dt_bias, a_log, d_skip, norm_w, B, S):
    T = B * S
    L = SSD_CPS * SSD_CHUNK
    nc = S // L
    H = SSD_HEADS
    full = lambda a: pl.BlockSpec(a.shape, lambda b, c: (0, 0))
    cb2 = conv_b.reshape(1, -1)
    dtb_c = dt_bias.reshape(1, H)
    dtb_r = dt_bias.reshape(H, 1)
    al_c = a_log.reshape(1, H)
    al_r = a_log.reshape(H, 1)
    dsk = jnp.repeat(d_skip, SSD_HEAD_DIM).reshape(1, SSD_INNER)
    nw = norm_w.reshape(1, SSD_INNER)
    return pl.pallas_call(
        _ssd_kernel,
        grid=(B, nc),
        in_specs=[pl.BlockSpec((L, SSD_INNER), lambda b, c: (b * nc + c, C_SSDZ // SSD_INNER)),
                  pl.BlockSpec((L, SSD_XBC_W), lambda b, c: (b * nc + c, C_XBC // SSD_XBC_W)),
                  pl.BlockSpec((L, H), lambda b, c: (b * nc + c, 0)),
                  pl.BlockSpec((H, L), lambda b, c: (0, b * nc + c)),
                  full(conv_w), full(cb2), full(dtb_c), full(dtb_r), full(al_c), full(al_r),
                  full(dsk), full(nw)],
        out_specs=pl.BlockSpec((L, SSD_INNER), lambda b, c: (b * nc + c, 0)),
        out_shape=jax.ShapeDtypeStruct((T, SSD_INNER), BF16),
        scratch_shapes=[pltpu.VMEM((H, SSD_STATE, SSD_HEAD_DIM), F32),
                        pltpu.VMEM((TAIL_ROWS, SSD_XBC_W), F32),
                        pltpu.VMEM((L, SSD_INNER), F32)],
        compiler_params=_cparams(("parallel", "arbitrary")),
        name="ssd_mixer",
    )(proj, proj, dt_col, dt_row, conv_w, cb2, dtb_c, dtb_r, al_c, al_r, dsk, nw)


SCAN_GROUP = 8


def _lru_kernel(x_ref, y_ref, cw_ref, cb_ref, wa_ref, ba_ref, wx_ref, bx_ref, lam_ref, o_ref,
                h_ref, tail_ref, *, tc):
    c = pl.program_id(1)

    @pl.when(c == 0)
    def _():
        h_ref[...] = jnp.zeros_like(h_ref)
        tail_ref[...] = jnp.zeros_like(tail_ref)

    xr = _causal_conv(x_ref[...].astype(F32), tail_ref, cw_ref[...], cb_ref[...])
    xrb = xr.astype(BF16)
    r = _sigmoid(_dot(xrb, wa_ref[...]) + ba_ref[...])
    ig = _sigmoid(_dot(xrb, wx_ref[...]) + bx_ref[...])
    log_a = -LRU_C * r * _softplus(-lam_ref[...])
    a = jnp.exp(log_a)
    b = jnp.sqrt(1.0 - jnp.exp(2.0 * log_a)) * (ig * xr)
    ng = tc // SCAN_GROUP
    a = a.reshape(ng, SCAN_GROUP, a.shape[1])
    b = b.reshape(ng, SCAN_GROUP, b.shape[1])
    sub = lax.broadcasted_iota(jnp.int32, a.shape, 1)
    k = 1
    while k < SCAN_GROUP:
        keep = sub >= k
        a_s = jnp.where(keep, pltpu.roll(a, k, axis=1), 1.0)
        b_s = jnp.where(keep, pltpu.roll(b, k, axis=1), 0.0)
        b = a * b_s + b
        a = a * a_s
        k *= 2
    carry = h_ref[0:1, :]
    groups = []
    for rg in range(ng):
        hg = a[rg] * carry + b[rg]
        groups.append(hg)
        carry = hg[SCAN_GROUP - 1:SCAN_GROUP]
    h = jnp.concatenate(groups, axis=0)
    h_ref[...] = jnp.broadcast_to(carry, h_ref.shape)
    o_ref[...] = (h * _gelu_tanh(y_ref[...].astype(F32))).astype(o_ref.dtype)


def _block_diag(w):
    nb, c, d = w.shape
    eye = jnp.eye(nb, dtype=w.dtype)
    return (eye[:, None, :, None] * w[:, :, None, :]).reshape(nb * c, nb * d)


def _lru_mixer(proj, conv_w, conv_b, wa, ba, wx, bx, lam, B, S):
    T = B * S
    tc = min(512, S)
    nt = S // tc
    W = LRU_WIDTH
    wa_bd = _block_diag(wa).astype(BF16)
    wx_bd = _block_diag(wx).astype(BF16)
    row = lambda v: v.reshape(1, W)
    full = lambda a: pl.BlockSpec(a.shape, lambda b, c: (0, 0))
    args = (conv_w, row(conv_b), wa_bd, row(ba), wx_bd, row(bx), row(lam))
    return pl.pallas_call(
        functools.partial(_lru_kernel, tc=tc),
        grid=(B, nt),
        in_specs=[pl.BlockSpec((tc, W), lambda b, c: (b * nt + c, C_LRUX // W)),
                  pl.BlockSpec((tc, W), lambda b, c: (b * nt + c, C_LRUY // W))]
                 + [full(a) for a in args],
        out_specs=pl.BlockSpec((tc, W), lambda b, c: (b * nt + c, 0)),
        out_shape=jax.ShapeDtypeStruct((T, W), BF16),
        scratch_shapes=[pltpu.VMEM((TAIL_ROWS, W), F32), pltpu.VMEM((TAIL_ROWS, W), F32)],
        compiler_params=_cparams(("parallel", "arbitrary")),
        name="lru_mixer",
    )(proj, proj, *args)


def _merge_kernel(x_ref, mg_ref, on_ref, os_ref, ol_ref, pn_ref, ps_ref, pl_ref, wo_ref, g_ref, b_ref,
                  of_ref, ob_ref):
    d = D_MODEL
    gate = _sigmoid(mg_ref[...].astype(F32))
    mixed = (gate[:, 0:d] * _dot(on_ref[...], pn_ref[...])
             + gate[:, d:2 * d] * _dot(os_ref[...], ps_ref[...])
             + gate[:, 2 * d:3 * d] * _dot(ol_ref[...], pl_ref[...]))
    v = ALPHA * x_ref[...] + _dot(mixed.astype(BF16), wo_ref[...])
    out = _layer_norm(v, g_ref[...], b_ref[...])
    of_ref[...] = out
    ob_ref[...] = out.astype(BF16)


def _merge(x, proj, o_nsa, o_ssd, o_lru, pn, ps, plru, wo, g, b, layer):
    T = x.shape[0]
    tm = min(512, T)
    d = D_MODEL
    rowblk = lambda w: pl.BlockSpec((tm, w), lambda i: (i, 0))
    full = lambda a: pl.BlockSpec((None,) + a.shape[1:], lambda i: (layer, 0, 0))
    g2, b2 = g.reshape(-1, 1, d), b.reshape(-1, 1, d)
    return pl.pallas_call(
        _merge_kernel,
        grid=(T // tm,),
        in_specs=[rowblk(d), pl.BlockSpec((tm, 3 * d), lambda i: (i, C_MERGE // (3 * d))),
                  rowblk(o_nsa.shape[1]), rowblk(o_ssd.shape[1]), rowblk(o_lru.shape[1]),
                  full(pn), full(ps), full(plru), full(wo), full(g2), full(b2)],
        out_specs=[rowblk(d), rowblk(d)],
        out_shape=[jax.ShapeDtypeStruct((T, d), F32), jax.ShapeDtypeStruct((T, d), BF16)],
        compiler_params=_cparams(("parallel",)),
        name="merge_ln",
    )(x, proj, o_nsa, o_ssd, o_lru, pn, ps, plru, wo, g2, b2)


def _route(sel, aff):
    epg = EXPERTS_PER_GROUP
    scores = []
    for gi in range(N_EXPERT_GROUPS):
        v = sel[gi * epg:(gi + 1) * epg]
        pair = None
        for a in range(epg):
            for b in range(a + 1, epg):
                sab = v[a] + v[b]
                pair = sab if pair is None else jnp.maximum(pair, sab)
        scores.append(pair)
    best = jnp.zeros_like(scores[0], dtype=jnp.int32)
    best_s = scores[0]
    for gi in range(1, N_EXPERT_GROUPS):
        better = scores[gi] > best_s
        best = jnp.where(better, gi, best)
        best_s = jnp.where(better, scores[gi], best_s)
    chosen = []
    for k in range(N_EXPERTS):
        gi = k // epg
        rank = jnp.zeros_like(best)
        for o in range(gi * epg, (gi + 1) * epg):
            if o == k:
                continue
            ahead = (sel[o] > sel[k]) | ((sel[o] == sel[k]) & (o < k))
            rank = rank + ahead.astype(jnp.int32)
        chosen.append((best == gi) & (rank < TOP_K))
    wsum = None
    for k in range(N_EXPERTS):
        wk = jnp.where(chosen[k], aff[k], 0.0)
        wsum = wk if wsum is None else wsum + wk
    inv = 1.0 / wsum
    return [jnp.where(chosen[k], aff[k], 0.0) * inv for k in range(N_EXPERTS)]


EXPERT_PAIRS = N_EXPERTS // 2
PAIRS_PER_ITER = 8


def _moe_kernel(xb_ref, xf_ref, p_ref, rw_ref, rb_ref, pg_ref, pp_ref, wg_ref, wu_ref, wd_ref,
                g_ref, b_ref, of_ref, ob_ref, acc_ref, gates_ref):
    xb = xb_ref[...]
    tm = xb.shape[0]
    logits = _dot_nt(rw_ref[...], xb)
    aff = _sigmoid(logits)
    sel = aff + rb_ref[...]
    gate_rows = _route([sel[k:k + 1, :] for k in range(N_EXPERTS)],
                       [aff[k:k + 1, :] for k in range(N_EXPERTS)])
    gt = jnp.concatenate(gate_rows + [jnp.zeros((LANES - N_EXPERTS, tm), F32)], axis=0)
    gates_ref[...] = gt.T
    acc_ref[...] = _sigmoid(_dot(xb, pg_ref[...])) * _dot(p_ref[...].astype(BF16), pp_ref[...])
    lane = lax.broadcasted_iota(jnp.int32, (tm, LANES), 1)

    def gate_cols(k):
        gates = gates_ref[...]
        cols = [jnp.broadcast_to(jnp.sum(jnp.where(lane == 2 * k + u, gates, 0.0), axis=-1, keepdims=True),
                                 (tm, D_EXPERT)) for u in range(2)]
        return jnp.concatenate(cols, axis=1)

    def step(it, carry):
        ks = [it * PAIRS_PER_ITER + u for u in range(PAIRS_PER_ITER)]
        hgs = [[_dot(xb, wg_ref[2 * k + u]) for u in range(2)] for k in ks]
        hus = [[_dot(xb, wu_ref[2 * k + u]) for u in range(2)] for k in ks]
        for k, hg2, hu2 in zip(ks, hgs, hus):
            hg = jnp.concatenate(hg2, axis=1)
            h = (hg * _sigmoid(hg)) * jnp.concatenate(hu2, axis=1) * gate_cols(k)
            acc_ref[...] += _dot(h.astype(BF16), wd_ref[k])
        return carry

    lax.fori_loop(0, EXPERT_PAIRS // PAIRS_PER_ITER, step, 0)
    out = _layer_norm(ALPHA * xf_ref[...] + acc_ref[...], g_ref[...], b_ref[...])
    of_ref[...] = out
    ob_ref[...] = out.astype(BF16)


def _moe_ple(xb, xf, p, layer, rw_t, rb, pg, pp, wg, wu, wd, g, b):
    T = xb.shape[0]
    tm = min(512, T)
    d = D_MODEL
    rowblk = lambda w: pl.BlockSpec((tm, w), lambda i: (i, 0))
    once = pl.Buffered(1)
    full = lambda a: pl.BlockSpec(a.shape, lambda i: (0,) * a.ndim, pipeline_mode=once)
    lyr = lambda a: pl.BlockSpec((None,) + a.shape[1:], lambda i: (layer,) + (0,) * (a.ndim - 1), pipeline_mode=once)
    g2, b2 = g.reshape(-1, 1, d), b.reshape(-1, 1, d)
    return pl.pallas_call(
        _moe_kernel,
        grid=(T // tm,),
        in_specs=[rowblk(d), rowblk(d), pl.BlockSpec((None, tm, PLE_DIM), lambda i: (layer, i, 0)),
                  full(rw_t), full(rb), lyr(pg), lyr(pp),
                  lyr(wg), lyr(wu), lyr(wd), lyr(g2), lyr(b2)],
        out_specs=[rowblk(d), rowblk(d)],
        out_shape=[jax.ShapeDtypeStruct((T, d), F32), jax.ShapeDtypeStruct((T, d), BF16)],
        scratch_shapes=[pltpu.VMEM((tm, d), F32), pltpu.VMEM((tm, LANES), F32)],
        compiler_params=_cparams(("parallel",)),
        name="moe_ple_ln",
    )(xb, xf, p, rw_t, rb, pg, pp, wg, wu, wd, g2, b2)


def _overlap_matrix(nc):
    n = jnp.arange(nc)[None, :]
    m = jnp.arange(SEL_LANES)[:, None]
    ratio = SEL_BLOCK // CMP_STRIDE
    ov = jnp.zeros((SEL_LANES, nc), F32)
    for k in range(CMP_BLOCK // CMP_STRIDE):
        ov = ov + ((n + k) // ratio == m).astype(F32)
    return ov.astype(BF16)


def _expand_matrix(S):
    c = jnp.arange(S)[:, None]
    m = jnp.arange(SEL_LANES)[None, :]
    return jnp.where(c // SEL_BLOCK == m, -MASK_VALUE, 0.0).astype(BF16)


def _pad_w2(w2):
    out = jnp.zeros((NSA_KV_GROUPS, CMP_HIDDEN, HEAD_SLOT), F32)
    for g in range(NSA_KV_GROUPS):
        out = out.at[g, :, g * NSA_HEAD_DIM:(g + 1) * NSA_HEAD_DIM].set(w2)
    return out.astype(BF16)


def kernel(x, p, w_in, nsa_pe_k, nsa_w1_k, nsa_w2_k, nsa_pe_v, nsa_w1_v, nsa_w2_v, ssd_conv_w, ssd_conv_b, ssd_dt_bias, ssd_a_log, ssd_d, ssd_norm_w, lru_conv_w, lru_conv_b, lru_wa, lru_ba, lru_wx, lru_bx, lru_lambda, proj_nsa, proj_ssd, proj_lru, w_out, ln1_g, ln1_b, router_w, router_b, exp_w_gate, exp_w_up, exp_w_down, ple_w_gate, ple_w_proj, ln2_g, ln2_b):
    B, S, d = x.shape
    T = B * S
    depth = w_in.shape[0]
    assert d == D_MODEL and S % SEL_TK == 0 and S >= WIN_KEYS and S // SEL_BLOCK <= SEL_LANES
    nr = S // CMP_STRIDE
    ov = _overlap_matrix(nr)
    emat = _expand_matrix(S)
    rw_t = router_w.T.astype(BF16)
    rb = router_b.reshape(N_EXPERTS, 1).astype(F32)

    w_main = _prep_w_in(w_in)
    aux_w = _aux_weights(w_in)
    pn, ps, plru, wo = (a.astype(BF16) for a in (proj_nsa, proj_ssd, proj_lru, w_out))
    pg, pp, wg, wu = (a.astype(BF16) for a in (ple_w_gate, ple_w_proj, exp_w_gate, exp_w_up))
    wd = exp_w_down.reshape(depth, EXPERT_PAIRS, 2 * D_EXPERT, d).astype(BF16)
    p3 = p.reshape(depth, T, PLE_DIM)

    xf = x.reshape(T, d)
    xb = xf
    for i in range(depth):
        proj = _in_proj(xb, w_main, i)
        cmp, vsel_t, vwin_t, dt_col, dt_row = _aux_proj(xb, aux_w, i)

        w1big, pebig = _cmp_first_layer(nsa_w1_k[i], nsa_w1_v[i], nsa_pe_k[i], nsa_pe_v[i])
        kc, vct = _nsa_compress(cmp, w1big, pebig, _pad_w2(nsa_w2_k[i]),
                                nsa_w2_v[i].T.astype(BF16), B, S)
        ocmpt, sel = _nsa_cmp_attn(proj, kc, vct, ov, B, S)
        o_nsa = _nsa_sel_win(proj, vsel_t, vwin_t, sel, emat, ocmpt, B, S)

        o_ssd = _ssd_mixer(proj, dt_col, dt_row, ssd_conv_w[i], ssd_conv_b[i], ssd_dt_bias[i],
                           ssd_a_log[i], ssd_d[i], ssd_norm_w[i], B, S)
        o_lru = _lru_mixer(proj, lru_conv_w[i], lru_conv_b[i], lru_wa[i], lru_ba[i], lru_wx[i],
                           lru_bx[i], lru_lambda[i], B, S)

        xf, xb = _merge(xf, proj, o_nsa, o_ssd, o_lru, pn, ps, plru, wo, ln1_g, ln1_b, i)
        xf, xb = _moe_ple(xb, xf, p3, i, rw_t, rb, pg, pp, wg, wu, wd, ln2_g, ln2_b)
    return xf.reshape(B, S, d)
```
